```python
import math
import jax, jax.numpy as jnp
from jax import lax
import numpy as np

D_MODEL = 1024
BATCH = 4
SEQ = 4096
DEPTH = 2
DEC_BATCH = 32
DEC_SEQ = 4
PAST_LEN = 8192
PAGE_SIZE = 128

N_MIXERS = 2
N_NSA = (DEPTH + 1) // 2
N_ML = DEPTH // 2
NSA_HEADS = 16
NSA_KV_HEADS = 4
NSA_HPG = NSA_HEADS // NSA_KV_HEADS
HEAD_DIM = D_MODEL // NSA_HEADS
CMP_LEN = 32
CMP_STRIDE = 16
CMP_HID = HEAD_DIM
SEL_BLK = 64
SEL_TOPK = 16
WINDOW = 512
Q_BLK = 128
FORCE_BONUS = 1e4
NSA_Q_W = NSA_HEADS * HEAD_DIM
NSA_KV_W = NSA_KV_HEADS * HEAD_DIM
NSA_IN = NSA_Q_W + 6 * NSA_KV_W + 3 * NSA_HEADS
ML_HEADS = 8
ML_DQK = D_MODEL // (2 * ML_HEADS)
ML_DV = D_MODEL // ML_HEADS
ML_CHUNK = 64
ML_QK_W = ML_HEADS * ML_DQK
ML_V_W = ML_HEADS * ML_DV
ML_IN = 2 * ML_QK_W + 2 * ML_V_W + 2 * ML_HEADS
D_FF = 4 * D_MODEL
CONV_W = 3
RMS_EPS = 1e-6
MASK_NEG = -1e30

kernel_name = "nsa_mlstm_hybrid_decode_step"


def rmsnorm(x, g):
    xf = x.astype(jnp.float32)
    y = xf * lax.rsqrt(jnp.mean(xf * xf, axis=-1, keepdims=True) + RMS_EPS)
    return (y * g.astype(jnp.float32)).astype(x.dtype)


def masked_softmax(s, mask):
    s = jnp.where(mask, s.astype(jnp.float32), MASK_NEG)
    e = jnp.exp(s - jnp.max(s, axis=-1, keepdims=True)) * mask
    return e / jnp.maximum(jnp.sum(e, axis=-1, keepdims=True), 1e-30)


def alibi_slopes():
    h = jnp.arange(1, NSA_HEADS + 1, dtype=jnp.float32)
    return jnp.exp2(-8.0 * h / NSA_HEADS).reshape(NSA_KV_HEADS, NSA_HPG)


def nsa_project(h, w_in):
    B, T, _ = h.shape
    p = jnp.einsum('btd,de->bte', h, w_in)
    q = p[..., :NSA_Q_W].reshape(B, T, NSA_KV_HEADS, NSA_HPG, HEAD_DIM)
    kv = p[..., NSA_Q_W:NSA_Q_W + 6 * NSA_KV_W].reshape(B, T, 6, NSA_KV_HEADS, HEAD_DIM)
    gates = p[..., NSA_Q_W + 6 * NSA_KV_W:].reshape(B, T, NSA_KV_HEADS, NSA_HPG, 3)
    return q, kv, gates


def nsa_context(kv, w_cmp1, w_cmp2, cmp_pos):
    B, T = kv.shape[:2]
    ratio = CMP_LEN // CMP_STRIDE
    n_chunk = T // CMP_STRIDE
    nc = n_chunk - ratio + 1
    ch = kv[:, :n_chunk * CMP_STRIDE, :2].reshape(B, n_chunk, CMP_STRIDE, 2, NSA_KV_HEADS, HEAD_DIM)
    pre = None
    for r in range(ratio):
        sl = slice(r * CMP_STRIDE, (r + 1) * CMP_STRIDE)
        part = jnp.einsum('bjlcgd,cldh->bjcgh', ch + cmp_pos[None, None, sl, :, None, :], w_cmp1[:, sl])
        part = part[:, r:r + nc]
        pre = part if pre is None else pre + part
    kvc = jnp.einsum('bjcgh,chd->bjcgd', jax.nn.gelu(pre, approximate=True), w_cmp2)
    kc, vc = kvc[:, :, 0], kvc[:, :, 1]
    c_start = jnp.arange(nc) * CMP_STRIDE
    c_end = c_start + CMP_LEN - 1
    n_sel = -(-T // SEL_BLK)
    sel = jnp.pad(kv[:, :, 2:4], ((0, 0), (0, n_sel * SEL_BLK - T), (0, 0), (0, 0), (0, 0)))
    sel = sel.reshape(B, n_sel, SEL_BLK, 2, NSA_KV_HEADS, HEAD_DIM).transpose(3, 0, 4, 1, 2, 5)
    ks, vs = sel[0], sel[1]
    s_start = jnp.arange(n_sel) * SEL_BLK
    ov = ((c_start[:, None] < s_start[None, :] + SEL_BLK) &
          (c_start[:, None] + CMP_LEN > s_start[None, :])).astype(jnp.float32)
    return kc, vc, c_end, ks, vs, ov


def nsa_core(q, gates, tq, kc, vc, c_end, ks, vs, ov, kw, vw, sw, slopes):
    B, Tq = q.shape[:2]
    scale = HEAD_DIM ** -0.5
    sl5 = slopes[None, :, :, None, None]
    dist_c = tq[:, None] - c_end[None, :]
    sc = jnp.einsum('btghd,bjgd->bghtj', q, kc).astype(jnp.float32) * scale - sl5 * dist_c.astype(jnp.float32)
    pc = masked_softmax(sc, dist_c >= 0)
    oc = jnp.einsum('bghtj,bjgd->btghd', pc.astype(vc.dtype), vc)
    imp = jnp.einsum('bghtj,jn->bgtn', pc, ov)
    n_sel = ks.shape[2]
    bid = jnp.arange(n_sel)
    cur = tq // SEL_BLK
    valid = bid[None, :] * SEL_BLK <= tq[:, None]
    forced = (bid[None, :] == 0) | (bid[None, :] == cur[:, None]) | (bid[None, :] == cur[:, None] - 1)
    score = jnp.where(valid, imp + jnp.where(forced, FORCE_BONUS, 0.0), MASK_NEG)
    k_eff = min(SEL_TOPK, n_sel)
    _, idx = lax.top_k(score, k_eff)
    bi = jnp.arange(B)[:, None, None, None]
    gi = jnp.arange(NSA_KV_HEADS)[None, :, None, None]
    kg = ks[bi, gi, idx].reshape(B, NSA_KV_HEADS, Tq, k_eff * SEL_BLK, HEAD_DIM)
    vg = vs[bi, gi, idx].reshape(B, NSA_KV_HEADS, Tq, k_eff * SEL_BLK, HEAD_DIM)
    s_pos = (idx[..., None] * SEL_BLK + jnp.arange(SEL_BLK)).reshape(B, NSA_KV_HEADS, Tq, k_eff * SEL_BLK)
    dist_s = (tq[None, None, :, None] - s_pos)[:, :, None]
    ss = jnp.einsum('btghd,bgtnd->bghtn', q, kg).astype(jnp.float32) * scale - sl5 * dist_s.astype(jnp.float32)
    ps = masked_softmax(ss, dist_s >= 0)
    osel = jnp.einsum('bghtn,bgtnd->btghd', ps.astype(vg.dtype), vg)
    dist_w = tq[:, None] - sw[None, :]
    swin = jnp.einsum('btghd,bsgd->bghts', q, kw).astype(jnp.float32) * scale - sl5 * dist_w.astype(jnp.float32)
    pw = masked_softmax(swin, (dist_w >= 0) & (dist_w < WINDOW) & (sw[None, :] >= 0))
    ow = jnp.einsum('bghts,bsgd->btghd', pw.astype(vw.dtype), vw)
    g = jax.nn.sigmoid(gates.astype(jnp.float32))
    return g[..., 0:1] * oc + g[..., 1:2] * osel + g[..., 2:3] * ow


def nsa_prompt(h, w_in, w_out, w_cmp1, w_cmp2, cmp_pos, slopes):
    B, T, _ = h.shape
    q, kv, gates = nsa_project(h, w_in)
    kc, vc, c_end, ks, vs, ov = nsa_context(kv[:, :, :4], w_cmp1, w_cmp2, cmp_pos)
    win = kv[:, :, 4:]
    winp = jnp.pad(win, ((0, 0), (WINDOW, 0), (0, 0), (0, 0), (0, 0)))
    qb_len = math.gcd(T, Q_BLK)

    def body(n):
        s0 = n * qb_len
        qb = lax.dynamic_slice_in_dim(q, s0, qb_len, axis=1)
        gb = lax.dynamic_slice_in_dim(gates, s0, qb_len, axis=1)
        wb = lax.dynamic_slice_in_dim(winp, s0, WINDOW + qb_len, axis=1)
        tq = s0 + jnp.arange(qb_len)
        sw = s0 - WINDOW + jnp.arange(WINDOW + qb_len)
        return nsa_core(qb, gb, tq, kc, vc, c_end, ks, vs, ov, wb[:, :, 0], wb[:, :, 1], sw, slopes)

    o = lax.map(body, jnp.arange(T // qb_len))
    o = o.transpose(1, 0, 2, 3, 4, 5).reshape(B, T, NSA_Q_W).astype(h.dtype)
    w_keep = min(WINDOW, T)
    return o @ w_out, kv[:, :, :4], win[:, T - w_keep:]


def nsa_sample(h, cache_kv, page_table, win_buf, w_in, w_out, w_cmp1, w_cmp2, cmp_pos, slopes):
    B, T, _ = h.shape
    q, kv, gates = nsa_project(h, w_in)
    n_pages = page_table.shape[1]
    past_len = n_pages * PAGE_SIZE
    past = cache_kv[page_table].reshape(B, past_len, 4, NSA_KV_HEADS, HEAD_DIM)
    full = jnp.concatenate([past.astype(kv.dtype), kv[:, :, :4]], axis=1)
    kc, vc, c_end, ks, vs, ov = nsa_context(full, w_cmp1, w_cmp2, cmp_pos)
    wfull = jnp.concatenate([win_buf.astype(kv.dtype), kv[:, :, 4:]], axis=1)
    wb_len = win_buf.shape[1]
    sw = past_len - wb_len + jnp.arange(wb_len + T)
    tq = past_len + jnp.arange(T)
    o = nsa_core(q, gates, tq, kc, vc, c_end, ks, vs, ov, wfull[:, :, 0], wfull[:, :, 1], sw, slopes)
    o = o.reshape(B, T, NSA_Q_W).astype(h.dtype)
    return o @ w_out, kv[:, :, :4], wfull[:, T:]


def mlstm_mixer(h, C0, n0, m0, w_in, b_gate, head_norm, w_out):
    B, T, _ = h.shape
    p = jnp.einsum('btd,de->bte', h, w_in)
    f32 = jnp.float32
    q = p[..., :ML_QK_W].reshape(B, T, ML_HEADS, ML_DQK).transpose(0, 2, 1, 3).astype(f32)
    k = p[..., ML_QK_W:2 * ML_QK_W].reshape(B, T, ML_HEADS, ML_DQK).transpose(0, 2, 1, 3).astype(f32) * ML_DQK ** -0.5
    v = p[..., 2 * ML_QK_W:2 * ML_QK_W + ML_V_W].reshape(B, T, ML_HEADS, ML_DV).transpose(0, 2, 1, 3).astype(f32)
    o = p[..., 2 * ML_QK_W + ML_V_W:2 * ML_QK_W + 2 * ML_V_W]
    gp = p[..., 2 * ML_QK_W + 2 * ML_V_W:].reshape(B, T, 2, ML_HEADS).astype(f32) + b_gate.astype(f32)
    ig = gp[:, :, 0].transpose(0, 2, 1)
    logf = jax.nn.log_sigmoid(gp[:, :, 1]).transpose(0, 2, 1)
    L = math.gcd(T, ML_CHUNK)
    nc = T // L

    def chunks(a):
        return jnp.moveaxis(a.reshape(B, ML_HEADS, nc, L, *a.shape[3:]), 2, 0)

    causal = jnp.tril(jnp.ones((L, L), dtype=bool))

    def step(carry, xs):
        C, n, m = carry
        qc, kc, vc, ic, fc = xs
        b = jnp.cumsum(fc, axis=-1)
        D = jnp.where(causal, b[..., :, None] - b[..., None, :] + ic[..., None, :], -jnp.inf)
        inter = b + m[..., None]
        mt = jnp.maximum(inter, jnp.max(D, axis=-1))
        S = jnp.einsum('bhtd,bhsd->bhts', qc, kc) * jnp.exp(D - mt[..., None])
        wi = jnp.exp(inter - mt)
        num = wi[..., None] * jnp.einsum('bhvd,bhtd->bhtv', C, qc) + jnp.einsum('bhts,bhsv->bhtv', S, vc)
        den = wi * jnp.einsum('bhd,bhtd->bht', n, qc) + jnp.sum(S, axis=-1)
        hc = num / jnp.maximum(jnp.abs(den), jnp.exp(-mt))[..., None]
        mL = mt[..., -1]
        decay = jnp.exp(b[..., -1] + m - mL)
        ws = jnp.exp(b[..., -1:] - b + ic - mL[..., None])
        C = decay[..., None, None] * C + jnp.einsum('bhs,bhsv,bhsd->bhvd', ws, vc, kc)
        n = decay[..., None] * n + jnp.einsum('bhs,bhsd->bhd', ws, kc)
        return (C, n, mL), hc

    (C, n, m), hs = lax.scan(step, (C0.astype(f32), n0.astype(f32), m0.astype(f32)),
                             (chunks(q), chunks(k), chunks(v), chunks(ig), chunks(logf)))
    hseq = jnp.moveaxis(hs, 0, 2).reshape(B, ML_HEADS, T, ML_DV).transpose(0, 2, 1, 3)
    hseq = hseq * lax.rsqrt(jnp.mean(hseq * hseq, axis=-1, keepdims=True) + RMS_EPS)
    hseq = hseq.reshape(B, T, ML_V_W) * head_norm.astype(f32) * jax.nn.sigmoid(o.astype(f32))
    return hseq.astype(h.dtype) @ w_out, C, n, m


def conv_ffn(h, buf, w_up, conv_w, conv_b, w_down):
    T = h.shape[1]
    u = h @ w_up
    full = jnp.concatenate([buf.astype(u.dtype), u], axis=1)
    c = conv_b + sum(conv_w[j] * full[:, j:j + T] for j in range(CONV_W))
    y = jax.nn.gelu(c[..., :D_FF], approximate=True) * c[..., D_FF:]
    return y @ w_down, full[:, T:]


def setup_inputs(seed: int = 0) -> dict:
    key = jax.random.key(seed)
    ks = jax.random.split(key, 24)
    n_pages = PAST_LEN // PAGE_SIZE
    n_used = DEC_BATCH * n_pages
    n_phys = n_used + max(1, n_used // 4)
    win_buf = min(WINDOW, PAST_LEN)

    def nrm(k, shape, s=1.0):
        return s * jax.random.normal(k, shape, jnp.float32)

    page_table = jax.random.permutation(ks[8], n_phys)[:n_used].reshape(DEC_BATCH, n_pages).astype(jnp.int32)
    b_gate = jnp.stack([nrm(ks[16], (N_ML, ML_HEADS), 0.1), 3.0 + nrm(ks[17], (N_ML, ML_HEADS), 0.5)], axis=1)
    return {
        "x_prompt": nrm(ks[0], (BATCH, SEQ, D_MODEL)),
        "x_sample": nrm(ks[1], (DEC_BATCH, DEC_SEQ, D_MODEL)),
        "cache_nsa_kv": nrm(ks[2], (N_NSA, n_phys, PAGE_SIZE, 4, NSA_KV_HEADS, HEAD_DIM)),
        "cache_nsa_win": nrm(ks[3], (N_NSA, DEC_BATCH, win_buf, 2, NSA_KV_HEADS, HEAD_DIM)),
        "state_mlstm_C": nrm(ks[4], (N_ML, DEC_BATCH, ML_HEADS, ML_DV, ML_DQK), 0.3),
        "state_mlstm_n": jnp.abs(nrm(ks[5], (N_ML, DEC_BATCH, ML_HEADS, ML_DQK), 0.5)),
        "state_mlstm_m": nrm(ks[6], (N_ML, DEC_BATCH, ML_HEADS)),
        "state_ffn_conv": nrm(ks[7], (DEPTH, DEC_BATCH, CONV_W - 1, 2 * D_FF)),
        "page_table": page_table,
        "norm_g": 1.0 + nrm(ks[9], (DEPTH, 4, D_MODEL), 0.1),
        "w_nsa_in": nrm(ks[10], (N_NSA, D_MODEL, NSA_IN), D_MODEL ** -0.5),
        "w_nsa_out": nrm(ks[11], (N_NSA, NSA_Q_W, D_MODEL), NSA_Q_W ** -0.5),
        "w_cmp1": nrm(ks[12], (N_NSA, 2, CMP_LEN, HEAD_DIM, CMP_HID), (CMP_LEN * HEAD_DIM) ** -0.5),
        "w_cmp2": nrm(ks[13], (N_NSA, 2, CMP_HID, HEAD_DIM), CMP_HID ** -0.5),
        "cmp_pos": nrm(ks[14], (N_NSA, CMP_LEN, 2, HEAD_DIM), 0.5),
        "w_ml_in": nrm(ks[15], (N_ML, D_MODEL, ML_IN), D_MODEL ** -0.5),
        "b_ml_gate": b_gate,
        "ml_head_norm": 1.0 + nrm(ks[18], (N_ML, ML_V_W), 0.1),
        "w_ml_out": nrm(ks[19], (N_ML, ML_V_W, D_MODEL), ML_V_W ** -0.5),
        "w_ffn_up": nrm(ks[20], (DEPTH, D_MODEL, 2 * D_FF), D_MODEL ** -0.5),
        "ffn_conv_w": nrm(ks[21], (DEPTH, CONV_W, 2 * D_FF), CONV_W ** -0.5),
        "ffn_conv_b": nrm(ks[22], (DEPTH, 2 * D_FF), 0.02),
        "w_ffn_down": nrm(ks[23], (DEPTH, D_FF, D_MODEL), D_FF ** -0.5),
    }


def reference(x_prompt, x_sample, cache_nsa_kv, cache_nsa_win, state_mlstm_C, state_mlstm_n, state_mlstm_m,
              state_ffn_conv, page_table, norm_g, w_nsa_in, w_nsa_out, w_cmp1, w_cmp2, cmp_pos, w_ml_in,
              b_ml_gate, ml_head_norm, w_ml_out, w_ffn_up, ffn_conv_w, ffn_conv_b, w_ffn_down):
    slopes = alibi_slopes()
    xp, xs = x_prompt, x_sample
    kv_p, kv_s, win_p, win_s = [], [], [], []
    C_p, C_s, n_p, n_s, m_p, m_s = [], [], [], [], [], []
    cv_p, cv_s = [], []
    for i in range(DEPTH):
        j = i // N_MIXERS
        hp = rmsnorm(xp, norm_g[i, 0])
        hs = rmsnorm(xs, norm_g[i, 0])
        if i % N_MIXERS == 0:
            yp, a, b = nsa_prompt(hp, w_nsa_in[j], w_nsa_out[j], w_cmp1[j], w_cmp2[j], cmp_pos[j], slopes)
            kv_p.append(a); win_p.append(b)
            ys, a, b = nsa_sample(hs, cache_nsa_kv[j], page_table, cache_nsa_win[j], w_nsa_in[j], w_nsa_out[j],
                                  w_cmp1[j], w_cmp2[j], cmp_pos[j], slopes)
            kv_s.append(a); win_s.append(b)
        else:
            Bp = hp.shape[0]
            zC = jnp.zeros((Bp, ML_HEADS, ML_DV, ML_DQK), jnp.float32)
            zn = jnp.zeros((Bp, ML_HEADS, ML_DQK), jnp.float32)
            zm = jnp.zeros((Bp, ML_HEADS), jnp.float32)
            yp, c, n, m = mlstm_mixer(hp, zC, zn, zm, w_ml_in[j], b_ml_gate[j], ml_head_norm[j], w_ml_out[j])
            C_p.append(c); n_p.append(n); m_p.append(m)
            ys, c, n, m = mlstm_mixer(hs, state_mlstm_C[j], state_mlstm_n[j], state_mlstm_m[j], w_ml_in[j],
                                      b_ml_gate[j], ml_head_norm[j], w_ml_out[j])
            C_s.append(c); n_s.append(n); m_s.append(m)
        xp = xp + rmsnorm(yp, norm_g[i, 1])
        xs = xs + rmsnorm(ys, norm_g[i, 1])
        zbuf = jnp.zeros((xp.shape[0], CONV_W - 1, 2 * D_FF), xp.dtype)
        fp, bp = conv_ffn(rmsnorm(xp, norm_g[i, 2]), zbuf, w_ffn_up[i], ffn_conv_w[i], ffn_conv_b[i], w_ffn_down[i])
        fs, bs = conv_ffn(rmsnorm(xs, norm_g[i, 2]), state_ffn_conv[i], w_ffn_up[i], ffn_conv_w[i], ffn_conv_b[i],
                          w_ffn_down[i])
        cv_p.append(bp); cv_s.append(bs)
        xp = xp + rmsnorm(fp, norm_g[i, 3])
        xs = xs + rmsnorm(fs, norm_g[i, 3])
    return (xp, xs, jnp.stack(kv_p), jnp.stack(kv_s), jnp.stack(win_p), jnp.stack(win_s),
            jnp.stack(C_p), jnp.stack(C_s), jnp.stack(n_p), jnp.stack(n_s), jnp.stack(m_p), jnp.stack(m_s),
            jnp.stack(cv_p), jnp.stack(cv_s))
```

```python
import functools
import math

import jax
import jax.numpy as jnp
from jax import lax
from jax.experimental import pallas as pl
from jax.experimental.pallas import tpu as pltpu

F32 = jnp.float32
BF16 = jnp.bfloat16

LANES = 128
SUBLANES = 8
VMEM_LIMIT_BYTES = 56 * 1024 * 1024

D_MODEL = 1024
NSA_HEADS = 16
NSA_KV_HEADS = 4
NSA_HPG = NSA_HEADS // NSA_KV_HEADS
HEAD_DIM = D_MODEL // NSA_HEADS
CMP_LEN = 32
CMP_STRIDE = 16
SEL_BLK = 64
SEL_TOPK = 16
WINDOW = 512
Q_BLK = 128
FORCE_BONUS = 1e4
NSA_Q_W = NSA_HEADS * HEAD_DIM
NSA_KV_W = NSA_KV_HEADS * HEAD_DIM
ML_HEADS = 8
ML_DQK = D_MODEL // (2 * ML_HEADS)
ML_DV = D_MODEL // ML_HEADS
ML_CHUNK = 64
ML_QK_W = ML_HEADS * ML_DQK
ML_V_W = ML_HEADS * ML_DV
D_FF = 4 * D_MODEL
CONV_W = 3
RMS_EPS = 1e-6
MASK_NEG = -1e30
SEL_CHUNK = 512
PAIR_W = 2 * HEAD_DIM


def _cparams(*sem):
    return pltpu.CompilerParams(dimension_semantics=sem, vmem_limit_bytes=VMEM_LIMIT_BYTES)


def _dot(a, b):
    return jnp.dot(a, b, preferred_element_type=F32)


def _dot_nt(a, b):
    return lax.dot_general(a, b, (((1,), (1,)), ((), ())), preferred_element_type=F32)


def _dot_tn(a, b):
    return lax.dot_general(a, b, (((0,), (0,)), ((), ())), preferred_element_type=F32)


def _split3(x):
    hi = x.astype(BF16)
    r1 = x - hi.astype(F32)
    mid = r1.astype(BF16)
    lo = (r1 - mid.astype(F32)).astype(BF16)
    return hi, mid, lo


def _rms(x, g):
    return x * lax.rsqrt(jnp.mean(x * x, axis=-1, keepdims=True) + RMS_EPS) * g


def _gelu_tanh(x):
    return 0.5 * x * (1.0 + jnp.tanh(0.7978845608028654 * (x + 0.044715 * (x * x * x))))


def _full(shape):
    return pl.BlockSpec(shape, lambda *_: (0,) * len(shape))


def _nsa_proj_kernel(x_ref, g_ref, w_ref, q_ref, kva_ref, kvb_ref, win_ref, gate_ref):
    h = _rms(x_ref[...], g_ref[...]).astype(BF16)
    p = _dot(h, w_ref[...])
    kvw = 4 * NSA_KV_W
    q_ref[...] = (p[:, :NSA_Q_W] * HEAD_DIM ** -0.5).astype(BF16)
    kva_ref[...] = p[:, NSA_Q_W:NSA_Q_W + kvw]
    kvb_ref[...] = p[:, NSA_Q_W:NSA_Q_W + 6 * NSA_KV_W].astype(BF16)
    win_ref[...] = p[:, NSA_Q_W + kvw:NSA_Q_W + 6 * NSA_KV_W]
    gate_ref[...] = p[:, NSA_Q_W + 6 * NSA_KV_W:]


def _nsa_project(x, g, w, tm):
    m = x.shape[0]
    n = w.shape[1]
    row = lambda i: (i, 0)
    return pl.pallas_call(
        _nsa_proj_kernel,
        grid=(m // tm,),
        in_specs=[pl.BlockSpec((tm, D_MODEL), row), _full((1, D_MODEL)), _full((D_MODEL, n))],
        out_specs=[pl.BlockSpec((tm, NSA_Q_W), row), pl.BlockSpec((tm, 4 * NSA_KV_W), row),
                   pl.BlockSpec((tm, 6 * NSA_KV_W), row), pl.BlockSpec((tm, 2 * NSA_KV_W), row),
                   pl.BlockSpec((tm, 2 * LANES), row)],
        out_shape=[jax.ShapeDtypeStruct((m, NSA_Q_W), BF16), jax.ShapeDtypeStruct((m, 4 * NSA_KV_W), F32),
                   jax.ShapeDtypeStruct((m, 6 * NSA_KV_W), BF16), jax.ShapeDtypeStruct((m, 2 * NSA_KV_W), F32),
                   jax.ShapeDtypeStruct((m, 2 * LANES), F32)],
        compiler_params=_cparams("parallel"),
        name="nsa_proj",
    )(x, g, w)


def _ml_proj_kernel(x_ref, g_ref, w_ref, qkv_ref, o_ref, gate_ref):
    h = _rms(x_ref[...], g_ref[...]).astype(BF16)
    p = _dot(h, w_ref[...])
    a = 2 * ML_QK_W + ML_V_W
    qkv_ref[:, :ML_QK_W] = p[:, :ML_QK_W].astype(BF16)
    qkv_ref[:, ML_QK_W:2 * ML_QK_W] = (p[:, ML_QK_W:2 * ML_QK_W] * ML_DQK ** -0.5).astype(BF16)
    qkv_ref[:, 2 * ML_QK_W:] = p[:, 2 * ML_QK_W:a].astype(BF16)
    o_ref[...] = p[:, a:a + ML_V_W]
    gate_ref[...] = p[:, a + ML_V_W:]


def _ml_project(x, g, w, tm):
    m = x.shape[0]
    n = w.shape[1]
    a = 2 * ML_QK_W + ML_V_W
    row = lambda i: (i, 0)
    return pl.pallas_call(
        _ml_proj_kernel,
        grid=(m // tm,),
        in_specs=[pl.BlockSpec((tm, D_MODEL), row), _full((1, D_MODEL)), _full((D_MODEL, n))],
        out_specs=[pl.BlockSpec((tm, a), row), pl.BlockSpec((tm, ML_V_W), row), pl.BlockSpec((tm, LANES), row)],
        out_shape=[jax.ShapeDtypeStruct((m, a), BF16), jax.ShapeDtypeStruct((m, ML_V_W), F32),
                   jax.ShapeDtypeStruct((m, LANES), F32)],
        compiler_params=_cparams("parallel"),
        name="ml_proj",
    )(x, g, w)


def _out_proj_kernel(o_ref, x_ref, w_ref, g_ref, xo_ref):
    y = _dot(o_ref[...], w_ref[...])
    xo_ref[...] = x_ref[...] + _rms(y, g_ref[...])


def _out_project(o, x, w, g, tm):
    m = x.shape[0]
    row = lambda i: (i, 0)
    return pl.pallas_call(
        _out_proj_kernel,
        grid=(m // tm,),
        in_specs=[pl.BlockSpec((tm, o.shape[1]), row), pl.BlockSpec((tm, D_MODEL), row),
                  _full(w.shape), _full((1, D_MODEL))],
        out_specs=pl.BlockSpec((tm, D_MODEL), row),
        out_shape=jax.ShapeDtypeStruct((m, D_MODEL), F32),
        compiler_params=_cparams("parallel"),
        name="out_proj",
    )(o, x, w, g)


def _ffn_kernel(x_ref, sta_ref, stb_ref, g2_ref, g3_ref, wa_ref, wb_ref, cwa_ref, cwb_ref, cba_ref, cbb_ref,
                wd_ref, xo_ref, taila_ref, tailb_ref, h_scr, ua_scr, ub_scr, ca_scr, cb_scr, acc_scr,
                *, tm, halo, shift, tiles_per_seq):
    i = pl.program_id(0)
    c = pl.program_id(1)

    @pl.when(c == 0)
    def _():
        h_scr[...] = _rms(x_ref[...], g2_ref[...]).astype(BF16)
        acc_scr[...] = jnp.zeros_like(acc_scr)

    first = (i % tiles_per_seq) == 0

    @pl.when(first)
    def _():
        ua_scr[0:halo, :] = sta_ref[...]
        ub_scr[0:halo, :] = stb_ref[...]

    @pl.when(jnp.logical_not(first))
    def _():
        ua_scr[0:halo, :] = ca_scr[c]
        ub_scr[0:halo, :] = cb_scr[c]

    h = h_scr[...]
    ua_scr[halo:halo + tm, :] = _dot(h, wa_ref[...])
    ub_scr[halo:halo + tm, :] = _dot(h, wb_ref[...])
    ta = ua_scr[tm:tm + halo, :]
    tb = ub_scr[tm:tm + halo, :]
    ca_scr[c] = ta
    cb_scr[c] = tb
    taila_ref[...] = ta
    tailb_ref[...] = tb

    def conv(u_scr, cw_ref, cb_ref):
        cw = cw_ref[...]
        return (cb_ref[...] + cw[2:3, :] * u_scr[halo:halo + tm, :]
                + cw[1:2, :] * u_scr[halo - shift:halo - shift + tm, :]
                + cw[0:1, :] * u_scr[halo - 2 * shift:halo - 2 * shift + tm, :])

    y = _gelu_tanh(conv(ua_scr, cwa_ref, cba_ref)) * conv(ub_scr, cwb_ref, cbb_ref)
    acc_scr[...] += _dot(y.astype(BF16), wd_ref[...])

    @pl.when(c == pl.num_programs(1) - 1)
    def _():
        xo_ref[...] = x_ref[...] + _rms(acc_scr[...], g3_ref[...])


def _conv_ffn(x, state, g2, g3, w_up, conv_w, conv_b, w_down, *, tm, fc, shift, tiles_per_seq):
    m = x.shape[0]
    halo = state.shape[1]
    nfc = D_FF // fc
    n_tiles = m // tm
    row = lambda i, c: (i, 0)
    const = lambda i, c: (0, 0)
    kern = functools.partial(_ffn_kernel, tm=tm, halo=halo, shift=shift, tiles_per_seq=tiles_per_seq)
    return pl.pallas_call(
        kern,
        grid=(n_tiles, nfc),
        in_specs=[
            pl.BlockSpec((tm, D_MODEL), row),
            pl.BlockSpec((None, halo, fc), lambda i, c: (i // tiles_per_seq, 0, c)),
            pl.BlockSpec((None, halo, fc), lambda i, c: (i // tiles_per_seq, 0, nfc + c)),
            pl.BlockSpec((1, D_MODEL), const), pl.BlockSpec((1, D_MODEL), const),
            pl.BlockSpec((D_MODEL, fc), lambda i, c: (0, c)),
            pl.BlockSpec((D_MODEL, fc), lambda i, c: (0, nfc + c)),
            pl.BlockSpec((CONV_W, fc), lambda i, c: (0, c)),
            pl.BlockSpec((CONV_W, fc), lambda i, c: (0, nfc + c)),
            pl.BlockSpec((1, fc), lambda i, c: (0, c)),
            pl.BlockSpec((1, fc), lambda i, c: (0, nfc + c)),
            pl.BlockSpec((fc, D_MODEL), lambda i, c: (c, 0)),
        ],
        out_specs=[pl.BlockSpec((tm, D_MODEL), row),
                   pl.BlockSpec((None, halo, fc), lambda i, c: (i, 0, c)),
                   pl.BlockSpec((None, halo, fc), lambda i, c: (i, 0, c))],
        out_shape=[jax.ShapeDtypeStruct((m, D_MODEL), F32),
                   jax.ShapeDtypeStruct((n_tiles, halo, D_FF), F32),
                   jax.ShapeDtypeStruct((n_tiles, halo, D_FF), F32)],
        scratch_shapes=[pltpu.VMEM((tm, D_MODEL), BF16),
                        pltpu.VMEM((halo + tm, fc), F32), pltpu.VMEM((halo + tm, fc), F32),
                        pltpu.VMEM((nfc, halo, fc), F32), pltpu.VMEM((nfc, halo, fc), F32),
                        pltpu.VMEM((tm, D_MODEL), F32)],
        compiler_params=_cparams("arbitrary", "arbitrary"),
        name="conv_ffn",
    )(x, state, state, g2, g3, w_up, w_up, conv_w, conv_w, conv_b, conv_b, w_down)


def _log_sigmoid(x):
    return jnp.minimum(x, 0.0) - jnp.log1p(jnp.exp(-jnp.abs(x)))


def _exact_nt(sel_bf16, x):
    hi, mid, lo = _split3(x)
    return _dot_nt(sel_bf16, hi) + _dot_nt(sel_bf16, mid) + _dot_nt(sel_bf16, lo)


def _eye(n, m, dtype):
    return (lax.broadcasted_iota(jnp.int32, (n, m), 0) == lax.broadcasted_iota(jnp.int32, (n, m), 1)).astype(dtype)


def _mlstm_kernel(qkv_ref, o_ref, gate_ref, bg_ref, hn_ref, c0_ref, n0_ref, m0_ref,
                  y_ref, cf_ref, nf_ref, mf_ref, ct_scr, n_scr, m_scr, *, lp, lv):
    ci = pl.program_id(1)
    eye_qk = _eye(ML_DQK, ML_DQK, BF16)

    @pl.when(ci == 0)
    def _():
        for h in range(ML_HEADS):
            ct_scr[h] = _exact_nt(eye_qk, c0_ref[h])
        n_scr[...] = n0_ref[...]
        m_scr[...] = m0_ref[...]

    gp = gate_ref[...] + bg_ref[...]
    ls = _log_sigmoid(gp)
    r_i = lax.broadcasted_iota(jnp.int32, (lp, lp), 0)
    c_i = lax.broadcasted_iota(jnp.int32, (lp, lp), 1)
    causal = c_i <= r_i
    tril = causal.astype(BF16)
    hi, mid, lo = _split3(ls)
    b_all = _dot(tril, hi) + _dot(tril, mid) + _dot(tril, lo)
    sel16 = _eye(2 * ML_HEADS, LANES, BF16)
    gp_t = _exact_nt(sel16, gp)
    b_t = _exact_nt(sel16, b_all)
    row_valid = lax.broadcasted_iota(jnp.int32, (lp, 1), 0) < lv

    for h in range(ML_HEADS):
        q = qkv_ref[:, h * ML_DQK:(h + 1) * ML_DQK]
        k = qkv_ref[:, ML_QK_W + h * ML_DQK:ML_QK_W + (h + 1) * ML_DQK]
        v = qkv_ref[:, 2 * ML_QK_W + h * ML_DV:2 * ML_QK_W + (h + 1) * ML_DV]
        ct = ct_scr[h]
        n_row = n_scr[h]
        m = m_scr[h]
        b_col = b_all[:, ML_HEADS + h:ML_HEADS + h + 1]
        i_col = gp[:, h:h + 1]
        b_row = b_t[ML_HEADS + h:ML_HEADS + h + 1, :]
        i_row = gp_t[h:h + 1, :]
        dmat = jnp.where(causal, b_col - b_row + i_row, MASK_NEG)
        inter = b_col + m
        mt = jnp.maximum(inter, jnp.max(dmat, axis=-1, keepdims=True))
        s = _dot_nt(q, k) * jnp.exp(dmat - mt)
        wi = jnp.exp(inter - mt)
        qf = q.astype(F32)
        num = wi * _dot(q, ct.astype(BF16)) + _dot(s.astype(BF16), v)
        den = wi * jnp.sum(qf * n_row, axis=-1, keepdims=True) + jnp.sum(s, axis=-1, keepdims=True)
        hc = num / jnp.maximum(jnp.abs(den), jnp.exp(-mt))
        m_last = mt[lv - 1:lv, :]
        b_last = b_col[lv - 1:lv, :]
        decay = jnp.exp(b_last + m - m_last)
        ws = jnp.where(row_valid, jnp.exp(b_last - b_col + i_col - m_last), 0.0)
        ct_scr[h] = decay * ct + _dot_tn(k, (ws * v.astype(F32)).astype(BF16))
        n_scr[h] = decay * n_row + jnp.sum(ws * k.astype(F32), axis=0, keepdims=True)
        m_scr[h] = m_last
        hn = hc * lax.rsqrt(jnp.mean(hc * hc, axis=-1, keepdims=True) + RMS_EPS)
        sl = slice(h * ML_DV, (h + 1) * ML_DV)
        y_ref[:, sl] = (hn * hn_ref[:, sl] * jax.nn.sigmoid(o_ref[:, sl])).astype(BF16)

    @pl.when(ci == pl.num_programs(1) - 1)
    def _():
        eye_v = _eye(ML_DV, ML_DV, BF16)
        for h in range(ML_HEADS):
            cf_ref[h] = _exact_nt(eye_v, ct_scr[h])
        nf_ref[...] = n_scr[...]
        mf_ref[...] = m_scr[...]


def _mlstm(qkv, o, gates, b_gate, head_norm, c0, n0, m0, *, nb, lp, lv):
    m = qkv.shape[0]
    nchunk = m // (nb * lp)
    row = lambda b, c: (b * nchunk + c, 0)
    st4 = lambda b, c: (b, 0, 0, 0)
    kern = functools.partial(_mlstm_kernel, lp=lp, lv=lv)
    return pl.pallas_call(
        kern,
        grid=(nb, nchunk),
        in_specs=[pl.BlockSpec((lp, qkv.shape[1]), row), pl.BlockSpec((lp, ML_V_W), row),
                  pl.BlockSpec((lp, LANES), row), pl.BlockSpec((1, LANES), lambda b, c: (0, 0)),
                  pl.BlockSpec((1, ML_V_W), lambda b, c: (0, 0)),
                  pl.BlockSpec((None, ML_HEADS, ML_DV, ML_DQK), st4),
                  pl.BlockSpec((None, ML_HEADS, 1, ML_DQK), st4),
                  pl.BlockSpec((None, ML_HEADS, 1, 1), st4)],
        out_specs=[pl.BlockSpec((lp, ML_V_W), row),
                   pl.BlockSpec((None, ML_HEADS, ML_DV, ML_DQK), st4),
                   pl.BlockSpec((None, ML_HEADS, 1, ML_DQK), st4),
                   pl.BlockSpec((None, ML_HEADS, 1, 1), st4)],
        out_shape=[jax.ShapeDtypeStruct((m, ML_V_W), BF16),
                   jax.ShapeDtypeStruct((nb, ML_HEADS, ML_DV, ML_DQK), F32),
                   jax.ShapeDtypeStruct((nb, ML_HEADS, 1, ML_DQK), F32),
                   jax.ShapeDtypeStruct((nb, ML_HEADS, 1, 1), F32)],
        scratch_shapes=[pltpu.VMEM((ML_HEADS, ML_DQK, ML_DV), F32),
                        pltpu.VMEM((ML_HEADS, 1, ML_DQK), F32),
                        pltpu.VMEM((ML_HEADS, 1, 1), F32)],
        compiler_params=_cparams("arbitrary", "arbitrary"),
        name="mlstm",
    )(qkv, o, gates, b_gate, head_norm, c0, n0, m0)


def _cmp_parts(load_rows, wbd_ref, c, p):
    acc = None
    for l in range(CMP_STRIDE):
        part = _dot(load_rows(l, c, p).astype(BF16), wbd_ref[c, l])
        acc = part if acc is None else acc + part
    return acc


def _cmp_pos_bias(pos_ref, wbd_ref, c):
    acc = None
    for l in range(CMP_STRIDE):
        part = _dot(pos_ref[c, l].astype(BF16), wbd_ref[c, l])
        acc = part if acc is None else acc + part
    return acc[0:1, :PAIR_W] + acc[1:2, PAIR_W:]


def _cmp_finish(parts, posb, w2_ref, c):
    n = parts.shape[0]
    pre = parts[:, :PAIR_W] + pltpu.roll(parts[:, PAIR_W:], n - 1, 0) + posb
    return _dot(_gelu_tanh(pre).astype(BF16), w2_ref[c])


def _nsa_compress_kernel(x00_ref, x01_ref, x10_ref, x11_ref, wbd_ref, w2_ref, pos_ref, kc_ref, vc_ref, *, nchunk):
    x_refs = ((x00_ref, x01_ref), (x10_ref, x11_ref))

    def load_rows(l, c, p):
        return x_refs[c][p][pl.ds(l, nchunk, stride=CMP_STRIDE), :]

    for c, out_ref in ((0, kc_ref), (1, vc_ref)):
        posb = _cmp_pos_bias(pos_ref, wbd_ref, c)
        for p in range(2):
            out_ref[p] = _cmp_finish(_cmp_parts(load_rows, wbd_ref, c, p), posb, w2_ref, c).astype(BF16)


def _nsa_compress(kva, wbd, w2bd, pos, *, nb, t):
    nchunk = t // CMP_STRIDE
    kern = functools.partial(_nsa_compress_kernel, nchunk=nchunk)
    out = jax.ShapeDtypeStruct((nb, 2, nchunk, PAIR_W), BF16)
    ospec = pl.BlockSpec((None, 2, nchunk, PAIR_W), lambda b: (b, 0, 0, 0))
    return pl.pallas_call(
        kern,
        grid=(nb,),
        in_specs=[pl.BlockSpec((t, PAIR_W), functools.partial(lambda j, b: (b, j), j)) for j in range(4)]
        + [_full(wbd.shape), _full(w2bd.shape), _full(pos.shape)],
        out_specs=[ospec, ospec],
        out_shape=[out, out],
        compiler_params=_cparams("parallel"),
        name="nsa_compress",
    )(kva, kva, kva, kva, wbd, w2bd, pos)


def _select_mask_t(imp_t, s0, score_scr, ns):
    qb = imp_t.shape[1]
    bid = lax.broadcasted_iota(jnp.int32, (ns, qb), 0)
    tq = s0 + lax.broadcasted_iota(jnp.int32, (ns, qb), 1)
    cur = tq // SEL_BLK
    valid = bid * SEL_BLK <= tq
    forced = (bid == 0) | (bid == cur) | (bid == cur - 1)
    score = jnp.where(valid, imp_t + jnp.where(forced, FORCE_BONUS, 0.0), MASK_NEG)
    score_scr[0:ns, :] = score

    def body(m, cnt):
        sm = score_scr[pl.ds(m, 1), :]
        ahead = (sm > score) | ((sm == score) & (m < bid))
        return cnt + ahead.astype(jnp.int32)

    cnt = lax.fori_loop(0, ns, body, jnp.zeros((ns, qb), jnp.int32), unroll=8)
    return cnt < min(SEL_TOPK, ns)


def _nsa_attn_kernel(slope_ref, q_ref, gate_ref, kc_ref, vc_ref, ksel_ref, vsel_ref, kwin_ref, vwin_ref,
                     e3_ref, o_ref, score_scr, *, t):
    pr = pl.program_id(1)
    nblk = pl.program_id(2)
    qb = Q_BLK
    s0 = nblk * qb
    ncp = t // CMP_STRIDE
    ns = t // SEL_BLK
    wk = WINDOW + qb
    hd = HEAD_DIM
    sg = jax.nn.sigmoid(gate_ref[...])

    tq_col = s0 + lax.broadcasted_iota(jnp.int32, (qb, 1), 0)
    cend = CMP_STRIDE * lax.broadcasted_iota(jnp.int32, (1, ncp), 1) + (CMP_LEN - 1)
    dist_c = tq_col - cend
    mask_c = dist_c >= 0
    distf_c = dist_c.astype(F32)
    jj = lax.broadcasted_iota(jnp.int32, (ns, ncp), 1) * CMP_STRIDE
    nn = lax.broadcasted_iota(jnp.int32, (ns, ncp), 0) * SEL_BLK
    ov_t = ((jj < nn + SEL_BLK) & (jj + CMP_LEN > nn)).astype(BF16)
    wstart = pl.multiple_of(jnp.maximum(s0 - WINDOW, 0), qb)
    dist_w = tq_col - (wstart + lax.broadcasted_iota(jnp.int32, (1, wk), 1))
    mask_w = (dist_w >= 0) & (dist_w < WINDOW)
    distf_w = dist_w.astype(F32)
    n_sel_chunks = (s0 + qb + SEL_CHUNK - 1) // SEL_CHUNK

    for half in range(2):
        lanes = slice(half * hd, (half + 1) * hd)
        qs = [q_ref[:, (half * NSA_HPG + hh) * hd:(half * NSA_HPG + hh + 1) * hd] for hh in range(NSA_HPG)]
        q_all = jnp.concatenate(qs, axis=0)
        slopes = [slope_ref[(pr * 2 + half) * NSA_HPG + hh] for hh in range(NSA_HPG)]

        s_c = _dot_nt(q_all, kc_ref[:, lanes])
        vc = vc_ref[...]
        pc_sum = jnp.zeros((qb, ncp), F32)
        o_cmp = []
        for hh in range(NSA_HPG):
            s = jnp.where(mask_c, s_c[hh * qb:(hh + 1) * qb] - slopes[hh] * distf_c, MASK_NEG)
            e = jnp.where(mask_c, jnp.exp(s - jnp.max(s, axis=-1, keepdims=True)), 0.0)
            pc = e / jnp.maximum(jnp.sum(e, axis=-1, keepdims=True), 1e-30)
            pc_sum = pc_sum + pc
            o_cmp.append(_dot(pc.astype(BF16), vc)[:, lanes])

        hi = pc_sum.astype(BF16)
        lo = (pc_sum - hi.astype(F32)).astype(BF16)
        imp_t = _dot_nt(ov_t, hi) + _dot_nt(ov_t, lo)
        sel_t = _select_mask_t(imp_t, s0, score_scr, ns)
        mneg = jnp.where(sel_t, 0.0, MASK_NEG).T.astype(BF16)

        def sel_chunk(kci, carry):
            m_run, l_run, acc = carry
            r0 = pl.multiple_of(kci * SEL_CHUNK, SEL_CHUNK)
            s_all = _dot_nt(q_all, ksel_ref[pl.ds(r0, SEL_CHUNK), lanes])
            kpos = r0 + lax.broadcasted_iota(jnp.int32, (1, SEL_CHUNK), 1)
            add = jnp.where(kpos > tq_col, MASK_NEG, _dot(mneg, e3_ref[kci]))
            prel = (kpos - s0).astype(F32)
            s = jnp.concatenate([s_all[hh * qb:(hh + 1) * qb] + (add + slopes[hh] * prel)
                                 for hh in range(NSA_HPG)], axis=0)
            m_new = jnp.maximum(m_run, jnp.max(s, axis=-1, keepdims=True))
            alpha = jnp.exp(m_run - m_new)
            p = jnp.exp(s - m_new)
            l_new = alpha * l_run + jnp.sum(p, axis=-1, keepdims=True)
            acc_new = alpha * acc + _dot(p.astype(BF16), vsel_ref[pl.ds(r0, SEL_CHUNK), :])
            return m_new, l_new, acc_new

        init = (jnp.full((NSA_HPG * qb, 1), MASK_NEG, F32), jnp.zeros((NSA_HPG * qb, 1), F32),
                jnp.zeros((NSA_HPG * qb, PAIR_W), F32))
        _, l_sel, acc_sel = lax.fori_loop(0, n_sel_chunks, sel_chunk, init)
        o_sel_all = acc_sel[:, lanes] / l_sel

        s_w = _dot_nt(q_all, kwin_ref[pl.ds(wstart, wk), lanes])
        vw = vwin_ref[pl.ds(wstart, wk), :]
        for hh in range(NSA_HPG):
            s = jnp.where(mask_w, s_w[hh * qb:(hh + 1) * qb] - slopes[hh] * distf_w, MASK_NEG)
            e = jnp.exp(s - jnp.max(s, axis=-1, keepdims=True))
            o_w = _dot(e.astype(BF16), vw)[:, lanes] / jnp.sum(e, axis=-1, keepdims=True)
            gi = (half * NSA_HPG + hh) * 3
            out = (sg[:, gi:gi + 1] * o_cmp[hh] + sg[:, gi + 1:gi + 2] * o_sel_all[hh * qb:(hh + 1) * qb]
                   + sg[:, gi + 2:gi + 3] * o_w)
            oc = (half * NSA_HPG + hh) * hd
            o_ref[:, oc:oc + hd] = out.astype(BF16)


def _nsa_attention(slopes, q, gates, kc, vc, kvb, e3, *, nb, t):
    nq = t // Q_BLK
    ncp = t // CMP_STRIDE
    pw = 2 * NSA_HPG * HEAD_DIM
    kvcol = lambda c: (lambda b, p, n: (b, 2 * c + p))
    kern = functools.partial(_nsa_attn_kernel, t=t)
    return pl.pallas_call(
        kern,
        grid=(nb, 2, nq),
        in_specs=[pl.BlockSpec(memory_space=pltpu.SMEM),
                  pl.BlockSpec((Q_BLK, pw), lambda b, p, n: (b * nq + n, p)),
                  pl.BlockSpec((Q_BLK, LANES), lambda b, p, n: (b * nq + n, p)),
                  pl.BlockSpec((None, None, ncp, PAIR_W), lambda b, p, n: (b, p, 0, 0)),
                  pl.BlockSpec((None, None, ncp, PAIR_W), lambda b, p, n: (b, p, 0, 0)),
                  pl.BlockSpec((t, PAIR_W), kvcol(2)), pl.BlockSpec((t, PAIR_W), kvcol(3)),
                  pl.BlockSpec((t, PAIR_W), kvcol(4)), pl.BlockSpec((t, PAIR_W), kvcol(5)),
                  _full(e3.shape)],
        out_specs=pl.BlockSpec((Q_BLK, pw), lambda b, p, n: (b * nq + n, p)),
        out_shape=jax.ShapeDtypeStruct((nb * t, NSA_Q_W), BF16),
        scratch_shapes=[pltpu.VMEM((max(t // SEL_BLK, SUBLANES), Q_BLK), F32)],
        compiler_params=_cparams("parallel", "parallel", "arbitrary"),
        name="nsa_attn",
    )(slopes, q, gates, kc, vc, kvb, kvb, kvb, kvb, e3)


PAGES_PER_STEP = 32


def _page_compress_kernel(pt_ref, cache_ref, wbd_ref, w2_ref, pos_ref, kvc_ref,
                          buf0, buf1, buf2, buf3, parts_scr, sem, *, n_pages, page):
    b = pl.program_id(0)
    hf = pl.program_id(1)
    bufs = (buf0, buf1, buf2, buf3)
    copies = []
    for i in range(PAGES_PER_STEP):
        pg = pt_ref[b * n_pages + hf * PAGES_PER_STEP + i]
        for cb in range(4):
            cp = pltpu.make_async_copy(cache_ref.at[pg, :, cb, :], bufs[cb].at[pl.ds(i * page, page), :], sem)
            cp.start()
            copies.append(cp)
    for cp in copies:
        cp.wait()

    nch = PAGES_PER_STEP * page // CMP_STRIDE

    def load_rows(l, c, p):
        return bufs[c * 2 + p][pl.ds(l, nch, stride=CMP_STRIDE), :]

    row0 = pl.multiple_of(hf * nch, nch)
    for c in range(2):
        for p in range(2):
            parts_scr[c * 2 + p, pl.ds(row0, nch), :] = _cmp_parts(load_rows, wbd_ref, c, p)

    @pl.when(hf == pl.num_programs(1) - 1)
    def _():
        for c in range(2):
            posb = _cmp_pos_bias(pos_ref, wbd_ref, c)
            for p in range(2):
                kvc_ref[c, p] = _cmp_finish(parts_scr[c * 2 + p], posb, w2_ref, c).astype(BF16)


def _page_compress(page_table, cache4, wbd, w2bd, pos):
    sb, n_pages = page_table.shape
    page = cache4.shape[1]
    ncp = n_pages * page // CMP_STRIDE
    rows = PAGES_PER_STEP * page
    kern = functools.partial(_page_compress_kernel, n_pages=n_pages, page=page)
    grid_spec = pltpu.PrefetchScalarGridSpec(
        num_scalar_prefetch=1,
        grid=(sb, n_pages // PAGES_PER_STEP),
        in_specs=[pl.BlockSpec(memory_space=pl.ANY),
                  pl.BlockSpec(wbd.shape, lambda b, h, pt: (0,) * wbd.ndim),
                  pl.BlockSpec(w2bd.shape, lambda b, h, pt: (0,) * w2bd.ndim),
                  pl.BlockSpec(pos.shape, lambda b, h, pt: (0,) * pos.ndim)],
        out_specs=pl.BlockSpec((None, 2, 2, ncp, PAIR_W), lambda b, h, pt: (b, 0, 0, 0, 0)),
        scratch_shapes=[pltpu.VMEM((rows, PAIR_W), F32)] * 4
        + [pltpu.VMEM((4, ncp, 2 * PAIR_W), F32), pltpu.SemaphoreType.DMA(())],
    )
    return pl.pallas_call(
        kern,
        grid_spec=grid_spec,
        out_shape=jax.ShapeDtypeStruct((sb, 2, 2, ncp, PAIR_W), BF16),
        compiler_params=_cparams("arbitrary", "arbitrary"),
        name="nsa_page_compress",
    )(page_table.reshape(-1), cache4, wbd, w2bd, pos)


def _row_slopes(slope_ref, g, rows):
    hh = lax.broadcasted_iota(jnp.int32, (rows, 1), 0) % NSA_HPG
    col = jnp.zeros((rows, 1), F32)
    for h in range(NSA_HPG):
        col = jnp.where(hh == h, slope_ref[g * NSA_HPG + h], col)
    return col


def _sample_cmp_kernel(slope_ref, q_ref, kvc_ref, ocmp_ref, imp_ref, *, past, nsp):
    rows = q_ref.shape[1]
    ncp = kvc_ref.shape[2]
    t_col = lax.broadcasted_iota(jnp.int32, (rows, 1), 0) // NSA_HPG
    jrow = lax.broadcasted_iota(jnp.int32, (1, ncp), 1)
    dist = (past + t_col) - (CMP_STRIDE * jrow + (CMP_LEN - 1))
    mask = (dist >= 0) & (jrow < ncp - 1)
    distf = dist.astype(F32)
    ri = lax.broadcasted_iota(jnp.int32, (rows, rows), 0) // NSA_HPG
    ci = lax.broadcasted_iota(jnp.int32, (rows, rows), 1) // NSA_HPG
    same_t = (ri == ci).astype(BF16)
    jj = lax.broadcasted_iota(jnp.int32, (ncp, nsp), 0) * CMP_STRIDE
    nn = lax.broadcasted_iota(jnp.int32, (ncp, nsp), 1) * SEL_BLK
    ov = ((jj < nn + SEL_BLK) & (jj + CMP_LEN > nn)).astype(BF16)
    for g in range(NSA_KV_HEADS):
        pr, half = divmod(g, 2)
        lanes = slice(half * HEAD_DIM, (half + 1) * HEAD_DIM)
        s_c = _dot_nt(q_ref[g], kvc_ref[0, pr, :, lanes])
        s = jnp.where(mask, s_c - _row_slopes(slope_ref, g, rows) * distf, MASK_NEG)
        e = jnp.where(mask, jnp.exp(s - jnp.max(s, axis=-1, keepdims=True)), 0.0)
        pc = e / jnp.maximum(jnp.sum(e, axis=-1, keepdims=True), 1e-30)
        ocmp_ref[g] = _dot(pc.astype(BF16), kvc_ref[1, pr])[:, lanes]
        hi = pc.astype(BF16)
        lo = (pc - hi.astype(F32)).astype(BF16)
        pcs = _dot(same_t, hi) + _dot(same_t, lo)
        hi = pcs.astype(BF16)
        lo = (pcs - hi.astype(F32)).astype(BF16)
        imp_ref[g] = _dot(hi, ov) + _dot(lo, ov)


def _sample_cmp(slopes, q_rows, kvc, *, past, nsp):
    sb, ng, rows, hd = q_rows.shape
    ncp = kvc.shape[3]
    kern = functools.partial(_sample_cmp_kernel, past=past, nsp=nsp)
    return pl.pallas_call(
        kern,
        grid=(sb,),
        in_specs=[pl.BlockSpec(memory_space=pltpu.SMEM),
                  pl.BlockSpec((None, ng, rows, hd), lambda b: (b, 0, 0, 0)),
                  pl.BlockSpec((None, 2, 2, ncp, PAIR_W), lambda b: (b, 0, 0, 0, 0))],
        out_specs=[pl.BlockSpec((None, ng, rows, hd), lambda b: (b, 0, 0, 0)),
                   pl.BlockSpec((None, ng, rows, nsp), lambda b: (b, 0, 0, 0))],
        out_shape=[jax.ShapeDtypeStruct((sb, ng, rows, hd), F32),
                   jax.ShapeDtypeStruct((sb, ng, rows, nsp), F32)],
        compiler_params=_cparams("parallel"),
        name="nsa_sample_cmp",
    )(slopes, q_rows, kvc)


def _sample_topk_kernel(imp_ref, idx_ref, score_scr, *, past, n_sel, st):
    nsr, ncol = imp_ref.shape
    bid = lax.broadcasted_iota(jnp.int32, (nsr, ncol), 0)
    tq = past + lax.broadcasted_iota(jnp.int32, (nsr, ncol), 1) % st
    cur = tq // SEL_BLK
    valid = (bid * SEL_BLK <= tq) & (bid < n_sel)
    forced = (bid == 0) | (bid == cur) | (bid == cur - 1)
    score = jnp.where(valid, imp_ref[...] + jnp.where(forced, FORCE_BONUS, 0.0), MASK_NEG)
    score_scr[...] = score

    def body(m, cnt):
        sm = score_scr[pl.ds(m, 1), :]
        ahead = (sm > score) | ((sm == score) & (m < bid))
        return cnt + ahead.astype(jnp.int32)

    cnt = lax.fori_loop(0, n_sel, body, jnp.zeros((nsr, ncol), jnp.int32))
    sel = (cnt < SEL_TOPK) & (bid < n_sel)
    tril = (lax.broadcasted_iota(jnp.int32, (nsr, nsr), 1) <= lax.broadcasted_iota(jnp.int32, (nsr, nsr), 0))
    prefix = _dot(tril.astype(BF16), sel.astype(BF16))
    for k in range(SEL_TOPK):
        hit = sel & (prefix == float(k + 1))
        idx_ref[k:k + 1, :] = jnp.sum(jnp.where(hit, bid, 0), axis=0, keepdims=True)


def _sample_topk(imp_t, *, past, n_sel, st):
    nsr, ncol = imp_t.shape
    kern = functools.partial(_sample_topk_kernel, past=past, n_sel=n_sel, st=st)
    return pl.pallas_call(
        kern,
        grid=(1,),
        in_specs=[_full(imp_t.shape)],
        out_specs=_full((SEL_TOPK, ncol)),
        out_shape=jax.ShapeDtypeStruct((SEL_TOPK, ncol), jnp.int32),
        scratch_shapes=[pltpu.VMEM((nsr, ncol), F32)],
        compiler_params=_cparams("arbitrary"),
        name="nsa_sample_topk",
    )(imp_t)


def _joint_softmax_pv(s_a, v_a_fn, s_b, v_b):
    m = jnp.maximum(jnp.max(s_a, axis=-1, keepdims=True), jnp.max(s_b, axis=-1, keepdims=True))
    p_a = jnp.where(s_a > 0.5 * MASK_NEG, jnp.exp(s_a - m), 0.0)
    p_b = jnp.where(s_b > 0.5 * MASK_NEG, jnp.exp(s_b - m), 0.0)
    l = jnp.sum(p_a, axis=-1, keepdims=True) + jnp.sum(p_b, axis=-1, keepdims=True)
    acc = v_a_fn(p_a.astype(BF16)) + _dot(p_b.astype(BF16), v_b)
    return acc / jnp.maximum(l, 1e-30)


def _sample_attn_kernel(idx_ref, pt_ref, slope_ref, cache_ref, q_ref, idxv_ref, e16_ref, gate_ref, ocmp_ref,
                        ksn_ref, vsn_ref, kwn_ref, vwn_ref, kwin_ref, vwin_ref, o_ref, kbuf, vbuf, sem,
                        *, past, n_pages, st):
    b = pl.program_id(0)
    pr = pl.program_id(1)
    rows = q_ref.shape[1]
    nkey = SEL_TOPK * SEL_BLK
    last_blk = past // SEL_BLK - 1
    copies = []
    for half in range(2):
        g = pr * 2 + half
        for t in range(st):
            for k in range(SEL_TOPK):
                blk = jnp.minimum(idx_ref[((b * NSA_KV_HEADS + g) * st + t) * SEL_TOPK + k], last_blk)
                pg = pt_ref[b * n_pages + lax.shift_right_logical(blk, 1)]
                r0 = pl.multiple_of((blk & 1) * SEL_BLK, SEL_BLK)
                dst = pl.ds(k * SEL_BLK, SEL_BLK)
                for cb, buf in ((4, kbuf), (6, vbuf)):
                    cp = pltpu.make_async_copy(cache_ref.at[pg, pl.ds(r0, SEL_BLK), cb + pr, :],
                                               buf.at[half, t, dst, :], sem)
                    cp.start()
                    copies.append(cp)
    for cp in copies:
        cp.wait()

    t_col = lax.broadcasted_iota(jnp.int32, (rows, 1), 0) // NSA_HPG
    tqf = (past + t_col).astype(F32)
    off = (lax.broadcasted_iota(jnp.int32, (1, nkey), 1) % SEL_BLK).astype(F32)
    tp = lax.broadcasted_iota(jnp.int32, (1, ksn_ref.shape[0]), 1)
    dn = t_col - tp
    valid_n = (dn >= 0) & (tp < st)
    dnf = dn.astype(F32)
    wb = kwin_ref.shape[0]
    dist_w = (wb + t_col) - lax.broadcasted_iota(jnp.int32, (1, wb), 1)
    valid_w = (dist_w >= 0) & (dist_w < WINDOW)

    for half in range(2):
        g = pr * 2 + half
        lanes = slice(half * HEAD_DIM, (half + 1) * HEAD_DIM)
        q = q_ref[half]
        slope = _row_slopes(slope_ref, g, rows)

        blk = _dot(idxv_ref[half], e16_ref[...])
        is_new = blk > (last_blk + 0.5)
        dist = tqf - (blk * SEL_BLK + off)
        valid = jnp.logical_not(is_new) & (dist >= 0)
        s_sel = jnp.full((rows, nkey), MASK_NEG, F32)
        for t in range(st):
            s_t = _dot_nt(q, kbuf[half, t, :, lanes].astype(BF16))
            s_sel = jnp.where(t_col == t, s_t, s_sel)
        s_sel = jnp.where(valid, s_sel - slope * dist, MASK_NEG)
        has_new = jnp.max(is_new.astype(F32), axis=-1, keepdims=True) > 0.5
        s_new = jnp.where(valid_n & has_new, _dot_nt(q, ksn_ref[:, lanes]) - slope * dnf, MASK_NEG)

        def pv_sel(p):
            acc = jnp.zeros((rows, PAIR_W), F32)
            for t in range(st):
                acc = jnp.where(t_col == t, _dot(p, vbuf[half, t].astype(BF16)), acc)
            return acc

        o_sel = _joint_softmax_pv(s_sel, pv_sel, s_new, vsn_ref[...])[:, lanes]

        s_w = _dot_nt(q, kwin_ref[:, lanes].astype(BF16))
        s_w = jnp.where(valid_w, s_w - slope * dist_w.astype(F32), MASK_NEG)
        s_wn = jnp.where(valid_n, _dot_nt(q, kwn_ref[:, lanes]) - slope * dnf, MASK_NEG)
        vw = vwin_ref[...].astype(BF16)
        o_w = _joint_softmax_pv(s_w, lambda p: _dot(p, vw), s_wn, vwn_ref[...])[:, lanes]

        sg = jax.nn.sigmoid(gate_ref[half])
        o_ref[half] = sg[:, 0:1] * ocmp_ref[half] + sg[:, 1:2] * o_sel + sg[:, 2:3] * o_w


def _sample_attention(idx_flat, pt_flat, slopes, cache4, q_rows, idx_rows, e16, gate_rows, ocmp, new16, win_buf,
                      *, past, n_pages, st):
    sb, ng, rows, hd = q_rows.shape
    wb = win_buf.shape[1]
    nkey = SEL_TOPK * SEL_BLK
    pairg = lambda w: pl.BlockSpec((None, 2, rows, w), lambda b, p, *_: (b, p, 0, 0))
    newc = lambda c: pl.BlockSpec((None, new16.shape[1], PAIR_W), lambda b, p, *_: (b, 0, 2 * c + p))
    kern = functools.partial(_sample_attn_kernel, past=past, n_pages=n_pages, st=st)
    grid_spec = pltpu.PrefetchScalarGridSpec(
        num_scalar_prefetch=2,
        grid=(sb, 2),
        in_specs=[pl.BlockSpec(memory_space=pltpu.SMEM), pl.BlockSpec(memory_space=pl.ANY),
                  pairg(hd), pairg(LANES), pl.BlockSpec(e16.shape, lambda b, p, *_: (0, 0)), pairg(LANES), pairg(hd),
                  newc(2), newc(3), newc(4), newc(5),
                  pl.BlockSpec((None, wb, PAIR_W), lambda b, p, *_: (b, 0, p)),
                  pl.BlockSpec((None, wb, PAIR_W), lambda b, p, *_: (b, 0, 2 + p))],
        out_specs=pairg(hd),
        scratch_shapes=[pltpu.VMEM((2, st, nkey, PAIR_W), F32), pltpu.VMEM((2, st, nkey, PAIR_W), F32),
                        pltpu.SemaphoreType.DMA(())],
    )
    return pl.pallas_call(
        kern,
        grid_spec=grid_spec,
        out_shape=jax.ShapeDtypeStruct((sb, ng, rows, hd), F32),
        compiler_params=_cparams("arbitrary", "arbitrary"),
        name="nsa_sample_attn",
    )(idx_flat, pt_flat, slopes, cache4, q_rows, idx_rows, e16, gate_rows, ocmp, new16, new16, new16, new16,
      win_buf, win_buf)


def _pad_cols(w, mult=LANES):
    n = w.shape[1]
    return jnp.pad(w, ((0, 0), (0, -n % mult)))


def _prep_nsa_in(w_in):
    body = w_in[:, :NSA_Q_W + 6 * NSA_KV_W]
    gates = w_in[:, NSA_Q_W + 6 * NSA_KV_W:]
    per_pair = 2 * NSA_HPG * 3
    blocks = [_pad_cols(gates[:, p * per_pair:(p + 1) * per_pair]) for p in range(2)]
    return jnp.concatenate([body] + blocks, axis=1).astype(BF16)


def _prep_cmp(w_cmp1, w_cmp2, cmp_pos):
    hd = HEAD_DIM
    z = jnp.zeros((2, CMP_STRIDE, hd, hd), F32)
    wa, wb = w_cmp1[:, :CMP_STRIDE], w_cmp1[:, CMP_STRIDE:]
    top = jnp.concatenate([wa, z, wb, z], axis=-1)
    bot = jnp.concatenate([z, wa, z, wb], axis=-1)
    wbd = jnp.concatenate([top, bot], axis=-2).astype(BF16)
    z2 = jnp.zeros((2, hd, hd), F32)
    w2bd = jnp.concatenate([jnp.concatenate([w_cmp2, z2], -1), jnp.concatenate([z2, w_cmp2], -1)], -2).astype(BF16)
    pos = cmp_pos.transpose(1, 0, 2)
    pos = jnp.concatenate([pos, pos], axis=-1)
    rows = jnp.stack([pos[:, :CMP_STRIDE], pos[:, CMP_STRIDE:]], axis=2)
    pos_tiles = jnp.pad(rows, ((0, 0), (0, 0), (0, SUBLANES - 2), (0, 0)))
    return wbd, w2bd, pos_tiles


def _alibi_slopes():
    h = jnp.arange(1, NSA_HEADS + 1, dtype=F32)
    return jnp.exp2(-8.0 * h / NSA_HEADS)


def _sel_expand(t):
    key_blk = jnp.arange(t, dtype=jnp.int32).reshape(t // SEL_CHUNK, 1, SEL_CHUNK) // SEL_BLK
    blk = jnp.arange(t // SEL_BLK, dtype=jnp.int32).reshape(1, t // SEL_BLK, 1)
    return (key_blk == blk).astype(BF16)


def _prompt_rows_tile(m):
    for tm in (512, 256, 128, 64, 32, 16, 8):
        if m % tm == 0:
            return tm
    raise ValueError(m)


def kernel(x_prompt, x_sample, cache_nsa_kv, cache_nsa_win, state_mlstm_C, state_mlstm_n, state_mlstm_m,
           state_ffn_conv, page_table, norm_g, w_nsa_in, w_nsa_out, w_cmp1, w_cmp2, cmp_pos, w_ml_in,
           b_ml_gate, ml_head_norm, w_ml_out, w_ffn_up, ffn_conv_w, ffn_conv_b, w_ffn_down):
    nb, t, d = x_prompt.shape
    sb, st, _ = x_sample.shape
    mp = nb * t
    tm = _prompt_rows_tile(mp)
    slopes = _alibi_slopes()

    w_in0 = _prep_nsa_in(w_nsa_in[0])
    w_out0 = w_nsa_out[0].astype(BF16)
    wbd, w2bd, pos_tiles = _prep_cmp(w_cmp1[0], w_cmp2[0], cmp_pos[0])
    w_ml = _pad_cols(w_ml_in[0]).astype(BF16)
    w_mlo = w_ml_out[0].astype(BF16)
    w_up = w_ffn_up.astype(BF16)
    w_dn = w_ffn_down.astype(BF16)
    bg_row = _pad_cols(b_ml_gate[0].reshape(1, 2 * ML_HEADS))
    g = norm_g[:, :, None, :]

    xp = x_prompt.reshape(mp, d)
    q, kva, kvb, win, gates = _nsa_project(xp, g[0, 0], w_in0, tm)
    kc, vc = _nsa_compress(kva, wbd, w2bd, pos_tiles, nb=nb, t=t)
    o = _nsa_attention(slopes, q, gates, kc, vc, kvb, _sel_expand(t), nb=nb, t=t)
    xp = _out_project(o, xp, w_out0, g[0, 1], tm)
    kv_p = kva.reshape(1, nb, t, 4, NSA_KV_HEADS, HEAD_DIM)
    w_keep = min(WINDOW, t)
    win_p = win.reshape(nb, t, 2, NSA_KV_HEADS, HEAD_DIM)[None, :, t - w_keep:]

    halo = SUBLANES
    zstate = jnp.zeros((nb, halo, 2 * D_FF), F32)
    tmf = min(t, 1024)
    conv_p = []

    def ffn_prompt(xp, i):
        xo, ta, tb = _conv_ffn(xp, zstate, g[i, 2], g[i, 3], w_up[i], ffn_conv_w[i], ffn_conv_b[i][None],
                               w_dn[i], tm=tmf, fc=512, shift=1, tiles_per_seq=t // tmf)
        tail = jnp.concatenate([ta, tb], axis=-1).reshape(nb, t // tmf, halo, 2 * D_FF)
        conv_p.append(tail[:, -1, halo - (CONV_W - 1):])
        return xo

    xp = ffn_prompt(xp, 0)

    lp = math.gcd(t, ML_CHUNK)
    qkv, om, gm = _ml_project(xp, g[1, 0], w_ml, tm)
    zc = jnp.zeros((nb, ML_HEADS, ML_DV, ML_DQK), F32)
    zn = jnp.zeros((nb, ML_HEADS, 1, ML_DQK), F32)
    zm = jnp.zeros((nb, ML_HEADS, 1, 1), F32)
    ym, c_p, n_p, m_p = _mlstm(qkv, om, gm, bg_row, ml_head_norm[0][None], zc, zn, zm, nb=nb, lp=lp, lv=lp)
    xp = _out_project(ym, xp, w_mlo, g[1, 1], tm)
    xp = ffn_prompt(xp, 1)

    ms = sb * st
    ng, hpg, hd = NSA_KV_HEADS, NSA_HPG, HEAD_DIM
    xs = x_sample.transpose(1, 0, 2).reshape(ms, d)
    qs, kva_s, kvb_s, win_s, gates_s = _nsa_project(xs, g[0, 0], w_in0, ms)
    n_phys, page = cache_nsa_kv.shape[1:3]
    n_pages = page_table.shape[1]
    past = n_pages * page
    cache4 = cache_nsa_kv[0].reshape(n_phys, page, 8, LANES)
    kvc = _page_compress(page_table, cache4, wbd, w2bd, pos_tiles)

    rows = st * hpg
    q_rows = qs.reshape(st, sb, ng, hpg, hd).transpose(1, 2, 0, 3, 4).reshape(sb, ng, rows, hd)
    n_sel = -(-(past + st) // SEL_BLK)
    nsp = -(-n_sel // LANES) * LANES
    nsr = -(-n_sel // SUBLANES) * SUBLANES
    ocmp, imp = _sample_cmp(slopes, q_rows, kvc, past=past, nsp=nsp)
    imp_t = imp[:, :, ::hpg, :nsr].reshape(sb * ng * st, nsr).T
    idx = _sample_topk(imp_t, past=past, n_sel=n_sel, st=st)
    idx_bgtk = idx.T.reshape(sb, ng, st, SEL_TOPK)
    idx_rows = _pad_cols(jnp.repeat(idx_bgtk, hpg, axis=2).reshape(sb * ng * rows, SEL_TOPK))
    idx_rows = idx_rows.reshape(sb, ng, rows, LANES).astype(BF16)
    e16 = (jnp.arange(LANES, dtype=jnp.int32)[:, None]
           == jnp.arange(SEL_TOPK * SEL_BLK, dtype=jnp.int32)[None, :] // SEL_BLK).astype(BF16)
    gate_rows = gates_s.reshape(st, sb, 2, LANES)[..., :2 * hpg * 3].reshape(st, sb, ng, hpg, 3)
    gate_rows = _pad_cols(gate_rows.transpose(1, 2, 0, 3, 4).reshape(sb * ng * rows, 3)).reshape(sb, ng, rows, LANES)
    new16 = jnp.pad(kvb_s.reshape(st, sb, -1).transpose(1, 0, 2), ((0, 0), (0, 16 - st), (0, 0)))
    wb = cache_nsa_win.shape[2]
    win_buf = cache_nsa_win[0].reshape(sb, wb, 2 * NSA_KV_W)
    o_s = _sample_attention(idx_bgtk.reshape(-1), page_table.reshape(-1), slopes, cache4, q_rows, idx_rows, e16,
                            gate_rows, ocmp, new16, win_buf, past=past, n_pages=n_pages, st=st)
    o_s = o_s.reshape(sb, ng, st, hpg, hd).transpose(2, 0, 1, 3, 4).reshape(ms, NSA_Q_W).astype(BF16)
    xs = _out_project(o_s, xs, w_out0, g[0, 1], ms)
    kv_s = kva_s.reshape(st, sb, 4, ng, hd).transpose(1, 0, 2, 3, 4)[None]
    win_new = win_s.reshape(st, sb, 2, ng, hd).transpose(1, 0, 2, 3, 4)
    win_s_out = jnp.concatenate([cache_nsa_win[0], win_new], axis=1)[None, :, st:]

    conv_s = []

    def ffn_sample(xs, i):
        state = state_ffn_conv[i].transpose(1, 0, 2).reshape(1, (CONV_W - 1) * sb, 2 * D_FF)
        xo, ta, tb = _conv_ffn(xs, state, g[i, 2], g[i, 3], w_up[i], ffn_conv_w[i], ffn_conv_b[i][None],
                               w_dn[i], tm=ms, fc=512, shift=sb, tiles_per_seq=1)
        tail = jnp.concatenate([ta, tb], axis=-1).reshape(CONV_W - 1, sb, 2 * D_FF)
        conv_s.append(tail.transpose(1, 0, 2))
        return xo

    xs = ffn_sample(xs, 0)

    lps = SUBLANES
    to_seq = lambda a: jnp.pad(a.reshape(st, sb, -1).transpose(1, 0, 2),
                               ((0, 0), (0, lps - st), (0, 0))).reshape(sb * lps, -1)
    qkv_s, om_s, gm_s = _ml_project(xs, g[1, 0], w_ml, ms)
    ym_s, c_s, n_s, m_s = _mlstm(to_seq(qkv_s), to_seq(om_s), to_seq(gm_s), bg_row, ml_head_norm[0][None],
                                 state_mlstm_C[0], state_mlstm_n[0][:, :, None, :],
                                 state_mlstm_m[0][:, :, None, None], nb=sb, lp=lps, lv=math.gcd(st, ML_CHUNK))
    ym_s = ym_s.reshape(sb, lps, ML_V_W)[:, :st].transpose(1, 0, 2).reshape(ms, ML_V_W)
    xs = _out_project(ym_s, xs, w_mlo, g[1, 1], ms)
    xs = ffn_sample(xs, 1)

    return (xp.reshape(nb, t, d), xs.reshape(st, sb, d).transpose(1, 0, 2), kv_p, kv_s, win_p, win_s_out,
            c_p[None], c_s[None], n_p[None, :, :, 0], n_s[None, :, :, 0],
            m_p[None, :, :, 0, 0], m_s[None, :, :, 0, 0], jnp.stack(conv_p), jnp.stack(conv_s))
```

```python
import functools
import math

import jax
import jax.numpy as jnp
from jax import lax
from jax.experimental import pallas as pl
from jax.experimental.pallas import tpu as pltpu

F32 = jnp.float32
BF16 = jnp.bfloat16

LANES = 128
SUBLANES = 8
VMEM_LIMIT_BYTES = 56 * 1024 * 1024

D_MODEL = 1024
NSA_HEADS = 16
NSA_KV_HEADS = 4
NSA_HPG = NSA_HEADS // NSA_KV_HEADS
HEAD_DIM = D_MODEL // NSA_HEADS
CMP_LEN = 32
CMP_STRIDE = 16
SEL_BLK = 64
SEL_TOPK = 16
WINDOW = 512
Q_BLK = 128
FORCE_BONUS = 1e4
NSA_Q_W = NSA_HEADS * HEAD_DIM
NSA_KV_W = NSA_KV_HEADS * HEAD_DIM
ML_HEADS = 8
ML_DQK = D_MODEL // (2 * ML_HEADS)
ML_DV = D_MODEL // ML_HEADS
ML_CHUNK = 64
ML_QK_W = ML_HEADS * ML_DQK
ML_V_W = ML_HEADS * ML_DV
D_FF = 4 * D_MODEL
CONV_W = 3
RMS_EPS = 1e-6
MASK_NEG = -1e30
ML_SEQ_PER_STEP = 4
SEL_CHUNK = 512
PAIR_W = 2 * HEAD_DIM


def _cparams(*sem):
    return pltpu.CompilerParams(dimension_semantics=sem, vmem_limit_bytes=VMEM_LIMIT_BYTES)


def _dot(a, b):
    return jnp.dot(a, b, preferred_element_type=F32)


def _dot_nt(a, b):
    return lax.dot_general(a, b, (((1,), (1,)), ((), ())), preferred_element_type=F32)


def _dot_tn(a, b):
    return lax.dot_general(a, b, (((0,), (0,)), ((), ())), preferred_element_type=F32)


def _split3(x):
    hi = x.astype(BF16)
    r1 = x - hi.astype(F32)
    mid = r1.astype(BF16)
    lo = (r1 - mid.astype(F32)).astype(BF16)
    return hi, mid, lo


def _rms(x, g):
    return x * lax.rsqrt(jnp.mean(x * x, axis=-1, keepdims=True) + RMS_EPS) * g


def _gelu_tanh(x):
    return 0.5 * x * (1.0 + jnp.tanh(0.7978845608028654 * (x + 0.044715 * (x * x * x))))


def _full(shape):
    return pl.BlockSpec(shape, lambda *_: (0,) * len(shape))


def _nsa_proj_kernel(x_ref, g_ref, w_ref, q_ref, kva_ref, kvb_ref, win_ref, gate_ref):
    h = _rms(x_ref[...], g_ref[...]).astype(BF16)
    p = _dot(h, w_ref[...])
    kvw = 4 * NSA_KV_W
    q_ref[...] = (p[:, :NSA_Q_W] * HEAD_DIM ** -0.5).astype(BF16)
    kva_ref[...] = p[:, NSA_Q_W:NSA_Q_W + kvw]
    kvb_ref[...] = p[:, NSA_Q_W:NSA_Q_W + 6 * NSA_KV_W].astype(BF16)
    win_ref[...] = p[:, NSA_Q_W + kvw:NSA_Q_W + 6 * NSA_KV_W]
    gate_ref[...] = p[:, NSA_Q_W + 6 * NSA_KV_W:]


def _nsa_project(x, g, w, tm):
    m = x.shape[0]
    n = w.shape[1]
    row = lambda i: (i, 0)
    return pl.pallas_call(
        _nsa_proj_kernel,
        grid=(m // tm,),
        in_specs=[pl.BlockSpec((tm, D_MODEL), row), _full((1, D_MODEL)), _full((D_MODEL, n))],
        out_specs=[pl.BlockSpec((tm, NSA_Q_W), row), pl.BlockSpec((tm, 4 * NSA_KV_W), row),
                   pl.BlockSpec((tm, 6 * NSA_KV_W), row), pl.BlockSpec((tm, 2 * NSA_KV_W), row),
                   pl.BlockSpec((tm, 2 * LANES), row)],
        out_shape=[jax.ShapeDtypeStruct((m, NSA_Q_W), BF16), jax.ShapeDtypeStruct((m, 4 * NSA_KV_W), F32),
                   jax.ShapeDtypeStruct((m, 6 * NSA_KV_W), BF16), jax.ShapeDtypeStruct((m, 2 * NSA_KV_W), F32),
                   jax.ShapeDtypeStruct((m, 2 * LANES), F32)],
        compiler_params=_cparams("parallel"),
        name="nsa_proj",
    )(x, g, w)


def _ml_proj_kernel(x_ref, g_ref, w_ref, qkv_ref, o_ref, gate_ref):
    h = _rms(x_ref[...], g_ref[...]).astype(BF16)
    p = _dot(h, w_ref[...])
    a = 2 * ML_QK_W + ML_V_W
    qkv_ref[:, :ML_QK_W] = p[:, :ML_QK_W].astype(BF16)
    qkv_ref[:, ML_QK_W:2 * ML_QK_W] = (p[:, ML_QK_W:2 * ML_QK_W] * ML_DQK ** -0.5).astype(BF16)
    qkv_ref[:, 2 * ML_QK_W:] = p[:, 2 * ML_QK_W:a].astype(BF16)
    o_ref[...] = p[:, a:a + ML_V_W]
    gate_ref[...] = p[:, a + ML_V_W:]


def _ml_project(x, g, w, tm):
    m = x.shape[0]
    n = w.shape[1]
    a = 2 * ML_QK_W + ML_V_W
    row = lambda i: (i, 0)
    return pl.pallas_call(
        _ml_proj_kernel,
        grid=(m // tm,),
        in_specs=[pl.BlockSpec((tm, D_MODEL), row), _full((1, D_MODEL)), _full((D_MODEL, n))],
        out_specs=[pl.BlockSpec((tm, a), row), pl.BlockSpec((tm, ML_V_W), row), pl.BlockSpec((tm, LANES), row)],
        out_shape=[jax.ShapeDtypeStruct((m, a), BF16), jax.ShapeDtypeStruct((m, ML_V_W), F32),
                   jax.ShapeDtypeStruct((m, LANES), F32)],
        compiler_params=_cparams("parallel"),
        name="ml_proj",
    )(x, g, w)


def _out_proj_kernel(o_ref, x_ref, w_ref, g_ref, xo_ref):
    y = _dot(o_ref[...], w_ref[...])
    xo_ref[...] = x_ref[...] + _rms(y, g_ref[...])


def _out_project(o, x, w, g, tm):
    m = x.shape[0]
    row = lambda i: (i, 0)
    return pl.pallas_call(
        _out_proj_kernel,
        grid=(m // tm,),
        in_specs=[pl.BlockSpec((tm, o.shape[1]), row), pl.BlockSpec((tm, D_MODEL), row),
                  _full(w.shape), _full((1, D_MODEL))],
        out_specs=pl.BlockSpec((tm, D_MODEL), row),
        out_shape=jax.ShapeDtypeStruct((m, D_MODEL), F32),
        compiler_params=_cparams("parallel"),
        name="out_proj",
    )(o, x, w, g)


def _ffn_kernel(x_ref, sta_ref, stb_ref, g2_ref, g3_ref, wa_ref, wb_ref, cwa_ref, cwb_ref, cba_ref, cbb_ref,
                wd_ref, xo_ref, taila_ref, tailb_ref, h_scr, ua_scr, ub_scr, ca_scr, cb_scr, acc_scr,
                *, tm, halo, shift, tiles_per_seq):
    i = pl.program_id(0)
    c = pl.program_id(1)

    @pl.when(c == 0)
    def _():
        h_scr[...] = _rms(x_ref[...], g2_ref[...]).astype(BF16)
        acc_scr[...] = jnp.zeros_like(acc_scr)

    first = (i % tiles_per_seq) == 0

    @pl.when(first)
    def _():
        ua_scr[0:halo, :] = sta_ref[...]
        ub_scr[0:halo, :] = stb_ref[...]

    @pl.when(jnp.logical_not(first))
    def _():
        ua_scr[0:halo, :] = ca_scr[c]
        ub_scr[0:halo, :] = cb_scr[c]

    h = h_scr[...]
    ua_scr[halo:halo + tm, :] = _dot(h, wa_ref[...])
    ub_scr[halo:halo + tm, :] = _dot(h, wb_ref[...])
    ta = ua_scr[tm:tm + halo, :]
    tb = ub_scr[tm:tm + halo, :]
    ca_scr[c] = ta
    cb_scr[c] = tb
    taila_ref[...] = ta
    tailb_ref[...] = tb

    def conv(u_scr, cw_ref, cb_ref):
        cw = cw_ref[...]
        return (cb_ref[...] + cw[2:3, :] * u_scr[halo:halo + tm, :]
                + cw[1:2, :] * u_scr[halo - shift:halo - shift + tm, :]
                + cw[0:1, :] * u_scr[halo - 2 * shift:halo - 2 * shift + tm, :])

    y = _gelu_tanh(conv(ua_scr, cwa_ref, cba_ref)) * conv(ub_scr, cwb_ref, cbb_ref)
    acc_scr[...] += _dot(y.astype(BF16), wd_ref[...])

    @pl.when(c == pl.num_programs(1) - 1)
    def _():
        xo_ref[...] = x_ref[...] + _rms(acc_scr[...], g3_ref[...])


def _conv_ffn(x, state, g2, g3, w_up, conv_w, conv_b, w_down, *, tm, fc, shift, tiles_per_seq):
    m = x.shape[0]
    halo = state.shape[1]
    nfc = D_FF // fc
    n_tiles = m // tm
    row = lambda i, c: (i, 0)
    const = lambda i, c: (0, 0)
    kern = functools.partial(_ffn_kernel, tm=tm, halo=halo, shift=shift, tiles_per_seq=tiles_per_seq)
    return pl.pallas_call(
        kern,
        grid=(n_tiles, nfc),
        in_specs=[
            pl.BlockSpec((tm, D_MODEL), row),
            pl.BlockSpec((None, halo, fc), lambda i, c: (i // tiles_per_seq, 0, c)),
            pl.BlockSpec((None, halo, fc), lambda i, c: (i // tiles_per_seq, 0, nfc + c)),
            pl.BlockSpec((1, D_MODEL), const), pl.BlockSpec((1, D_MODEL), const),
            pl.BlockSpec((D_MODEL, fc), lambda i, c: (0, c)),
            pl.BlockSpec((D_MODEL, fc), lambda i, c: (0, nfc + c)),
            pl.BlockSpec((CONV_W, fc), lambda i, c: (0, c)),
            pl.BlockSpec((CONV_W, fc), lambda i, c: (0, nfc + c)),
            pl.BlockSpec((1, fc), lambda i, c: (0, c)),
            pl.BlockSpec((1, fc), lambda i, c: (0, nfc + c)),
            pl.BlockSpec((fc, D_MODEL), lambda i, c: (c, 0)),
        ],
        out_specs=[pl.BlockSpec((tm, D_MODEL), row),
                   pl.BlockSpec((None, halo, fc), lambda i, c: (i, 0, c)),
                   pl.BlockSpec((None, halo, fc), lambda i, c: (i, 0, c))],
        out_shape=[jax.ShapeDtypeStruct((m, D_MODEL), F32),
                   jax.ShapeDtypeStruct((n_tiles, halo, D_FF), F32),
                   jax.ShapeDtypeStruct((n_tiles, halo, D_FF), F32)],
        scratch_shapes=[pltpu.VMEM((tm, D_MODEL), BF16),
                        pltpu.VMEM((halo + tm, fc), F32), pltpu.VMEM((halo + tm, fc), F32),
                        pltpu.VMEM((nfc, halo, fc), F32), pltpu.VMEM((nfc, halo, fc), F32),
                        pltpu.VMEM((tm, D_MODEL), F32)],
        compiler_params=_cparams("arbitrary", "arbitrary"),
        name="conv_ffn",
    )(x, state, state, g2, g3, w_up, w_up, conv_w, conv_w, conv_b, conv_b, w_down)


def _log_sigmoid(x):
    return jnp.minimum(x, 0.0) - jnp.log1p(jnp.exp(-jnp.abs(x)))


def _exact_nt(sel_bf16, x):
    hi, mid, lo = _split3(x)
    return _dot_nt(sel_bf16, hi) + _dot_nt(sel_bf16, mid) + _dot_nt(sel_bf16, lo)


def _eye(n, m, dtype):
    return (lax.broadcasted_iota(jnp.int32, (n, m), 0) == lax.broadcasted_iota(jnp.int32, (n, m), 1)).astype(dtype)


def _mlstm_kernel(qkv_ref, o_ref, gate_ref, bg_ref, hn_ref, c0_ref, n0_ref, m0_ref,
                  y_ref, cf_ref, nf_ref, mf_ref, ct_scr, n_scr, m_scr, *, lp, lv, bb):
    for bi in range(bb):
        _mlstm_one(qkv_ref.at[bi], o_ref.at[bi], gate_ref.at[bi], bg_ref, hn_ref, c0_ref.at[bi], n0_ref.at[bi],
                   m0_ref.at[bi], y_ref.at[bi], cf_ref.at[bi], nf_ref.at[bi], mf_ref.at[bi],
                   ct_scr.at[bi], n_scr.at[bi], m_scr.at[bi], lp=lp, lv=lv)


def _mlstm_one(qkv_ref, o_ref, gate_ref, bg_ref, hn_ref, c0_ref, n0_ref, m0_ref,
               y_ref, cf_ref, nf_ref, mf_ref, ct_scr, n_scr, m_scr, *, lp, lv):
    ci = pl.program_id(1)
    eye_qk = _eye(ML_DQK, ML_DQK, BF16)

    @pl.when(ci == 0)
    def _():
        for h in range(ML_HEADS):
            ct_scr[h] = _exact_nt(eye_qk, c0_ref[h])
        n_scr[...] = n0_ref[...]
        m_scr[...] = m0_ref[...]

    gp = gate_ref[...] + bg_ref[...]
    ls = _log_sigmoid(gp)
    r_i = lax.broadcasted_iota(jnp.int32, (lp, lp), 0)
    c_i = lax.broadcasted_iota(jnp.int32, (lp, lp), 1)
    causal = c_i <= r_i
    tril = causal.astype(BF16)
    hi, mid, lo = _split3(ls)
    b_all = _dot(tril, hi) + _dot(tril, mid) + _dot(tril, lo)
    sel16 = _eye(2 * ML_HEADS, LANES, BF16)
    gp_t = _exact_nt(sel16, gp)
    b_t = _exact_nt(sel16, b_all)
    row_valid = lax.broadcasted_iota(jnp.int32, (lp, 1), 0) < lv

    for h in range(ML_HEADS):
        q = qkv_ref[:, h * ML_DQK:(h + 1) * ML_DQK]
        k = qkv_ref[:, ML_QK_W + h * ML_DQK:ML_QK_W + (h + 1) * ML_DQK]
        v = qkv_ref[:, 2 * ML_QK_W + h * ML_DV:2 * ML_QK_W + (h + 1) * ML_DV]
        ct = ct_scr[h]
        n_row = n_scr[h]
        m = m_scr[h]
        b_col = b_all[:, ML_HEADS + h:ML_HEADS + h + 1]
        i_col = gp[:, h:h + 1]
        b_row = b_t[ML_HEADS + h:ML_HEADS + h + 1, :]
        i_row = gp_t[h:h + 1, :]
        dmat = jnp.where(causal, b_col - b_row + i_row, MASK_NEG)
        inter = b_col + m
        mt = jnp.maximum(inter, jnp.max(dmat, axis=-1, keepdims=True))
        s = _dot_nt(q, k) * jnp.exp(dmat - mt)
        wi = jnp.exp(inter - mt)
        qf = q.astype(F32)
        num = wi * _dot(q, ct.astype(BF16)) + _dot(s.astype(BF16), v)
        den = wi * jnp.sum(qf * n_row, axis=-1, keepdims=True) + jnp.sum(s, axis=-1, keepdims=True)
        hc = num / jnp.maximum(jnp.abs(den), jnp.exp(-mt))
        m_last = mt[lv - 1:lv, :]
        b_last = b_col[lv - 1:lv, :]
        decay = jnp.exp(b_last + m - m_last)
        ws = jnp.where(row_valid, jnp.exp(b_last - b_col + i_col - m_last), 0.0)
        ct_scr[h] = decay * ct + _dot_tn(k, (ws * v.astype(F32)).astype(BF16))
        n_scr[h] = decay * n_row + jnp.sum(ws * k.astype(F32), axis=0, keepdims=True)
        m_scr[h] = m_last
        hn = hc * lax.rsqrt(jnp.mean(hc * hc, axis=-1, keepdims=True) + RMS_EPS)
        sl = slice(h * ML_DV, (h + 1) * ML_DV)
        y_ref[:, sl] = (hn * hn_ref[:, sl] * jax.nn.sigmoid(o_ref[:, sl])).astype(BF16)

    @pl.when(ci == pl.num_programs(1) - 1)
    def _():
        eye_v = _eye(ML_DV, ML_DV, BF16)
        for h in range(ML_HEADS):
            cf_ref[h] = _exact_nt(eye_v, ct_scr[h])
        nf_ref[...] = n_scr[...]
        mf_ref[...] = m_scr[...]


def _mlstm(qkv, o, gates, b_gate, head_norm, c0, n0, m0, *, nb, lp, lv):
    m = qkv.shape[0]
    tseq = m // nb
    bb = math.gcd(nb, ML_SEQ_PER_STEP)
    seq3 = lambda a: a.reshape(nb, tseq, a.shape[1])
    row = lambda b, c: (b, c, 0)
    st4 = lambda b, c: (b, 0, 0, 0)
    kern = functools.partial(_mlstm_kernel, lp=lp, lv=lv, bb=bb)
    y, cf, nf, mf = pl.pallas_call(
        kern,
        grid=(nb // bb, tseq // lp),
        in_specs=[pl.BlockSpec((bb, lp, qkv.shape[1]), row), pl.BlockSpec((bb, lp, ML_V_W), row),
                  pl.BlockSpec((bb, lp, LANES), row), pl.BlockSpec((1, LANES), lambda b, c: (0, 0)),
                  pl.BlockSpec((1, ML_V_W), lambda b, c: (0, 0)),
                  pl.BlockSpec((bb, ML_HEADS, ML_DV, ML_DQK), st4),
                  pl.BlockSpec((bb, ML_HEADS, 1, ML_DQK), st4),
                  pl.BlockSpec((bb, ML_HEADS, 1, 1), st4)],
        out_specs=[pl.BlockSpec((bb, lp, ML_V_W), row),
                   pl.BlockSpec((bb, ML_HEADS, ML_DV, ML_DQK), st4),
                   pl.BlockSpec((bb, ML_HEADS, 1, ML_DQK), st4),
                   pl.BlockSpec((bb, ML_HEADS, 1, 1), st4)],
        out_shape=[jax.ShapeDtypeStruct((nb, tseq, ML_V_W), BF16),
                   jax.ShapeDtypeStruct((nb, ML_HEADS, ML_DV, ML_DQK), F32),
                   jax.ShapeDtypeStruct((nb, ML_HEADS, 1, ML_DQK), F32),
                   jax.ShapeDtypeStruct((nb, ML_HEADS, 1, 1), F32)],
        scratch_shapes=[pltpu.VMEM((bb, ML_HEADS, ML_DQK, ML_DV), F32),
                        pltpu.VMEM((bb, ML_HEADS, 1, ML_DQK), F32),
                        pltpu.VMEM((bb, ML_HEADS, 1, 1), F32)],
        compiler_params=_cparams("arbitrary", "arbitrary"),
        name="mlstm",
    )(seq3(qkv), seq3(o), seq3(gates), b_gate, head_norm, c0, n0, m0)
    return y.reshape(m, ML_V_W), cf, nf, mf


def _cmp_parts(load_rows, wbd_ref, c, p):
    acc = None
    for l in range(CMP_STRIDE):
        part = _dot(load_rows(l, c, p).astype(BF16), wbd_ref[c, l])
        acc = part if acc is None else acc + part
    return acc


def _cmp_pos_bias(pos_ref, wbd_ref, c):
    acc = None
    for l in range(CMP_STRIDE):
        part = _dot(pos_ref[c, l].astype(BF16), wbd_ref[c, l])
        acc = part if acc is None else acc + part
    return acc[0:1, :PAIR_W] + acc[1:2, PAIR_W:]


def _cmp_finish(parts, posb, w2_ref, c):
    n = parts.shape[0]
    pre = parts[:, :PAIR_W] + pltpu.roll(parts[:, PAIR_W:], n - 1, 0) + posb
    return _dot(_gelu_tanh(pre).astype(BF16), w2_ref[c])


def _nsa_compress_kernel(x00_ref, x01_ref, x10_ref, x11_ref, wbd_ref, w2_ref, pos_ref, kc_ref, vc_ref, *, nchunk):
    x_refs = ((x00_ref, x01_ref), (x10_ref, x11_ref))

    def load_rows(l, c, p):
        return x_refs[c][p][pl.ds(l, nchunk, stride=CMP_STRIDE), :]

    for c, out_ref in ((0, kc_ref), (1, vc_ref)):
        posb = _cmp_pos_bias(pos_ref, wbd_ref, c)
        for p in range(2):
            out_ref[p] = _cmp_finish(_cmp_parts(load_rows, wbd_ref, c, p), posb, w2_ref, c).astype(BF16)


def _nsa_compress(kva, wbd, w2bd, pos, *, nb, t):
    nchunk = t // CMP_STRIDE
    kern = functools.partial(_nsa_compress_kernel, nchunk=nchunk)
    out = jax.ShapeDtypeStruct((nb, 2, nchunk, PAIR_W), BF16)
    ospec = pl.BlockSpec((None, 2, nchunk, PAIR_W), lambda b: (b, 0, 0, 0))
    return pl.pallas_call(
        kern,
        grid=(nb,),
        in_specs=[pl.BlockSpec((t, PAIR_W), functools.partial(lambda j, b: (b, j), j)) for j in range(4)]
        + [_full(wbd.shape), _full(w2bd.shape), _full(pos.shape)],
        out_specs=[ospec, ospec],
        out_shape=[out, out],
        compiler_params=_cparams("parallel"),
        name="nsa_compress",
    )(kva, kva, kva, kva, wbd, w2bd, pos)


def _select_mask_t(imp_t, s0, score_scr, ns):
    qb = imp_t.shape[1]
    bid = lax.broadcasted_iota(jnp.int32, (ns, qb), 0)
    tq = s0 + lax.broadcasted_iota(jnp.int32, (ns, qb), 1)
    cur = tq // SEL_BLK
    valid = bid * SEL_BLK <= tq
    forced = (bid == 0) | (bid == cur) | (bid == cur - 1)
    score = jnp.where(valid, imp_t + jnp.where(forced, FORCE_BONUS, 0.0), MASK_NEG)
    score_scr[0:ns, :] = score

    def body(m, cnt):
        sm = score_scr[pl.ds(m, 1), :]
        ahead = (sm > score) | ((sm == score) & (m < bid))
        return cnt + ahead.astype(jnp.int32)

    cnt = lax.fori_loop(0, ns, body, jnp.zeros((ns, qb), jnp.int32), unroll=8)
    return cnt < min(SEL_TOPK, ns)


def _nsa_attn_kernel(slope_ref, q_ref, gate_ref, kc_ref, vc_ref, ksel_ref, vsel_ref, kwin_ref, vwin_ref,
                     e3_ref, o_ref, score_scr, *, t):
    pr = pl.program_id(1)
    nblk = pl.program_id(2)
    qb = Q_BLK
    s0 = nblk * qb
    ncp = t // CMP_STRIDE
    ns = t // SEL_BLK
    wk = WINDOW + qb
    hd = HEAD_DIM
    sg = jax.nn.sigmoid(gate_ref[...])

    tq_col = s0 + lax.broadcasted_iota(jnp.int32, (qb, 1), 0)
    cend = CMP_STRIDE * lax.broadcasted_iota(jnp.int32, (1, ncp), 1) + (CMP_LEN - 1)
    dist_c = tq_col - cend
    mask_c = dist_c >= 0
    distf_c = dist_c.astype(F32)
    jj = lax.broadcasted_iota(jnp.int32, (ns, ncp), 1) * CMP_STRIDE
    nn = lax.broadcasted_iota(jnp.int32, (ns, ncp), 0) * SEL_BLK
    ov_t = ((jj < nn + SEL_BLK) & (jj + CMP_LEN > nn)).astype(BF16)
    wstart = pl.multiple_of(jnp.maximum(s0 - WINDOW, 0), qb)
    dist_w = tq_col - (wstart + lax.broadcasted_iota(jnp.int32, (1, wk), 1))
    mask_w = (dist_w >= 0) & (dist_w < WINDOW)
    distf_w = dist_w.astype(F32)
    n_sel_chunks = (s0 + qb + SEL_CHUNK - 1) // SEL_CHUNK

    for half in range(2):
        lanes = slice(half * hd, (half + 1) * hd)
        qs = [q_ref[:, (half * NSA_HPG + hh) * hd:(half * NSA_HPG + hh + 1) * hd] for hh in range(NSA_HPG)]
        q_all = jnp.concatenate(qs, axis=0)
        slopes = [slope_ref[(pr * 2 + half) * NSA_HPG + hh] for hh in range(NSA_HPG)]

        s_c = _dot_nt(q_all, kc_ref[:, lanes])
        vc = vc_ref[...]
        pc_sum = jnp.zeros((qb, ncp), F32)
        o_cmp = []
        for hh in range(NSA_HPG):
            s = jnp.where(mask_c, s_c[hh * qb:(hh + 1) * qb] - slopes[hh] * distf_c, MASK_NEG)
            e = jnp.where(mask_c, jnp.exp(s - jnp.max(s, axis=-1, keepdims=True)), 0.0)
            pc = e / jnp.maximum(jnp.sum(e, axis=-1, keepdims=True), 1e-30)
            pc_sum = pc_sum + pc
            o_cmp.append(_dot(pc.astype(BF16), vc)[:, lanes])

        hi = pc_sum.astype(BF16)
        lo = (pc_sum - hi.astype(F32)).astype(BF16)
        imp_t = _dot_nt(ov_t, hi) + _dot_nt(ov_t, lo)
        sel_t = _select_mask_t(imp_t, s0, score_scr, ns)
        mneg = jnp.where(sel_t, 0.0, MASK_NEG).T.astype(BF16)

        def sel_chunk(kci, carry):
            m_run, l_run, acc = carry
            r0 = pl.multiple_of(kci * SEL_CHUNK, SEL_CHUNK)
            s_all = _dot_nt(q_all, ksel_ref[pl.ds(r0, SEL_CHUNK), lanes])
            kpos = r0 + lax.broadcasted_iota(jnp.int32, (1, SEL_CHUNK), 1)
            add = jnp.where(kpos > tq_col, MASK_NEG, _dot(mneg, e3_ref[kci]))
            prel = (kpos - s0).astype(F32)
            s = jnp.concatenate([s_all[hh * qb:(hh + 1) * qb] + (add + slopes[hh] * prel)
                                 for hh in range(NSA_HPG)], axis=0)
            m_new = jnp.maximum(m_run, jnp.max(s, axis=-1, keepdims=True))
            alpha = jnp.exp(m_run - m_new)
            p = jnp.exp(s - m_new)
            l_new = alpha * l_run + jnp.sum(p, axis=-1, keepdims=True)
            acc_new = alpha * acc + _dot(p.astype(BF16), vsel_ref[pl.ds(r0, SEL_CHUNK), :])
            return m_new, l_new, acc_new

        init = (jnp.full((NSA_HPG * qb, 1), MASK_NEG, F32), jnp.zeros((NSA_HPG * qb, 1), F32),
                jnp.zeros((NSA_HPG * qb, PAIR_W), F32))
        _, l_sel, acc_sel = lax.fori_loop(0, n_sel_chunks, sel_chunk, init)
        o_sel_all = acc_sel[:, lanes] / l_sel

        s_w = _dot_nt(q_all, kwin_ref[pl.ds(wstart, wk), lanes])
        vw = vwin_ref[pl.ds(wstart, wk), :]
        for hh in range(NSA_HPG):
            s = jnp.where(mask_w, s_w[hh * qb:(hh + 1) * qb] - slopes[hh] * distf_w, MASK_NEG)
            e = jnp.exp(s - jnp.max(s, axis=-1, keepdims=True))
            o_w = _dot(e.astype(BF16), vw)[:, lanes] / jnp.sum(e, axis=-1, keepdims=True)
            gi = (half * NSA_HPG + hh) * 3
            out = (sg[:, gi:gi + 1] * o_cmp[hh] + sg[:, gi + 1:gi + 2] * o_sel_all[hh * qb:(hh + 1) * qb]
                   + sg[:, gi + 2:gi + 3] * o_w)
            oc = (half * NSA_HPG + hh) * hd
            o_ref[:, oc:oc + hd] = out.astype(BF16)


def _nsa_attention(slopes, q, gates, kc, vc, kvb, e3, *, nb, t):
    nq = t // Q_BLK
    ncp = t // CMP_STRIDE
    pw = 2 * NSA_HPG * HEAD_DIM
    kvcol = lambda c: (lambda b, p, n: (b, 2 * c + p))
    kern = functools.partial(_nsa_attn_kernel, t=t)
    return pl.pallas_call(
        kern,
        grid=(nb, 2, nq),
        in_specs=[pl.BlockSpec(memory_space=pltpu.SMEM),
                  pl.BlockSpec((Q_BLK, pw), lambda b, p, n: (b * nq + n, p)),
                  pl.BlockSpec((Q_BLK, LANES), lambda b, p, n: (b * nq + n, p)),
                  pl.BlockSpec((None, None, ncp, PAIR_W), lambda b, p, n: (b, p, 0, 0)),
                  pl.BlockSpec((None, None, ncp, PAIR_W), lambda b, p, n: (b, p, 0, 0)),
                  pl.BlockSpec((t, PAIR_W), kvcol(2)), pl.BlockSpec((t, PAIR_W), kvcol(3)),
                  pl.BlockSpec((t, PAIR_W), kvcol(4)), pl.BlockSpec((t, PAIR_W), kvcol(5)),
                  _full(e3.shape)],
        out_specs=pl.BlockSpec((Q_BLK, pw), lambda b, p, n: (b * nq + n, p)),
        out_shape=jax.ShapeDtypeStruct((nb * t, NSA_Q_W), BF16),
        scratch_shapes=[pltpu.VMEM((max(t // SEL_BLK, SUBLANES), Q_BLK), F32)],
        compiler_params=_cparams("parallel", "parallel", "arbitrary"),
        name="nsa_attn",
    )(slopes, q, gates, kc, vc, kvb, kvb, kvb, kvb, e3)


KEY_CHUNK = 256


def _rank_topk_t(score, ns, qb):
    blocks = [score[SUBLANES * r:SUBLANES * (r + 1)] for r in range(ns // SUBLANES)]
    cnts = [jnp.zeros((SUBLANES, qb), jnp.int32) for _ in blocks]
    sub = lax.broadcasted_iota(jnp.int32, (SUBLANES, qb), 0)
    for m in range(ns):
        sm = jnp.broadcast_to(score[m:m + 1, :], (SUBLANES, qb))
        for r, blk in enumerate(blocks):
            lo = SUBLANES * r
            if lo > m:
                ahead = (sm >= blk).astype(jnp.int32)
            elif lo + SUBLANES - 1 <= m:
                ahead = (sm > blk).astype(jnp.int32)
            else:
                ahead = jnp.where(sub > m - lo, (sm >= blk).astype(jnp.int32), (sm > blk).astype(jnp.int32))
            cnts[r] = cnts[r] + ahead
    return jnp.concatenate(cnts, axis=0) < min(SEL_TOPK, ns)


def _online_unit(q, k, vaug, bias, u, m_scr, acc_scr):
    s = _dot_nt(q, k) + bias
    m_old = m_scr[u]
    m_new = jnp.maximum(m_old, jnp.max(s, axis=-1, keepdims=True))
    alpha = jnp.exp(m_old - m_new)
    p = jnp.exp(s - m_new)
    acc_scr[u] = alpha * acc_scr[u] + _dot(p.astype(BF16), vaug)
    m_scr[u] = m_new


def _nsa_attn2_kernel(slope_ref, q_ref, gate_ref, kc_ref, vc_ref, ksel_ref, vsel_ref, kwin_ref, vwin_ref,
                      e3_ref, o_ref, vs_scr, vw_scr, m_scr, acc_scr, pcs_scr, *, t):
    pr = pl.program_id(1)
    nblk = pl.program_id(2)
    qb = Q_BLK
    s0 = nblk * qb
    ncp = t // CMP_STRIDE
    ns = t // SEL_BLK
    hd = HEAD_DIM
    w = KEY_CHUNK
    nunit = 2 * NSA_HPG

    @pl.when(nblk == 0)
    def _():
        lane = lax.broadcasted_iota(jnp.int32, (t, PAIR_W), 1)
        for src, dst in ((vsel_ref, vs_scr), (vwin_ref, vw_scr)):
            v = src[...].astype(F32)
            dst[0] = jnp.where(lane < hd, v, 1.0).astype(BF16)
            dst[1] = jnp.where(lane < hd, pltpu.roll(v, hd, 1), 1.0).astype(BF16)

    sg = jax.nn.sigmoid(gate_ref[...])
    tq_col = s0 + lax.broadcasted_iota(jnp.int32, (qb, 1), 0)
    slopes = [[slope_ref[(pr * 2 + half) * NSA_HPG + hh] for hh in range(NSA_HPG)] for half in range(2)]
    q_of = lambda half, hh: q_ref[:, (half * NSA_HPG + hh) * hd:(half * NSA_HPG + hh + 1) * hd]
    lanes_of = lambda half: slice(half * hd, (half + 1) * hd)

    def reset_state():
        m_scr[...] = jnp.full(m_scr.shape, MASK_NEG, F32)
        acc_scr[...] = jnp.zeros(acc_scr.shape, F32)

    def read_out(u):
        a = acc_scr[u]
        return a[:, :hd] * (1.0 / a[:, hd:hd + 1])

    cend = CMP_STRIDE * lax.broadcasted_iota(jnp.int32, (1, ncp), 1) + (CMP_LEN - 1)
    add_c = jnp.where(tq_col >= cend, 0.0, MASK_NEG)
    prel_c = (cend - s0).astype(F32)
    jj = lax.broadcasted_iota(jnp.int32, (ns, ncp), 1) * CMP_STRIDE
    nn = lax.broadcasted_iota(jnp.int32, (ns, ncp), 0) * SEL_BLK
    ov_t = ((jj < nn + SEL_BLK) & (jj + CMP_LEN > nn)).astype(BF16)
    bid = lax.broadcasted_iota(jnp.int32, (ns, qb), 0)
    tq_row = s0 + lax.broadcasted_iota(jnp.int32, (ns, qb), 1)
    cur = tq_row // SEL_BLK
    valid_b = bid * SEL_BLK <= tq_row
    bonus = jnp.where((bid == 0) | (bid == cur) | (bid == cur - 1), FORCE_BONUS, 0.0)
    o_cmp = [[None] * NSA_HPG for _ in range(2)]
    mneg = []
    for half in range(2):
        kc = kc_ref[:, lanes_of(half)]
        vc = vc_ref[...]
        for hh in range(NSA_HPG):
            s = _dot_nt(q_of(half, hh), kc) + (add_c + slopes[half][hh] * prel_c)
            m = jnp.max(s, axis=-1, keepdims=True)
            e = jnp.where(s > 0.5 * MASK_NEG, jnp.exp(s - m), 0.0)
            pc = e * (1.0 / jnp.maximum(jnp.sum(e, axis=-1, keepdims=True), 1e-30))
            if hh == 0:
                pcs_scr[...] = pc
            else:
                pcs_scr[...] += pc
            o_cmp[half][hh] = _dot(pc.astype(BF16), vc)[:, lanes_of(half)]
        pc_sum = pcs_scr[...]
        hi = pc_sum.astype(BF16)
        lo = (pc_sum - hi.astype(F32)).astype(BF16)
        imp_t = _dot_nt(ov_t, hi) + _dot_nt(ov_t, lo)
        score = jnp.where(valid_b, imp_t + bonus, MASK_NEG)
        sel_t = _rank_topk_t(score, ns, qb)
        mneg.append(jnp.where(sel_t, 0.0, MASK_NEG).T.astype(BF16))

    reset_state()

    def sel_body(kci, carry):
        r0 = pl.multiple_of(kci * w, w)
        kpos = r0 + lax.broadcasted_iota(jnp.int32, (1, w), 1)
        prel = (kpos - s0).astype(F32)
        for half in range(2):
            add = jnp.where(kpos > tq_col, MASK_NEG, _dot(mneg[half], e3_ref[kci]))
            k = ksel_ref[pl.ds(r0, w), lanes_of(half)]
            vaug = vs_scr[half, pl.ds(r0, w), :]
            for hh in range(NSA_HPG):
                _online_unit(q_of(half, hh), k, vaug, add + slopes[half][hh] * prel,
                             half * NSA_HPG + hh, m_scr, acc_scr)
        return carry

    lax.fori_loop(0, (s0 + qb + w - 1) // w, sel_body, 0)
    o_sel = [read_out(u) for u in range(nunit)]

    reset_state()
    wspan = -(-(WINDOW + qb) // w) * w
    wstart = pl.multiple_of(jnp.maximum(s0 + qb - wspan, 0), qb)
    for ci in range(wspan // w):
        r0 = pl.multiple_of(wstart + ci * w, qb)
        kpos = r0 + lax.broadcasted_iota(jnp.int32, (1, w), 1)
        dist = tq_col - kpos
        add = jnp.where((dist >= 0) & (dist < WINDOW), 0.0, MASK_NEG)
        prel = (kpos - s0).astype(F32)
        for half in range(2):
            k = kwin_ref[pl.ds(r0, w), lanes_of(half)]
            vaug = vw_scr[half, pl.ds(r0, w), :]
            for hh in range(NSA_HPG):
                _online_unit(q_of(half, hh), k, vaug, add + slopes[half][hh] * prel,
                             half * NSA_HPG + hh, m_scr, acc_scr)

    for half in range(2):
        for hh in range(NSA_HPG):
            u = half * NSA_HPG + hh
            gi = u * 3
            out = (sg[:, gi:gi + 1] * o_cmp[half][hh] + sg[:, gi + 1:gi + 2] * o_sel[u]
                   + sg[:, gi + 2:gi + 3] * read_out(u))
            o_ref[:, u * hd:(u + 1) * hd] = out.astype(BF16)


def _nsa_attention2(slopes, q, gates, kc, vc, kvb, e3, *, nb, t):
    nq = t // Q_BLK
    ncp = t // CMP_STRIDE
    pw = 2 * NSA_HPG * HEAD_DIM
    nunit = 2 * NSA_HPG
    kvcol = lambda c: (lambda b, p, n: (b, 2 * c + p))
    kern = functools.partial(_nsa_attn2_kernel, t=t)
    return pl.pallas_call(
        kern,
        grid=(nb, 2, nq),
        in_specs=[pl.BlockSpec(memory_space=pltpu.SMEM),
                  pl.BlockSpec((Q_BLK, pw), lambda b, p, n: (b * nq + n, p)),
                  pl.BlockSpec((Q_BLK, LANES), lambda b, p, n: (b * nq + n, p)),
                  pl.BlockSpec((None, None, ncp, PAIR_W), lambda b, p, n: (b, p, 0, 0)),
                  pl.BlockSpec((None, None, ncp, PAIR_W), lambda b, p, n: (b, p, 0, 0)),
                  pl.BlockSpec((t, PAIR_W), kvcol(2)), pl.BlockSpec((t, PAIR_W), kvcol(3)),
                  pl.BlockSpec((t, PAIR_W), kvcol(4)), pl.BlockSpec((t, PAIR_W), kvcol(5)),
                  _full(e3.shape)],
        out_specs=pl.BlockSpec((Q_BLK, pw), lambda b, p, n: (b * nq + n, p)),
        out_shape=jax.ShapeDtypeStruct((nb * t, NSA_Q_W), BF16),
        scratch_shapes=[pltpu.VMEM((2, t, PAIR_W), BF16), pltpu.VMEM((2, t, PAIR_W), BF16),
                        pltpu.VMEM((nunit, Q_BLK, 1), F32), pltpu.VMEM((nunit, Q_BLK, PAIR_W), F32),
                        pltpu.VMEM((Q_BLK, ncp), F32)],
        compiler_params=_cparams("parallel", "parallel", "arbitrary"),
        name="nsa_attn",
    )(slopes, q, gates, kc, vc, kvb, kvb, kvb, kvb, e3)


def _sel_expand2(t):
    key_blk = jnp.arange(t, dtype=jnp.int32).reshape(t // KEY_CHUNK, 1, KEY_CHUNK) // SEL_BLK
    blk = jnp.arange(t // SEL_BLK, dtype=jnp.int32).reshape(1, t // SEL_BLK, 1)
    return (key_blk == blk).astype(BF16)


N_FEAT = 6
AUG_W = 2 * LANES


def _key_feats(kpos_col):
    r = kpos_col.shape[0]
    lane = lax.broadcasted_iota(jnp.int32, (r, LANES), 1)
    a = lax.shift_right_logical(kpos_col, 6).astype(F32)
    b = (kpos_col & (SEL_BLK - 1)).astype(F32)
    return jnp.where(lane < 3, a, jnp.where(lane < N_FEAT, b, 0.0))


def _tile_rows(x, n):
    return jnp.concatenate([x] * n, axis=0)


def _nsa_attn3_kernel(sf_ref, q_ref, gate_ref, kc_ref, vc_ref, ksel_ref, vsel_ref, kwin_ref, vwin_ref, o_ref,
                      ks_scr, kw_scr, kc_scr, vs_scr, vw_scr, qs_scr, qw_scr, qc_scr, mrun_scr, m_scr, acc_scr,
                      *, t):
    pr = pl.program_id(1)
    nblk = pl.program_id(2)
    qb = Q_BLK
    s0 = nblk * qb
    ncp = t // CMP_STRIDE
    ns = t // SEL_BLK
    hd = HEAD_DIM
    w = KEY_CHUNK
    hpg = NSA_HPG
    rows_all = hpg * qb

    @pl.when(nblk == 0)
    def _():
        lane = lax.broadcasted_iota(jnp.int32, (t, LANES), 1)
        row = lax.broadcasted_iota(jnp.int32, (t, 1), 0)
        feats = _key_feats(row).astype(BF16)
        onehot = jnp.where(lane - hd == lax.shift_right_logical(row, 6), 1.0, 0.0)
        lane_w = lax.broadcasted_iota(jnp.int32, (WINDOW, LANES), 1)
        lane_c = lax.broadcasted_iota(jnp.int32, (ncp, LANES), 1)
        cend = CMP_STRIDE * lax.broadcasted_iota(jnp.int32, (ncp, 1), 0) + (CMP_LEN - 1)
        ks = ksel_ref[...].astype(F32)
        vs = vsel_ref[...].astype(F32)
        kwn = kwin_ref[...].astype(F32)
        vwn = vwin_ref[...].astype(F32)
        kcv = kc_ref[...].astype(F32)
        for half in range(2):
            low = (lambda x: x) if half == 0 else (lambda x: pltpu.roll(x, hd, 1))
            ks_scr[half, :, 0:LANES] = jnp.where(lane < hd, low(ks), onehot).astype(BF16)
            ks_scr[half, :, LANES:AUG_W] = feats
            vs_scr[half] = jnp.where(lane < hd, low(vs), 1.0).astype(BF16)
            kw_scr[half, 0:WINDOW, 0:LANES] = jnp.where(lane_w == hd, 1.0, 0.0).astype(BF16)
            kw_scr[half, 0:WINDOW, LANES:AUG_W] = jnp.zeros((WINDOW, LANES), BF16)
            kw_scr[half, WINDOW:WINDOW + t, 0:LANES] = jnp.where(lane < hd, low(kwn), 0.0).astype(BF16)
            kw_scr[half, WINDOW:WINDOW + t, LANES:AUG_W] = feats
            vw_scr[half, 0:WINDOW] = jnp.ones((WINDOW, LANES), BF16)
            vw_scr[half, WINDOW:WINDOW + t] = jnp.where(lane < hd, low(vwn), 1.0).astype(BF16)
            kc_scr[half, :, 0:LANES] = jnp.where(lane_c < hd, low(kcv), 0.0).astype(BF16)
            kc_scr[half, :, LANES:AUG_W] = _key_feats(cend).astype(BF16)

    sg = jax.nn.sigmoid(gate_ref[...])
    lane = lax.broadcasted_iota(jnp.int32, (qb, LANES), 1)
    lane1 = lax.broadcasted_iota(jnp.int32, (1, LANES), 1)
    t_loc = lax.broadcasted_iota(jnp.int32, (qb, 1), 0)
    tq_col = s0 + t_loc
    pad_row = jnp.where(lane1 == hd, MASK_NEG, 0.0)

    for half in range(2):
        for hh in range(hpg):
            u = half * hpg + hh
            qcol = q_ref[:, (u // 2) * LANES:(u // 2 + 1) * LANES].astype(F32)
            qlow = jnp.where(lane < hd, qcol if u % 2 == 0 else pltpu.roll(qcol, hd, 1), 0.0)
            feat = jnp.zeros((1, LANES), F32)
            for j in range(N_FEAT):
                feat = jnp.where(lane1 == j, sf_ref[(pr * 2 + half) * hpg + hh, j], feat)
            feat = jnp.broadcast_to(feat, (qb, LANES)).astype(BF16)
            rows = slice(hh * qb, (hh + 1) * qb)
            qc_scr[half, rows, 0:LANES] = qlow.astype(BF16)
            qw_scr[half, rows, 0:LANES] = (qlow + pad_row).astype(BF16)
            for scr in (qc_scr, qw_scr, qs_scr):
                scr[half, rows, LANES:AUG_W] = feat

    def online(half, s, vaug):
        nj = s.shape[1] // LANES
        mx = s[:, 0:LANES]
        for j in range(1, nj):
            mx = jnp.maximum(mx, s[:, j * LANES:(j + 1) * LANES])
        m_old = m_scr[half]
        m_new = jnp.maximum(m_old, jnp.max(mx, axis=-1, keepdims=True))
        alpha = jnp.exp(m_old - m_new)
        p = jnp.exp(s - jnp.concatenate([m_new] * nj, axis=1))
        acc_scr[half] = alpha * acc_scr[half] + _dot(p.astype(BF16), vaug)
        m_scr[half] = m_new

    def read_out(half):
        a = acc_scr[half]
        return a[:, 0:hd] * (1.0 / a[:, hd:hd + 1])

    def reset(scr, val):
        scr[...] = jnp.full(scr.shape, val, F32)

    cend_row = CMP_STRIDE * lax.broadcasted_iota(jnp.int32, (1, ncp), 1) + (CMP_LEN - 1)
    add_c = _tile_rows(jnp.where(tq_col >= cend_row, 0.0, MASK_NEG), hpg)
    jj = lax.broadcasted_iota(jnp.int32, (ns, ncp), 1) * CMP_STRIDE
    nn = lax.broadcasted_iota(jnp.int32, (ns, ncp), 0) * SEL_BLK
    ov_t = ((jj < nn + SEL_BLK) & (jj + CMP_LEN > nn)).astype(BF16)
    bid = lax.broadcasted_iota(jnp.int32, (ns, qb), 0)
    tq_row = s0 + lax.broadcasted_iota(jnp.int32, (ns, qb), 1)
    cur = tq_row // SEL_BLK
    valid_b = bid * SEL_BLK <= tq_row
    bonus = jnp.where((bid == 0) | (bid == cur) | (bid == cur - 1), FORCE_BONUS, 0.0)
    o_cmp = []
    for half in range(2):
        s = _dot_nt(qc_scr[half], kc_scr[half]) + add_c
        m = jnp.max(s, axis=-1, keepdims=True)
        e = jnp.where(s > 0.5 * MASK_NEG, jnp.exp(s - m), 0.0)
        pc = e * (1.0 / jnp.maximum(jnp.sum(e, axis=-1, keepdims=True), 1e-30))
        o_cmp.append(_dot(pc.astype(BF16), vc_ref[...])[:, half * hd:(half + 1) * hd])
        pc_sum = pc[0:qb]
        for hh in range(1, hpg):
            pc_sum = pc_sum + pc[hh * qb:(hh + 1) * qb]
        hi = pc_sum.astype(BF16)
        lo = (pc_sum - hi.astype(F32)).astype(BF16)
        imp_t = _dot_nt(ov_t, hi) + _dot_nt(ov_t, lo)
        sel_t = _rank_topk_t(jnp.where(valid_b, imp_t + bonus, MASK_NEG), ns, qb)
        pieces = [jnp.zeros((hd, qb), F32), jnp.where(sel_t, 0.0, MASK_NEG)]
        if LANES - hd - ns > 0:
            pieces.append(jnp.zeros((LANES - hd - ns, qb), F32))
        mneg = jnp.concatenate(pieces, axis=0).T
        for hh in range(hpg):
            rows = slice(hh * qb, (hh + 1) * qb)
            qs_scr[half, rows, 0:LANES] = (qc_scr[half, rows, 0:LANES].astype(F32) + mneg).astype(BF16)

    nfull = nblk // (w // qb)
    r_tail = pl.multiple_of(nfull * w, w)
    kpos_tail = r_tail + lax.broadcasted_iota(jnp.int32, (1, w), 1)
    causal_add = _tile_rows(jnp.where(kpos_tail > tq_col, MASK_NEG, 0.0), hpg)

    def sel_scores(half, r0, causal):
        s = _dot_nt(qs_scr[half], ks_scr[half, pl.ds(r0, w), :])
        return s + causal_add if causal else s

    def sel_body(kci, carry):
        r0 = pl.multiple_of(kci * w, w)
        for half in range(2):
            online(half, sel_scores(half, r0, False), vs_scr[half, pl.ds(r0, w), :])
        return carry

    reset(m_scr, MASK_NEG)
    reset(acc_scr, 0.0)
    lax.fori_loop(0, nfull, sel_body, 0)
    for half in range(2):
        online(half, sel_scores(half, r_tail, True), vs_scr[half, pl.ds(r_tail, w), :])
    o_sel = [read_out(half) for half in range(2)]

    c_loc = lax.broadcasted_iota(jnp.int32, (1, qb), 1)
    left_add = _tile_rows(jnp.where(c_loc > t_loc, 0.0, MASK_NEG), hpg)
    diag_add = _tile_rows(jnp.concatenate([jnp.zeros((qb, qb), F32),
                                           jnp.where(c_loc <= t_loc, 0.0, MASK_NEG)], axis=1), hpg)
    win_chunks = ((0, qb, left_add), (qb, w, None), (qb + w, w, diag_add))

    reset(m_scr, MASK_NEG)
    reset(acc_scr, 0.0)
    for off, wd, add in win_chunks:
        r0 = pl.multiple_of(s0 + off, qb)
        for half in range(2):
            s = _dot_nt(qw_scr[half], kw_scr[half, pl.ds(r0, wd), :])
            online(half, s if add is None else s + add, vw_scr[half, pl.ds(r0, wd), :])

    for half in range(2):
        o_w = read_out(half)
        for hh in range(hpg):
            u = half * hpg + hh
            gi = u * 3
            rows = slice(hh * qb, (hh + 1) * qb)
            out = (sg[:, gi:gi + 1] * o_cmp[half][rows] + sg[:, gi + 1:gi + 2] * o_sel[half][rows]
                   + sg[:, gi + 2:gi + 3] * o_w[rows])
            o_ref[:, u * hd:(u + 1) * hd] = out.astype(BF16)


def _nsa_attention3(slope_feats, q, gates, kc, vc, kvb, *, nb, t):
    assert WINDOW + Q_BLK == Q_BLK + 2 * KEY_CHUNK and t % KEY_CHUNK == 0 and t // SEL_BLK <= LANES - HEAD_DIM
    nq = t // Q_BLK
    ncp = t // CMP_STRIDE
    pw = 2 * NSA_HPG * HEAD_DIM
    rows_all = NSA_HPG * Q_BLK
    kvcol = lambda c: (lambda b, p, n: (b, 2 * c + p))
    kern = functools.partial(_nsa_attn3_kernel, t=t)
    return pl.pallas_call(
        kern,
        grid=(nb, 2, nq),
        in_specs=[pl.BlockSpec(memory_space=pltpu.SMEM),
                  pl.BlockSpec((Q_BLK, pw), lambda b, p, n: (b * nq + n, p)),
                  pl.BlockSpec((Q_BLK, LANES), lambda b, p, n: (b * nq + n, p)),
                  pl.BlockSpec((None, None, ncp, PAIR_W), lambda b, p, n: (b, p, 0, 0)),
                  pl.BlockSpec((None, None, ncp, PAIR_W), lambda b, p, n: (b, p, 0, 0)),
                  pl.BlockSpec((t, PAIR_W), kvcol(2)), pl.BlockSpec((t, PAIR_W), kvcol(3)),
                  pl.BlockSpec((t, PAIR_W), kvcol(4)), pl.BlockSpec((t, PAIR_W), kvcol(5))],
        out_specs=pl.BlockSpec((Q_BLK, pw), lambda b, p, n: (b * nq + n, p)),
        out_shape=jax.ShapeDtypeStruct((nb * t, NSA_Q_W), BF16),
        scratch_shapes=[pltpu.VMEM((2, t, AUG_W), BF16), pltpu.VMEM((2, WINDOW + t, AUG_W), BF16),
                        pltpu.VMEM((2, ncp, AUG_W), BF16),
                        pltpu.VMEM((2, t, LANES), BF16), pltpu.VMEM((2, WINDOW + t, LANES), BF16),
                        pltpu.VMEM((2, rows_all, AUG_W), BF16), pltpu.VMEM((2, rows_all, AUG_W), BF16),
                        pltpu.VMEM((2, rows_all, AUG_W), BF16),
                        pltpu.VMEM((2, rows_all, LANES), F32), pltpu.VMEM((2, rows_all, LANES), F32),
                        pltpu.VMEM((2, rows_all, LANES), F32)],
        compiler_params=_cparams("parallel", "parallel", "arbitrary"),
        name="nsa_attn",
    )(slope_feats, q, gates, kc, vc, kvb, kvb, kvb, kvb)


def _slope_feats():
    s = _alibi_slopes()
    hi = s.astype(BF16).astype(F32)
    mid = (s - hi).astype(BF16).astype(F32)
    lo = (s - hi - mid).astype(BF16).astype(F32)
    z = jnp.zeros_like(s)
    return jnp.stack([SEL_BLK * hi, SEL_BLK * mid, SEL_BLK * lo, hi, mid, lo, z, z], axis=1)


WIN_JOBS = 3
WIN_PAD = WIN_JOBS * KEY_CHUNK - Q_BLK
PAD_FEAT = N_FEAT
NO_SPAN = 1 << 30


def _nsa_attn5_kernel(sf_ref, q_ref, gate_ref, kc_ref, vc_ref, ksel_ref, vsel_ref, kwin_ref, vwin_ref, o_ref,
                      k_scr, v_scr, kc_scr, qa_scr, qc_scr, s_scr, m_scr, acc_scr, *, t):
    pr = pl.program_id(1)
    nblk = pl.program_id(2)
    qb = Q_BLK
    s0 = nblk * qb
    ncp = t // CMP_STRIDE
    ns = t // SEL_BLK
    hd = HEAD_DIM
    w = KEY_CHUNK
    hpg = NSA_HPG
    rows_all = hpg * qb
    wrow0 = t

    @pl.when(nblk == 0)
    def _():
        lane = lax.broadcasted_iota(jnp.int32, (t, LANES), 1)
        row = lax.broadcasted_iota(jnp.int32, (t, 1), 0)
        feats = _key_feats(row).astype(BF16)
        onehot = jnp.where(lane - hd == lax.shift_right_logical(row, 6), 1.0, 0.0)
        lane_p = lax.broadcasted_iota(jnp.int32, (WIN_PAD, LANES), 1)
        lane_c = lax.broadcasted_iota(jnp.int32, (ncp, LANES), 1)
        cend = CMP_STRIDE * lax.broadcasted_iota(jnp.int32, (ncp, 1), 0) + (CMP_LEN - 1)
        ks = ksel_ref[...].astype(F32)
        vs = vsel_ref[...].astype(F32)
        kwn = kwin_ref[...].astype(F32)
        vwn = vwin_ref[...].astype(F32)
        kcv = kc_ref[...].astype(F32)
        for half in range(2):
            low = (lambda x: x) if half == 0 else (lambda x: pltpu.roll(x, hd, 1))
            k_scr[half, 0:t, 0:LANES] = jnp.where(lane < hd, low(ks), onehot).astype(BF16)
            k_scr[half, 0:t, LANES:AUG_W] = feats
            v_scr[half, 0:t] = jnp.where(lane < hd, low(vs), 1.0).astype(BF16)
            k_scr[half, wrow0:wrow0 + WIN_PAD, 0:LANES] = jnp.zeros((WIN_PAD, LANES), BF16)
            k_scr[half, wrow0:wrow0 + WIN_PAD, LANES:AUG_W] = jnp.where(lane_p == PAD_FEAT, 1.0, 0.0).astype(BF16)
            v_scr[half, wrow0:wrow0 + WIN_PAD] = jnp.ones((WIN_PAD, LANES), BF16)
            wr = wrow0 + WIN_PAD
            k_scr[half, wr:wr + t, 0:LANES] = jnp.where(lane < hd, low(kwn), 0.0).astype(BF16)
            k_scr[half, wr:wr + t, LANES:AUG_W] = feats
            v_scr[half, wr:wr + t] = jnp.where(lane < hd, low(vwn), 1.0).astype(BF16)
            kc_scr[half, :, 0:LANES] = jnp.where(lane_c < hd, low(kcv), 0.0).astype(BF16)
            kc_scr[half, :, LANES:AUG_W] = _key_feats(cend).astype(BF16)

    sg = jax.nn.sigmoid(gate_ref[...])
    lane = lax.broadcasted_iota(jnp.int32, (qb, LANES), 1)
    lane1 = lax.broadcasted_iota(jnp.int32, (1, LANES), 1)
    t_loc = lax.broadcasted_iota(jnp.int32, (qb, 1), 0)
    tq_col = s0 + t_loc

    for half in range(2):
        for hh in range(hpg):
            u = half * hpg + hh
            qcol = q_ref[:, (u // 2) * LANES:(u // 2 + 1) * LANES].astype(F32)
            qlow = jnp.where(lane < hd, qcol if u % 2 == 0 else pltpu.roll(qcol, hd, 1), 0.0)
            feat = jnp.where(lane1 == PAD_FEAT, MASK_NEG, 0.0)
            for j in range(N_FEAT):
                feat = jnp.where(lane1 == j, sf_ref[(pr * 2 + half) * hpg + hh, j], feat)
            feat = jnp.broadcast_to(feat, (qb, LANES)).astype(BF16)
            rows = slice(hh * qb, (hh + 1) * qb)
            qc_scr[half, rows, 0:LANES] = qlow.astype(BF16)
            qc_scr[half, rows, LANES:AUG_W] = feat
            qa_scr[half, rows, LANES:AUG_W] = feat

    cend_row = CMP_STRIDE * lax.broadcasted_iota(jnp.int32, (1, ncp), 1) + (CMP_LEN - 1)
    add_c = _tile_rows(jnp.where(tq_col >= cend_row, 0.0, MASK_NEG), hpg)
    jj = lax.broadcasted_iota(jnp.int32, (ns, ncp), 1) * CMP_STRIDE
    nn = lax.broadcasted_iota(jnp.int32, (ns, ncp), 0) * SEL_BLK
    ov_t = ((jj < nn + SEL_BLK) & (jj + CMP_LEN > nn)).astype(BF16)
    bid = lax.broadcasted_iota(jnp.int32, (ns, qb), 0)
    tq_row = s0 + lax.broadcasted_iota(jnp.int32, (ns, qb), 1)
    cur = tq_row // SEL_BLK
    valid_b = bid * SEL_BLK <= tq_row
    bonus = jnp.where((bid == 0) | (bid == cur) | (bid == cur - 1), FORCE_BONUS, 0.0)
    o_cmp = []
    for half in range(2):
        s = _dot_nt(qc_scr[half], kc_scr[half]) + add_c
        m = jnp.max(s, axis=-1, keepdims=True)
        e = jnp.where(s > 0.5 * MASK_NEG, jnp.exp(s - m), 0.0)
        pc = e * (1.0 / jnp.maximum(jnp.sum(e, axis=-1, keepdims=True), 1e-30))
        o_cmp.append(_dot(pc.astype(BF16), vc_ref[...])[:, half * hd:(half + 1) * hd])
        pc_sum = pc[0:qb]
        for hh in range(1, hpg):
            pc_sum = pc_sum + pc[hh * qb:(hh + 1) * qb]
        hi = pc_sum.astype(BF16)
        lo = (pc_sum - hi.astype(F32)).astype(BF16)
        imp_t = _dot_nt(ov_t, hi) + _dot_nt(ov_t, lo)
        sel_t = _rank_topk_t(jnp.where(valid_b, imp_t + bonus, MASK_NEG), ns, qb)
        pieces = [jnp.zeros((hd, qb), F32), jnp.where(sel_t, 0.0, MASK_NEG)]
        if LANES - hd - ns > 0:
            pieces.append(jnp.zeros((LANES - hd - ns, qb), F32))
        mneg = jnp.concatenate(pieces, axis=0).T
        for hh in range(hpg):
            rows = slice(hh * qb, (hh + 1) * qb)
            qa_scr[half, rows, 0:LANES] = (qc_scr[half, rows, 0:LANES].astype(F32) + mneg).astype(BF16)

    n_sel = nblk // (w // qb) + 1
    n_jobs = n_sel + WIN_JOBS
    col = lax.broadcasted_iota(jnp.int32, (1, w), 1)

    def job(j):
        is_win = j >= n_sel
        jw = j - n_sel
        r0 = jnp.where(is_win, wrow0 + s0 + jw * w, j * w)
        base = jnp.where(is_win, s0 + qb - WIN_JOBS * w + jw * w, j * w)
        span = jnp.where(is_win, WINDOW, NO_SPAN)
        return pl.multiple_of(r0, qb), base, span, is_win.astype(jnp.int32)

    def scores(j, slot):
        r0, base, span, _ = job(jnp.minimum(j, n_jobs - 1))
        dist = tq_col - (base + col)
        ok = (dist >= 0) & (dist < span) & (j < n_jobs)
        add = _tile_rows(jnp.where(ok, 0.0, MASK_NEG), hpg)
        for half in range(2):
            s_scr[slot, half] = _dot_nt(qa_scr[half], k_scr[half, pl.ds(r0, w), :]) + add

    def consume(j, slot):
        r0, _, _, st = job(jnp.minimum(j, n_jobs - 1))
        for half in range(2):
            s = s_scr[slot, half]
            mx = jnp.maximum(s[:, 0:LANES], s[:, LANES:w])
            m_old = m_scr[st, half]
            m_new = jnp.maximum(m_old, jnp.max(mx, axis=-1, keepdims=True))
            alpha = jnp.exp(m_old - m_new)
            p = jnp.exp(s - jnp.concatenate([m_new, m_new], axis=1))
            acc_scr[st, half] = alpha * acc_scr[st, half] + _dot(p.astype(BF16), v_scr[half, pl.ds(r0, w), :])
            m_scr[st, half] = m_new

    m_scr[...] = jnp.full(m_scr.shape, MASK_NEG, F32)
    acc_scr[...] = jnp.zeros(acc_scr.shape, F32)
    scores(0, 0)

    def step(i, carry):
        scores(2 * i + 1, 1)
        consume(2 * i, 0)
        scores(2 * i + 2, 0)
        consume(2 * i + 1, 1)
        return carry

    lax.fori_loop(0, (n_jobs + 1) // 2, step, 0)

    for half in range(2):
        outs = []
        for st in range(2):
            a = acc_scr[st, half]
            outs.append(a[:, 0:hd] * (1.0 / a[:, hd:hd + 1]))
        for hh in range(hpg):
            u = half * hpg + hh
            gi = u * 3
            rows = slice(hh * qb, (hh + 1) * qb)
            out = (sg[:, gi:gi + 1] * o_cmp[half][rows] + sg[:, gi + 1:gi + 2] * outs[0][rows]
                   + sg[:, gi + 2:gi + 3] * outs[1][rows])
            o_ref[:, u * hd:(u + 1) * hd] = out.astype(BF16)


def _nsa_attention5(slope_feats, q, gates, kc, vc, kvb, *, nb, t):
    assert WIN_JOBS * KEY_CHUNK >= WINDOW + Q_BLK and t % KEY_CHUNK == 0 and t // SEL_BLK <= LANES - HEAD_DIM
    nq = t // Q_BLK
    ncp = t // CMP_STRIDE
    pw = 2 * NSA_HPG * HEAD_DIM
    rows_all = NSA_HPG * Q_BLK
    krows = 2 * t + WIN_PAD
    kvcol = lambda c: (lambda b, p, n: (b, 2 * c + p))
    kern = functools.partial(_nsa_attn5_kernel, t=t)
    return pl.pallas_call(
        kern,
        grid=(nb, 2, nq),
        in_specs=[pl.BlockSpec(memory_space=pltpu.SMEM),
                  pl.BlockSpec((Q_BLK, pw), lambda b, p, n: (b * nq + n, p)),
                  pl.BlockSpec((Q_BLK, LANES), lambda b, p, n: (b * nq + n, p)),
                  pl.BlockSpec((None, None, ncp, PAIR_W), lambda b, p, n: (b, p, 0, 0)),
                  pl.BlockSpec((None, None, ncp, PAIR_W), lambda b, p, n: (b, p, 0, 0)),
                  pl.BlockSpec((t, PAIR_W), kvcol(2)), pl.BlockSpec((t, PAIR_W), kvcol(3)),
                  pl.BlockSpec((t, PAIR_W), kvcol(4)), pl.BlockSpec((t, PAIR_W), kvcol(5))],
        out_specs=pl.BlockSpec((Q_BLK, pw), lambda b, p, n: (b * nq + n, p)),
        out_shape=jax.ShapeDtypeStruct((nb * t, NSA_Q_W), BF16),
        scratch_shapes=[pltpu.VMEM((2, krows, AUG_W), BF16), pltpu.VMEM((2, krows, LANES), BF16),
                        pltpu.VMEM((2, ncp, AUG_W), BF16),
                        pltpu.VMEM((2, rows_all, AUG_W), BF16), pltpu.VMEM((2, rows_all, AUG_W), BF16),
                        pltpu.VMEM((2, 2, rows_all, KEY_CHUNK), F32),
                        pltpu.VMEM((2, 2, rows_all, LANES), F32), pltpu.VMEM((2, 2, rows_all, LANES), F32)],
        compiler_params=_cparams("parallel", "parallel", "arbitrary"),
        name="nsa_attn",
    )(slope_feats, q, gates, kc, vc, kvb, kvb, kvb, kvb)


PAGES_PER_STEP = 32


def _page_compress_kernel(pt_ref, cache_ref, wbd_ref, w2_ref, pos_ref, kvc_ref,
                          buf0, buf1, buf2, buf3, parts_scr, sem, *, n_pages, page):
    b = pl.program_id(0)
    hf = pl.program_id(1)
    bufs = (buf0, buf1, buf2, buf3)
    copies = []
    for i in range(PAGES_PER_STEP):
        pg = pt_ref[b * n_pages + hf * PAGES_PER_STEP + i]
        for cb in range(4):
            cp = pltpu.make_async_copy(cache_ref.at[pg, :, cb, :], bufs[cb].at[pl.ds(i * page, page), :], sem)
            cp.start()
            copies.append(cp)
    for cp in copies:
        cp.wait()

    nch = PAGES_PER_STEP * page // CMP_STRIDE

    def load_rows(l, c, p):
        return bufs[c * 2 + p][pl.ds(l, nch, stride=CMP_STRIDE), :]

    row0 = pl.multiple_of(hf * nch, nch)
    for c in range(2):
        for p in range(2):
            parts_scr[c * 2 + p, pl.ds(row0, nch), :] = _cmp_parts(load_rows, wbd_ref, c, p)

    @pl.when(hf == pl.num_programs(1) - 1)
    def _():
        for c in range(2):
            posb = _cmp_pos_bias(pos_ref, wbd_ref, c)
            for p in range(2):
                kvc_ref[c, p] = _cmp_finish(parts_scr[c * 2 + p], posb, w2_ref, c).astype(BF16)


def _page_compress(page_table, cache4, wbd, w2bd, pos):
    sb, n_pages = page_table.shape
    page = cache4.shape[1]
    ncp = n_pages * page // CMP_STRIDE
    rows = PAGES_PER_STEP * page
    kern = functools.partial(_page_compress_kernel, n_pages=n_pages, page=page)
    grid_spec = pltpu.PrefetchScalarGridSpec(
        num_scalar_prefetch=1,
        grid=(sb, n_pages // PAGES_PER_STEP),
        in_specs=[pl.BlockSpec(memory_space=pl.ANY),
                  pl.BlockSpec(wbd.shape, lambda b, h, pt: (0,) * wbd.ndim),
                  pl.BlockSpec(w2bd.shape, lambda b, h, pt: (0,) * w2bd.ndim),
                  pl.BlockSpec(pos.shape, lambda b, h, pt: (0,) * pos.ndim)],
        out_specs=pl.BlockSpec((None, 2, 2, ncp, PAIR_W), lambda b, h, pt: (b, 0, 0, 0, 0)),
        scratch_shapes=[pltpu.VMEM((rows, PAIR_W), F32)] * 4
        + [pltpu.VMEM((4, ncp, 2 * PAIR_W), F32), pltpu.SemaphoreType.DMA(())],
    )
    return pl.pallas_call(
        kern,
        grid_spec=grid_spec,
        out_shape=jax.ShapeDtypeStruct((sb, 2, 2, ncp, PAIR_W), BF16),
        compiler_params=_cparams("arbitrary", "arbitrary"),
        name="nsa_page_compress",
    )(page_table.reshape(-1), cache4, wbd, w2bd, pos)


def _row_slopes(slope_ref, g, rows):
    hh = lax.broadcasted_iota(jnp.int32, (rows, 1), 0) % NSA_HPG
    col = jnp.zeros((rows, 1), F32)
    for h in range(NSA_HPG):
        col = jnp.where(hh == h, slope_ref[g * NSA_HPG + h], col)
    return col


def _sample_cmp_kernel(slope_ref, q_ref, kvc_ref, ocmp_ref, imp_ref, *, past, nsp):
    rows = q_ref.shape[1]
    ncp = kvc_ref.shape[2]
    t_col = lax.broadcasted_iota(jnp.int32, (rows, 1), 0) // NSA_HPG
    jrow = lax.broadcasted_iota(jnp.int32, (1, ncp), 1)
    dist = (past + t_col) - (CMP_STRIDE * jrow + (CMP_LEN - 1))
    mask = (dist >= 0) & (jrow < ncp - 1)
    distf = dist.astype(F32)
    ri = lax.broadcasted_iota(jnp.int32, (rows, rows), 0) // NSA_HPG
    ci = lax.broadcasted_iota(jnp.int32, (rows, rows), 1) // NSA_HPG
    same_t = (ri == ci).astype(BF16)
    jj = lax.broadcasted_iota(jnp.int32, (ncp, nsp), 0) * CMP_STRIDE
    nn = lax.broadcasted_iota(jnp.int32, (ncp, nsp), 1) * SEL_BLK
    ov = ((jj < nn + SEL_BLK) & (jj + CMP_LEN > nn)).astype(BF16)
    for g in range(NSA_KV_HEADS):
        pr, half = divmod(g, 2)
        lanes = slice(half * HEAD_DIM, (half + 1) * HEAD_DIM)
        s_c = _dot_nt(q_ref[g], kvc_ref[0, pr, :, lanes])
        s = jnp.where(mask, s_c - _row_slopes(slope_ref, g, rows) * distf, MASK_NEG)
        e = jnp.where(mask, jnp.exp(s - jnp.max(s, axis=-1, keepdims=True)), 0.0)
        pc = e / jnp.maximum(jnp.sum(e, axis=-1, keepdims=True), 1e-30)
        ocmp_ref[g] = _dot(pc.astype(BF16), kvc_ref[1, pr])[:, lanes]
        hi = pc.astype(BF16)
        lo = (pc - hi.astype(F32)).astype(BF16)
        pcs = _dot(same_t, hi) + _dot(same_t, lo)
        hi = pcs.astype(BF16)
        lo = (pcs - hi.astype(F32)).astype(BF16)
        imp_ref[g] = _dot(hi, ov) + _dot(lo, ov)


def _sample_cmp(slopes, q_rows, kvc, *, past, nsp):
    sb, ng, rows, hd = q_rows.shape
    ncp = kvc.shape[3]
    kern = functools.partial(_sample_cmp_kernel, past=past, nsp=nsp)
    return pl.pallas_call(
        kern,
        grid=(sb,),
        in_specs=[pl.BlockSpec(memory_space=pltpu.SMEM),
                  pl.BlockSpec((None, ng, rows, hd), lambda b: (b, 0, 0, 0)),
                  pl.BlockSpec((None, 2, 2, ncp, PAIR_W), lambda b: (b, 0, 0, 0, 0))],
        out_specs=[pl.BlockSpec((None, ng, rows, hd), lambda b: (b, 0, 0, 0)),
                   pl.BlockSpec((None, ng, rows, nsp), lambda b: (b, 0, 0, 0))],
        out_shape=[jax.ShapeDtypeStruct((sb, ng, rows, hd), F32),
                   jax.ShapeDtypeStruct((sb, ng, rows, nsp), F32)],
        compiler_params=_cparams("parallel"),
        name="nsa_sample_cmp",
    )(slopes, q_rows, kvc)


def _sample_topk_kernel(imp_ref, idx_ref, score_scr, *, past, n_sel, st):
    nsr, ncol = imp_ref.shape
    bid = lax.broadcasted_iota(jnp.int32, (nsr, ncol), 0)
    tq = past + lax.broadcasted_iota(jnp.int32, (nsr, ncol), 1) % st
    cur = tq // SEL_BLK
    valid = (bid * SEL_BLK <= tq) & (bid < n_sel)
    forced = (bid == 0) | (bid == cur) | (bid == cur - 1)
    score = jnp.where(valid, imp_ref[...] + jnp.where(forced, FORCE_BONUS, 0.0), MASK_NEG)
    score_scr[...] = score

    def body(m, cnt):
        sm = score_scr[pl.ds(m, 1), :]
        ahead = (sm > score) | ((sm == score) & (m < bid))
        return cnt + ahead.astype(jnp.int32)

    cnt = lax.fori_loop(0, n_sel, body, jnp.zeros((nsr, ncol), jnp.int32))
    sel = (cnt < SEL_TOPK) & (bid < n_sel)
    tril = (lax.broadcasted_iota(jnp.int32, (nsr, nsr), 1) <= lax.broadcasted_iota(jnp.int32, (nsr, nsr), 0))
    prefix = _dot(tril.astype(BF16), sel.astype(BF16))
    for k in range(SEL_TOPK):
        hit = sel & (prefix == float(k + 1))
        idx_ref[k:k + 1, :] = jnp.sum(jnp.where(hit, bid, 0), axis=0, keepdims=True)


def _sample_topk(imp_t, *, past, n_sel, st):
    nsr, ncol = imp_t.shape
    kern = functools.partial(_sample_topk_kernel, past=past, n_sel=n_sel, st=st)
    return pl.pallas_call(
        kern,
        grid=(1,),
        in_specs=[_full(imp_t.shape)],
        out_specs=_full((SEL_TOPK, ncol)),
        out_shape=jax.ShapeDtypeStruct((SEL_TOPK, ncol), jnp.int32),
        scratch_shapes=[pltpu.VMEM((nsr, ncol), F32)],
        compiler_params=_cparams("arbitrary"),
        name="nsa_sample_topk",
    )(imp_t)


def _joint_softmax_pv(s_a, v_a_fn, s_b, v_b):
    m = jnp.maximum(jnp.max(s_a, axis=-1, keepdims=True), jnp.max(s_b, axis=-1, keepdims=True))
    p_a = jnp.where(s_a > 0.5 * MASK_NEG, jnp.exp(s_a - m), 0.0)
    p_b = jnp.where(s_b > 0.5 * MASK_NEG, jnp.exp(s_b - m), 0.0)
    l = jnp.sum(p_a, axis=-1, keepdims=True) + jnp.sum(p_b, axis=-1, keepdims=True)
    acc = v_a_fn(p_a.astype(BF16)) + _dot(p_b.astype(BF16), v_b)
    return acc / jnp.maximum(l, 1e-30)


def _sample_attn_kernel(idx_ref, pt_ref, slope_ref, cache_ref, q_ref, idxv_ref, e16_ref, gate_ref, ocmp_ref,
                        ksn_ref, vsn_ref, kwn_ref, vwn_ref, kwin_ref, vwin_ref, o_ref, kbuf, vbuf, sem,
                        *, past, n_pages, st):
    b = pl.program_id(0)
    pr = pl.program_id(1)
    rows = q_ref.shape[1]
    nkey = SEL_TOPK * SEL_BLK
    last_blk = past // SEL_BLK - 1

    copies = []
    for half in range(2):
        g = pr * 2 + half
        for t in range(st):
            for k in range(SEL_TOPK):
                blk = jnp.minimum(idx_ref[((b * NSA_KV_HEADS + g) * st + t) * SEL_TOPK + k], last_blk)
                pg = pt_ref[b * n_pages + lax.shift_right_logical(blk, 1)]
                r0 = pl.multiple_of((blk & 1) * SEL_BLK, SEL_BLK)
                dst = pl.ds(k * SEL_BLK, SEL_BLK)
                for cb, buf in ((4, kbuf), (6, vbuf)):
                    cp = pltpu.make_async_copy(cache_ref.at[pg, pl.ds(r0, SEL_BLK), cb + pr, :],
                                               buf.at[half, t, dst, :], sem)
                    cp.start()
                    copies.append(cp)
    for cp in copies:
        cp.wait()

    t_col = lax.broadcasted_iota(jnp.int32, (rows, 1), 0) // NSA_HPG
    tqf = (past + t_col).astype(F32)
    off = (lax.broadcasted_iota(jnp.int32, (1, nkey), 1) % SEL_BLK).astype(F32)
    tp = lax.broadcasted_iota(jnp.int32, (1, ksn_ref.shape[0]), 1)
    dn = t_col - tp
    valid_n = (dn >= 0) & (tp < st)
    dnf = dn.astype(F32)
    wb = kwin_ref.shape[0]
    dist_w = (wb + t_col) - lax.broadcasted_iota(jnp.int32, (1, wb), 1)
    valid_w = (dist_w >= 0) & (dist_w < WINDOW)

    for half in range(2):
        g = pr * 2 + half
        lanes = slice(half * HEAD_DIM, (half + 1) * HEAD_DIM)
        q = q_ref[half]
        slope = _row_slopes(slope_ref, g, rows)

        blk = _dot(idxv_ref[half], e16_ref[...])
        is_new = blk > (last_blk + 0.5)
        dist = tqf - (blk * SEL_BLK + off)
        valid = jnp.logical_not(is_new) & (dist >= 0)
        s_sel = jnp.full((rows, nkey), MASK_NEG, F32)
        for t in range(st):
            s_t = _dot_nt(q, kbuf[half, t, :, lanes].astype(BF16))
            s_sel = jnp.where(t_col == t, s_t, s_sel)
        s_sel = jnp.where(valid, s_sel - slope * dist, MASK_NEG)
        has_new = jnp.max(is_new.astype(F32), axis=-1, keepdims=True) > 0.5
        s_new = jnp.where(valid_n & has_new, _dot_nt(q, ksn_ref[:, lanes]) - slope * dnf, MASK_NEG)

        def pv_sel(p):
            acc = jnp.zeros((rows, PAIR_W), F32)
            for t in range(st):
                acc = jnp.where(t_col == t, _dot(p, vbuf[half, t].astype(BF16)), acc)
            return acc

        o_sel = _joint_softmax_pv(s_sel, pv_sel, s_new, vsn_ref[...])[:, lanes]

        s_w = _dot_nt(q, kwin_ref[:, lanes].astype(BF16))
        s_w = jnp.where(valid_w, s_w - slope * dist_w.astype(F32), MASK_NEG)
        s_wn = jnp.where(valid_n, _dot_nt(q, kwn_ref[:, lanes]) - slope * dnf, MASK_NEG)
        vw = vwin_ref[...].astype(BF16)
        o_w = _joint_softmax_pv(s_w, lambda p: _dot(p, vw), s_wn, vwn_ref[...])[:, lanes]

        sg = jax.nn.sigmoid(gate_ref[half])
        o_ref[half] = sg[:, 0:1] * ocmp_ref[half] + sg[:, 1:2] * o_sel + sg[:, 2:3] * o_w


def _sample_attention(idx_flat, pt_flat, slopes, cache4, q_rows, idx_rows, e16, gate_rows, ocmp, new16, win_buf,
                      *, past, n_pages, st):
    sb, ng, rows, hd = q_rows.shape
    wb = win_buf.shape[1]
    nkey = SEL_TOPK * SEL_BLK
    pairg = lambda w: pl.BlockSpec((None, 2, rows, w), lambda b, p, *_: (b, p, 0, 0))
    newc = lambda c: pl.BlockSpec((None, new16.shape[1], PAIR_W), lambda b, p, *_: (b, 0, 2 * c + p))
    kern = functools.partial(_sample_attn_kernel, past=past, n_pages=n_pages, st=st)
    grid_spec = pltpu.PrefetchScalarGridSpec(
        num_scalar_prefetch=2,
        grid=(sb, 2),
        in_specs=[pl.BlockSpec(memory_space=pltpu.SMEM), pl.BlockSpec(memory_space=pl.ANY),
                  pairg(hd), pairg(LANES), pl.BlockSpec(e16.shape, lambda b, p, *_: (0, 0)), pairg(LANES), pairg(hd),
                  newc(2), newc(3), newc(4), newc(5),
                  pl.BlockSpec((None, wb, PAIR_W), lambda b, p, *_: (b, 0, p)),
                  pl.BlockSpec((None, wb, PAIR_W), lambda b, p, *_: (b, 0, 2 + p))],
        out_specs=pairg(hd),
        scratch_shapes=[pltpu.VMEM((2, st, nkey, PAIR_W), F32), pltpu.VMEM((2, st, nkey, PAIR_W), F32),
                        pltpu.SemaphoreType.DMA(())],
    )
    return pl.pallas_call(
        kern,
        grid_spec=grid_spec,
        out_shape=jax.ShapeDtypeStruct((sb, ng, rows, hd), F32),
        compiler_params=_cparams("arbitrary", "arbitrary"),
        name="nsa_sample_attn",
    )(idx_flat, pt_flat, slopes, cache4, q_rows, idx_rows, e16, gate_rows, ocmp, new16, new16, new16, new16,
      win_buf, win_buf)


def _pad_cols(w, mult=LANES):
    n = w.shape[1]
    return jnp.pad(w, ((0, 0), (0, -n % mult)))


def _prep_nsa_in(w_in):
    body = w_in[:, :NSA_Q_W + 6 * NSA_KV_W]
    gates = w_in[:, NSA_Q_W + 6 * NSA_KV_W:]
    per_pair = 2 * NSA_HPG * 3
    blocks = [_pad_cols(gates[:, p * per_pair:(p + 1) * per_pair]) for p in range(2)]
    return jnp.concatenate([body] + blocks, axis=1).astype(BF16)


def _prep_cmp(w_cmp1, w_cmp2, cmp_pos):
    hd = HEAD_DIM
    z = jnp.zeros((2, CMP_STRIDE, hd, hd), F32)
    wa, wb = w_cmp1[:, :CMP_STRIDE], w_cmp1[:, CMP_STRIDE:]
    top = jnp.concatenate([wa, z, wb, z], axis=-1)
    bot = jnp.concatenate([z, wa, z, wb], axis=-1)
    wbd = jnp.concatenate([top, bot], axis=-2).astype(BF16)
    z2 = jnp.zeros((2, hd, hd), F32)
    w2bd = jnp.concatenate([jnp.concatenate([w_cmp2, z2], -1), jnp.concatenate([z2, w_cmp2], -1)], -2).astype(BF16)
    pos = cmp_pos.transpose(1, 0, 2)
    pos = jnp.concatenate([pos, pos], axis=-1)
    rows = jnp.stack([pos[:, :CMP_STRIDE], pos[:, CMP_STRIDE:]], axis=2)
    pos_tiles = jnp.pad(rows, ((0, 0), (0, 0), (0, SUBLANES - 2), (0, 0)))
    return wbd, w2bd, pos_tiles


def _alibi_slopes():
    h = jnp.arange(1, NSA_HEADS + 1, dtype=F32)
    return jnp.exp2(-8.0 * h / NSA_HEADS)


def _sel_expand(t):
    key_blk = jnp.arange(t, dtype=jnp.int32).reshape(t // SEL_CHUNK, 1, SEL_CHUNK) // SEL_BLK
    blk = jnp.arange(t // SEL_BLK, dtype=jnp.int32).reshape(1, t // SEL_BLK, 1)
    return (key_blk == blk).astype(BF16)


def _prompt_rows_tile(m):
    for tm in (512, 256, 128, 64, 32, 16, 8):
        if m % tm == 0:
            return tm
    raise ValueError(m)


def kernel(x_prompt, x_sample, cache_nsa_kv, cache_nsa_win, state_mlstm_C, state_mlstm_n, state_mlstm_m,
           state_ffn_conv, page_table, norm_g, w_nsa_in, w_nsa_out, w_cmp1, w_cmp2, cmp_pos, w_ml_in,
           b_ml_gate, ml_head_norm, w_ml_out, w_ffn_up, ffn_conv_w, ffn_conv_b, w_ffn_down):
    nb, t, d = x_prompt.shape
    sb, st, _ = x_sample.shape
    mp = nb * t
    tm = _prompt_rows_tile(mp)
    slopes = _alibi_slopes()

    w_in0 = _prep_nsa_in(w_nsa_in[0])
    w_out0 = w_nsa_out[0].astype(BF16)
    wbd, w2bd, pos_tiles = _prep_cmp(w_cmp1[0], w_cmp2[0], cmp_pos[0])
    w_ml = _pad_cols(w_ml_in[0]).astype(BF16)
    w_mlo = w_ml_out[0].astype(BF16)
    w_up = w_ffn_up.astype(BF16)
    w_dn = w_ffn_down.astype(BF16)
    bg_row = _pad_cols(b_ml_gate[0].reshape(1, 2 * ML_HEADS))
    g = norm_g[:, :, None, :]

    xp = x_prompt.reshape(mp, d)
    q, kva, kvb, win, gates = _nsa_project(xp, g[0, 0], w_in0, tm)
    kc, vc = _nsa_compress(kva, wbd, w2bd, pos_tiles, nb=nb, t=t)
    o = _nsa_attention5(_slope_feats(), q, gates, kc, vc, kvb, nb=nb, t=t)
    xp = _out_project(o, xp, w_out0, g[0, 1], tm)
    kv_p = kva.reshape(1, nb, t, 4, NSA_KV_HEADS, HEAD_DIM)
    w_keep = min(WINDOW, t)
    win_p = win.reshape(nb, t, 2, NSA_KV_HEADS, HEAD_DIM)[None, :, t - w_keep:]

    halo = SUBLANES
    zstate = jnp.zeros((nb, halo, 2 * D_FF), F32)
    tmf = min(t, 1024)
    conv_p = []

    def ffn_prompt(xp, i):
        xo, ta, tb = _conv_ffn(xp, zstate, g[i, 2], g[i, 3], w_up[i], ffn_conv_w[i], ffn_conv_b[i][None],
                               w_dn[i], tm=tmf, fc=512, shift=1, tiles_per_seq=t // tmf)
        tail = jnp.concatenate([ta, tb], axis=-1).reshape(nb, t // tmf, halo, 2 * D_FF)
        conv_p.append(tail[:, -1, halo - (CONV_W - 1):])
        return xo

    xp = ffn_prompt(xp, 0)

    lp = math.gcd(t, ML_CHUNK)
    qkv, om, gm = _ml_project(xp, g[1, 0], w_ml, tm)
    zc = jnp.zeros((nb, ML_HEADS, ML_DV, ML_DQK), F32)
    zn = jnp.zeros((nb, ML_HEADS, 1, ML_DQK), F32)
    zm = jnp.zeros((nb, ML_HEADS, 1, 1), F32)
    ym, c_p, n_p, m_p = _mlstm(qkv, om, gm, bg_row, ml_head_norm[0][None], zc, zn, zm, nb=nb, lp=lp, lv=lp)
    xp = _out_project(ym, xp, w_mlo, g[1, 1], tm)
    xp = ffn_prompt(xp, 1)

    ms = sb * st
    ng, hpg, hd = NSA_KV_HEADS, NSA_HPG, HEAD_DIM
    xs = x_sample.transpose(1, 0, 2).reshape(ms, d)
    qs, kva_s, kvb_s, win_s, gates_s = _nsa_project(xs, g[0, 0], w_in0, ms)
    n_phys, page = cache_nsa_kv.shape[1:3]
    n_pages = page_table.shape[1]
    past = n_pages * page
    cache4 = cache_nsa_kv[0].reshape(n_phys, page, 8, LANES)
    kvc = _page_compress(page_table, cache4, wbd, w2bd, pos_tiles)

    rows = st * hpg
    q_rows = qs.reshape(st, sb, ng, hpg, hd).transpose(1, 2, 0, 3, 4).reshape(sb, ng, rows, hd)
    n_sel = -(-(past + st) // SEL_BLK)
    nsp = -(-n_sel // LANES) * LANES
    nsr = -(-n_sel // SUBLANES) * SUBLANES
    ocmp, imp = _sample_cmp(slopes, q_rows, kvc, past=past, nsp=nsp)
    imp_t = imp[:, :, ::hpg, :nsr].reshape(sb * ng * st, nsr).T
    idx = _sample_topk(imp_t, past=past, n_sel=n_sel, st=st)
    idx_bgtk = idx.T.reshape(sb, ng, st, SEL_TOPK)
    idx_rows = _pad_cols(jnp.repeat(idx_bgtk, hpg, axis=2).reshape(sb * ng * rows, SEL_TOPK))
    idx_rows = idx_rows.reshape(sb, ng, rows, LANES).astype(BF16)
    e16 = (jnp.arange(LANES, dtype=jnp.int32)[:, None]
           == jnp.arange(SEL_TOPK * SEL_BLK, dtype=jnp.int32)[None, :] // SEL_BLK).astype(BF16)
    gate_rows = gates_s.reshape(st, sb, 2, LANES)[..., :2 * hpg * 3].reshape(st, sb, ng, hpg, 3)
    gate_rows = _pad_cols(gate_rows.transpose(1, 2, 0, 3, 4).reshape(sb * ng * rows, 3)).reshape(sb, ng, rows, LANES)
    new16 = jnp.pad(kvb_s.reshape(st, sb, -1).transpose(1, 0, 2), ((0, 0), (0, 16 - st), (0, 0)))
    wb = cache_nsa_win.shape[2]
    win_buf = cache_nsa_win[0].reshape(sb, wb, 2 * NSA_KV_W)
    o_s = _sample_attention(idx_bgtk.reshape(-1), page_table.reshape(-1), slopes, cache4, q_rows, idx_rows, e16,
                            gate_rows, ocmp, new16, win_buf, past=past, n_pages=n_pages, st=st)
    o_s = o_s.reshape(sb, ng, st, hpg, hd).transpose(2, 0, 1, 3, 4).reshape(ms, NSA_Q_W).astype(BF16)
    xs = _out_project(o_s, xs, w_out0, g[0, 1], ms)
    kv_s = kva_s.reshape(st, sb, 4, ng, hd).transpose(1, 0, 2, 3, 4)[None]
    win_new = win_s.reshape(st, sb, 2, ng, hd).transpose(1, 0, 2, 3, 4)
    win_s_out = jnp.concatenate([cache_nsa_win[0], win_new], axis=1)[None, :, st:]

    conv_s = []

    def ffn_sample(xs, i):
        state = state_ffn_conv[i].transpose(1, 0, 2).reshape(1, (CONV_W - 1) * sb, 2 * D_FF)
        xo, ta, tb = _conv_ffn(xs, state, g[i, 2], g[i, 3], w_up[i], ffn_conv_w[i], ffn_conv_b[i][None],
                               w_dn[i], tm=ms, fc=512, shift=sb, tiles_per_seq=1)
        tail = jnp.concatenate([ta, tb], axis=-1).reshape(CONV_W - 1, sb, 2 * D_FF)
        conv_s.append(tail.transpose(1, 0, 2))
        return xo

    xs = ffn_sample(xs, 0)

    lps = SUBLANES
    to_seq = lambda a: jnp.pad(a.reshape(st, sb, -1).transpose(1, 0, 2),
                               ((0, 0), (0, lps - st), (0, 0))).reshape(sb * lps, -1)
    qkv_s, om_s, gm_s = _ml_project(xs, g[1, 0], w_ml, ms)
    ym_s, c_s, n_s, m_s = _mlstm(to_seq(qkv_s), to_seq(om_s), to_seq(gm_s), bg_row, ml_head_norm[0][None],
                                 state_mlstm_C[0], state_mlstm_n[0][:, :, None, :],
                                 state_mlstm_m[0][:, :, None, None], nb=sb, lp=lps, lv=math.gcd(st, ML_CHUNK))
    ym_s = ym_s.reshape(sb, lps, ML_V_W)[:, :st].transpose(1, 0, 2).reshape(ms, ML_V_W)
    xs = _out_project(ym_s, xs, w_mlo, g[1, 1], ms)
    xs = ffn_sample(xs, 1)

    return (xp.reshape(nb, t, d), xs.reshape(st, sb, d).transpose(1, 0, 2), kv_p, kv_s, win_p, win_s_out,
            c_p[None], c_s[None], n_p[None, :, :, 0], n_s[None, :, :, 0],
            m_p[None, :, :, 0, 0], m_s[None, :, :, 0, 0], jnp.stack(conv_p), jnp.stack(conv_s))
```

```python
import functools
import math

import jax
import jax.numpy as jnp
from jax import lax
from jax.experimental import pallas as pl
from jax.experimental.pallas import tpu as pltpu

F32 = jnp.float32
BF16 = jnp.bfloat16

LANES = 128
SUBLANES = 8
VMEM_LIMIT_BYTES = 56 * 1024 * 1024

D_MODEL = 1024
NSA_HEADS = 16
NSA_KV_HEADS = 4
NSA_HPG = NSA_HEADS // NSA_KV_HEADS
HEAD_DIM = D_MODEL // NSA_HEADS
CMP_LEN = 32
CMP_STRIDE = 16
SEL_BLK = 64
SEL_TOPK = 16
WINDOW = 512
Q_BLK = 128
FORCE_BONUS = 1e4
NSA_Q_W = NSA_HEADS * HEAD_DIM
NSA_KV_W = NSA_KV_HEADS * HEAD_DIM
ML_HEADS = 8
ML_DQK = D_MODEL // (2 * ML_HEADS)
ML_DV = D_MODEL // ML_HEADS
ML_CHUNK = 64
ML_QK_W = ML_HEADS * ML_DQK
ML_V_W = ML_HEADS * ML_DV
D_FF = 4 * D_MODEL
CONV_W = 3
RMS_EPS = 1e-6
MASK_NEG = -1e30
ML_SEQ_PER_STEP = 4
SEL_CHUNK = 512
PAIR_W = 2 * HEAD_DIM


def _cparams(*sem):
    return pltpu.CompilerParams(dimension_semantics=sem, vmem_limit_bytes=VMEM_LIMIT_BYTES)


def _dot(a, b):
    return jnp.dot(a, b, preferred_element_type=F32)


def _dot_nt(a, b):
    return lax.dot_general(a, b, (((1,), (1,)), ((), ())), preferred_element_type=F32)


def _dot_tn(a, b):
    return lax.dot_general(a, b, (((0,), (0,)), ((), ())), preferred_element_type=F32)


def _split3(x):
    hi = x.astype(BF16)
    r1 = x - hi.astype(F32)
    mid = r1.astype(BF16)
    lo = (r1 - mid.astype(F32)).astype(BF16)
    return hi, mid, lo


def _rms(x, g):
    return x * lax.rsqrt(jnp.mean(x * x, axis=-1, keepdims=True) + RMS_EPS) * g


def _gelu_tanh(x):
    return 0.5 * x * (1.0 + jnp.tanh(0.7978845608028654 * (x + 0.044715 * (x * x * x))))


def _full(shape):
    return pl.BlockSpec(shape, lambda *_: (0,) * len(shape))


def _nsa_proj_kernel(x_ref, g_ref, w_ref, q_ref, kva_ref, kvb_ref, win_ref, gate_ref):
    h = _rms(x_ref[...], g_ref[...]).astype(BF16)
    p = _dot(h, w_ref[...])
    kvw = 4 * NSA_KV_W
    q_ref[...] = (p[:, :NSA_Q_W] * HEAD_DIM ** -0.5).astype(BF16)
    kva_ref[...] = p[:, NSA_Q_W:NSA_Q_W + kvw]
    kvb_ref[...] = p[:, NSA_Q_W:NSA_Q_W + 6 * NSA_KV_W].astype(BF16)
    win_ref[...] = p[:, NSA_Q_W + kvw:NSA_Q_W + 6 * NSA_KV_W]
    gate_ref[...] = p[:, NSA_Q_W + 6 * NSA_KV_W:]


def _nsa_project(x, g, w, tm):
    m = x.shape[0]
    n = w.shape[1]
    row = lambda i: (i, 0)
    return pl.pallas_call(
        _nsa_proj_kernel,
        grid=(m // tm,),
        in_specs=[pl.BlockSpec((tm, D_MODEL), row), _full((1, D_MODEL)), _full((D_MODEL, n))],
        out_specs=[pl.BlockSpec((tm, NSA_Q_W), row), pl.BlockSpec((tm, 4 * NSA_KV_W), row),
                   pl.BlockSpec((tm, 6 * NSA_KV_W), row), pl.BlockSpec((tm, 2 * NSA_KV_W), row),
                   pl.BlockSpec((tm, 2 * LANES), row)],
        out_shape=[jax.ShapeDtypeStruct((m, NSA_Q_W), BF16), jax.ShapeDtypeStruct((m, 4 * NSA_KV_W), F32),
                   jax.ShapeDtypeStruct((m, 6 * NSA_KV_W), BF16), jax.ShapeDtypeStruct((m, 2 * NSA_KV_W), F32),
                   jax.ShapeDtypeStruct((m, 2 * LANES), F32)],
        compiler_params=_cparams("parallel"),
        name="nsa_proj",
    )(x, g, w)


def _ml_proj_kernel(x_ref, g_ref, w_ref, qkv_ref, o_ref, gate_ref):
    h = _rms(x_ref[...], g_ref[...]).astype(BF16)
    p = _dot(h, w_ref[...])
    a = 2 * ML_QK_W + ML_V_W
    qkv_ref[:, :ML_QK_W] = p[:, :ML_QK_W].astype(BF16)
    qkv_ref[:, ML_QK_W:2 * ML_QK_W] = (p[:, ML_QK_W:2 * ML_QK_W] * ML_DQK ** -0.5).astype(BF16)
    qkv_ref[:, 2 * ML_QK_W:] = p[:, 2 * ML_QK_W:a].astype(BF16)
    o_ref[...] = p[:, a:a + ML_V_W]
    gate_ref[...] = p[:, a + ML_V_W:]


def _ml_project(x, g, w, tm):
    m = x.shape[0]
    n = w.shape[1]
    a = 2 * ML_QK_W + ML_V_W
    row = lambda i: (i, 0)
    return pl.pallas_call(
        _ml_proj_kernel,
        grid=(m // tm,),
        in_specs=[pl.BlockSpec((tm, D_MODEL), row), _full((1, D_MODEL)), _full((D_MODEL, n))],
        out_specs=[pl.BlockSpec((tm, a), row), pl.BlockSpec((tm, ML_V_W), row), pl.BlockSpec((tm, LANES), row)],
        out_shape=[jax.ShapeDtypeStruct((m, a), BF16), jax.ShapeDtypeStruct((m, ML_V_W), F32),
                   jax.ShapeDtypeStruct((m, LANES), F32)],
        compiler_params=_cparams("parallel"),
        name="ml_proj",
    )(x, g, w)


def _out_proj_kernel(o_ref, x_ref, w_ref, g_ref, xo_ref):
    y = _dot(o_ref[...], w_ref[...])
    xo_ref[...] = x_ref[...] + _rms(y, g_ref[...])


def _out_project(o, x, w, g, tm):
    m = x.shape[0]
    row = lambda i: (i, 0)
    return pl.pallas_call(
        _out_proj_kernel,
        grid=(m // tm,),
        in_specs=[pl.BlockSpec((tm, o.shape[1]), row), pl.BlockSpec((tm, D_MODEL), row),
                  _full(w.shape), _full((1, D_MODEL))],
        out_specs=pl.BlockSpec((tm, D_MODEL), row),
        out_shape=jax.ShapeDtypeStruct((m, D_MODEL), F32),
        compiler_params=_cparams("parallel"),
        name="out_proj",
    )(o, x, w, g)


def _ffn_kernel(x_ref, sta_ref, stb_ref, g2_ref, g3_ref, wa_ref, wb_ref, cwa_ref, cwb_ref, cba_ref, cbb_ref,
                wd_ref, xo_ref, taila_ref, tailb_ref, h_scr, ua_scr, ub_scr, ca_scr, cb_scr, acc_scr,
                *, tm, halo, shift, tiles_per_seq):
    i = pl.program_id(0)
    c = pl.program_id(1)

    @pl.when(c == 0)
    def _():
        h_scr[...] = _rms(x_ref[...], g2_ref[...]).astype(BF16)
        acc_scr[...] = jnp.zeros_like(acc_scr)

    first = (i % tiles_per_seq) == 0

    @pl.when(first)
    def _():
        ua_scr[0:halo, :] = sta_ref[...]
        ub_scr[0:halo, :] = stb_ref[...]

    @pl.when(jnp.logical_not(first))
    def _():
        ua_scr[0:halo, :] = ca_scr[c]
        ub_scr[0:halo, :] = cb_scr[c]

    h = h_scr[...]
    ua_scr[halo:halo + tm, :] = _dot(h, wa_ref[...])
    ub_scr[halo:halo + tm, :] = _dot(h, wb_ref[...])
    ta = ua_scr[tm:tm + halo, :]
    tb = ub_scr[tm:tm + halo, :]
    ca_scr[c] = ta
    cb_scr[c] = tb
    taila_ref[...] = ta
    tailb_ref[...] = tb

    def conv(u_scr, cw_ref, cb_ref):
        cw = cw_ref[...]
        return (cb_ref[...] + cw[2:3, :] * u_scr[halo:halo + tm, :]
                + cw[1:2, :] * u_scr[halo - shift:halo - shift + tm, :]
                + cw[0:1, :] * u_scr[halo - 2 * shift:halo - 2 * shift + tm, :])

    y = _gelu_tanh(conv(ua_scr, cwa_ref, cba_ref)) * conv(ub_scr, cwb_ref, cbb_ref)
    acc_scr[...] += _dot(y.astype(BF16), wd_ref[...])

    @pl.when(c == pl.num_programs(1) - 1)
    def _():
        xo_ref[...] = x_ref[...] + _rms(acc_scr[...], g3_ref[...])


def _conv_ffn(x, state, g2, g3, w_up, conv_w, conv_b, w_down, *, tm, fc, shift, tiles_per_seq):
    m = x.shape[0]
    halo = state.shape[1]
    nfc = D_FF // fc
    n_tiles = m // tm
    row = lambda i, c: (i, 0)
    const = lambda i, c: (0, 0)
    kern = functools.partial(_ffn_kernel, tm=tm, halo=halo, shift=shift, tiles_per_seq=tiles_per_seq)
    return pl.pallas_call(
        kern,
        grid=(n_tiles, nfc),
        in_specs=[
            pl.BlockSpec((tm, D_MODEL), row),
            pl.BlockSpec((None, halo, fc), lambda i, c: (i // tiles_per_seq, 0, c)),
            pl.BlockSpec((None, halo, fc), lambda i, c: (i // tiles_per_seq, 0, nfc + c)),
            pl.BlockSpec((1, D_MODEL), const), pl.BlockSpec((1, D_MODEL), const),
            pl.BlockSpec((D_MODEL, fc), lambda i, c: (0, c)),
            pl.BlockSpec((D_MODEL, fc), lambda i, c: (0, nfc + c)),
            pl.BlockSpec((CONV_W, fc), lambda i, c: (0, c)),
            pl.BlockSpec((CONV_W, fc), lambda i, c: (0, nfc + c)),
            pl.BlockSpec((1, fc), lambda i, c: (0, c)),
            pl.BlockSpec((1, fc), lambda i, c: (0, nfc + c)),
            pl.BlockSpec((fc, D_MODEL), lambda i, c: (c, 0)),
        ],
        out_specs=[pl.BlockSpec((tm, D_MODEL), row),
                   pl.BlockSpec((None, halo, fc), lambda i, c: (i, 0, c)),
                   pl.BlockSpec((None, halo, fc), lambda i, c: (i, 0, c))],
        out_shape=[jax.ShapeDtypeStruct((m, D_MODEL), F32),
                   jax.ShapeDtypeStruct((n_tiles, halo, D_FF), F32),
                   jax.ShapeDtypeStruct((n_tiles, halo, D_FF), F32)],
        scratch_shapes=[pltpu.VMEM((tm, D_MODEL), BF16),
                        pltpu.VMEM((halo + tm, fc), F32), pltpu.VMEM((halo + tm, fc), F32),
                        pltpu.VMEM((nfc, halo, fc), F32), pltpu.VMEM((nfc, halo, fc), F32),
                        pltpu.VMEM((tm, D_MODEL), F32)],
        compiler_params=_cparams("arbitrary", "arbitrary"),
        name="conv_ffn",
    )(x, state, state, g2, g3, w_up, w_up, conv_w, conv_w, conv_b, conv_b, w_down)


def _log_sigmoid(x):
    return jnp.minimum(x, 0.0) - jnp.log1p(jnp.exp(-jnp.abs(x)))


def _exact_nt(sel_bf16, x):
    hi, mid, lo = _split3(x)
    return _dot_nt(sel_bf16, hi) + _dot_nt(sel_bf16, mid) + _dot_nt(sel_bf16, lo)


def _eye(n, m, dtype):
    return (lax.broadcasted_iota(jnp.int32, (n, m), 0) == lax.broadcasted_iota(jnp.int32, (n, m), 1)).astype(dtype)


def _mlstm_kernel(qkv_ref, o_ref, gate_ref, bg_ref, hn_ref, c0_ref, n0_ref, m0_ref,
                  y_ref, cf_ref, nf_ref, mf_ref, ct_scr, n_scr, m_scr, *, lp, lv, bb):
    for bi in range(bb):
        _mlstm_one(qkv_ref.at[bi], o_ref.at[bi], gate_ref.at[bi], bg_ref, hn_ref, c0_ref.at[bi], n0_ref.at[bi],
                   m0_ref.at[bi], y_ref.at[bi], cf_ref.at[bi], nf_ref.at[bi], mf_ref.at[bi],
                   ct_scr.at[bi], n_scr.at[bi], m_scr.at[bi], lp=lp, lv=lv)


def _mlstm_one(qkv_ref, o_ref, gate_ref, bg_ref, hn_ref, c0_ref, n0_ref, m0_ref,
               y_ref, cf_ref, nf_ref, mf_ref, ct_scr, n_scr, m_scr, *, lp, lv):
    ci = pl.program_id(1)
    eye_qk = _eye(ML_DQK, ML_DQK, BF16)

    @pl.when(ci == 0)
    def _():
        for h in range(ML_HEADS):
            ct_scr[h] = _exact_nt(eye_qk, c0_ref[h])
        n_scr[...] = n0_ref[...]
        m_scr[...] = m0_ref[...]

    gp = gate_ref[...] + bg_ref[...]
    ls = _log_sigmoid(gp)
    r_i = lax.broadcasted_iota(jnp.int32, (lp, lp), 0)
    c_i = lax.broadcasted_iota(jnp.int32, (lp, lp), 1)
    causal = c_i <= r_i
    tril = causal.astype(BF16)
    hi, mid, lo = _split3(ls)
    b_all = _dot(tril, hi) + _dot(tril, mid) + _dot(tril, lo)
    sel16 = _eye(2 * ML_HEADS, LANES, BF16)
    gp_t = _exact_nt(sel16, gp)
    b_t = _exact_nt(sel16, b_all)
    row_valid = lax.broadcasted_iota(jnp.int32, (lp, 1), 0) < lv

    for h in range(ML_HEADS):
        q = qkv_ref[:, h * ML_DQK:(h + 1) * ML_DQK]
        k = qkv_ref[:, ML_QK_W + h * ML_DQK:ML_QK_W + (h + 1) * ML_DQK]
        v = qkv_ref[:, 2 * ML_QK_W + h * ML_DV:2 * ML_QK_W + (h + 1) * ML_DV]
        ct = ct_scr[h]
        n_row = n_scr[h]
        m = m_scr[h]
        b_col = b_all[:, ML_HEADS + h:ML_HEADS + h + 1]
        i_col = gp[:, h:h + 1]
        b_row = b_t[ML_HEADS + h:ML_HEADS + h + 1, :]
        i_row = gp_t[h:h + 1, :]
        dmat = jnp.where(causal, b_col - b_row + i_row, MASK_NEG)
        inter = b_col + m
        mt = jnp.maximum(inter, jnp.max(dmat, axis=-1, keepdims=True))
        s = _dot_nt(q, k) * jnp.exp(dmat - mt)
        wi = jnp.exp(inter - mt)
        qf = q.astype(F32)
        num = wi * _dot(q, ct.astype(BF16)) + _dot(s.astype(BF16), v)
        den = wi * jnp.sum(qf * n_row, axis=-1, keepdims=True) + jnp.sum(s, axis=-1, keepdims=True)
        hc = num / jnp.maximum(jnp.abs(den), jnp.exp(-mt))
        m_last = mt[lv - 1:lv, :]
        b_last = b_col[lv - 1:lv, :]
        decay = jnp.exp(b_last + m - m_last)
        ws = jnp.where(row_valid, jnp.exp(b_last - b_col + i_col - m_last), 0.0)
        ct_scr[h] = decay * ct + _dot_tn(k, (ws * v.astype(F32)).astype(BF16))
        n_scr[h] = decay * n_row + jnp.sum(ws * k.astype(F32), axis=0, keepdims=True)
        m_scr[h] = m_last
        hn = hc * lax.rsqrt(jnp.mean(hc * hc, axis=-1, keepdims=True) + RMS_EPS)
        sl = slice(h * ML_DV, (h + 1) * ML_DV)
        y_ref[:, sl] = (hn * hn_ref[:, sl] * jax.nn.sigmoid(o_ref[:, sl])).astype(BF16)

    @pl.when(ci == pl.num_programs(1) - 1)
    def _():
        eye_v = _eye(ML_DV, ML_DV, BF16)
        for h in range(ML_HEADS):
            cf_ref[h] = _exact_nt(eye_v, ct_scr[h])
        nf_ref[...] = n_scr[...]
        mf_ref[...] = m_scr[...]


def _mlstm(qkv, o, gates, b_gate, head_norm, c0, n0, m0, *, nb, lp, lv):
    m = qkv.shape[0]
    tseq = m // nb
    bb = math.gcd(nb, ML_SEQ_PER_STEP)
    seq3 = lambda a: a.reshape(nb, tseq, a.shape[1])
    row = lambda b, c: (b, c, 0)
    st4 = lambda b, c: (b, 0, 0, 0)
    kern = functools.partial(_mlstm_kernel, lp=lp, lv=lv, bb=bb)
    y, cf, nf, mf = pl.pallas_call(
        kern,
        grid=(nb // bb, tseq // lp),
        in_specs=[pl.BlockSpec((bb, lp, qkv.shape[1]), row), pl.BlockSpec((bb, lp, ML_V_W), row),
                  pl.BlockSpec((bb, lp, LANES), row), pl.BlockSpec((1, LANES), lambda b, c: (0, 0)),
                  pl.BlockSpec((1, ML_V_W), lambda b, c: (0, 0)),
                  pl.BlockSpec((bb, ML_HEADS, ML_DV, ML_DQK), st4),
                  pl.BlockSpec((bb, ML_HEADS, 1, ML_DQK), st4),
                  pl.BlockSpec((bb, ML_HEADS, 1, 1), st4)],
        out_specs=[pl.BlockSpec((bb, lp, ML_V_W), row),
                   pl.BlockSpec((bb, ML_HEADS, ML_DV, ML_DQK), st4),
                   pl.BlockSpec((bb, ML_HEADS, 1, ML_DQK), st4),
                   pl.BlockSpec((bb, ML_HEADS, 1, 1), st4)],
        out_shape=[jax.ShapeDtypeStruct((nb, tseq, ML_V_W), BF16),
                   jax.ShapeDtypeStruct((nb, ML_HEADS, ML_DV, ML_DQK), F32),
                   jax.ShapeDtypeStruct((nb, ML_HEADS, 1, ML_DQK), F32),
                   jax.ShapeDtypeStruct((nb, ML_HEADS, 1, 1), F32)],
        scratch_shapes=[pltpu.VMEM((bb, ML_HEADS, ML_DQK, ML_DV), F32),
                        pltpu.VMEM((bb, ML_HEADS, 1, ML_DQK), F32),
                        pltpu.VMEM((bb, ML_HEADS, 1, 1), F32)],
        compiler_params=_cparams("arbitrary", "arbitrary"),
        name="mlstm",
    )(seq3(qkv), seq3(o), seq3(gates), b_gate, head_norm, c0, n0, m0)
    return y.reshape(m, ML_V_W), cf, nf, mf


def _cmp_parts(load_rows, wbd_ref, c, p):
    acc = None
    for l in range(CMP_STRIDE):
        part = _dot(load_rows(l, c, p).astype(BF16), wbd_ref[c, l])
        acc = part if acc is None else acc + part
    return acc


def _cmp_pos_bias(pos_ref, wbd_ref, c):
    acc = None
    for l in range(CMP_STRIDE):
        part = _dot(pos_ref[c, l].astype(BF16), wbd_ref[c, l])
        acc = part if acc is None else acc + part
    return acc[0:1, :PAIR_W] + acc[1:2, PAIR_W:]


def _cmp_finish(parts, posb, w2_ref, c):
    n = parts.shape[0]
    pre = parts[:, :PAIR_W] + pltpu.roll(parts[:, PAIR_W:], n - 1, 0) + posb
    return _dot(_gelu_tanh(pre).astype(BF16), w2_ref[c])


def _nsa_compress_kernel(x00_ref, x01_ref, x10_ref, x11_ref, wbd_ref, w2_ref, pos_ref, kc_ref, vc_ref, *, nchunk):
    x_refs = ((x00_ref, x01_ref), (x10_ref, x11_ref))

    def load_rows(l, c, p):
        return x_refs[c][p][pl.ds(l, nchunk, stride=CMP_STRIDE), :]

    for c, out_ref in ((0, kc_ref), (1, vc_ref)):
        posb = _cmp_pos_bias(pos_ref, wbd_ref, c)
        for p in range(2):
            out_ref[p] = _cmp_finish(_cmp_parts(load_rows, wbd_ref, c, p), posb, w2_ref, c).astype(BF16)


def _nsa_compress(kva, wbd, w2bd, pos, *, nb, t):
    nchunk = t // CMP_STRIDE
    kern = functools.partial(_nsa_compress_kernel, nchunk=nchunk)
    out = jax.ShapeDtypeStruct((nb, 2, nchunk, PAIR_W), BF16)
    ospec = pl.BlockSpec((None, 2, nchunk, PAIR_W), lambda b: (b, 0, 0, 0))
    return pl.pallas_call(
        kern,
        grid=(nb,),
        in_specs=[pl.BlockSpec((t, PAIR_W), functools.partial(lambda j, b: (b, j), j)) for j in range(4)]
        + [_full(wbd.shape), _full(w2bd.shape), _full(pos.shape)],
        out_specs=[ospec, ospec],
        out_shape=[out, out],
        compiler_params=_cparams("parallel"),
        name="nsa_compress",
    )(kva, kva, kva, kva, wbd, w2bd, pos)


def _select_mask_t(imp_t, s0, score_scr, ns):
    qb = imp_t.shape[1]
    bid = lax.broadcasted_iota(jnp.int32, (ns, qb), 0)
    tq = s0 + lax.broadcasted_iota(jnp.int32, (ns, qb), 1)
    cur = tq // SEL_BLK
    valid = bid * SEL_BLK <= tq
    forced = (bid == 0) | (bid == cur) | (bid == cur - 1)
    score = jnp.where(valid, imp_t + jnp.where(forced, FORCE_BONUS, 0.0), MASK_NEG)
    score_scr[0:ns, :] = score

    def body(m, cnt):
        sm = score_scr[pl.ds(m, 1), :]
        ahead = (sm > score) | ((sm == score) & (m < bid))
        return cnt + ahead.astype(jnp.int32)

    cnt = lax.fori_loop(0, ns, body, jnp.zeros((ns, qb), jnp.int32), unroll=8)
    return cnt < min(SEL_TOPK, ns)


def _nsa_attn_kernel(slope_ref, q_ref, gate_ref, kc_ref, vc_ref, ksel_ref, vsel_ref, kwin_ref, vwin_ref,
                     e3_ref, o_ref, score_scr, *, t):
    pr = pl.program_id(1)
    nblk = pl.program_id(2)
    qb = Q_BLK
    s0 = nblk * qb
    ncp = t // CMP_STRIDE
    ns = t // SEL_BLK
    wk = WINDOW + qb
    hd = HEAD_DIM
    sg = jax.nn.sigmoid(gate_ref[...])

    tq_col = s0 + lax.broadcasted_iota(jnp.int32, (qb, 1), 0)
    cend = CMP_STRIDE * lax.broadcasted_iota(jnp.int32, (1, ncp), 1) + (CMP_LEN - 1)
    dist_c = tq_col - cend
    mask_c = dist_c >= 0
    distf_c = dist_c.astype(F32)
    jj = lax.broadcasted_iota(jnp.int32, (ns, ncp), 1) * CMP_STRIDE
    nn = lax.broadcasted_iota(jnp.int32, (ns, ncp), 0) * SEL_BLK
    ov_t = ((jj < nn + SEL_BLK) & (jj + CMP_LEN > nn)).astype(BF16)
    wstart = pl.multiple_of(jnp.maximum(s0 - WINDOW, 0), qb)
    dist_w = tq_col - (wstart + lax.broadcasted_iota(jnp.int32, (1, wk), 1))
    mask_w = (dist_w >= 0) & (dist_w < WINDOW)
    distf_w = dist_w.astype(F32)
    n_sel_chunks = (s0 + qb + SEL_CHUNK - 1) // SEL_CHUNK

    for half in range(2):
        lanes = slice(half * hd, (half + 1) * hd)
        qs = [q_ref[:, (half * NSA_HPG + hh) * hd:(half * NSA_HPG + hh + 1) * hd] for hh in range(NSA_HPG)]
        q_all = jnp.concatenate(qs, axis=0)
        slopes = [slope_ref[(pr * 2 + half) * NSA_HPG + hh] for hh in range(NSA_HPG)]

        s_c = _dot_nt(q_all, kc_ref[:, lanes])
        vc = vc_ref[...]
        pc_sum = jnp.zeros((qb, ncp), F32)
        o_cmp = []
        for hh in range(NSA_HPG):
            s = jnp.where(mask_c, s_c[hh * qb:(hh + 1) * qb] - slopes[hh] * distf_c, MASK_NEG)
            e = jnp.where(mask_c, jnp.exp(s - jnp.max(s, axis=-1, keepdims=True)), 0.0)
            pc = e / jnp.maximum(jnp.sum(e, axis=-1, keepdims=True), 1e-30)
            pc_sum = pc_sum + pc
            o_cmp.append(_dot(pc.astype(BF16), vc)[:, lanes])

        hi = pc_sum.astype(BF16)
        lo = (pc_sum - hi.astype(F32)).astype(BF16)
        imp_t = _dot_nt(ov_t, hi) + _dot_nt(ov_t, lo)
        sel_t = _select_mask_t(imp_t, s0, score_scr, ns)
        mneg = jnp.where(sel_t, 0.0, MASK_NEG).T.astype(BF16)

        def sel_chunk(kci, carry):
            m_run, l_run, acc = carry
            r0 = pl.multiple_of(kci * SEL_CHUNK, SEL_CHUNK)
            s_all = _dot_nt(q_all, ksel_ref[pl.ds(r0, SEL_CHUNK), lanes])
            kpos = r0 + lax.broadcasted_iota(jnp.int32, (1, SEL_CHUNK), 1)
            add = jnp.where(kpos > tq_col, MASK_NEG, _dot(mneg, e3_ref[kci]))
            prel = (kpos - s0).astype(F32)
            s = jnp.concatenate([s_all[hh * qb:(hh + 1) * qb] + (add + slopes[hh] * prel)
                                 for hh in range(NSA_HPG)], axis=0)
            m_new = jnp.maximum(m_run, jnp.max(s, axis=-1, keepdims=True))
            alpha = jnp.exp(m_run - m_new)
            p = jnp.exp(s - m_new)
            l_new = alpha * l_run + jnp.sum(p, axis=-1, keepdims=True)
            acc_new = alpha * acc + _dot(p.astype(BF16), vsel_ref[pl.ds(r0, SEL_CHUNK), :])
            return m_new, l_new, acc_new

        init = (jnp.full((NSA_HPG * qb, 1), MASK_NEG, F32), jnp.zeros((NSA_HPG * qb, 1), F32),
                jnp.zeros((NSA_HPG * qb, PAIR_W), F32))
        _, l_sel, acc_sel = lax.fori_loop(0, n_sel_chunks, sel_chunk, init)
        o_sel_all = acc_sel[:, lanes] / l_sel

        s_w = _dot_nt(q_all, kwin_ref[pl.ds(wstart, wk), lanes])
        vw = vwin_ref[pl.ds(wstart, wk), :]
        for hh in range(NSA_HPG):
            s = jnp.where(mask_w, s_w[hh * qb:(hh + 1) * qb] - slopes[hh] * distf_w, MASK_NEG)
            e = jnp.exp(s - jnp.max(s, axis=-1, keepdims=True))
            o_w = _dot(e.astype(BF16), vw)[:, lanes] / jnp.sum(e, axis=-1, keepdims=True)
            gi = (half * NSA_HPG + hh) * 3
            out = (sg[:, gi:gi + 1] * o_cmp[hh] + sg[:, gi + 1:gi + 2] * o_sel_all[hh * qb:(hh + 1) * qb]
                   + sg[:, gi + 2:gi + 3] * o_w)
            oc = (half * NSA_HPG + hh) * hd
            o_ref[:, oc:oc + hd] = out.astype(BF16)


def _nsa_attention(slopes, q, gates, kc, vc, kvb, e3, *, nb, t):
    nq = t // Q_BLK
    ncp = t // CMP_STRIDE
    pw = 2 * NSA_HPG * HEAD_DIM
    kvcol = lambda c: (lambda b, p, n: (b, 2 * c + p))
    kern = functools.partial(_nsa_attn_kernel, t=t)
    return pl.pallas_call(
        kern,
        grid=(nb, 2, nq),
        in_specs=[pl.BlockSpec(memory_space=pltpu.SMEM),
                  pl.BlockSpec((Q_BLK, pw), lambda b, p, n: (b * nq + n, p)),
                  pl.BlockSpec((Q_BLK, LANES), lambda b, p, n: (b * nq + n, p)),
                  pl.BlockSpec((None, None, ncp, PAIR_W), lambda b, p, n: (b, p, 0, 0)),
                  pl.BlockSpec((None, None, ncp, PAIR_W), lambda b, p, n: (b, p, 0, 0)),
                  pl.BlockSpec((t, PAIR_W), kvcol(2)), pl.BlockSpec((t, PAIR_W), kvcol(3)),
                  pl.BlockSpec((t, PAIR_W), kvcol(4)), pl.BlockSpec((t, PAIR_W), kvcol(5)),
                  _full(e3.shape)],
        out_specs=pl.BlockSpec((Q_BLK, pw), lambda b, p, n: (b * nq + n, p)),
        out_shape=jax.ShapeDtypeStruct((nb * t, NSA_Q_W), BF16),
        scratch_shapes=[pltpu.VMEM((max(t // SEL_BLK, SUBLANES), Q_BLK), F32)],
        compiler_params=_cparams("parallel", "parallel", "arbitrary"),
        name="nsa_attn",
    )(slopes, q, gates, kc, vc, kvb, kvb, kvb, kvb, e3)


KEY_CHUNK = 256


def _rank_topk_t(score, ns, qb):
    blocks = [score[SUBLANES * r:SUBLANES * (r + 1)] for r in range(ns // SUBLANES)]
    cnts = [jnp.zeros((SUBLANES, qb), jnp.int32) for _ in blocks]
    sub = lax.broadcasted_iota(jnp.int32, (SUBLANES, qb), 0)
    for m in range(ns):
        sm = jnp.broadcast_to(score[m:m + 1, :], (SUBLANES, qb))
        for r, blk in enumerate(blocks):
            lo = SUBLANES * r
            if lo > m:
                ahead = (sm >= blk).astype(jnp.int32)
            elif lo + SUBLANES - 1 <= m:
                ahead = (sm > blk).astype(jnp.int32)
            else:
                ahead = jnp.where(sub > m - lo, (sm >= blk).astype(jnp.int32), (sm > blk).astype(jnp.int32))
            cnts[r] = cnts[r] + ahead
    return jnp.concatenate(cnts, axis=0) < min(SEL_TOPK, ns)


def _online_unit(q, k, vaug, bias, u, m_scr, acc_scr):
    s = _dot_nt(q, k) + bias
    m_old = m_scr[u]
    m_new = jnp.maximum(m_old, jnp.max(s, axis=-1, keepdims=True))
    alpha = jnp.exp(m_old - m_new)
    p = jnp.exp(s - m_new)
    acc_scr[u] = alpha * acc_scr[u] + _dot(p.astype(BF16), vaug)
    m_scr[u] = m_new


def _nsa_attn2_kernel(slope_ref, q_ref, gate_ref, kc_ref, vc_ref, ksel_ref, vsel_ref, kwin_ref, vwin_ref,
                      e3_ref, o_ref, vs_scr, vw_scr, m_scr, acc_scr, pcs_scr, *, t):
    pr = pl.program_id(1)
    nblk = pl.program_id(2)
    qb = Q_BLK
    s0 = nblk * qb
    ncp = t // CMP_STRIDE
    ns = t // SEL_BLK
    hd = HEAD_DIM
    w = KEY_CHUNK
    nunit = 2 * NSA_HPG

    @pl.when(nblk == 0)
    def _():
        lane = lax.broadcasted_iota(jnp.int32, (t, PAIR_W), 1)
        for src, dst in ((vsel_ref, vs_scr), (vwin_ref, vw_scr)):
            v = src[...].astype(F32)
            dst[0] = jnp.where(lane < hd, v, 1.0).astype(BF16)
            dst[1] = jnp.where(lane < hd, pltpu.roll(v, hd, 1), 1.0).astype(BF16)

    sg = jax.nn.sigmoid(gate_ref[...])
    tq_col = s0 + lax.broadcasted_iota(jnp.int32, (qb, 1), 0)
    slopes = [[slope_ref[(pr * 2 + half) * NSA_HPG + hh] for hh in range(NSA_HPG)] for half in range(2)]
    q_of = lambda half, hh: q_ref[:, (half * NSA_HPG + hh) * hd:(half * NSA_HPG + hh + 1) * hd]
    lanes_of = lambda half: slice(half * hd, (half + 1) * hd)

    def reset_state():
        m_scr[...] = jnp.full(m_scr.shape, MASK_NEG, F32)
        acc_scr[...] = jnp.zeros(acc_scr.shape, F32)

    def read_out(u):
        a = acc_scr[u]
        return a[:, :hd] * (1.0 / a[:, hd:hd + 1])

    cend = CMP_STRIDE * lax.broadcasted_iota(jnp.int32, (1, ncp), 1) + (CMP_LEN - 1)
    add_c = jnp.where(tq_col >= cend, 0.0, MASK_NEG)
    prel_c = (cend - s0).astype(F32)
    jj = lax.broadcasted_iota(jnp.int32, (ns, ncp), 1) * CMP_STRIDE
    nn = lax.broadcasted_iota(jnp.int32, (ns, ncp), 0) * SEL_BLK
    ov_t = ((jj < nn + SEL_BLK) & (jj + CMP_LEN > nn)).astype(BF16)
    bid = lax.broadcasted_iota(jnp.int32, (ns, qb), 0)
    tq_row = s0 + lax.broadcasted_iota(jnp.int32, (ns, qb), 1)
    cur = tq_row // SEL_BLK
    valid_b = bid * SEL_BLK <= tq_row
    bonus = jnp.where((bid == 0) | (bid == cur) | (bid == cur - 1), FORCE_BONUS, 0.0)
    o_cmp = [[None] * NSA_HPG for _ in range(2)]
    mneg = []
    for half in range(2):
        kc = kc_ref[:, lanes_of(half)]
        vc = vc_ref[...]
        for hh in range(NSA_HPG):
            s = _dot_nt(q_of(half, hh), kc) + (add_c + slopes[half][hh] * prel_c)
            m = jnp.max(s, axis=-1, keepdims=True)
            e = jnp.where(s > 0.5 * MASK_NEG, jnp.exp(s - m), 0.0)
            pc = e * (1.0 / jnp.maximum(jnp.sum(e, axis=-1, keepdims=True), 1e-30))
            if hh == 0:
                pcs_scr[...] = pc
            else:
                pcs_scr[...] += pc
            o_cmp[half][hh] = _dot(pc.astype(BF16), vc)[:, lanes_of(half)]
        pc_sum = pcs_scr[...]
        hi = pc_sum.astype(BF16)
        lo = (pc_sum - hi.astype(F32)).astype(BF16)
        imp_t = _dot_nt(ov_t, hi) + _dot_nt(ov_t, lo)
        score = jnp.where(valid_b, imp_t + bonus, MASK_NEG)
        sel_t = _rank_topk_t(score, ns, qb)
        mneg.append(jnp.where(sel_t, 0.0, MASK_NEG).T.astype(BF16))

    reset_state()

    def sel_body(kci, carry):
        r0 = pl.multiple_of(kci * w, w)
        kpos = r0 + lax.broadcasted_iota(jnp.int32, (1, w), 1)
        prel = (kpos - s0).astype(F32)
        for half in range(2):
            add = jnp.where(kpos > tq_col, MASK_NEG, _dot(mneg[half], e3_ref[kci]))
            k = ksel_ref[pl.ds(r0, w), lanes_of(half)]
            vaug = vs_scr[half, pl.ds(r0, w), :]
            for hh in range(NSA_HPG):
                _online_unit(q_of(half, hh), k, vaug, add + slopes[half][hh] * prel,
                             half * NSA_HPG + hh, m_scr, acc_scr)
        return carry

    lax.fori_loop(0, (s0 + qb + w - 1) // w, sel_body, 0)
    o_sel = [read_out(u) for u in range(nunit)]

    reset_state()
    wspan = -(-(WINDOW + qb) // w) * w
    wstart = pl.multiple_of(jnp.maximum(s0 + qb - wspan, 0), qb)
    for ci in range(wspan // w):
        r0 = pl.multiple_of(wstart + ci * w, qb)
        kpos = r0 + lax.broadcasted_iota(jnp.int32, (1, w), 1)
        dist = tq_col - kpos
        add = jnp.where((dist >= 0) & (dist < WINDOW), 0.0, MASK_NEG)
        prel = (kpos - s0).astype(F32)
        for half in range(2):
            k = kwin_ref[pl.ds(r0, w), lanes_of(half)]
            vaug = vw_scr[half, pl.ds(r0, w), :]
            for hh in range(NSA_HPG):
                _online_unit(q_of(half, hh), k, vaug, add + slopes[half][hh] * prel,
                             half * NSA_HPG + hh, m_scr, acc_scr)

    for half in range(2):
        for hh in range(NSA_HPG):
            u = half * NSA_HPG + hh
            gi = u * 3
            out = (sg[:, gi:gi + 1] * o_cmp[half][hh] + sg[:, gi + 1:gi + 2] * o_sel[u]
                   + sg[:, gi + 2:gi + 3] * read_out(u))
            o_ref[:, u * hd:(u + 1) * hd] = out.astype(BF16)


def _nsa_attention2(slopes, q, gates, kc, vc, kvb, e3, *, nb, t):
    nq = t // Q_BLK
    ncp = t // CMP_STRIDE
    pw = 2 * NSA_HPG * HEAD_DIM
    nunit = 2 * NSA_HPG
    kvcol = lambda c: (lambda b, p, n: (b, 2 * c + p))
    kern = functools.partial(_nsa_attn2_kernel, t=t)
    return pl.pallas_call(
        kern,
        grid=(nb, 2, nq),
        in_specs=[pl.BlockSpec(memory_space=pltpu.SMEM),
                  pl.BlockSpec((Q_BLK, pw), lambda b, p, n: (b * nq + n, p)),
                  pl.BlockSpec((Q_BLK, LANES), lambda b, p, n: (b * nq + n, p)),
                  pl.BlockSpec((None, None, ncp, PAIR_W), lambda b, p, n: (b, p, 0, 0)),
                  pl.BlockSpec((None, None, ncp, PAIR_W), lambda b, p, n: (b, p, 0, 0)),
                  pl.BlockSpec((t, PAIR_W), kvcol(2)), pl.BlockSpec((t, PAIR_W), kvcol(3)),
                  pl.BlockSpec((t, PAIR_W), kvcol(4)), pl.BlockSpec((t, PAIR_W), kvcol(5)),
                  _full(e3.shape)],
        out_specs=pl.BlockSpec((Q_BLK, pw), lambda b, p, n: (b * nq + n, p)),
        out_shape=jax.ShapeDtypeStruct((nb * t, NSA_Q_W), BF16),
        scratch_shapes=[pltpu.VMEM((2, t, PAIR_W), BF16), pltpu.VMEM((2, t, PAIR_W), BF16),
                        pltpu.VMEM((nunit, Q_BLK, 1), F32), pltpu.VMEM((nunit, Q_BLK, PAIR_W), F32),
                        pltpu.VMEM((Q_BLK, ncp), F32)],
        compiler_params=_cparams("parallel", "parallel", "arbitrary"),
        name="nsa_attn",
    )(slopes, q, gates, kc, vc, kvb, kvb, kvb, kvb, e3)


def _sel_expand2(t):
    key_blk = jnp.arange(t, dtype=jnp.int32).reshape(t // KEY_CHUNK, 1, KEY_CHUNK) // SEL_BLK
    blk = jnp.arange(t // SEL_BLK, dtype=jnp.int32).reshape(1, t // SEL_BLK, 1)
    return (key_blk == blk).astype(BF16)


N_FEAT = 6
AUG_W = 2 * LANES


def _key_feats(kpos_col):
    r = kpos_col.shape[0]
    lane = lax.broadcasted_iota(jnp.int32, (r, LANES), 1)
    a = lax.shift_right_logical(kpos_col, 6).astype(F32)
    b = (kpos_col & (SEL_BLK - 1)).astype(F32)
    return jnp.where(lane < 3, a, jnp.where(lane < N_FEAT, b, 0.0))


def _tile_rows(x, n):
    return jnp.concatenate([x] * n, axis=0)


def _nsa_attn3_kernel(sf_ref, q_ref, gate_ref, kc_ref, vc_ref, ksel_ref, vsel_ref, kwin_ref, vwin_ref, o_ref,
                      ks_scr, kw_scr, kc_scr, vs_scr, vw_scr, qs_scr, qw_scr, qc_scr, mrun_scr, m_scr, acc_scr,
                      *, t):
    pr = pl.program_id(1)
    nblk = pl.program_id(2)
    qb = Q_BLK
    s0 = nblk * qb
    ncp = t // CMP_STRIDE
    ns = t // SEL_BLK
    hd = HEAD_DIM
    w = KEY_CHUNK
    hpg = NSA_HPG
    rows_all = hpg * qb

    @pl.when(nblk == 0)
    def _():
        lane = lax.broadcasted_iota(jnp.int32, (t, LANES), 1)
        row = lax.broadcasted_iota(jnp.int32, (t, 1), 0)
        feats = _key_feats(row).astype(BF16)
        onehot = jnp.where(lane - hd == lax.shift_right_logical(row, 6), 1.0, 0.0)
        lane_w = lax.broadcasted_iota(jnp.int32, (WINDOW, LANES), 1)
        lane_c = lax.broadcasted_iota(jnp.int32, (ncp, LANES), 1)
        cend = CMP_STRIDE * lax.broadcasted_iota(jnp.int32, (ncp, 1), 0) + (CMP_LEN - 1)
        ks = ksel_ref[...].astype(F32)
        vs = vsel_ref[...].astype(F32)
        kwn = kwin_ref[...].astype(F32)
        vwn = vwin_ref[...].astype(F32)
        kcv = kc_ref[...].astype(F32)
        for half in range(2):
            low = (lambda x: x) if half == 0 else (lambda x: pltpu.roll(x, hd, 1))
            ks_scr[half, :, 0:LANES] = jnp.where(lane < hd, low(ks), onehot).astype(BF16)
            ks_scr[half, :, LANES:AUG_W] = feats
            vs_scr[half] = jnp.where(lane < hd, low(vs), 1.0).astype(BF16)
            kw_scr[half, 0:WINDOW, 0:LANES] = jnp.where(lane_w == hd, 1.0, 0.0).astype(BF16)
            kw_scr[half, 0:WINDOW, LANES:AUG_W] = jnp.zeros((WINDOW, LANES), BF16)
            kw_scr[half, WINDOW:WINDOW + t, 0:LANES] = jnp.where(lane < hd, low(kwn), 0.0).astype(BF16)
            kw_scr[half, WINDOW:WINDOW + t, LANES:AUG_W] = feats
            vw_scr[half, 0:WINDOW] = jnp.ones((WINDOW, LANES), BF16)
            vw_scr[half, WINDOW:WINDOW + t] = jnp.where(lane < hd, low(vwn), 1.0).astype(BF16)
            kc_scr[half, :, 0:LANES] = jnp.where(lane_c < hd, low(kcv), 0.0).astype(BF16)
            kc_scr[half, :, LANES:AUG_W] = _key_feats(cend).astype(BF16)

    sg = jax.nn.sigmoid(gate_ref[...])
    lane = lax.broadcasted_iota(jnp.int32, (qb, LANES), 1)
    lane1 = lax.broadcasted_iota(jnp.int32, (1, LANES), 1)
    t_loc = lax.broadcasted_iota(jnp.int32, (qb, 1), 0)
    tq_col = s0 + t_loc
    pad_row = jnp.where(lane1 == hd, MASK_NEG, 0.0)

    for half in range(2):
        for hh in range(hpg):
            u = half * hpg + hh
            qcol = q_ref[:, (u // 2) * LANES:(u // 2 + 1) * LANES].astype(F32)
            qlow = jnp.where(lane < hd, qcol if u % 2 == 0 else pltpu.roll(qcol, hd, 1), 0.0)
            feat = jnp.zeros((1, LANES), F32)
            for j in range(N_FEAT):
                feat = jnp.where(lane1 == j, sf_ref[(pr * 2 + half) * hpg + hh, j], feat)
            feat = jnp.broadcast_to(feat, (qb, LANES)).astype(BF16)
            rows = slice(hh * qb, (hh + 1) * qb)
            qc_scr[half, rows, 0:LANES] = qlow.astype(BF16)
            qw_scr[half, rows, 0:LANES] = (qlow + pad_row).astype(BF16)
            for scr in (qc_scr, qw_scr, qs_scr):
                scr[half, rows, LANES:AUG_W] = feat

    def online(half, s, vaug):
        nj = s.shape[1] // LANES
        mx = s[:, 0:LANES]
        for j in range(1, nj):
            mx = jnp.maximum(mx, s[:, j * LANES:(j + 1) * LANES])
        m_old = m_scr[half]
        m_new = jnp.maximum(m_old, jnp.max(mx, axis=-1, keepdims=True))
        alpha = jnp.exp(m_old - m_new)
        p = jnp.exp(s - jnp.concatenate([m_new] * nj, axis=1))
        acc_scr[half] = alpha * acc_scr[half] + _dot(p.astype(BF16), vaug)
        m_scr[half] = m_new

    def read_out(half):
        a = acc_scr[half]
        return a[:, 0:hd] * (1.0 / a[:, hd:hd + 1])

    def reset(scr, val):
        scr[...] = jnp.full(scr.shape, val, F32)

    cend_row = CMP_STRIDE * lax.broadcasted_iota(jnp.int32, (1, ncp), 1) + (CMP_LEN - 1)
    add_c = _tile_rows(jnp.where(tq_col >= cend_row, 0.0, MASK_NEG), hpg)
    jj = lax.broadcasted_iota(jnp.int32, (ns, ncp), 1) * CMP_STRIDE
    nn = lax.broadcasted_iota(jnp.int32, (ns, ncp), 0) * SEL_BLK
    ov_t = ((jj < nn + SEL_BLK) & (jj + CMP_LEN > nn)).astype(BF16)
    bid = lax.broadcasted_iota(jnp.int32, (ns, qb), 0)
    tq_row = s0 + lax.broadcasted_iota(jnp.int32, (ns, qb), 1)
    cur = tq_row // SEL_BLK
    valid_b = bid * SEL_BLK <= tq_row
    bonus = jnp.where((bid == 0) | (bid == cur) | (bid == cur - 1), FORCE_BONUS, 0.0)
    o_cmp = []
    for half in range(2):
        s = _dot_nt(qc_scr[half], kc_scr[half]) + add_c
        m = jnp.max(s, axis=-1, keepdims=True)
        e = jnp.where(s > 0.5 * MASK_NEG, jnp.exp(s - m), 0.0)
        pc = e * (1.0 / jnp.maximum(jnp.sum(e, axis=-1, keepdims=True), 1e-30))
        o_cmp.append(_dot(pc.astype(BF16), vc_ref[...])[:, half * hd:(half + 1) * hd])
        pc_sum = pc[0:qb]
        for hh in range(1, hpg):
            pc_sum = pc_sum + pc[hh * qb:(hh + 1) * qb]
        hi = pc_sum.astype(BF16)
        lo = (pc_sum - hi.astype(F32)).astype(BF16)
        imp_t = _dot_nt(ov_t, hi) + _dot_nt(ov_t, lo)
        sel_t = _rank_topk_t(jnp.where(valid_b, imp_t + bonus, MASK_NEG), ns, qb)
        pieces = [jnp.zeros((hd, qb), F32), jnp.where(sel_t, 0.0, MASK_NEG)]
        if LANES - hd - ns > 0:
            pieces.append(jnp.zeros((LANES - hd - ns, qb), F32))
        mneg = jnp.concatenate(pieces, axis=0).T
        for hh in range(hpg):
            rows = slice(hh * qb, (hh + 1) * qb)
            qs_scr[half, rows, 0:LANES] = (qc_scr[half, rows, 0:LANES].astype(F32) + mneg).astype(BF16)

    nfull = nblk // (w // qb)
    r_tail = pl.multiple_of(nfull * w, w)
    kpos_tail = r_tail + lax.broadcasted_iota(jnp.int32, (1, w), 1)
    causal_add = _tile_rows(jnp.where(kpos_tail > tq_col, MASK_NEG, 0.0), hpg)

    def sel_scores(half, r0, causal):
        s = _dot_nt(qs_scr[half], ks_scr[half, pl.ds(r0, w), :])
        return s + causal_add if causal else s

    def sel_body(kci, carry):
        r0 = pl.multiple_of(kci * w, w)
        for half in range(2):
            online(half, sel_scores(half, r0, False), vs_scr[half, pl.ds(r0, w), :])
        return carry

    reset(m_scr, MASK_NEG)
    reset(acc_scr, 0.0)
    lax.fori_loop(0, nfull, sel_body, 0)
    for half in range(2):
        online(half, sel_scores(half, r_tail, True), vs_scr[half, pl.ds(r_tail, w), :])
    o_sel = [read_out(half) for half in range(2)]

    c_loc = lax.broadcasted_iota(jnp.int32, (1, qb), 1)
    left_add = _tile_rows(jnp.where(c_loc > t_loc, 0.0, MASK_NEG), hpg)
    diag_add = _tile_rows(jnp.concatenate([jnp.zeros((qb, qb), F32),
                                           jnp.where(c_loc <= t_loc, 0.0, MASK_NEG)], axis=1), hpg)
    win_chunks = ((0, qb, left_add), (qb, w, None), (qb + w, w, diag_add))

    reset(m_scr, MASK_NEG)
    reset(acc_scr, 0.0)
    for off, wd, add in win_chunks:
        r0 = pl.multiple_of(s0 + off, qb)
        for half in range(2):
            s = _dot_nt(qw_scr[half], kw_scr[half, pl.ds(r0, wd), :])
            online(half, s if add is None else s + add, vw_scr[half, pl.ds(r0, wd), :])

    for half in range(2):
        o_w = read_out(half)
        for hh in range(hpg):
            u = half * hpg + hh
            gi = u * 3
            rows = slice(hh * qb, (hh + 1) * qb)
            out = (sg[:, gi:gi + 1] * o_cmp[half][rows] + sg[:, gi + 1:gi + 2] * o_sel[half][rows]
                   + sg[:, gi + 2:gi + 3] * o_w[rows])
            o_ref[:, u * hd:(u + 1) * hd] = out.astype(BF16)


def _nsa_attention3(slope_feats, q, gates, kc, vc, kvb, *, nb, t):
    assert WINDOW + Q_BLK == Q_BLK + 2 * KEY_CHUNK and t % KEY_CHUNK == 0 and t // SEL_BLK <= LANES - HEAD_DIM
    nq = t // Q_BLK
    ncp = t // CMP_STRIDE
    pw = 2 * NSA_HPG * HEAD_DIM
    rows_all = NSA_HPG * Q_BLK
    kvcol = lambda c: (lambda b, p, n: (b, 2 * c + p))
    kern = functools.partial(_nsa_attn3_kernel, t=t)
    return pl.pallas_call(
        kern,
        grid=(nb, 2, nq),
        in_specs=[pl.BlockSpec(memory_space=pltpu.SMEM),
                  pl.BlockSpec((Q_BLK, pw), lambda b, p, n: (b * nq + n, p)),
                  pl.BlockSpec((Q_BLK, LANES), lambda b, p, n: (b * nq + n, p)),
                  pl.BlockSpec((None, None, ncp, PAIR_W), lambda b, p, n: (b, p, 0, 0)),
                  pl.BlockSpec((None, None, ncp, PAIR_W), lambda b, p, n: (b, p, 0, 0)),
                  pl.BlockSpec((t, PAIR_W), kvcol(2)), pl.BlockSpec((t, PAIR_W), kvcol(3)),
                  pl.BlockSpec((t, PAIR_W), kvcol(4)), pl.BlockSpec((t, PAIR_W), kvcol(5))],
        out_specs=pl.BlockSpec((Q_BLK, pw), lambda b, p, n: (b * nq + n, p)),
        out_shape=jax.ShapeDtypeStruct((nb * t, NSA_Q_W), BF16),
        scratch_shapes=[pltpu.VMEM((2, t, AUG_W), BF16), pltpu.VMEM((2, WINDOW + t, AUG_W), BF16),
                        pltpu.VMEM((2, ncp, AUG_W), BF16),
                        pltpu.VMEM((2, t, LANES), BF16), pltpu.VMEM((2, WINDOW + t, LANES), BF16),
                        pltpu.VMEM((2, rows_all, AUG_W), BF16), pltpu.VMEM((2, rows_all, AUG_W), BF16),
                        pltpu.VMEM((2, rows_all, AUG_W), BF16),
                        pltpu.VMEM((2, rows_all, LANES), F32), pltpu.VMEM((2, rows_all, LANES), F32),
                        pltpu.VMEM((2, rows_all, LANES), F32)],
        compiler_params=_cparams("parallel", "parallel", "arbitrary"),
        name="nsa_attn",
    )(slope_feats, q, gates, kc, vc, kvb, kvb, kvb, kvb)


def _slope_feats():
    s = _alibi_slopes()
    hi = s.astype(BF16).astype(F32)
    mid = (s - hi).astype(BF16).astype(F32)
    lo = (s - hi - mid).astype(BF16).astype(F32)
    z = jnp.zeros_like(s)
    return jnp.stack([SEL_BLK * hi, SEL_BLK * mid, SEL_BLK * lo, hi, mid, lo, z, z], axis=1)


WIN_JOBS = 3
WIN_PAD = WIN_JOBS * KEY_CHUNK - Q_BLK
PAD_FEAT = N_FEAT
NO_SPAN = 1 << 30


def _nsa_attn5_kernel(sf_ref, q_ref, gate_ref, kc_ref, vc_ref, ksel_ref, vsel_ref, kwin_ref, vwin_ref, o_ref,
                      k_scr, v_scr, kc_scr, qa_scr, qc_scr, s_scr, mx_scr, m_scr, acc_scr, *, t):
    pr = pl.program_id(1)
    nblk = pl.program_id(2)
    qb = Q_BLK
    s0 = nblk * qb
    ncp = t // CMP_STRIDE
    ns = t // SEL_BLK
    hd = HEAD_DIM
    w = KEY_CHUNK
    hpg = NSA_HPG
    rows_all = hpg * qb
    wrow0 = t

    @pl.when(nblk == 0)
    def _():
        lane = lax.broadcasted_iota(jnp.int32, (t, LANES), 1)
        row = lax.broadcasted_iota(jnp.int32, (t, 1), 0)
        feats = _key_feats(row).astype(BF16)
        onehot = jnp.where(lane - hd == lax.shift_right_logical(row, 6), 1.0, 0.0)
        lane_p = lax.broadcasted_iota(jnp.int32, (WIN_PAD, LANES), 1)
        lane_c = lax.broadcasted_iota(jnp.int32, (ncp, LANES), 1)
        cend = CMP_STRIDE * lax.broadcasted_iota(jnp.int32, (ncp, 1), 0) + (CMP_LEN - 1)
        ks = ksel_ref[...].astype(F32)
        vs = vsel_ref[...].astype(F32)
        kwn = kwin_ref[...].astype(F32)
        vwn = vwin_ref[...].astype(F32)
        kcv = kc_ref[...].astype(F32)
        for half in range(2):
            low = (lambda x: x) if half == 0 else (lambda x: pltpu.roll(x, hd, 1))
            k_scr[half, 0:t, 0:LANES] = jnp.where(lane < hd, low(ks), onehot).astype(BF16)
            k_scr[half, 0:t, LANES:AUG_W] = feats
            v_scr[half, 0:t] = jnp.where(lane < hd, low(vs), 1.0).astype(BF16)
            k_scr[half, wrow0:wrow0 + WIN_PAD, 0:LANES] = jnp.zeros((WIN_PAD, LANES), BF16)
            k_scr[half, wrow0:wrow0 + WIN_PAD, LANES:AUG_W] = jnp.where(lane_p == PAD_FEAT, 1.0, 0.0).astype(BF16)
            v_scr[half, wrow0:wrow0 + WIN_PAD] = jnp.ones((WIN_PAD, LANES), BF16)
            wr = wrow0 + WIN_PAD
            k_scr[half, wr:wr + t, 0:LANES] = jnp.where(lane < hd, low(kwn), 0.0).astype(BF16)
            k_scr[half, wr:wr + t, LANES:AUG_W] = feats
            v_scr[half, wr:wr + t] = jnp.where(lane < hd, low(vwn), 1.0).astype(BF16)
            kc_scr[half, :, 0:LANES] = jnp.where(lane_c < hd, low(kcv), 0.0).astype(BF16)
            kc_scr[half, :, LANES:AUG_W] = _key_feats(cend).astype(BF16)

    sg = jax.nn.sigmoid(gate_ref[...])
    lane = lax.broadcasted_iota(jnp.int32, (qb, LANES), 1)
    lane1 = lax.broadcasted_iota(jnp.int32, (1, LANES), 1)
    t_loc = lax.broadcasted_iota(jnp.int32, (qb, 1), 0)
    tq_col = s0 + t_loc

    for half in range(2):
        for hh in range(hpg):
            u = half * hpg + hh
            qcol = q_ref[:, (u // 2) * LANES:(u // 2 + 1) * LANES].astype(F32)
            qlow = jnp.where(lane < hd, qcol if u % 2 == 0 else pltpu.roll(qcol, hd, 1), 0.0)
            feat = jnp.where(lane1 == PAD_FEAT, MASK_NEG, 0.0)
            for j in range(N_FEAT):
                feat = jnp.where(lane1 == j, sf_ref[(pr * 2 + half) * hpg + hh, j], feat)
            feat = jnp.broadcast_to(feat, (qb, LANES)).astype(BF16)
            rows = slice(hh * qb, (hh + 1) * qb)
            qc_scr[half, rows, 0:LANES] = qlow.astype(BF16)
            qc_scr[half, rows, LANES:AUG_W] = feat
            qa_scr[half, rows, LANES:AUG_W] = feat

    cend_row = CMP_STRIDE * lax.broadcasted_iota(jnp.int32, (1, ncp), 1) + (CMP_LEN - 1)
    add_c = _tile_rows(jnp.where(tq_col >= cend_row, 0.0, MASK_NEG), hpg)
    jj = lax.broadcasted_iota(jnp.int32, (ns, ncp), 1) * CMP_STRIDE
    nn = lax.broadcasted_iota(jnp.int32, (ns, ncp), 0) * SEL_BLK
    ov_t = ((jj < nn + SEL_BLK) & (jj + CMP_LEN > nn)).astype(BF16)
    bid = lax.broadcasted_iota(jnp.int32, (ns, qb), 0)
    tq_row = s0 + lax.broadcasted_iota(jnp.int32, (ns, qb), 1)
    cur = tq_row // SEL_BLK
    valid_b = bid * SEL_BLK <= tq_row
    bonus = jnp.where((bid == 0) | (bid == cur) | (bid == cur - 1), FORCE_BONUS, 0.0)
    o_cmp = []
    for half in range(2):
        s = _dot_nt(qc_scr[half], kc_scr[half]) + add_c
        m = jnp.max(s, axis=-1, keepdims=True)
        e = jnp.where(s > 0.5 * MASK_NEG, jnp.exp(s - m), 0.0)
        pc = e * (1.0 / jnp.maximum(jnp.sum(e, axis=-1, keepdims=True), 1e-30))
        o_cmp.append(_dot(pc.astype(BF16), vc_ref[...])[:, half * hd:(half + 1) * hd])
        pc_sum = pc[0:qb]
        for hh in range(1, hpg):
            pc_sum = pc_sum + pc[hh * qb:(hh + 1) * qb]
        hi = pc_sum.astype(BF16)
        lo = (pc_sum - hi.astype(F32)).astype(BF16)
        imp_t = _dot_nt(ov_t, hi) + _dot_nt(ov_t, lo)
        sel_t = _rank_topk_t(jnp.where(valid_b, imp_t + bonus, MASK_NEG), ns, qb)
        pieces = [jnp.zeros((hd, qb), F32), jnp.where(sel_t, 0.0, MASK_NEG)]
        if LANES - hd - ns > 0:
            pieces.append(jnp.zeros((LANES - hd - ns, qb), F32))
        mneg = jnp.concatenate(pieces, axis=0).T
        for hh in range(hpg):
            rows = slice(hh * qb, (hh + 1) * qb)
            qa_scr[half, rows, 0:LANES] = (qc_scr[half, rows, 0:LANES].astype(F32) + mneg).astype(BF16)

    n_sel = nblk // (w // qb) + 1
    n_jobs = n_sel + WIN_JOBS
    col = lax.broadcasted_iota(jnp.int32, (1, w), 1)

    def job(j):
        is_win = j >= n_sel
        jw = j - n_sel
        r0 = jnp.where(is_win, wrow0 + s0 + jw * w, j * w)
        base = jnp.where(is_win, s0 + qb - WIN_JOBS * w + jw * w, j * w)
        span = jnp.where(is_win, WINDOW, NO_SPAN)
        return pl.multiple_of(r0, qb), base, span, is_win.astype(jnp.int32)

    def scores(j, slot):
        r0, base, span, _ = job(jnp.minimum(j, n_jobs - 1))
        dist = tq_col - (base + col)
        ok = (dist >= 0) & (dist < span) & (j < n_jobs)
        add = _tile_rows(jnp.where(ok, 0.0, MASK_NEG), hpg)
        for half in range(2):
            s = _dot_nt(qa_scr[half], k_scr[half, pl.ds(r0, w), :]) + add
            s_scr[slot, half] = s
            mx = jnp.max(jnp.maximum(s[:, 0:LANES], s[:, LANES:w]), axis=-1, keepdims=True)
            mx_scr[slot, half] = jnp.broadcast_to(mx, (rows_all, LANES))

    def consume(j, slot):
        r0, _, _, st = job(jnp.minimum(j, n_jobs - 1))
        for half in range(2):
            m_old = m_scr[st, half]
            m_new = jnp.maximum(m_old, mx_scr[slot, half])
            alpha = jnp.exp(m_old - m_new)
            p = jnp.exp(s_scr[slot, half] - jnp.concatenate([m_new, m_new], axis=1))
            acc_scr[st, half] = alpha * acc_scr[st, half] + _dot(p.astype(BF16), v_scr[half, pl.ds(r0, w), :])
            m_scr[st, half] = m_new

    m_scr[...] = jnp.full(m_scr.shape, MASK_NEG, F32)
    acc_scr[...] = jnp.zeros(acc_scr.shape, F32)
    scores(0, 0)

    def step(i, carry):
        scores(2 * i + 1, 1)
        consume(2 * i, 0)
        scores(2 * i + 2, 0)
        consume(2 * i + 1, 1)
        return carry

    lax.fori_loop(0, (n_jobs + 1) // 2, step, 0)

    for half in range(2):
        outs = []
        for st in range(2):
            a = acc_scr[st, half]
            outs.append(a[:, 0:hd] * (1.0 / a[:, hd:hd + 1]))
        for hh in range(hpg):
            u = half * hpg + hh
            gi = u * 3
            rows = slice(hh * qb, (hh + 1) * qb)
            out = (sg[:, gi:gi + 1] * o_cmp[half][rows] + sg[:, gi + 1:gi + 2] * outs[0][rows]
                   + sg[:, gi + 2:gi + 3] * outs[1][rows])
            o_ref[:, u * hd:(u + 1) * hd] = out.astype(BF16)


def _nsa_attention5(slope_feats, q, gates, kc, vc, kvb, *, nb, t):
    assert WIN_JOBS * KEY_CHUNK >= WINDOW + Q_BLK and t % KEY_CHUNK == 0 and t // SEL_BLK <= LANES - HEAD_DIM
    nq = t // Q_BLK
    ncp = t // CMP_STRIDE
    pw = 2 * NSA_HPG * HEAD_DIM
    rows_all = NSA_HPG * Q_BLK
    krows = 2 * t + WIN_PAD
    kvcol = lambda c: (lambda b, p, n: (b, 2 * c + p))
    kern = functools.partial(_nsa_attn5_kernel, t=t)
    return pl.pallas_call(
        kern,
        grid=(nb, 2, nq),
        in_specs=[pl.BlockSpec(memory_space=pltpu.SMEM),
                  pl.BlockSpec((Q_BLK, pw), lambda b, p, n: (b * nq + n, p)),
                  pl.BlockSpec((Q_BLK, LANES), lambda b, p, n: (b * nq + n, p)),
                  pl.BlockSpec((None, None, ncp, PAIR_W), lambda b, p, n: (b, p, 0, 0)),
                  pl.BlockSpec((None, None, ncp, PAIR_W), lambda b, p, n: (b, p, 0, 0)),
                  pl.BlockSpec((t, PAIR_W), kvcol(2)), pl.BlockSpec((t, PAIR_W), kvcol(3)),
                  pl.BlockSpec((t, PAIR_W), kvcol(4)), pl.BlockSpec((t, PAIR_W), kvcol(5))],
        out_specs=pl.BlockSpec((Q_BLK, pw), lambda b, p, n: (b * nq + n, p)),
        out_shape=jax.ShapeDtypeStruct((nb * t, NSA_Q_W), BF16),
        scratch_shapes=[pltpu.VMEM((2, krows, AUG_W), BF16), pltpu.VMEM((2, krows, LANES), BF16),
                        pltpu.VMEM((2, ncp, AUG_W), BF16),
                        pltpu.VMEM((2, rows_all, AUG_W), BF16), pltpu.VMEM((2, rows_all, AUG_W), BF16),
                        pltpu.VMEM((2, 2, rows_all, KEY_CHUNK), F32), pltpu.VMEM((2, 2, rows_all, LANES), F32),
                        pltpu.VMEM((2, 2, rows_all, LANES), F32), pltpu.VMEM((2, 2, rows_all, LANES), F32)],
        compiler_params=_cparams("parallel", "parallel", "arbitrary"),
        name="nsa_attn",
    )(slope_feats, q, gates, kc, vc, kvb, kvb, kvb, kvb)


PAGES_PER_STEP = 32


def _page_compress_kernel(pt_ref, cache_ref, wbd_ref, w2_ref, pos_ref, kvc_ref,
                          buf0, buf1, buf2, buf3, parts_scr, sem, *, n_pages, page):
    b = pl.program_id(0)
    hf = pl.program_id(1)
    bufs = (buf0, buf1, buf2, buf3)
    copies = []
    for i in range(PAGES_PER_STEP):
        pg = pt_ref[b * n_pages + hf * PAGES_PER_STEP + i]
        for cb in range(4):
            cp = pltpu.make_async_copy(cache_ref.at[pg, :, cb, :], bufs[cb].at[pl.ds(i * page, page), :], sem)
            cp.start()
            copies.append(cp)
    for cp in copies:
        cp.wait()

    nch = PAGES_PER_STEP * page // CMP_STRIDE

    def load_rows(l, c, p):
        return bufs[c * 2 + p][pl.ds(l, nch, stride=CMP_STRIDE), :]

    row0 = pl.multiple_of(hf * nch, nch)
    for c in range(2):
        for p in range(2):
            parts_scr[c * 2 + p, pl.ds(row0, nch), :] = _cmp_parts(load_rows, wbd_ref, c, p)

    @pl.when(hf == pl.num_programs(1) - 1)
    def _():
        for c in range(2):
            posb = _cmp_pos_bias(pos_ref, wbd_ref, c)
            for p in range(2):
                kvc_ref[c, p] = _cmp_finish(parts_scr[c * 2 + p], posb, w2_ref, c).astype(BF16)


def _page_compress_t_kernel(pt_ref, cache_ref, wbd_ref, w2_ref, pos_ref, kvc_ref,
                            buf, x0, x1, x2, x3, parts_scr, sem, *, page):
    b = pl.program_id(0)
    hf = pl.program_id(1)
    nhf = pl.num_programs(1)
    step = b * nhf + hf
    nstep = pl.num_programs(0) * nhf
    xs = (x0, x1, x2, x3)

    def copies(s, slot):
        out = []
        for i in range(PAGES_PER_STEP):
            pg = pt_ref[s * PAGES_PER_STEP + i]
            for c in range(2):
                out.append(pltpu.make_async_copy(cache_ref.at[0, pg, c], buf.at[slot, c, i], sem.at[slot]))
        return out

    @pl.when(step == 0)
    def _():
        for cp in copies(0, 0):
            cp.start()

    slot = step % 2

    @pl.when(step + 1 < nstep)
    def _():
        for cp in copies(step + 1, 1 - slot):
            cp.start()

    for cp in copies(step, slot):
        cp.wait()

    def to_rows(i, carry):
        r0 = pl.multiple_of(i * page, page)
        for c in range(2):
            tile = buf[slot, c, i].reshape(NSA_KV_HEADS * HEAD_DIM, page)
            for p in range(2):
                xs[c * 2 + p][pl.ds(r0, page), :] = tile[p * PAIR_W:(p + 1) * PAIR_W].T
        return carry

    lax.fori_loop(0, PAGES_PER_STEP, to_rows, 0)

    nch = PAGES_PER_STEP * page // CMP_STRIDE

    def load_rows(l, c, p):
        return xs[c * 2 + p][pl.ds(l, nch, stride=CMP_STRIDE), :]

    row0 = pl.multiple_of(hf * nch, nch)
    for c in range(2):
        for p in range(2):
            parts_scr[c * 2 + p, pl.ds(row0, nch), :] = _cmp_parts(load_rows, wbd_ref, c, p)

    @pl.when(hf == nhf - 1)
    def _():
        for c in range(2):
            posb = _cmp_pos_bias(pos_ref, wbd_ref, c)
            for p in range(2):
                kvc_ref[c, p] = _cmp_finish(parts_scr[c * 2 + p], posb, w2_ref, c).astype(BF16)


def _page_compress_t(page_table, cache_t, wbd, w2bd, pos):
    sb, n_pages = page_table.shape
    page = cache_t.shape[-1]
    ncp = n_pages * page // CMP_STRIDE
    rows = PAGES_PER_STEP * page
    kern = functools.partial(_page_compress_t_kernel, page=page)
    grid_spec = pltpu.PrefetchScalarGridSpec(
        num_scalar_prefetch=1,
        grid=(sb, n_pages // PAGES_PER_STEP),
        in_specs=[pl.BlockSpec(memory_space=pl.ANY),
                  pl.BlockSpec(wbd.shape, lambda b, h, pt: (0,) * wbd.ndim),
                  pl.BlockSpec(w2bd.shape, lambda b, h, pt: (0,) * w2bd.ndim),
                  pl.BlockSpec(pos.shape, lambda b, h, pt: (0,) * pos.ndim)],
        out_specs=pl.BlockSpec((None, 2, 2, ncp, PAIR_W), lambda b, h, pt: (b, 0, 0, 0, 0)),
        scratch_shapes=[pltpu.VMEM((2, 2, PAGES_PER_STEP, NSA_KV_HEADS, HEAD_DIM, page), F32)]
        + [pltpu.VMEM((rows, PAIR_W), F32)] * 4
        + [pltpu.VMEM((4, ncp, 2 * PAIR_W), F32), pltpu.SemaphoreType.DMA((2,))],
    )
    return pl.pallas_call(
        kern,
        grid_spec=grid_spec,
        out_shape=jax.ShapeDtypeStruct((sb, 2, 2, ncp, PAIR_W), BF16),
        compiler_params=_cparams("arbitrary", "arbitrary"),
        name="nsa_page_compress",
    )(page_table.reshape(-1), cache_t, wbd, w2bd, pos)


L_GROUP = 4


def _page_compress_native_kernel(pt_ref, cache_ref, wq_ref, w2_ref, xpos_ref, kvc_ref,
                                 buf, parts_scr, tmp_scr, sem, *, n_pages, page):
    b = pl.program_id(0)
    hf = pl.program_id(1)
    nhf = pl.num_programs(1)
    step = b * nhf + hf
    nstep = pl.num_programs(0) * nhf
    ng = NSA_KV_HEADS
    hd = HEAD_DIM

    def copies(s, slot):
        out = []
        for i in range(PAGES_PER_STEP):
            pg = pt_ref[s * PAGES_PER_STEP + i]
            for c in range(2):
                out.append(pltpu.make_async_copy(cache_ref.at[0, pg, :, c, :, :],
                                                 buf.at[slot, c, pl.ds(i * page, page)], sem.at[slot]))
        return out

    @pl.when(step == 0)
    def _():
        for cp in copies(0, 0):
            cp.start()

    slot = step % 2

    @pl.when(step + 1 < nstep)
    def _():
        for cp in copies(step + 1, 1 - slot):
            cp.start()

    for cp in copies(step, slot):
        cp.wait()

    nch = PAGES_PER_STEP * page // CMP_STRIDE
    row0 = pl.multiple_of(hf * nch * ng, nch * ng)
    for c in range(2):
        acc = None
        for lq in range(CMP_STRIDE // L_GROUP):
            xs = [buf[slot, c, pl.ds(lq * L_GROUP + i, nch, stride=CMP_STRIDE), :, :].reshape(nch * ng, hd)
                  for i in range(L_GROUP)]
            part = _dot(jnp.concatenate(xs, axis=1).astype(BF16), wq_ref[c, lq])
            acc = part if acc is None else acc + part
        parts_scr[c, pl.ds(row0, nch * ng), :] = acc

    @pl.when(hf == nhf - 1)
    def _():
        n = parts_scr.shape[1]
        lane = lax.broadcasted_iota(jnp.int32, (1, PAIR_W), 1)
        for c in range(2):
            pacc = None
            for lq in range(CMP_STRIDE // L_GROUP):
                part = _dot(xpos_ref[c, lq].astype(BF16), wq_ref[c, lq])
                pacc = part if pacc is None else pacc + part
            posb = jnp.where(lane < hd, pacc[0:1, :], 0.0) + pltpu.roll(jnp.where(lane >= hd, pacc[1:2, :], 0.0), hd, 1)
            p = parts_scr[c]
            pre = p + pltpu.roll(pltpu.roll(p, n - ng, 0), hd, 1) + posb
            tmp_scr[...] = _dot(_gelu_tanh(pre).astype(BF16), w2_ref[c])
            for pr in range(2):
                even = tmp_scr[pl.ds(2 * pr, n // ng, stride=ng), :]
                odd = tmp_scr[pl.ds(2 * pr + 1, n // ng, stride=ng), :]
                kvc_ref[c, pr] = (even + pltpu.roll(odd, hd, 1)).astype(BF16)


def _page_compress_native(page_table, cache, wq, w2pad, xpos):
    sb, n_pages = page_table.shape
    page = cache.shape[2]
    ncp = n_pages * page // CMP_STRIDE
    rows = PAGES_PER_STEP * page
    kern = functools.partial(_page_compress_native_kernel, n_pages=n_pages, page=page)
    grid_spec = pltpu.PrefetchScalarGridSpec(
        num_scalar_prefetch=1,
        grid=(sb, n_pages // PAGES_PER_STEP),
        in_specs=[pl.BlockSpec(memory_space=pl.ANY),
                  pl.BlockSpec(wq.shape, lambda b, h, pt: (0,) * wq.ndim),
                  pl.BlockSpec(w2pad.shape, lambda b, h, pt: (0,) * w2pad.ndim),
                  pl.BlockSpec(xpos.shape, lambda b, h, pt: (0,) * xpos.ndim)],
        out_specs=pl.BlockSpec((None, 2, 2, ncp, PAIR_W), lambda b, h, pt: (b, 0, 0, 0, 0)),
        scratch_shapes=[pltpu.VMEM((2, 2, rows, NSA_KV_HEADS, HEAD_DIM), F32),
                        pltpu.VMEM((2, ncp * NSA_KV_HEADS, PAIR_W), F32),
                        pltpu.VMEM((ncp * NSA_KV_HEADS, PAIR_W), F32),
                        pltpu.SemaphoreType.DMA((2,))],
    )
    return pl.pallas_call(
        kern,
        grid_spec=grid_spec,
        out_shape=jax.ShapeDtypeStruct((sb, 2, 2, ncp, PAIR_W), BF16),
        compiler_params=_cparams("arbitrary", "arbitrary"),
        name="nsa_page_compress",
    )(page_table.reshape(-1), cache, wq, w2pad, xpos)


def _prep_cmp_native(w_cmp1, w_cmp2, cmp_pos):
    hd = HEAD_DIM
    nq = CMP_STRIDE // L_GROUP
    both = jnp.concatenate([w_cmp1[:, :CMP_STRIDE], w_cmp1[:, CMP_STRIDE:]], axis=-1)
    wq = both.reshape(2, nq, L_GROUP * hd, 2 * hd).astype(BF16)
    w2pad = jnp.pad(w_cmp2, ((0, 0), (0, PAIR_W - hd), (0, PAIR_W - hd))).astype(BF16)
    pos = cmp_pos.transpose(1, 0, 2)
    rows = jnp.stack([pos[:, :CMP_STRIDE].reshape(2, nq, L_GROUP * hd),
                      pos[:, CMP_STRIDE:].reshape(2, nq, L_GROUP * hd)], axis=2)
    xpos = jnp.pad(rows, ((0, 0), (0, 0), (0, SUBLANES - 2), (0, 0)))
    return wq, w2pad, xpos


def _page_compress(page_table, cache4, wbd, w2bd, pos):
    sb, n_pages = page_table.shape
    page = cache4.shape[1]
    ncp = n_pages * page // CMP_STRIDE
    rows = PAGES_PER_STEP * page
    kern = functools.partial(_page_compress_kernel, n_pages=n_pages, page=page)
    grid_spec = pltpu.PrefetchScalarGridSpec(
        num_scalar_prefetch=1,
        grid=(sb, n_pages // PAGES_PER_STEP),
        in_specs=[pl.BlockSpec(memory_space=pl.ANY),
                  pl.BlockSpec(wbd.shape, lambda b, h, pt: (0,) * wbd.ndim),
                  pl.BlockSpec(w2bd.shape, lambda b, h, pt: (0,) * w2bd.ndim),
                  pl.BlockSpec(pos.shape, lambda b, h, pt: (0,) * pos.ndim)],
        out_specs=pl.BlockSpec((None, 2, 2, ncp, PAIR_W), lambda b, h, pt: (b, 0, 0, 0, 0)),
        scratch_shapes=[pltpu.VMEM((rows, PAIR_W), F32)] * 4
        + [pltpu.VMEM((4, ncp, 2 * PAIR_W), F32), pltpu.SemaphoreType.DMA(())],
    )
    return pl.pallas_call(
        kern,
        grid_spec=grid_spec,
        out_shape=jax.ShapeDtypeStruct((sb, 2, 2, ncp, PAIR_W), BF16),
        compiler_params=_cparams("arbitrary", "arbitrary"),
        name="nsa_page_compress",
    )(page_table.reshape(-1), cache4, wbd, w2bd, pos)


def _row_slopes(slope_ref, g, rows):
    hh = lax.broadcasted_iota(jnp.int32, (rows, 1), 0) % NSA_HPG
    col = jnp.zeros((rows, 1), F32)
    for h in range(NSA_HPG):
        col = jnp.where(hh == h, slope_ref[g * NSA_HPG + h], col)
    return col


def _sample_cmp_kernel(slope_ref, q_ref, kvc_ref, ocmp_ref, imp_ref, *, past, nsp):
    rows = q_ref.shape[1]
    ncp = kvc_ref.shape[2]
    t_col = lax.broadcasted_iota(jnp.int32, (rows, 1), 0) // NSA_HPG
    jrow = lax.broadcasted_iota(jnp.int32, (1, ncp), 1)
    dist = (past + t_col) - (CMP_STRIDE * jrow + (CMP_LEN - 1))
    mask = (dist >= 0) & (jrow < ncp - 1)
    distf = dist.astype(F32)
    ri = lax.broadcasted_iota(jnp.int32, (rows, rows), 0) // NSA_HPG
    ci = lax.broadcasted_iota(jnp.int32, (rows, rows), 1) // NSA_HPG
    same_t = (ri == ci).astype(BF16)
    jj = lax.broadcasted_iota(jnp.int32, (ncp, nsp), 0) * CMP_STRIDE
    nn = lax.broadcasted_iota(jnp.int32, (ncp, nsp), 1) * SEL_BLK
    ov = ((jj < nn + SEL_BLK) & (jj + CMP_LEN > nn)).astype(BF16)
    for g in range(NSA_KV_HEADS):
        pr, half = divmod(g, 2)
        lanes = slice(half * HEAD_DIM, (half + 1) * HEAD_DIM)
        s_c = _dot_nt(q_ref[g], kvc_ref[0, pr, :, lanes])
        s = jnp.where(mask, s_c - _row_slopes(slope_ref, g, rows) * distf, MASK_NEG)
        e = jnp.where(mask, jnp.exp(s - jnp.max(s, axis=-1, keepdims=True)), 0.0)
        pc = e / jnp.maximum(jnp.sum(e, axis=-1, keepdims=True), 1e-30)
        ocmp_ref[g] = _dot(pc.astype(BF16), kvc_ref[1, pr])[:, lanes]
        hi = pc.astype(BF16)
        lo = (pc - hi.astype(F32)).astype(BF16)
        pcs = _dot(same_t, hi) + _dot(same_t, lo)
        hi = pcs.astype(BF16)
        lo = (pcs - hi.astype(F32)).astype(BF16)
        imp_ref[g] = _dot(hi, ov) + _dot(lo, ov)


def _sample_cmp(slopes, q_rows, kvc, *, past, nsp):
    sb, ng, rows, hd = q_rows.shape
    ncp = kvc.shape[3]
    kern = functools.partial(_sample_cmp_kernel, past=past, nsp=nsp)
    return pl.pallas_call(
        kern,
        grid=(sb,),
        in_specs=[pl.BlockSpec(memory_space=pltpu.SMEM),
                  pl.BlockSpec((None, ng, rows, hd), lambda b: (b, 0, 0, 0)),
                  pl.BlockSpec((None, 2, 2, ncp, PAIR_W), lambda b: (b, 0, 0, 0, 0))],
        out_specs=[pl.BlockSpec((None, ng, rows, hd), lambda b: (b, 0, 0, 0)),
                   pl.BlockSpec((None, ng, rows, nsp), lambda b: (b, 0, 0, 0))],
        out_shape=[jax.ShapeDtypeStruct((sb, ng, rows, hd), F32),
                   jax.ShapeDtypeStruct((sb, ng, rows, nsp), F32)],
        compiler_params=_cparams("parallel"),
        name="nsa_sample_cmp",
    )(slopes, q_rows, kvc)


def _sample_topk_kernel(imp_ref, idx_ref, score_scr, *, past, n_sel, st):
    nsr, ncol = imp_ref.shape
    bid = lax.broadcasted_iota(jnp.int32, (nsr, ncol), 0)
    tq = past + lax.broadcasted_iota(jnp.int32, (nsr, ncol), 1) % st
    cur = tq // SEL_BLK
    valid = (bid * SEL_BLK <= tq) & (bid < n_sel)
    forced = (bid == 0) | (bid == cur) | (bid == cur - 1)
    score = jnp.where(valid, imp_ref[...] + jnp.where(forced, FORCE_BONUS, 0.0), MASK_NEG)
    score_scr[...] = score

    def body(m, cnt):
        sm = score_scr[pl.ds(m, 1), :]
        ahead = (sm > score) | ((sm == score) & (m < bid))
        return cnt + ahead.astype(jnp.int32)

    cnt = lax.fori_loop(0, n_sel, body, jnp.zeros((nsr, ncol), jnp.int32))
    sel = (cnt < SEL_TOPK) & (bid < n_sel)
    tril = (lax.broadcasted_iota(jnp.int32, (nsr, nsr), 1) <= lax.broadcasted_iota(jnp.int32, (nsr, nsr), 0))
    prefix = _dot(tril.astype(BF16), sel.astype(BF16))
    for k in range(SEL_TOPK):
        hit = sel & (prefix == float(k + 1))
        idx_ref[k:k + 1, :] = jnp.sum(jnp.where(hit, bid, 0), axis=0, keepdims=True)


def _sample_topk(imp_t, *, past, n_sel, st):
    nsr, ncol = imp_t.shape
    kern = functools.partial(_sample_topk_kernel, past=past, n_sel=n_sel, st=st)
    return pl.pallas_call(
        kern,
        grid=(1,),
        in_specs=[_full(imp_t.shape)],
        out_specs=_full((SEL_TOPK, ncol)),
        out_shape=jax.ShapeDtypeStruct((SEL_TOPK, ncol), jnp.int32),
        scratch_shapes=[pltpu.VMEM((nsr, ncol), F32)],
        compiler_params=_cparams("arbitrary"),
        name="nsa_sample_topk",
    )(imp_t)


def _joint_softmax_pv(s_a, v_a_fn, s_b, v_b):
    m = jnp.maximum(jnp.max(s_a, axis=-1, keepdims=True), jnp.max(s_b, axis=-1, keepdims=True))
    p_a = jnp.where(s_a > 0.5 * MASK_NEG, jnp.exp(s_a - m), 0.0)
    p_b = jnp.where(s_b > 0.5 * MASK_NEG, jnp.exp(s_b - m), 0.0)
    l = jnp.sum(p_a, axis=-1, keepdims=True) + jnp.sum(p_b, axis=-1, keepdims=True)
    acc = v_a_fn(p_a.astype(BF16)) + _dot(p_b.astype(BF16), v_b)
    return acc / jnp.maximum(l, 1e-30)


def _sample_attn_kernel(idx_ref, pt_ref, slope_ref, cache_ref, q_ref, idxv_ref, e16_ref, gate_ref, ocmp_ref,
                        ksn_ref, vsn_ref, kwn_ref, vwn_ref, kwin_ref, vwin_ref, o_ref, kbuf, vbuf, sem,
                        *, past, n_pages, st):
    b = pl.program_id(0)
    pr = pl.program_id(1)
    rows = q_ref.shape[1]
    page = kbuf.shape[-1] // SEL_TOPK
    nkey = SEL_TOPK * page
    last_blk = past // SEL_BLK - 1

    copies = []
    for half in range(2):
        g = pr * 2 + half
        for t in range(st):
            for k in range(SEL_TOPK):
                blk = jnp.minimum(idx_ref[((b * NSA_KV_HEADS + g) * st + t) * SEL_TOPK + k], last_blk)
                pg = pt_ref[b * n_pages + lax.shift_right_logical(blk, 1)]
                dst = pl.ds(k * page, page)
                for c, buf in ((2, kbuf), (3, vbuf)):
                    cp = pltpu.make_async_copy(cache_ref.at[0, pg, c, g], buf.at[half, t, :, dst], sem)
                    cp.start()
                    copies.append(cp)
    for cp in copies:
        cp.wait()

    t_col = lax.broadcasted_iota(jnp.int32, (rows, 1), 0) // NSA_HPG
    tqf = (past + t_col).astype(F32)
    col = lax.broadcasted_iota(jnp.int32, (1, nkey), 1)
    off = (col % SEL_BLK).astype(F32)
    col_half = ((col % page) // SEL_BLK).astype(F32)
    tp = lax.broadcasted_iota(jnp.int32, (1, ksn_ref.shape[0]), 1)
    dn = t_col - tp
    valid_n = (dn >= 0) & (tp < st)
    dnf = dn.astype(F32)
    wb = kwin_ref.shape[0]
    dist_w = (wb + t_col) - lax.broadcasted_iota(jnp.int32, (1, wb), 1)
    valid_w = (dist_w >= 0) & (dist_w < WINDOW)

    for half in range(2):
        g = pr * 2 + half
        lanes = slice(half * HEAD_DIM, (half + 1) * HEAD_DIM)
        q = q_ref[half]
        slope = _row_slopes(slope_ref, g, rows)

        blk = _dot(idxv_ref[half], e16_ref[...])
        is_new = blk > (last_blk + 0.5)
        dist = tqf - (blk * SEL_BLK + off)
        in_half = (blk - 2.0 * jnp.floor(0.5 * blk)) == col_half
        valid = jnp.logical_not(is_new) & in_half & (dist >= 0)
        s_sel = jnp.full((rows, nkey), MASK_NEG, F32)
        for t in range(st):
            s_t = _dot(q, kbuf[half, t].astype(BF16))
            s_sel = jnp.where(t_col == t, s_t, s_sel)
        s_sel = jnp.where(valid, s_sel - slope * dist, MASK_NEG)
        has_new = jnp.max(is_new.astype(F32), axis=-1, keepdims=True) > 0.5
        s_new = jnp.where(valid_n & has_new, _dot_nt(q, ksn_ref[:, lanes]) - slope * dnf, MASK_NEG)

        def pv_sel(p):
            acc = jnp.zeros((rows, HEAD_DIM), F32)
            for t in range(st):
                acc = jnp.where(t_col == t, _dot_nt(p, vbuf[half, t].astype(BF16)), acc)
            return acc

        o_sel = _joint_softmax_pv(s_sel, pv_sel, s_new, vsn_ref[:, lanes])

        s_w = _dot_nt(q, kwin_ref[:, lanes].astype(BF16))
        s_w = jnp.where(valid_w, s_w - slope * dist_w.astype(F32), MASK_NEG)
        s_wn = jnp.where(valid_n, _dot_nt(q, kwn_ref[:, lanes]) - slope * dnf, MASK_NEG)
        vw = vwin_ref[...].astype(BF16)
        o_w = _joint_softmax_pv(s_w, lambda p: _dot(p, vw), s_wn, vwn_ref[...])[:, lanes]

        sg = jax.nn.sigmoid(gate_ref[half])
        o_ref[half] = sg[:, 0:1] * ocmp_ref[half] + sg[:, 1:2] * o_sel + sg[:, 2:3] * o_w


def _sample_attention(idx_flat, pt_flat, slopes, cache4, q_rows, idx_rows, e16, gate_rows, ocmp, new16, win_buf,
                      *, past, n_pages, st):
    sb, ng, rows, hd = q_rows.shape
    wb = win_buf.shape[1]
    nkey = e16.shape[1]
    pairg = lambda w: pl.BlockSpec((None, 2, rows, w), lambda b, p, *_: (b, p, 0, 0))
    newc = lambda c: pl.BlockSpec((None, new16.shape[1], PAIR_W), lambda b, p, *_: (b, 0, 2 * c + p))
    kern = functools.partial(_sample_attn_kernel, past=past, n_pages=n_pages, st=st)
    grid_spec = pltpu.PrefetchScalarGridSpec(
        num_scalar_prefetch=2,
        grid=(sb, 2),
        in_specs=[pl.BlockSpec(memory_space=pltpu.SMEM), pl.BlockSpec(memory_space=pl.ANY),
                  pairg(hd), pairg(LANES), pl.BlockSpec(e16.shape, lambda b, p, *_: (0, 0)), pairg(LANES), pairg(hd),
                  newc(2), newc(3), newc(4), newc(5),
                  pl.BlockSpec((None, wb, PAIR_W), lambda b, p, *_: (b, 0, p)),
                  pl.BlockSpec((None, wb, PAIR_W), lambda b, p, *_: (b, 0, 2 + p))],
        out_specs=pairg(hd),
        scratch_shapes=[pltpu.VMEM((2, st, HEAD_DIM, nkey), F32), pltpu.VMEM((2, st, HEAD_DIM, nkey), F32),
                        pltpu.SemaphoreType.DMA(())],
    )
    return pl.pallas_call(
        kern,
        grid_spec=grid_spec,
        out_shape=jax.ShapeDtypeStruct((sb, ng, rows, hd), F32),
        compiler_params=_cparams("arbitrary", "arbitrary"),
        name="nsa_sample_attn",
    )(idx_flat, pt_flat, slopes, cache4, q_rows, idx_rows, e16, gate_rows, ocmp, new16, new16, new16, new16,
      win_buf, win_buf)


def _pad_cols(w, mult=LANES):
    n = w.shape[1]
    return jnp.pad(w, ((0, 0), (0, -n % mult)))


def _prep_nsa_in(w_in):
    body = w_in[:, :NSA_Q_W + 6 * NSA_KV_W]
    gates = w_in[:, NSA_Q_W + 6 * NSA_KV_W:]
    per_pair = 2 * NSA_HPG * 3
    blocks = [_pad_cols(gates[:, p * per_pair:(p + 1) * per_pair]) for p in range(2)]
    return jnp.concatenate([body] + blocks, axis=1).astype(BF16)


def _prep_cmp(w_cmp1, w_cmp2, cmp_pos):
    hd = HEAD_DIM
    z = jnp.zeros((2, CMP_STRIDE, hd, hd), F32)
    wa, wb = w_cmp1[:, :CMP_STRIDE], w_cmp1[:, CMP_STRIDE:]
    top = jnp.concatenate([wa, z, wb, z], axis=-1)
    bot = jnp.concatenate([z, wa, z, wb], axis=-1)
    wbd = jnp.concatenate([top, bot], axis=-2).astype(BF16)
    z2 = jnp.zeros((2, hd, hd), F32)
    w2bd = jnp.concatenate([jnp.concatenate([w_cmp2, z2], -1), jnp.concatenate([z2, w_cmp2], -1)], -2).astype(BF16)
    pos = cmp_pos.transpose(1, 0, 2)
    pos = jnp.concatenate([pos, pos], axis=-1)
    rows = jnp.stack([pos[:, :CMP_STRIDE], pos[:, CMP_STRIDE:]], axis=2)
    pos_tiles = jnp.pad(rows, ((0, 0), (0, 0), (0, SUBLANES - 2), (0, 0)))
    return wbd, w2bd, pos_tiles


def _alibi_slopes():
    h = jnp.arange(1, NSA_HEADS + 1, dtype=F32)
    return jnp.exp2(-8.0 * h / NSA_HEADS)


def _sel_expand(t):
    key_blk = jnp.arange(t, dtype=jnp.int32).reshape(t // SEL_CHUNK, 1, SEL_CHUNK) // SEL_BLK
    blk = jnp.arange(t // SEL_BLK, dtype=jnp.int32).reshape(1, t // SEL_BLK, 1)
    return (key_blk == blk).astype(BF16)


def _prompt_rows_tile(m):
    for tm in (512, 256, 128, 64, 32, 16, 8):
        if m % tm == 0:
            return tm
    raise ValueError(m)


def kernel(x_prompt, x_sample, cache_nsa_kv, cache_nsa_win, state_mlstm_C, state_mlstm_n, state_mlstm_m,
           state_ffn_conv, page_table, norm_g, w_nsa_in, w_nsa_out, w_cmp1, w_cmp2, cmp_pos, w_ml_in,
           b_ml_gate, ml_head_norm, w_ml_out, w_ffn_up, ffn_conv_w, ffn_conv_b, w_ffn_down):
    nb, t, d = x_prompt.shape
    sb, st, _ = x_sample.shape
    mp = nb * t
    tm = _prompt_rows_tile(mp)
    slopes = _alibi_slopes()

    w_in0 = _prep_nsa_in(w_nsa_in[0])
    w_out0 = w_nsa_out[0].astype(BF16)
    wbd, w2bd, pos_tiles = _prep_cmp(w_cmp1[0], w_cmp2[0], cmp_pos[0])
    w_ml = _pad_cols(w_ml_in[0]).astype(BF16)
    w_mlo = w_ml_out[0].astype(BF16)
    w_up = w_ffn_up.astype(BF16)
    w_dn = w_ffn_down.astype(BF16)
    bg_row = _pad_cols(b_ml_gate[0].reshape(1, 2 * ML_HEADS))
    g = norm_g[:, :, None, :]

    xp = x_prompt.reshape(mp, d)
    q, kva, kvb, win, gates = _nsa_project(xp, g[0, 0], w_in0, tm)
    kc, vc = _nsa_compress(kva, wbd, w2bd, pos_tiles, nb=nb, t=t)
    o = _nsa_attention5(_slope_feats(), q, gates, kc, vc, kvb, nb=nb, t=t)
    xp = _out_project(o, xp, w_out0, g[0, 1], tm)
    kv_p = kva.reshape(1, nb, t, 4, NSA_KV_HEADS, HEAD_DIM)
    w_keep = min(WINDOW, t)
    win_p = win.reshape(nb, t, 2, NSA_KV_HEADS, HEAD_DIM)[None, :, t - w_keep:]

    halo = SUBLANES
    zstate = jnp.zeros((nb, halo, 2 * D_FF), F32)
    tmf = min(t, 1024)
    conv_p = []

    def ffn_prompt(xp, i):
        xo, ta, tb = _conv_ffn(xp, zstate, g[i, 2], g[i, 3], w_up[i], ffn_conv_w[i], ffn_conv_b[i][None],
                               w_dn[i], tm=tmf, fc=512, shift=1, tiles_per_seq=t // tmf)
        tail = jnp.concatenate([ta, tb], axis=-1).reshape(nb, t // tmf, halo, 2 * D_FF)
        conv_p.append(tail[:, -1, halo - (CONV_W - 1):])
        return xo

    xp = ffn_prompt(xp, 0)

    lp = math.gcd(t, ML_CHUNK)
    qkv, om, gm = _ml_project(xp, g[1, 0], w_ml, tm)
    zc = jnp.zeros((nb, ML_HEADS, ML_DV, ML_DQK), F32)
    zn = jnp.zeros((nb, ML_HEADS, 1, ML_DQK), F32)
    zm = jnp.zeros((nb, ML_HEADS, 1, 1), F32)
    ym, c_p, n_p, m_p = _mlstm(qkv, om, gm, bg_row, ml_head_norm[0][None], zc, zn, zm, nb=nb, lp=lp, lv=lp)
    xp = _out_project(ym, xp, w_mlo, g[1, 1], tm)
    xp = ffn_prompt(xp, 1)

    ms = sb * st
    ng, hpg, hd = NSA_KV_HEADS, NSA_HPG, HEAD_DIM
    xs = x_sample.transpose(1, 0, 2).reshape(ms, d)
    qs, kva_s, kvb_s, win_s, gates_s = _nsa_project(xs, g[0, 0], w_in0, ms)
    n_phys, page = cache_nsa_kv.shape[1:3]
    n_pages = page_table.shape[1]
    past = n_pages * page
    cache4 = cache_nsa_kv.transpose(0, 1, 3, 4, 5, 2)
    kvc = _page_compress_t(page_table, cache4, wbd, w2bd, pos_tiles)

    rows = st * hpg
    q_rows = qs.reshape(st, sb, ng, hpg, hd).transpose(1, 2, 0, 3, 4).reshape(sb, ng, rows, hd)
    n_sel = -(-(past + st) // SEL_BLK)
    nsp = -(-n_sel // LANES) * LANES
    nsr = -(-n_sel // SUBLANES) * SUBLANES
    ocmp, imp = _sample_cmp(slopes, q_rows, kvc, past=past, nsp=nsp)
    imp_t = imp[:, :, ::hpg, :nsr].reshape(sb * ng * st, nsr).T
    idx = _sample_topk(imp_t, past=past, n_sel=n_sel, st=st)
    idx_bgtk = idx.T.reshape(sb, ng, st, SEL_TOPK)
    idx_rows = _pad_cols(jnp.repeat(idx_bgtk, hpg, axis=2).reshape(sb * ng * rows, SEL_TOPK))
    idx_rows = idx_rows.reshape(sb, ng, rows, LANES).astype(BF16)
    e16 = (jnp.arange(LANES, dtype=jnp.int32)[:, None]
           == jnp.arange(SEL_TOPK * page, dtype=jnp.int32)[None, :] // page).astype(BF16)
    gate_rows = gates_s.reshape(st, sb, 2, LANES)[..., :2 * hpg * 3].reshape(st, sb, ng, hpg, 3)
    gate_rows = _pad_cols(gate_rows.transpose(1, 2, 0, 3, 4).reshape(sb * ng * rows, 3)).reshape(sb, ng, rows, LANES)
    new16 = jnp.pad(kvb_s.reshape(st, sb, -1).transpose(1, 0, 2), ((0, 0), (0, 16 - st), (0, 0)))
    wb = cache_nsa_win.shape[2]
    win_buf = cache_nsa_win[0].reshape(sb, wb, 2 * NSA_KV_W)
    o_s = _sample_attention(idx_bgtk.reshape(-1), page_table.reshape(-1), slopes, cache4, q_rows, idx_rows, e16,
                            gate_rows, ocmp, new16, win_buf, past=past, n_pages=n_pages, st=st)
    o_s = o_s.reshape(sb, ng, st, hpg, hd).transpose(2, 0, 1, 3, 4).reshape(ms, NSA_Q_W).astype(BF16)
    xs = _out_project(o_s, xs, w_out0, g[0, 1], ms)
    kv_s = kva_s.reshape(st, sb, 4, ng, hd).transpose(1, 0, 2, 3, 4)[None]
    win_new = win_s.reshape(st, sb, 2, ng, hd).transpose(1, 0, 2, 3, 4)
    win_s_out = jnp.concatenate([cache_nsa_win[0], win_new], axis=1)[None, :, st:]

    conv_s = []

    def ffn_sample(xs, i):
        state = state_ffn_conv[i].transpose(1, 0, 2).reshape(1, (CONV_W - 1) * sb, 2 * D_FF)
        xo, ta, tb = _conv_ffn(xs, state, g[i, 2], g[i, 3], w_up[i], ffn_conv_w[i], ffn_conv_b[i][None],
                               w_dn[i], tm=ms, fc=512, shift=sb, tiles_per_seq=1)
        tail = jnp.concatenate([ta, tb], axis=-1).reshape(CONV_W - 1, sb, 2 * D_FF)
        conv_s.append(tail.transpose(1, 0, 2))
        return xo

    xs = ffn_sample(xs, 0)

    lps = SUBLANES
    to_seq = lambda a: jnp.pad(a.reshape(st, sb, -1).transpose(1, 0, 2),
                               ((0, 0), (0, lps - st), (0, 0))).reshape(sb * lps, -1)
    qkv_s, om_s, gm_s = _ml_project(xs, g[1, 0], w_ml, ms)
    ym_s, c_s, n_s, m_s = _mlstm(to_seq(qkv_s), to_seq(om_s), to_seq(gm_s), bg_row, ml_head_norm[0][None],
                                 state_mlstm_C[0], state_mlstm_n[0][:, :, None, :],
                                 state_mlstm_m[0][:, :, None, None], nb=sb, lp=lps, lv=math.gcd(st, ML_CHUNK))
    ym_s = ym_s.reshape(sb, lps, ML_V_W)[:, :st].transpose(1, 0, 2).reshape(ms, ML_V_W)
    xs = _out_project(ym_s, xs, w_mlo, g[1, 1], ms)
    xs = ffn_sample(xs, 1)

    return (xp.reshape(nb, t, d), xs.reshape(st, sb, d).transpose(1, 0, 2), kv_p, kv_s, win_p, win_s_out,
            c_p[None], c_s[None], n_p[None, :, :, 0], n_s[None, :, :, 0],
            m_p[None, :, :, 0, 0], m_s[None, :, :, 0, 0], jnp.stack(conv_p), jnp.stack(conv_s))
```

```python
import functools
import math

import jax
import jax.numpy as jnp
from jax import lax
from jax.experimental import pallas as pl
from jax.experimental.pallas import tpu as pltpu

F32 = jnp.float32
BF16 = jnp.bfloat16

LANES = 128
SUBLANES = 8
VMEM_LIMIT_BYTES = 56 * 1024 * 1024

D_MODEL = 1024
NSA_HEADS = 16
NSA_KV_HEADS = 4
NSA_HPG = NSA_HEADS // NSA_KV_HEADS
HEAD_DIM = D_MODEL // NSA_HEADS
CMP_LEN = 32
CMP_STRIDE = 16
SEL_BLK = 64
SEL_TOPK = 16
WINDOW = 512
Q_BLK = 128
FORCE_BONUS = 1e4
NSA_Q_W = NSA_HEADS * HEAD_DIM
NSA_KV_W = NSA_KV_HEADS * HEAD_DIM
ML_HEADS = 8
ML_DQK = D_MODEL // (2 * ML_HEADS)
ML_DV = D_MODEL // ML_HEADS
ML_CHUNK = 64
ML_QK_W = ML_HEADS * ML_DQK
ML_V_W = ML_HEADS * ML_DV
D_FF = 4 * D_MODEL
CONV_W = 3
RMS_EPS = 1e-6
MASK_NEG = -1e30
LOG2E = 1.4426950408889634
ML_SEQ_PER_STEP = 4
SEL_CHUNK = 512
PAIR_W = 2 * HEAD_DIM


def _cparams(*sem):
    return pltpu.CompilerParams(dimension_semantics=sem, vmem_limit_bytes=VMEM_LIMIT_BYTES)


def _dot(a, b):
    return jnp.dot(a, b, preferred_element_type=F32)


def _dot_nt(a, b):
    return lax.dot_general(a, b, (((1,), (1,)), ((), ())), preferred_element_type=F32)


def _dot_tn(a, b):
    return lax.dot_general(a, b, (((0,), (0,)), ((), ())), preferred_element_type=F32)


def _split3(x):
    hi = x.astype(BF16)
    r1 = x - hi.astype(F32)
    mid = r1.astype(BF16)
    lo = (r1 - mid.astype(F32)).astype(BF16)
    return hi, mid, lo


def _rms(x, g):
    return x * lax.rsqrt(jnp.mean(x * x, axis=-1, keepdims=True) + RMS_EPS) * g


def _gelu_tanh(x):
    return 0.5 * x * (1.0 + jnp.tanh(0.7978845608028654 * (x + 0.044715 * (x * x * x))))


def _full(shape):
    return pl.BlockSpec(shape, lambda *_: (0,) * len(shape))


def _nsa_proj_kernel(x_ref, g_ref, w_ref, q_ref, q2_ref, kva_ref, kvb_ref, win_ref, gate_ref):
    h = _rms(x_ref[...], g_ref[...]).astype(BF16)
    p = _dot(h, w_ref[...])
    kvw = 4 * NSA_KV_W
    q_ref[...] = (p[:, :NSA_Q_W] * HEAD_DIM ** -0.5).astype(BF16)
    q2_ref[...] = (p[:, :NSA_Q_W] * (HEAD_DIM ** -0.5 * LOG2E)).astype(BF16)
    kva_ref[...] = p[:, NSA_Q_W:NSA_Q_W + kvw]
    kvb_ref[...] = p[:, NSA_Q_W:NSA_Q_W + 6 * NSA_KV_W].astype(BF16)
    win_ref[...] = p[:, NSA_Q_W + kvw:NSA_Q_W + 6 * NSA_KV_W]
    gate_ref[...] = p[:, NSA_Q_W + 6 * NSA_KV_W:]


def _nsa_project(x, g, w, tm):
    m = x.shape[0]
    n = w.shape[1]
    row = lambda i: (i, 0)
    return pl.pallas_call(
        _nsa_proj_kernel,
        grid=(m // tm,),
        in_specs=[pl.BlockSpec((tm, D_MODEL), row), _full((1, D_MODEL)), _full((D_MODEL, n))],
        out_specs=[pl.BlockSpec((tm, NSA_Q_W), row), pl.BlockSpec((tm, NSA_Q_W), row),
                   pl.BlockSpec((tm, 4 * NSA_KV_W), row),
                   pl.BlockSpec((tm, 6 * NSA_KV_W), row), pl.BlockSpec((tm, 2 * NSA_KV_W), row),
                   pl.BlockSpec((tm, 2 * LANES), row)],
        out_shape=[jax.ShapeDtypeStruct((m, NSA_Q_W), BF16), jax.ShapeDtypeStruct((m, NSA_Q_W), BF16),
                   jax.ShapeDtypeStruct((m, 4 * NSA_KV_W), F32),
                   jax.ShapeDtypeStruct((m, 6 * NSA_KV_W), BF16), jax.ShapeDtypeStruct((m, 2 * NSA_KV_W), F32),
                   jax.ShapeDtypeStruct((m, 2 * LANES), F32)],
        compiler_params=_cparams("parallel"),
        name="nsa_proj",
    )(x, g, w)


def _ml_proj_kernel(x_ref, g_ref, w_ref, qkv_ref, o_ref, gate_ref):
    h = _rms(x_ref[...], g_ref[...]).astype(BF16)
    p = _dot(h, w_ref[...])
    a = 2 * ML_QK_W + ML_V_W
    qkv_ref[:, :ML_QK_W] = p[:, :ML_QK_W].astype(BF16)
    qkv_ref[:, ML_QK_W:2 * ML_QK_W] = (p[:, ML_QK_W:2 * ML_QK_W] * ML_DQK ** -0.5).astype(BF16)
    qkv_ref[:, 2 * ML_QK_W:] = p[:, 2 * ML_QK_W:a].astype(BF16)
    o_ref[...] = p[:, a:a + ML_V_W]
    gate_ref[...] = p[:, a + ML_V_W:]


def _ml_project(x, g, w, tm):
    m = x.shape[0]
    n = w.shape[1]
    a = 2 * ML_QK_W + ML_V_W
    row = lambda i: (i, 0)
    return pl.pallas_call(
        _ml_proj_kernel,
        grid=(m // tm,),
        in_specs=[pl.BlockSpec((tm, D_MODEL), row), _full((1, D_MODEL)), _full((D_MODEL, n))],
        out_specs=[pl.BlockSpec((tm, a), row), pl.BlockSpec((tm, ML_V_W), row), pl.BlockSpec((tm, LANES), row)],
        out_shape=[jax.ShapeDtypeStruct((m, a), BF16), jax.ShapeDtypeStruct((m, ML_V_W), F32),
                   jax.ShapeDtypeStruct((m, LANES), F32)],
        compiler_params=_cparams("parallel"),
        name="ml_proj",
    )(x, g, w)


def _out_proj_kernel(o_ref, x_ref, w_ref, g_ref, xo_ref):
    y = _dot(o_ref[...], w_ref[...])
    xo_ref[...] = x_ref[...] + _rms(y, g_ref[...])


def _out_project(o, x, w, g, tm):
    m = x.shape[0]
    row = lambda i: (i, 0)
    return pl.pallas_call(
        _out_proj_kernel,
        grid=(m // tm,),
        in_specs=[pl.BlockSpec((tm, o.shape[1]), row), pl.BlockSpec((tm, D_MODEL), row),
                  _full(w.shape), _full((1, D_MODEL))],
        out_specs=pl.BlockSpec((tm, D_MODEL), row),
        out_shape=jax.ShapeDtypeStruct((m, D_MODEL), F32),
        compiler_params=_cparams("parallel"),
        name="out_proj",
    )(o, x, w, g)


def _ffn_kernel(x_ref, sta_ref, stb_ref, g2_ref, g3_ref, wa_ref, wb_ref, cwa_ref, cwb_ref, cba_ref, cbb_ref,
                wd_ref, xo_ref, taila_ref, tailb_ref, h_scr, ua_scr, ub_scr, ca_scr, cb_scr, acc_scr,
                *, tm, halo, shift, tiles_per_seq):
    i = pl.program_id(0)
    c = pl.program_id(1)

    @pl.when(c == 0)
    def _():
        h_scr[...] = _rms(x_ref[...], g2_ref[...]).astype(BF16)
        acc_scr[...] = jnp.zeros_like(acc_scr)

    first = (i % tiles_per_seq) == 0

    @pl.when(first)
    def _():
        ua_scr[0:halo, :] = sta_ref[...]
        ub_scr[0:halo, :] = stb_ref[...]

    @pl.when(jnp.logical_not(first))
    def _():
        ua_scr[0:halo, :] = ca_scr[c]
        ub_scr[0:halo, :] = cb_scr[c]

    h = h_scr[...]
    ua_scr[halo:halo + tm, :] = _dot(h, wa_ref[...])
    ub_scr[halo:halo + tm, :] = _dot(h, wb_ref[...])
    ta = ua_scr[tm:tm + halo, :]
    tb = ub_scr[tm:tm + halo, :]
    ca_scr[c] = ta
    cb_scr[c] = tb
    taila_ref[...] = ta
    tailb_ref[...] = tb

    def conv(u_scr, cw_ref, cb_ref):
        cw = cw_ref[...]
        return (cb_ref[...] + cw[2:3, :] * u_scr[halo:halo + tm, :]
                + cw[1:2, :] * u_scr[halo - shift:halo - shift + tm, :]
                + cw[0:1, :] * u_scr[halo - 2 * shift:halo - 2 * shift + tm, :])

    y = _gelu_tanh(conv(ua_scr, cwa_ref, cba_ref)) * conv(ub_scr, cwb_ref, cbb_ref)
    acc_scr[...] += _dot(y.astype(BF16), wd_ref[...])

    @pl.when(c == pl.num_programs(1) - 1)
    def _():
        xo_ref[...] = x_ref[...] + _rms(acc_scr[...], g3_ref[...])


def _conv_ffn(x, state, g2, g3, w_up, conv_w, conv_b, w_down, *, tm, fc, shift, tiles_per_seq):
    m = x.shape[0]
    halo = state.shape[1]
    nfc = D_FF // fc
    n_tiles = m // tm
    row = lambda i, c: (i, 0)
    const = lambda i, c: (0, 0)
    kern = functools.partial(_ffn_kernel, tm=tm, halo=halo, shift=shift, tiles_per_seq=tiles_per_seq)
    return pl.pallas_call(
        kern,
        grid=(n_tiles, nfc),
        in_specs=[
            pl.BlockSpec((tm, D_MODEL), row),
            pl.BlockSpec((None, halo, fc), lambda i, c: (i // tiles_per_seq, 0, c)),
            pl.BlockSpec((None, halo, fc), lambda i, c: (i // tiles_per_seq, 0, nfc + c)),
            pl.BlockSpec((1, D_MODEL), const), pl.BlockSpec((1, D_MODEL), const),
            pl.BlockSpec((D_MODEL, fc), lambda i, c: (0, c)),
            pl.BlockSpec((D_MODEL, fc), lambda i, c: (0, nfc + c)),
            pl.BlockSpec((CONV_W, fc), lambda i, c: (0, c)),
            pl.BlockSpec((CONV_W, fc), lambda i, c: (0, nfc + c)),
            pl.BlockSpec((1, fc), lambda i, c: (0, c)),
            pl.BlockSpec((1, fc), lambda i, c: (0, nfc + c)),
            pl.BlockSpec((fc, D_MODEL), lambda i, c: (c, 0)),
        ],
        out_specs=[pl.BlockSpec((tm, D_MODEL), row),
                   pl.BlockSpec((None, halo, fc), lambda i, c: (i, 0, c)),
                   pl.BlockSpec((None, halo, fc), lambda i, c: (i, 0, c))],
        out_shape=[jax.ShapeDtypeStruct((m, D_MODEL), F32),
                   jax.ShapeDtypeStruct((n_tiles, halo, D_FF), F32),
                   jax.ShapeDtypeStruct((n_tiles, halo, D_FF), F32)],
        scratch_shapes=[pltpu.VMEM((tm, D_MODEL), BF16),
                        pltpu.VMEM((halo + tm, fc), F32), pltpu.VMEM((halo + tm, fc), F32),
                        pltpu.VMEM((nfc, halo, fc), F32), pltpu.VMEM((nfc, halo, fc), F32),
                        pltpu.VMEM((tm, D_MODEL), F32)],
        compiler_params=_cparams("arbitrary", "arbitrary"),
        name="conv_ffn",
    )(x, state, state, g2, g3, w_up, w_up, conv_w, conv_w, conv_b, conv_b, w_down)


def _log_sigmoid(x):
    return jnp.minimum(x, 0.0) - jnp.log1p(jnp.exp(-jnp.abs(x)))


def _exact_nt(sel_bf16, x):
    hi, mid, lo = _split3(x)
    return _dot_nt(sel_bf16, hi) + _dot_nt(sel_bf16, mid) + _dot_nt(sel_bf16, lo)


def _eye(n, m, dtype):
    return (lax.broadcasted_iota(jnp.int32, (n, m), 0) == lax.broadcasted_iota(jnp.int32, (n, m), 1)).astype(dtype)


def _mlstm_kernel(qkv_ref, o_ref, gate_ref, bg_ref, hn_ref, c0_ref, n0_ref, m0_ref,
                  y_ref, cf_ref, nf_ref, mf_ref, ct_scr, n_scr, m_scr, *, lp, lv, bb):
    for bi in range(bb):
        _mlstm_one(qkv_ref.at[bi], o_ref.at[bi], gate_ref.at[bi], bg_ref, hn_ref, c0_ref.at[bi], n0_ref.at[bi],
                   m0_ref.at[bi], y_ref.at[bi], cf_ref.at[bi], nf_ref.at[bi], mf_ref.at[bi],
                   ct_scr.at[bi], n_scr.at[bi], m_scr.at[bi], lp=lp, lv=lv)


def _mlstm_one(qkv_ref, o_ref, gate_ref, bg_ref, hn_ref, c0_ref, n0_ref, m0_ref,
               y_ref, cf_ref, nf_ref, mf_ref, ct_scr, n_scr, m_scr, *, lp, lv):
    ci = pl.program_id(1)
    eye_qk = _eye(ML_DQK, ML_DQK, BF16)

    @pl.when(ci == 0)
    def _():
        for h in range(ML_HEADS):
            ct_scr[h] = _exact_nt(eye_qk, c0_ref[h])
        n_scr[...] = n0_ref[...]
        m_scr[...] = m0_ref[...]

    gp = gate_ref[...] + bg_ref[...]
    ls = _log_sigmoid(gp)
    r_i = lax.broadcasted_iota(jnp.int32, (lp, lp), 0)
    c_i = lax.broadcasted_iota(jnp.int32, (lp, lp), 1)
    causal = c_i <= r_i
    tril = causal.astype(BF16)
    hi, mid, lo = _split3(ls)
    b_all = _dot(tril, hi) + _dot(tril, mid) + _dot(tril, lo)
    sel16 = _eye(2 * ML_HEADS, LANES, BF16)
    gp_t = _exact_nt(sel16, gp)
    b_t = _exact_nt(sel16, b_all)
    row_valid = lax.broadcasted_iota(jnp.int32, (lp, 1), 0) < lv

    for h in range(ML_HEADS):
        q = qkv_ref[:, h * ML_DQK:(h + 1) * ML_DQK]
        k = qkv_ref[:, ML_QK_W + h * ML_DQK:ML_QK_W + (h + 1) * ML_DQK]
        v = qkv_ref[:, 2 * ML_QK_W + h * ML_DV:2 * ML_QK_W + (h + 1) * ML_DV]
        ct = ct_scr[h]
        n_row = n_scr[h]
        m = m_scr[h]
        b_col = b_all[:, ML_HEADS + h:ML_HEADS + h + 1]
        i_col = gp[:, h:h + 1]
        b_row = b_t[ML_HEADS + h:ML_HEADS + h + 1, :]
        i_row = gp_t[h:h + 1, :]
        dmat = jnp.where(causal, b_col - b_row + i_row, MASK_NEG)
        inter = b_col + m
        mt = jnp.maximum(inter, jnp.max(dmat, axis=-1, keepdims=True))
        s = _dot_nt(q, k) * jnp.exp(dmat - mt)
        wi = jnp.exp(inter - mt)
        qf = q.astype(F32)
        num = wi * _dot(q, ct.astype(BF16)) + _dot(s.astype(BF16), v)
        den = wi * jnp.sum(qf * n_row, axis=-1, keepdims=True) + jnp.sum(s, axis=-1, keepdims=True)
        hc = num / jnp.maximum(jnp.abs(den), jnp.exp(-mt))
        m_last = mt[lv - 1:lv, :]
        b_last = b_col[lv - 1:lv, :]
        decay = jnp.exp(b_last + m - m_last)
        ws = jnp.where(row_valid, jnp.exp(b_last - b_col + i_col - m_last), 0.0)
        ct_scr[h] = decay * ct + _dot_tn(k, (ws * v.astype(F32)).astype(BF16))
        n_scr[h] = decay * n_row + jnp.sum(ws * k.astype(F32), axis=0, keepdims=True)
        m_scr[h] = m_last
        hn = hc * lax.rsqrt(jnp.mean(hc * hc, axis=-1, keepdims=True) + RMS_EPS)
        sl = slice(h * ML_DV, (h + 1) * ML_DV)
        y_ref[:, sl] = (hn * hn_ref[:, sl] * jax.nn.sigmoid(o_ref[:, sl])).astype(BF16)

    @pl.when(ci == pl.num_programs(1) - 1)
    def _():
        eye_v = _eye(ML_DV, ML_DV, BF16)
        for h in range(ML_HEADS):
            cf_ref[h] = _exact_nt(eye_v, ct_scr[h])
        nf_ref[...] = n_scr[...]
        mf_ref[...] = m_scr[...]


def _mlstm(qkv, o, gates, b_gate, head_norm, c0, n0, m0, *, nb, lp, lv):
    m = qkv.shape[0]
    tseq = m // nb
    bb = math.gcd(nb, ML_SEQ_PER_STEP)
    seq3 = lambda a: a.reshape(nb, tseq, a.shape[1])
    row = lambda b, c: (b, c, 0)
    st4 = lambda b, c: (b, 0, 0, 0)
    kern = functools.partial(_mlstm_kernel, lp=lp, lv=lv, bb=bb)
    y, cf, nf, mf = pl.pallas_call(
        kern,
        grid=(nb // bb, tseq // lp),
        in_specs=[pl.BlockSpec((bb, lp, qkv.shape[1]), row), pl.BlockSpec((bb, lp, ML_V_W), row),
                  pl.BlockSpec((bb, lp, LANES), row), pl.BlockSpec((1, LANES), lambda b, c: (0, 0)),
                  pl.BlockSpec((1, ML_V_W), lambda b, c: (0, 0)),
                  pl.BlockSpec((bb, ML_HEADS, ML_DV, ML_DQK), st4),
                  pl.BlockSpec((bb, ML_HEADS, 1, ML_DQK), st4),
                  pl.BlockSpec((bb, ML_HEADS, 1, 1), st4)],
        out_specs=[pl.BlockSpec((bb, lp, ML_V_W), row),
                   pl.BlockSpec((bb, ML_HEADS, ML_DV, ML_DQK), st4),
                   pl.BlockSpec((bb, ML_HEADS, 1, ML_DQK), st4),
                   pl.BlockSpec((bb, ML_HEADS, 1, 1), st4)],
        out_shape=[jax.ShapeDtypeStruct((nb, tseq, ML_V_W), BF16),
                   jax.ShapeDtypeStruct((nb, ML_HEADS, ML_DV, ML_DQK), F32),
                   jax.ShapeDtypeStruct((nb, ML_HEADS, 1, ML_DQK), F32),
                   jax.ShapeDtypeStruct((nb, ML_HEADS, 1, 1), F32)],
        scratch_shapes=[pltpu.VMEM((bb, ML_HEADS, ML_DQK, ML_DV), F32),
                        pltpu.VMEM((bb, ML_HEADS, 1, ML_DQK), F32),
                        pltpu.VMEM((bb, ML_HEADS, 1, 1), F32)],
        compiler_params=_cparams("arbitrary", "arbitrary"),
        name="mlstm",
    )(seq3(qkv), seq3(o), seq3(gates), b_gate, head_norm, c0, n0, m0)
    return y.reshape(m, ML_V_W), cf, nf, mf


def _stack_cols(x, cols):
    return jnp.concatenate([x[:, c:c + 1] for c in cols], axis=0)


def _block_const(col, l, row):
    nh = col.shape[0] // l
    return jnp.concatenate([jnp.broadcast_to(col[h * l + row:h * l + row + 1, :], (l, 1)) for h in range(nh)], axis=0)


def _mlstm_stacked_one(qkv_ref, o_ref, gate_ref, bg_ref, hn_ref, c0_ref, n0_ref, m0_ref,
                       y_ref, cf_ref, nf_ref, mf_ref, ctn_scr, m_scr, blockmask, *, l):
    ci = pl.program_id(1)
    nh = ML_HEADS
    ht = nh * l
    dv = ML_DV
    lane = lax.broadcasted_iota(jnp.int32, (ht, LANES), 1)

    @pl.when(ci == 0)
    def _():
        eye = _eye(ML_DQK, ML_DQK, BF16)
        for h in range(nh):
            rows = slice(h * ML_DQK, (h + 1) * ML_DQK)
            ctn_scr[rows, 0:dv] = _exact_nt(eye, c0_ref[h])
        ctn_scr[:, dv:2 * dv] = jnp.where(lane == 0, n0_ref[...], 0.0)
        m_scr[...] = m0_ref[...]

    gp = gate_ref[...] + bg_ref[...]
    ls = _log_sigmoid(gp)
    tril = (lax.broadcasted_iota(jnp.int32, (l, l), 1) <= lax.broadcasted_iota(jnp.int32, (l, l), 0)).astype(BF16)
    hi, mid, lo = _split3(ls)
    b_all = _dot(tril, hi) + _dot(tril, mid) + _dot(tril, lo)
    sel16 = _eye(2 * nh, LANES, BF16)
    dup = lambda x: jnp.concatenate([x, x], axis=0)
    c_t = _exact_nt(sel16, dup(gp))[0:nh] - _exact_nt(sel16, dup(b_all))[nh:2 * nh]
    c_row = jnp.concatenate([jnp.broadcast_to(c_t[h:h + 1, :], (l, 2 * l)) for h in range(nh)], axis=0)
    r_b = _stack_cols(b_all, range(nh, 2 * nh))
    r_i = _stack_cols(gp, range(nh))
    t_row = lax.broadcasted_iota(jnp.int32, (ht, 1), 0) % l
    causal = (lane % l) <= t_row
    dmat = jnp.where(causal, r_b + c_row, MASK_NEG)
    m_col = m_scr[...]
    inter = r_b + m_col
    mt = jnp.maximum(inter, jnp.max(dmat, axis=-1, keepdims=True))

    q = qkv_ref[:, 0:ML_QK_W]
    k = qkv_ref[:, ML_QK_W:2 * ML_QK_W]
    zero = jnp.zeros((), BF16)
    q_bd = jnp.where(blockmask, _tile_rows(q, nh), zero)
    k_bd = jnp.where(blockmask, _tile_rows(k, nh), zero)
    s = _dot_nt(q_bd, dup(k)) * jnp.exp(dmat - mt)
    wi = jnp.exp(inter - mt)
    ones_lane = jnp.where(lane == 0, 1.0, 0.0).astype(BF16)
    v_st = jnp.concatenate([qkv_ref[:, 2 * ML_QK_W + h * dv:2 * ML_QK_W + (h + 1) * dv] for h in range(nh)], axis=0)
    v_aug = jnp.concatenate([v_st, ones_lane], axis=1)
    a = _dot(q_bd, ctn_scr[...].astype(BF16))
    s_bd = jnp.where(blockmask, jnp.concatenate([s.astype(BF16)] * (ML_QK_W // (2 * l)), axis=1), zero)
    bm = _dot(s_bd, v_aug)
    num = wi * a[:, 0:dv] + bm[:, 0:dv]
    den = wi * a[:, dv:dv + 1] + bm[:, dv:dv + 1]
    hc = num / jnp.maximum(jnp.abs(den), jnp.exp(-mt))

    m_last = _block_const(mt, l, l - 1)
    b_last = _block_const(r_b, l, l - 1)
    decay = jnp.exp(b_last + m_col - m_last)
    ws = jnp.exp(b_last - r_b + r_i - m_last)
    wv = (ws * v_aug.astype(F32)).astype(BF16)
    ctn_scr[...] = decay * ctn_scr[...] + _dot_tn(k_bd, wv)
    m_scr[...] = m_last

    hn = hc * lax.rsqrt(jnp.mean(hc * hc, axis=-1, keepdims=True) + RMS_EPS)
    for h in range(nh):
        sl = slice(h * dv, (h + 1) * dv)
        y_ref[:, sl] = (hn[h * l:(h + 1) * l] * hn_ref[:, sl] * jax.nn.sigmoid(o_ref[:, sl])).astype(BF16)

    @pl.when(ci == pl.num_programs(1) - 1)
    def _():
        eye_v = _eye(dv, dv, BF16)
        for h in range(nh):
            rows = slice(h * ML_DQK, (h + 1) * ML_DQK)
            cf_ref[h] = _exact_nt(eye_v, ctn_scr[rows, 0:dv])
        nf_ref[...] = ctn_scr[:, dv:dv + 1]
        mf_ref[...] = m_scr[...]


def _mlstm_stacked_kernel(qkv_ref, o_ref, gate_ref, bg_ref, hn_ref, c0_ref, n0_ref, m0_ref,
                          y_ref, cf_ref, nf_ref, mf_ref, ctn_scr, m_scr, *, l, bb):
    ht = ML_HEADS * l
    blockmask = (lax.broadcasted_iota(jnp.int32, (ht, ML_QK_W), 0) // l
                 == lax.broadcasted_iota(jnp.int32, (ht, ML_QK_W), 1) // ML_DQK)
    for bi in range(bb):
        _mlstm_stacked_one(qkv_ref.at[bi], o_ref.at[bi], gate_ref.at[bi], bg_ref, hn_ref, c0_ref.at[bi],
                           n0_ref.at[bi], m0_ref.at[bi], y_ref.at[bi], cf_ref.at[bi], nf_ref.at[bi],
                           mf_ref.at[bi], ctn_scr.at[bi], m_scr.at[bi], blockmask, l=l)


def _mlstm_stacked(qkv, o, gates, b_gate, head_norm, c0, n0, m0, *, nb, l):
    assert l == ML_DQK and ML_QK_W % (2 * l) == 0
    m = qkv.shape[0]
    tseq = m // nb
    bb = math.gcd(nb, ML_SEQ_PER_STEP)
    ht = ML_HEADS * l
    seq3 = lambda a: a.reshape(nb, tseq, a.shape[1])
    row = lambda b, c: (b, c, 0)
    st4 = lambda b, c: (b, 0, 0, 0)
    st3 = lambda b, c: (b, 0, 0)
    kern = functools.partial(_mlstm_stacked_kernel, l=l, bb=bb)
    y, cf, nf, mf = pl.pallas_call(
        kern,
        grid=(nb // bb, tseq // l),
        in_specs=[pl.BlockSpec((bb, l, qkv.shape[1]), row), pl.BlockSpec((bb, l, ML_V_W), row),
                  pl.BlockSpec((bb, l, LANES), row), pl.BlockSpec((1, LANES), lambda b, c: (0, 0)),
                  pl.BlockSpec((1, ML_V_W), lambda b, c: (0, 0)),
                  pl.BlockSpec((bb, ML_HEADS, ML_DV, ML_DQK), st4),
                  pl.BlockSpec((bb, ht, 1), st3), pl.BlockSpec((bb, ht, 1), st3)],
        out_specs=[pl.BlockSpec((bb, l, ML_V_W), row),
                   pl.BlockSpec((bb, ML_HEADS, ML_DV, ML_DQK), st4),
                   pl.BlockSpec((bb, ht, 1), st3), pl.BlockSpec((bb, ht, 1), st3)],
        out_shape=[jax.ShapeDtypeStruct((nb, tseq, ML_V_W), BF16),
                   jax.ShapeDtypeStruct((nb, ML_HEADS, ML_DV, ML_DQK), F32),
                   jax.ShapeDtypeStruct((nb, ht, 1), F32), jax.ShapeDtypeStruct((nb, ht, 1), F32)],
        scratch_shapes=[pltpu.VMEM((bb, ht, 2 * ML_DV), F32), pltpu.VMEM((bb, ht, 1), F32)],
        compiler_params=_cparams("arbitrary", "arbitrary"),
        name="mlstm",
    )(seq3(qkv), seq3(o), seq3(gates), b_gate, head_norm, c0,
      n0.reshape(nb, ht, 1), jnp.repeat(m0, l, axis=1))
    return (y.reshape(m, ML_V_W), cf, nf.reshape(nb, ML_HEADS, ML_DQK), mf.reshape(nb, ML_HEADS, l)[:, :, :1])


def _cmp_parts(load_rows, wbd_ref, c, p):
    acc = None
    for l in range(CMP_STRIDE):
        part = _dot(load_rows(l, c, p).astype(BF16), wbd_ref[c, l])
        acc = part if acc is None else acc + part
    return acc


def _cmp_pos_bias(pos_ref, wbd_ref, c):
    acc = None
    for l in range(CMP_STRIDE):
        part = _dot(pos_ref[c, l].astype(BF16), wbd_ref[c, l])
        acc = part if acc is None else acc + part
    return acc[0:1, :PAIR_W] + acc[1:2, PAIR_W:]


def _cmp_finish(parts, posb, w2_ref, c):
    n = parts.shape[0]
    pre = parts[:, :PAIR_W] + pltpu.roll(parts[:, PAIR_W:], n - 1, 0) + posb
    return _dot(_gelu_tanh(pre).astype(BF16), w2_ref[c])


def _nsa_compress_kernel(x00_ref, x01_ref, x10_ref, x11_ref, wbd_ref, w2_ref, pos_ref, kc_ref, vc_ref, *, nchunk):
    x_refs = ((x00_ref, x01_ref), (x10_ref, x11_ref))

    def load_rows(l, c, p):
        return x_refs[c][p][pl.ds(l, nchunk, stride=CMP_STRIDE), :]

    for c, out_ref in ((0, kc_ref), (1, vc_ref)):
        posb = _cmp_pos_bias(pos_ref, wbd_ref, c)
        for p in range(2):
            out_ref[p] = _cmp_finish(_cmp_parts(load_rows, wbd_ref, c, p), posb, w2_ref, c).astype(BF16)


def _nsa_compress(kva, wbd, w2bd, pos, *, nb, t):
    nchunk = t // CMP_STRIDE
    kern = functools.partial(_nsa_compress_kernel, nchunk=nchunk)
    out = jax.ShapeDtypeStruct((nb, 2, nchunk, PAIR_W), BF16)
    ospec = pl.BlockSpec((None, 2, nchunk, PAIR_W), lambda b: (b, 0, 0, 0))
    return pl.pallas_call(
        kern,
        grid=(nb,),
        in_specs=[pl.BlockSpec((t, PAIR_W), functools.partial(lambda j, b: (b, j), j)) for j in range(4)]
        + [_full(wbd.shape), _full(w2bd.shape), _full(pos.shape)],
        out_specs=[ospec, ospec],
        out_shape=[out, out],
        compiler_params=_cparams("parallel"),
        name="nsa_compress",
    )(kva, kva, kva, kva, wbd, w2bd, pos)


def _select_mask_t(imp_t, s0, score_scr, ns):
    qb = imp_t.shape[1]
    bid = lax.broadcasted_iota(jnp.int32, (ns, qb), 0)
    tq = s0 + lax.broadcasted_iota(jnp.int32, (ns, qb), 1)
    cur = tq // SEL_BLK
    valid = bid * SEL_BLK <= tq
    forced = (bid == 0) | (bid == cur) | (bid == cur - 1)
    score = jnp.where(valid, imp_t + jnp.where(forced, FORCE_BONUS, 0.0), MASK_NEG)
    score_scr[0:ns, :] = score

    def body(m, cnt):
        sm = score_scr[pl.ds(m, 1), :]
        ahead = (sm > score) | ((sm == score) & (m < bid))
        return cnt + ahead.astype(jnp.int32)

    cnt = lax.fori_loop(0, ns, body, jnp.zeros((ns, qb), jnp.int32), unroll=8)
    return cnt < min(SEL_TOPK, ns)


def _nsa_attn_kernel(slope_ref, q_ref, gate_ref, kc_ref, vc_ref, ksel_ref, vsel_ref, kwin_ref, vwin_ref,
                     e3_ref, o_ref, score_scr, *, t):
    pr = pl.program_id(1)
    nblk = pl.program_id(2)
    qb = Q_BLK
    s0 = nblk * qb
    ncp = t // CMP_STRIDE
    ns = t // SEL_BLK
    wk = WINDOW + qb
    hd = HEAD_DIM
    sg = jax.nn.sigmoid(gate_ref[...])

    tq_col = s0 + lax.broadcasted_iota(jnp.int32, (qb, 1), 0)
    cend = CMP_STRIDE * lax.broadcasted_iota(jnp.int32, (1, ncp), 1) + (CMP_LEN - 1)
    dist_c = tq_col - cend
    mask_c = dist_c >= 0
    distf_c = dist_c.astype(F32)
    jj = lax.broadcasted_iota(jnp.int32, (ns, ncp), 1) * CMP_STRIDE
    nn = lax.broadcasted_iota(jnp.int32, (ns, ncp), 0) * SEL_BLK
    ov_t = ((jj < nn + SEL_BLK) & (jj + CMP_LEN > nn)).astype(BF16)
    wstart = pl.multiple_of(jnp.maximum(s0 - WINDOW, 0), qb)
    dist_w = tq_col - (wstart + lax.broadcasted_iota(jnp.int32, (1, wk), 1))
    mask_w = (dist_w >= 0) & (dist_w < WINDOW)
    distf_w = dist_w.astype(F32)
    n_sel_chunks = (s0 + qb + SEL_CHUNK - 1) // SEL_CHUNK

    for half in range(2):
        lanes = slice(half * hd, (half + 1) * hd)
        qs = [q_ref[:, (half * NSA_HPG + hh) * hd:(half * NSA_HPG + hh + 1) * hd] for hh in range(NSA_HPG)]
        q_all = jnp.concatenate(qs, axis=0)
        slopes = [slope_ref[(pr * 2 + half) * NSA_HPG + hh] for hh in range(NSA_HPG)]

        s_c = _dot_nt(q_all, kc_ref[:, lanes])
        vc = vc_ref[...]
        pc_sum = jnp.zeros((qb, ncp), F32)
        o_cmp = []
        for hh in range(NSA_HPG):
            s = jnp.where(mask_c, s_c[hh * qb:(hh + 1) * qb] - slopes[hh] * distf_c, MASK_NEG)
            e = jnp.where(mask_c, jnp.exp(s - jnp.max(s, axis=-1, keepdims=True)), 0.0)
            pc = e / jnp.maximum(jnp.sum(e, axis=-1, keepdims=True), 1e-30)
            pc_sum = pc_sum + pc
            o_cmp.append(_dot(pc.astype(BF16), vc)[:, lanes])

        hi = pc_sum.astype(BF16)
        lo = (pc_sum - hi.astype(F32)).astype(BF16)
        imp_t = _dot_nt(ov_t, hi) + _dot_nt(ov_t, lo)
        sel_t = _select_mask_t(imp_t, s0, score_scr, ns)
        mneg = jnp.where(sel_t, 0.0, MASK_NEG).T.astype(BF16)

        def sel_chunk(kci, carry):
            m_run, l_run, acc = carry
            r0 = pl.multiple_of(kci * SEL_CHUNK, SEL_CHUNK)
            s_all = _dot_nt(q_all, ksel_ref[pl.ds(r0, SEL_CHUNK), lanes])
            kpos = r0 + lax.broadcasted_iota(jnp.int32, (1, SEL_CHUNK), 1)
            add = jnp.where(kpos > tq_col, MASK_NEG, _dot(mneg, e3_ref[kci]))
            prel = (kpos - s0).astype(F32)
            s = jnp.concatenate([s_all[hh * qb:(hh + 1) * qb] + (add + slopes[hh] * prel)
                                 for hh in range(NSA_HPG)], axis=0)
            m_new = jnp.maximum(m_run, jnp.max(s, axis=-1, keepdims=True))
            alpha = jnp.exp(m_run - m_new)
            p = jnp.exp(s - m_new)
            l_new = alpha * l_run + jnp.sum(p, axis=-1, keepdims=True)
            acc_new = alpha * acc + _dot(p.astype(BF16), vsel_ref[pl.ds(r0, SEL_CHUNK), :])
            return m_new, l_new, acc_new

        init = (jnp.full((NSA_HPG * qb, 1), MASK_NEG, F32), jnp.zeros((NSA_HPG * qb, 1), F32),
                jnp.zeros((NSA_HPG * qb, PAIR_W), F32))
        _, l_sel, acc_sel = lax.fori_loop(0, n_sel_chunks, sel_chunk, init)
        o_sel_all = acc_sel[:, lanes] / l_sel

        s_w = _dot_nt(q_all, kwin_ref[pl.ds(wstart, wk), lanes])
        vw = vwin_ref[pl.ds(wstart, wk), :]
        for hh in range(NSA_HPG):
            s = jnp.where(mask_w, s_w[hh * qb:(hh + 1) * qb] - slopes[hh] * distf_w, MASK_NEG)
            e = jnp.exp(s - jnp.max(s, axis=-1, keepdims=True))
            o_w = _dot(e.astype(BF16), vw)[:, lanes] / jnp.sum(e, axis=-1, keepdims=True)
            gi = (half * NSA_HPG + hh) * 3
            out = (sg[:, gi:gi + 1] * o_cmp[hh] + sg[:, gi + 1:gi + 2] * o_sel_all[hh * qb:(hh + 1) * qb]
                   + sg[:, gi + 2:gi + 3] * o_w)
            oc = (half * NSA_HPG + hh) * hd
            o_ref[:, oc:oc + hd] = out.astype(BF16)


def _nsa_attention(slopes, q, gates, kc, vc, kvb, e3, *, nb, t):
    nq = t // Q_BLK
    ncp = t // CMP_STRIDE
    pw = 2 * NSA_HPG * HEAD_DIM
    kvcol = lambda c: (lambda b, p, n: (b, 2 * c + p))
    kern = functools.partial(_nsa_attn_kernel, t=t)
    return pl.pallas_call(
        kern,
        grid=(nb, 2, nq),
        in_specs=[pl.BlockSpec(memory_space=pltpu.SMEM),
                  pl.BlockSpec((Q_BLK, pw), lambda b, p, n: (b * nq + n, p)),
                  pl.BlockSpec((Q_BLK, LANES), lambda b, p, n: (b * nq + n, p)),
                  pl.BlockSpec((None, None, ncp, PAIR_W), lambda b, p, n: (b, p, 0, 0)),
                  pl.BlockSpec((None, None, ncp, PAIR_W), lambda b, p, n: (b, p, 0, 0)),
                  pl.BlockSpec((t, PAIR_W), kvcol(2)), pl.BlockSpec((t, PAIR_W), kvcol(3)),
                  pl.BlockSpec((t, PAIR_W), kvcol(4)), pl.BlockSpec((t, PAIR_W), kvcol(5)),
                  _full(e3.shape)],
        out_specs=pl.BlockSpec((Q_BLK, pw), lambda b, p, n: (b * nq + n, p)),
        out_shape=jax.ShapeDtypeStruct((nb * t, NSA_Q_W), BF16),
        scratch_shapes=[pltpu.VMEM((max(t // SEL_BLK, SUBLANES), Q_BLK), F32)],
        compiler_params=_cparams("parallel", "parallel", "arbitrary"),
        name="nsa_attn",
    )(slopes, q, gates, kc, vc, kvb, kvb, kvb, kvb, e3)


KEY_CHUNK = 256


def _rank_topk_t(score, ns, qb):
    blocks = [score[SUBLANES * r:SUBLANES * (r + 1)] for r in range(ns // SUBLANES)]
    cnts = [jnp.zeros((SUBLANES, qb), jnp.int32) for _ in blocks]
    sub = lax.broadcasted_iota(jnp.int32, (SUBLANES, qb), 0)
    for m in range(ns):
        sm = jnp.broadcast_to(score[m:m + 1, :], (SUBLANES, qb))
        for r, blk in enumerate(blocks):
            lo = SUBLANES * r
            if lo > m:
                ahead = (sm >= blk).astype(jnp.int32)
            elif lo + SUBLANES - 1 <= m:
                ahead = (sm > blk).astype(jnp.int32)
            else:
                ahead = jnp.where(sub > m - lo, (sm >= blk).astype(jnp.int32), (sm > blk).astype(jnp.int32))
            cnts[r] = cnts[r] + ahead
    return jnp.concatenate(cnts, axis=0) < min(SEL_TOPK, ns)


def _online_unit(q, k, vaug, bias, u, m_scr, acc_scr):
    s = _dot_nt(q, k) + bias
    m_old = m_scr[u]
    m_new = jnp.maximum(m_old, jnp.max(s, axis=-1, keepdims=True))
    alpha = jnp.exp(m_old - m_new)
    p = jnp.exp(s - m_new)
    acc_scr[u] = alpha * acc_scr[u] + _dot(p.astype(BF16), vaug)
    m_scr[u] = m_new


def _nsa_attn2_kernel(slope_ref, q_ref, gate_ref, kc_ref, vc_ref, ksel_ref, vsel_ref, kwin_ref, vwin_ref,
                      e3_ref, o_ref, vs_scr, vw_scr, m_scr, acc_scr, pcs_scr, *, t):
    pr = pl.program_id(1)
    nblk = pl.program_id(2)
    qb = Q_BLK
    s0 = nblk * qb
    ncp = t // CMP_STRIDE
    ns = t // SEL_BLK
    hd = HEAD_DIM
    w = KEY_CHUNK
    nunit = 2 * NSA_HPG

    @pl.when(nblk == 0)
    def _():
        lane = lax.broadcasted_iota(jnp.int32, (t, PAIR_W), 1)
        for src, dst in ((vsel_ref, vs_scr), (vwin_ref, vw_scr)):
            v = src[...].astype(F32)
            dst[0] = jnp.where(lane < hd, v, 1.0).astype(BF16)
            dst[1] = jnp.where(lane < hd, pltpu.roll(v, hd, 1), 1.0).astype(BF16)

    sg = jax.nn.sigmoid(gate_ref[...])
    tq_col = s0 + lax.broadcasted_iota(jnp.int32, (qb, 1), 0)
    slopes = [[slope_ref[(pr * 2 + half) * NSA_HPG + hh] for hh in range(NSA_HPG)] for half in range(2)]
    q_of = lambda half, hh: q_ref[:, (half * NSA_HPG + hh) * hd:(half * NSA_HPG + hh + 1) * hd]
    lanes_of = lambda half: slice(half * hd, (half + 1) * hd)

    def reset_state():
        m_scr[...] = jnp.full(m_scr.shape, MASK_NEG, F32)
        acc_scr[...] = jnp.zeros(acc_scr.shape, F32)

    def read_out(u):
        a = acc_scr[u]
        return a[:, :hd] * (1.0 / a[:, hd:hd + 1])

    cend = CMP_STRIDE * lax.broadcasted_iota(jnp.int32, (1, ncp), 1) + (CMP_LEN - 1)
    add_c = jnp.where(tq_col >= cend, 0.0, MASK_NEG)
    prel_c = (cend - s0).astype(F32)
    jj = lax.broadcasted_iota(jnp.int32, (ns, ncp), 1) * CMP_STRIDE
    nn = lax.broadcasted_iota(jnp.int32, (ns, ncp), 0) * SEL_BLK
    ov_t = ((jj < nn + SEL_BLK) & (jj + CMP_LEN > nn)).astype(BF16)
    bid = lax.broadcasted_iota(jnp.int32, (ns, qb), 0)
    tq_row = s0 + lax.broadcasted_iota(jnp.int32, (ns, qb), 1)
    cur = tq_row // SEL_BLK
    valid_b = bid * SEL_BLK <= tq_row
    bonus = jnp.where((bid == 0) | (bid == cur) | (bid == cur - 1), FORCE_BONUS, 0.0)
    o_cmp = [[None] * NSA_HPG for _ in range(2)]
    mneg = []
    for half in range(2):
        kc = kc_ref[:, lanes_of(half)]
        vc = vc_ref[...]
        for hh in range(NSA_HPG):
            s = _dot_nt(q_of(half, hh), kc) + (add_c + slopes[half][hh] * prel_c)
            m = jnp.max(s, axis=-1, keepdims=True)
            e = jnp.where(s > 0.5 * MASK_NEG, jnp.exp(s - m), 0.0)
            pc = e * (1.0 / jnp.maximum(jnp.sum(e, axis=-1, keepdims=True), 1e-30))
            if hh == 0:
                pcs_scr[...] = pc
            else:
                pcs_scr[...] += pc
            o_cmp[half][hh] = _dot(pc.astype(BF16), vc)[:, lanes_of(half)]
        pc_sum = pcs_scr[...]
        hi = pc_sum.astype(BF16)
        lo = (pc_sum - hi.astype(F32)).astype(BF16)
        imp_t = _dot_nt(ov_t, hi) + _dot_nt(ov_t, lo)
        score = jnp.where(valid_b, imp_t + bonus, MASK_NEG)
        sel_t = _rank_topk_t(score, ns, qb)
        mneg.append(jnp.where(sel_t, 0.0, MASK_NEG).T.astype(BF16))

    reset_state()

    def sel_body(kci, carry):
        r0 = pl.multiple_of(kci * w, w)
        kpos = r0 + lax.broadcasted_iota(jnp.int32, (1, w), 1)
        prel = (kpos - s0).astype(F32)
        for half in range(2):
            add = jnp.where(kpos > tq_col, MASK_NEG, _dot(mneg[half], e3_ref[kci]))
            k = ksel_ref[pl.ds(r0, w), lanes_of(half)]
            vaug = vs_scr[half, pl.ds(r0, w), :]
            for hh in range(NSA_HPG):
                _online_unit(q_of(half, hh), k, vaug, add + slopes[half][hh] * prel,
                             half * NSA_HPG + hh, m_scr, acc_scr)
        return carry

    lax.fori_loop(0, (s0 + qb + w - 1) // w, sel_body, 0)
    o_sel = [read_out(u) for u in range(nunit)]

    reset_state()
    wspan = -(-(WINDOW + qb) // w) * w
    wstart = pl.multiple_of(jnp.maximum(s0 + qb - wspan, 0), qb)
    for ci in range(wspan // w):
        r0 = pl.multiple_of(wstart + ci * w, qb)
        kpos = r0 + lax.broadcasted_iota(jnp.int32, (1, w), 1)
        dist = tq_col - kpos
        add = jnp.where((dist >= 0) & (dist < WINDOW), 0.0, MASK_NEG)
        prel = (kpos - s0).astype(F32)
        for half in range(2):
            k = kwin_ref[pl.ds(r0, w), lanes_of(half)]
            vaug = vw_scr[half, pl.ds(r0, w), :]
            for hh in range(NSA_HPG):
                _online_unit(q_of(half, hh), k, vaug, add + slopes[half][hh] * prel,
                             half * NSA_HPG + hh, m_scr, acc_scr)

    for half in range(2):
        for hh in range(NSA_HPG):
            u = half * NSA_HPG + hh
            gi = u * 3
            out = (sg[:, gi:gi + 1] * o_cmp[half][hh] + sg[:, gi + 1:gi + 2] * o_sel[u]
                   + sg[:, gi + 2:gi + 3] * read_out(u))
            o_ref[:, u * hd:(u + 1) * hd] = out.astype(BF16)


def _nsa_attention2(slopes, q, gates, kc, vc, kvb, e3, *, nb, t):
    nq = t // Q_BLK
    ncp = t // CMP_STRIDE
    pw = 2 * NSA_HPG * HEAD_DIM
    nunit = 2 * NSA_HPG
    kvcol = lambda c: (lambda b, p, n: (b, 2 * c + p))
    kern = functools.partial(_nsa_attn2_kernel, t=t)
    return pl.pallas_call(
        kern,
        grid=(nb, 2, nq),
        in_specs=[pl.BlockSpec(memory_space=pltpu.SMEM),
                  pl.BlockSpec((Q_BLK, pw), lambda b, p, n: (b * nq + n, p)),
                  pl.BlockSpec((Q_BLK, LANES), lambda b, p, n: (b * nq + n, p)),
                  pl.BlockSpec((None, None, ncp, PAIR_W), lambda b, p, n: (b, p, 0, 0)),
                  pl.BlockSpec((None, None, ncp, PAIR_W), lambda b, p, n: (b, p, 0, 0)),
                  pl.BlockSpec((t, PAIR_W), kvcol(2)), pl.BlockSpec((t, PAIR_W), kvcol(3)),
                  pl.BlockSpec((t, PAIR_W), kvcol(4)), pl.BlockSpec((t, PAIR_W), kvcol(5)),
                  _full(e3.shape)],
        out_specs=pl.BlockSpec((Q_BLK, pw), lambda b, p, n: (b * nq + n, p)),
        out_shape=jax.ShapeDtypeStruct((nb * t, NSA_Q_W), BF16),
        scratch_shapes=[pltpu.VMEM((2, t, PAIR_W), BF16), pltpu.VMEM((2, t, PAIR_W), BF16),
                        pltpu.VMEM((nunit, Q_BLK, 1), F32), pltpu.VMEM((nunit, Q_BLK, PAIR_W), F32),
                        pltpu.VMEM((Q_BLK, ncp), F32)],
        compiler_params=_cparams("parallel", "parallel", "arbitrary"),
        name="nsa_attn",
    )(slopes, q, gates, kc, vc, kvb, kvb, kvb, kvb, e3)


def _sel_expand2(t):
    key_blk = jnp.arange(t, dtype=jnp.int32).reshape(t // KEY_CHUNK, 1, KEY_CHUNK) // SEL_BLK
    blk = jnp.arange(t // SEL_BLK, dtype=jnp.int32).reshape(1, t // SEL_BLK, 1)
    return (key_blk == blk).astype(BF16)


N_FEAT = 6
AUG_W = 2 * LANES


def _key_feats(kpos_col):
    r = kpos_col.shape[0]
    lane = lax.broadcasted_iota(jnp.int32, (r, LANES), 1)
    a = lax.shift_right_logical(kpos_col, 6).astype(F32)
    b = (kpos_col & (SEL_BLK - 1)).astype(F32)
    return jnp.where(lane < 3, a, jnp.where(lane < N_FEAT, b, 0.0))


def _tile_rows(x, n):
    return jnp.concatenate([x] * n, axis=0)


def _nsa_attn3_kernel(sf_ref, q_ref, gate_ref, kc_ref, vc_ref, ksel_ref, vsel_ref, kwin_ref, vwin_ref, o_ref,
                      ks_scr, kw_scr, kc_scr, vs_scr, vw_scr, qs_scr, qw_scr, qc_scr, mrun_scr, m_scr, acc_scr,
                      *, t):
    pr = pl.program_id(1)
    nblk = pl.program_id(2)
    qb = Q_BLK
    s0 = nblk * qb
    ncp = t // CMP_STRIDE
    ns = t // SEL_BLK
    hd = HEAD_DIM
    w = KEY_CHUNK
    hpg = NSA_HPG
    rows_all = hpg * qb

    @pl.when(nblk == 0)
    def _():
        lane = lax.broadcasted_iota(jnp.int32, (t, LANES), 1)
        row = lax.broadcasted_iota(jnp.int32, (t, 1), 0)
        feats = _key_feats(row).astype(BF16)
        onehot = jnp.where(lane - hd == lax.shift_right_logical(row, 6), 1.0, 0.0)
        lane_w = lax.broadcasted_iota(jnp.int32, (WINDOW, LANES), 1)
        lane_c = lax.broadcasted_iota(jnp.int32, (ncp, LANES), 1)
        cend = CMP_STRIDE * lax.broadcasted_iota(jnp.int32, (ncp, 1), 0) + (CMP_LEN - 1)
        ks = ksel_ref[...].astype(F32)
        vs = vsel_ref[...].astype(F32)
        kwn = kwin_ref[...].astype(F32)
        vwn = vwin_ref[...].astype(F32)
        kcv = kc_ref[...].astype(F32)
        for half in range(2):
            low = (lambda x: x) if half == 0 else (lambda x: pltpu.roll(x, hd, 1))
            ks_scr[half, :, 0:LANES] = jnp.where(lane < hd, low(ks), onehot).astype(BF16)
            ks_scr[half, :, LANES:AUG_W] = feats
            vs_scr[half] = jnp.where(lane < hd, low(vs), 1.0).astype(BF16)
            kw_scr[half, 0:WINDOW, 0:LANES] = jnp.where(lane_w == hd, 1.0, 0.0).astype(BF16)
            kw_scr[half, 0:WINDOW, LANES:AUG_W] = jnp.zeros((WINDOW, LANES), BF16)
            kw_scr[half, WINDOW:WINDOW + t, 0:LANES] = jnp.where(lane < hd, low(kwn), 0.0).astype(BF16)
            kw_scr[half, WINDOW:WINDOW + t, LANES:AUG_W] = feats
            vw_scr[half, 0:WINDOW] = jnp.ones((WINDOW, LANES), BF16)
            vw_scr[half, WINDOW:WINDOW + t] = jnp.where(lane < hd, low(vwn), 1.0).astype(BF16)
            kc_scr[half, :, 0:LANES] = jnp.where(lane_c < hd, low(kcv), 0.0).astype(BF16)
            kc_scr[half, :, LANES:AUG_W] = _key_feats(cend).astype(BF16)

    sg = jax.nn.sigmoid(gate_ref[...])
    lane = lax.broadcasted_iota(jnp.int32, (qb, LANES), 1)
    lane1 = lax.broadcasted_iota(jnp.int32, (1, LANES), 1)
    t_loc = lax.broadcasted_iota(jnp.int32, (qb, 1), 0)
    tq_col = s0 + t_loc
    pad_row = jnp.where(lane1 == hd, MASK_NEG, 0.0)

    for half in range(2):
        for hh in range(hpg):
            u = half * hpg + hh
            qcol = q_ref[:, (u // 2) * LANES:(u // 2 + 1) * LANES].astype(F32)
            qlow = jnp.where(lane < hd, qcol if u % 2 == 0 else pltpu.roll(qcol, hd, 1), 0.0)
            feat = jnp.zeros((1, LANES), F32)
            for j in range(N_FEAT):
                feat = jnp.where(lane1 == j, sf_ref[(pr * 2 + half) * hpg + hh, j], feat)
            feat = jnp.broadcast_to(feat, (qb, LANES)).astype(BF16)
            rows = slice(hh * qb, (hh + 1) * qb)
            qc_scr[half, rows, 0:LANES] = qlow.astype(BF16)
            qw_scr[half, rows, 0:LANES] = (qlow + pad_row).astype(BF16)
            for scr in (qc_scr, qw_scr, qs_scr):
                scr[half, rows, LANES:AUG_W] = feat

    def online(half, s, vaug):
        nj = s.shape[1] // LANES
        mx = s[:, 0:LANES]
        for j in range(1, nj):
            mx = jnp.maximum(mx, s[:, j * LANES:(j + 1) * LANES])
        m_old = m_scr[half]
        m_new = jnp.maximum(m_old, jnp.max(mx, axis=-1, keepdims=True))
        alpha = jnp.exp(m_old - m_new)
        p = jnp.exp(s - jnp.concatenate([m_new] * nj, axis=1))
        acc_scr[half] = alpha * acc_scr[half] + _dot(p.astype(BF16), vaug)
        m_scr[half] = m_new

    def read_out(half):
        a = acc_scr[half]
        return a[:, 0:hd] * (1.0 / a[:, hd:hd + 1])

    def reset(scr, val):
        scr[...] = jnp.full(scr.shape, val, F32)

    cend_row = CMP_STRIDE * lax.broadcasted_iota(jnp.int32, (1, ncp), 1) + (CMP_LEN - 1)
    add_c = _tile_rows(jnp.where(tq_col >= cend_row, 0.0, MASK_NEG), hpg)
    jj = lax.broadcasted_iota(jnp.int32, (ns, ncp), 1) * CMP_STRIDE
    nn = lax.broadcasted_iota(jnp.int32, (ns, ncp), 0) * SEL_BLK
    ov_t = ((jj < nn + SEL_BLK) & (jj + CMP_LEN > nn)).astype(BF16)
    bid = lax.broadcasted_iota(jnp.int32, (ns, qb), 0)
    tq_row = s0 + lax.broadcasted_iota(jnp.int32, (ns, qb), 1)
    cur = tq_row // SEL_BLK
    valid_b = bid * SEL_BLK <= tq_row
    bonus = jnp.where((bid == 0) | (bid == cur) | (bid == cur - 1), FORCE_BONUS, 0.0)
    o_cmp = []
    for half in range(2):
        s = _dot_nt(qc_scr[half], kc_scr[half]) + add_c
        m = jnp.max(s, axis=-1, keepdims=True)
        e = jnp.where(s > 0.5 * MASK_NEG, jnp.exp(s - m), 0.0)
        pc = e * (1.0 / jnp.maximum(jnp.sum(e, axis=-1, keepdims=True), 1e-30))
        o_cmp.append(_dot(pc.astype(BF16), vc_ref[...])[:, half * hd:(half + 1) * hd])
        pc_sum = pc[0:qb]
        for hh in range(1, hpg):
            pc_sum = pc_sum + pc[hh * qb:(hh + 1) * qb]
        hi = pc_sum.astype(BF16)
        lo = (pc_sum - hi.astype(F32)).astype(BF16)
        imp_t = _dot_nt(ov_t, hi) + _dot_nt(ov_t, lo)
        sel_t = _rank_topk_t(jnp.where(valid_b, imp_t + bonus, MASK_NEG), ns, qb)
        pieces = [jnp.zeros((hd, qb), F32), jnp.where(sel_t, 0.0, MASK_NEG)]
        if LANES - hd - ns > 0:
            pieces.append(jnp.zeros((LANES - hd - ns, qb), F32))
        mneg = jnp.concatenate(pieces, axis=0).T
        for hh in range(hpg):
            rows = slice(hh * qb, (hh + 1) * qb)
            qs_scr[half, rows, 0:LANES] = (qc_scr[half, rows, 0:LANES].astype(F32) + mneg).astype(BF16)

    nfull = nblk // (w // qb)
    r_tail = pl.multiple_of(nfull * w, w)
    kpos_tail = r_tail + lax.broadcasted_iota(jnp.int32, (1, w), 1)
    causal_add = _tile_rows(jnp.where(kpos_tail > tq_col, MASK_NEG, 0.0), hpg)

    def sel_scores(half, r0, causal):
        s = _dot_nt(qs_scr[half], ks_scr[half, pl.ds(r0, w), :])
        return s + causal_add if causal else s

    def sel_body(kci, carry):
        r0 = pl.multiple_of(kci * w, w)
        for half in range(2):
            online(half, sel_scores(half, r0, False), vs_scr[half, pl.ds(r0, w), :])
        return carry

    reset(m_scr, MASK_NEG)
    reset(acc_scr, 0.0)
    lax.fori_loop(0, nfull, sel_body, 0)
    for half in range(2):
        online(half, sel_scores(half, r_tail, True), vs_scr[half, pl.ds(r_tail, w), :])
    o_sel = [read_out(half) for half in range(2)]

    c_loc = lax.broadcasted_iota(jnp.int32, (1, qb), 1)
    left_add = _tile_rows(jnp.where(c_loc > t_loc, 0.0, MASK_NEG), hpg)
    diag_add = _tile_rows(jnp.concatenate([jnp.zeros((qb, qb), F32),
                                           jnp.where(c_loc <= t_loc, 0.0, MASK_NEG)], axis=1), hpg)
    win_chunks = ((0, qb, left_add), (qb, w, None), (qb + w, w, diag_add))

    reset(m_scr, MASK_NEG)
    reset(acc_scr, 0.0)
    for off, wd, add in win_chunks:
        r0 = pl.multiple_of(s0 + off, qb)
        for half in range(2):
            s = _dot_nt(qw_scr[half], kw_scr[half, pl.ds(r0, wd), :])
            online(half, s if add is None else s + add, vw_scr[half, pl.ds(r0, wd), :])

    for half in range(2):
        o_w = read_out(half)
        for hh in range(hpg):
            u = half * hpg + hh
            gi = u * 3
            rows = slice(hh * qb, (hh + 1) * qb)
            out = (sg[:, gi:gi + 1] * o_cmp[half][rows] + sg[:, gi + 1:gi + 2] * o_sel[half][rows]
                   + sg[:, gi + 2:gi + 3] * o_w[rows])
            o_ref[:, u * hd:(u + 1) * hd] = out.astype(BF16)


def _nsa_attention3(slope_feats, q, gates, kc, vc, kvb, *, nb, t):
    assert WINDOW + Q_BLK == Q_BLK + 2 * KEY_CHUNK and t % KEY_CHUNK == 0 and t // SEL_BLK <= LANES - HEAD_DIM
    nq = t // Q_BLK
    ncp = t // CMP_STRIDE
    pw = 2 * NSA_HPG * HEAD_DIM
    rows_all = NSA_HPG * Q_BLK
    kvcol = lambda c: (lambda b, p, n: (b, 2 * c + p))
    kern = functools.partial(_nsa_attn3_kernel, t=t)
    return pl.pallas_call(
        kern,
        grid=(nb, 2, nq),
        in_specs=[pl.BlockSpec(memory_space=pltpu.SMEM),
                  pl.BlockSpec((Q_BLK, pw), lambda b, p, n: (b * nq + n, p)),
                  pl.BlockSpec((Q_BLK, LANES), lambda b, p, n: (b * nq + n, p)),
                  pl.BlockSpec((None, None, ncp, PAIR_W), lambda b, p, n: (b, p, 0, 0)),
                  pl.BlockSpec((None, None, ncp, PAIR_W), lambda b, p, n: (b, p, 0, 0)),
                  pl.BlockSpec((t, PAIR_W), kvcol(2)), pl.BlockSpec((t, PAIR_W), kvcol(3)),
                  pl.BlockSpec((t, PAIR_W), kvcol(4)), pl.BlockSpec((t, PAIR_W), kvcol(5))],
        out_specs=pl.BlockSpec((Q_BLK, pw), lambda b, p, n: (b * nq + n, p)),
        out_shape=jax.ShapeDtypeStruct((nb * t, NSA_Q_W), BF16),
        scratch_shapes=[pltpu.VMEM((2, t, AUG_W), BF16), pltpu.VMEM((2, WINDOW + t, AUG_W), BF16),
                        pltpu.VMEM((2, ncp, AUG_W), BF16),
                        pltpu.VMEM((2, t, LANES), BF16), pltpu.VMEM((2, WINDOW + t, LANES), BF16),
                        pltpu.VMEM((2, rows_all, AUG_W), BF16), pltpu.VMEM((2, rows_all, AUG_W), BF16),
                        pltpu.VMEM((2, rows_all, AUG_W), BF16),
                        pltpu.VMEM((2, rows_all, LANES), F32), pltpu.VMEM((2, rows_all, LANES), F32),
                        pltpu.VMEM((2, rows_all, LANES), F32)],
        compiler_params=_cparams("parallel", "parallel", "arbitrary"),
        name="nsa_attn",
    )(slope_feats, q, gates, kc, vc, kvb, kvb, kvb, kvb)


def _slope_feats():
    s = _alibi_slopes() * LOG2E
    hi = s.astype(BF16).astype(F32)
    mid = (s - hi).astype(BF16).astype(F32)
    lo = (s - hi - mid).astype(BF16).astype(F32)
    z = jnp.zeros_like(s)
    return jnp.stack([SEL_BLK * hi, SEL_BLK * mid, SEL_BLK * lo, hi, mid, lo, z, z], axis=1)


WIN_JOBS = 3
WIN_PAD = WIN_JOBS * KEY_CHUNK - Q_BLK
PAD_FEAT = N_FEAT
NO_SPAN = 1 << 30


def _nsa_attn5_kernel(sf_ref, q_ref, gate_ref, kc_ref, vc_ref, ksel_ref, vsel_ref, kwin_ref, vwin_ref, o_ref,
                      k_scr, v_scr, kc_scr, qa_scr, qc_scr, s_scr, mx_scr, m_scr, acc_scr, *, t):
    pr = pl.program_id(1)
    nblk = pl.program_id(2)
    qb = Q_BLK
    s0 = nblk * qb
    ncp = t // CMP_STRIDE
    ns = t // SEL_BLK
    hd = HEAD_DIM
    w = KEY_CHUNK
    hpg = NSA_HPG
    rows_all = hpg * qb
    wrow0 = t

    @pl.when(nblk == 0)
    def _():
        lane = lax.broadcasted_iota(jnp.int32, (t, LANES), 1)
        row = lax.broadcasted_iota(jnp.int32, (t, 1), 0)
        feats = _key_feats(row).astype(BF16)
        onehot = jnp.where(lane - hd == lax.shift_right_logical(row, 6), 1.0, 0.0)
        lane_p = lax.broadcasted_iota(jnp.int32, (WIN_PAD, LANES), 1)
        lane_c = lax.broadcasted_iota(jnp.int32, (ncp, LANES), 1)
        cend = CMP_STRIDE * lax.broadcasted_iota(jnp.int32, (ncp, 1), 0) + (CMP_LEN - 1)
        ks = ksel_ref[...].astype(F32)
        vs = vsel_ref[...].astype(F32)
        kwn = kwin_ref[...].astype(F32)
        vwn = vwin_ref[...].astype(F32)
        kcv = kc_ref[...].astype(F32)
        for half in range(2):
            low = (lambda x: x) if half == 0 else (lambda x: pltpu.roll(x, hd, 1))
            k_scr[half, 0:t, 0:LANES] = jnp.where(lane < hd, low(ks), onehot).astype(BF16)
            k_scr[half, 0:t, LANES:AUG_W] = feats
            v_scr[half, 0:t] = jnp.where(lane < hd, low(vs), 1.0).astype(BF16)
            k_scr[half, wrow0:wrow0 + WIN_PAD, 0:LANES] = jnp.zeros((WIN_PAD, LANES), BF16)
            k_scr[half, wrow0:wrow0 + WIN_PAD, LANES:AUG_W] = jnp.where(lane_p == PAD_FEAT, 1.0, 0.0).astype(BF16)
            v_scr[half, wrow0:wrow0 + WIN_PAD] = jnp.ones((WIN_PAD, LANES), BF16)
            wr = wrow0 + WIN_PAD
            k_scr[half, wr:wr + t, 0:LANES] = jnp.where(lane < hd, low(kwn), 0.0).astype(BF16)
            k_scr[half, wr:wr + t, LANES:AUG_W] = feats
            v_scr[half, wr:wr + t] = jnp.where(lane < hd, low(vwn), 1.0).astype(BF16)
            kc_scr[half, :, 0:LANES] = jnp.where(lane_c < hd, low(kcv), 0.0).astype(BF16)
            kc_scr[half, :, LANES:AUG_W] = _key_feats(cend).astype(BF16)

    sg = jax.nn.sigmoid(gate_ref[...])
    lane = lax.broadcasted_iota(jnp.int32, (qb, LANES), 1)
    lane1 = lax.broadcasted_iota(jnp.int32, (1, LANES), 1)
    t_loc = lax.broadcasted_iota(jnp.int32, (qb, 1), 0)
    tq_col = s0 + t_loc

    for half in range(2):
        for hh in range(hpg):
            u = half * hpg + hh
            qcol = q_ref[:, (u // 2) * LANES:(u // 2 + 1) * LANES].astype(F32)
            qlow = jnp.where(lane < hd, qcol if u % 2 == 0 else pltpu.roll(qcol, hd, 1), 0.0)
            feat = jnp.where(lane1 == PAD_FEAT, MASK_NEG, 0.0)
            for j in range(N_FEAT):
                feat = jnp.where(lane1 == j, sf_ref[(pr * 2 + half) * hpg + hh, j], feat)
            feat = jnp.broadcast_to(feat, (qb, LANES)).astype(BF16)
            rows = slice(hh * qb, (hh + 1) * qb)
            qc_scr[half, rows, 0:LANES] = qlow.astype(BF16)
            qc_scr[half, rows, LANES:AUG_W] = feat
            qa_scr[half, rows, LANES:AUG_W] = feat

    cend_row = CMP_STRIDE * lax.broadcasted_iota(jnp.int32, (1, ncp), 1) + (CMP_LEN - 1)
    add_c = _tile_rows(jnp.where(tq_col >= cend_row, 0.0, MASK_NEG), hpg)
    jj = lax.broadcasted_iota(jnp.int32, (ns, ncp), 1) * CMP_STRIDE
    nn = lax.broadcasted_iota(jnp.int32, (ns, ncp), 0) * SEL_BLK
    ov_t = ((jj < nn + SEL_BLK) & (jj + CMP_LEN > nn)).astype(BF16)
    bid = lax.broadcasted_iota(jnp.int32, (ns, qb), 0)
    tq_row = s0 + lax.broadcasted_iota(jnp.int32, (ns, qb), 1)
    cur = tq_row // SEL_BLK
    valid_b = bid * SEL_BLK <= tq_row
    bonus = jnp.where((bid == 0) | (bid == cur) | (bid == cur - 1), FORCE_BONUS, 0.0)
    o_cmp = []
    for half in range(2):
        s = _dot_nt(qc_scr[half], kc_scr[half]) + add_c
        m = jnp.max(s, axis=-1, keepdims=True)
        e = jnp.where(s > 0.5 * MASK_NEG, jnp.exp2(s - m), 0.0)
        pc = e * (1.0 / jnp.maximum(jnp.sum(e, axis=-1, keepdims=True), 1e-30))
        o_cmp.append(_dot(pc.astype(BF16), vc_ref[...])[:, half * hd:(half + 1) * hd])
        pc_sum = pc[0:qb]
        for hh in range(1, hpg):
            pc_sum = pc_sum + pc[hh * qb:(hh + 1) * qb]
        hi = pc_sum.astype(BF16)
        lo = (pc_sum - hi.astype(F32)).astype(BF16)
        imp_t = _dot_nt(ov_t, hi) + _dot_nt(ov_t, lo)
        sel_t = _rank_topk_t(jnp.where(valid_b, imp_t + bonus, MASK_NEG), ns, qb)
        pieces = [jnp.zeros((hd, qb), F32), jnp.where(sel_t, 0.0, MASK_NEG)]
        if LANES - hd - ns > 0:
            pieces.append(jnp.zeros((LANES - hd - ns, qb), F32))
        mneg = jnp.concatenate(pieces, axis=0).T
        for hh in range(hpg):
            rows = slice(hh * qb, (hh + 1) * qb)
            qa_scr[half, rows, 0:LANES] = (qc_scr[half, rows, 0:LANES].astype(F32) + mneg).astype(BF16)

    nfull = nblk // (w // qb)
    npairs = nfull // 2
    c_loc = lax.broadcasted_iota(jnp.int32, (1, qb), 1)
    add_w0 = _tile_rows(jnp.concatenate([jnp.full((qb, qb), MASK_NEG, F32),
                                         jnp.where(c_loc > t_loc, 0.0, MASK_NEG)], axis=1), hpg)
    add_w2 = _tile_rows(jnp.concatenate([jnp.zeros((qb, qb), F32),
                                         jnp.where(c_loc <= t_loc, 0.0, MASK_NEG)], axis=1), hpg)
    r_tail = pl.multiple_of(nfull * w, w)
    kpos_tail = r_tail + lax.broadcasted_iota(jnp.int32, (1, w), 1)
    add_tail = _tile_rows(jnp.where(kpos_tail > tq_col, MASK_NEG, 0.0), hpg)
    r_odd = pl.multiple_of(jnp.maximum(nfull - 1, 0) * w, w)
    add_odd = jnp.where(nfull % 2 == 1, 0.0, MASK_NEG)
    w_row = lambda jw: pl.multiple_of(wrow0 + s0 + jw * w, qb)
    masked_jobs = ((qc_scr, w_row(0), add_w0, 1), (qc_scr, w_row(1), None, 1), (qc_scr, w_row(2), add_w2, 1),
                   (qa_scr, r_tail, add_tail, 0), (qa_scr, r_odd, add_odd, 0))

    def loop_row(j):
        return pl.multiple_of(jnp.where(j < 2 * npairs, j * w, wrow0), w)

    def scores(q_scr, r0, add, slot):
        for half in range(2):
            s = _dot_nt(q_scr[half], k_scr[half, pl.ds(r0, w), :])
            if add is not None:
                s = s + add
            s_scr[slot, half] = s
            mx = jnp.max(jnp.maximum(s[:, 0:LANES], s[:, LANES:w]), axis=-1, keepdims=True)
            mx_scr[slot, half] = jnp.broadcast_to(mx, (rows_all, LANES))

    def consume(r0, st, slot):
        for half in range(2):
            m_old = m_scr[st, half]
            m_new = jnp.maximum(m_old, mx_scr[slot, half])
            alpha = jnp.exp2(m_old - m_new)
            p = jnp.exp2(s_scr[slot, half] - jnp.concatenate([m_new, m_new], axis=1))
            acc_scr[st, half] = alpha * acc_scr[st, half] + _dot(p.astype(BF16), v_scr[half, pl.ds(r0, w), :])
            m_scr[st, half] = m_new

    m_scr[...] = jnp.full(m_scr.shape, MASK_NEG, F32)
    acc_scr[...] = jnp.zeros(acc_scr.shape, F32)
    scores(*masked_jobs[0][:3], 0)
    for i, (_, r0, _, st) in enumerate(masked_jobs):
        if i + 1 < len(masked_jobs):
            scores(*masked_jobs[i + 1][:3], (i + 1) % 2)
        else:
            scores(qa_scr, loop_row(0), None, (i + 1) % 2)
        consume(r0, st, i % 2)
    first = len(masked_jobs) % 2

    def step(i, carry):
        scores(qa_scr, loop_row(2 * i + 1), None, 1 - first)
        consume(loop_row(2 * i), 0, first)
        scores(qa_scr, loop_row(2 * i + 2), None, first)
        consume(loop_row(2 * i + 1), 0, 1 - first)
        return carry

    lax.fori_loop(0, npairs, step, 0)

    for half in range(2):
        outs = []
        for st in range(2):
            a = acc_scr[st, half]
            outs.append(a[:, 0:hd] * (1.0 / a[:, hd:hd + 1]))
        for hh in range(hpg):
            u = half * hpg + hh
            gi = u * 3
            rows = slice(hh * qb, (hh + 1) * qb)
            out = (sg[:, gi:gi + 1] * o_cmp[half][rows] + sg[:, gi + 1:gi + 2] * outs[0][rows]
                   + sg[:, gi + 2:gi + 3] * outs[1][rows])
            o_ref[:, u * hd:(u + 1) * hd] = out.astype(BF16)


def _nsa_attention5(slope_feats, q, gates, kc, vc, kvb, *, nb, t):
    assert WIN_JOBS * KEY_CHUNK >= WINDOW + Q_BLK and t % KEY_CHUNK == 0 and t // SEL_BLK <= LANES - HEAD_DIM
    nq = t // Q_BLK
    ncp = t // CMP_STRIDE
    pw = 2 * NSA_HPG * HEAD_DIM
    rows_all = NSA_HPG * Q_BLK
    krows = 2 * t + WIN_PAD
    kvcol = lambda c: (lambda b, p, n: (b, 2 * c + p))
    kern = functools.partial(_nsa_attn5_kernel, t=t)
    return pl.pallas_call(
        kern,
        grid=(nb, 2, nq),
        in_specs=[pl.BlockSpec(memory_space=pltpu.SMEM),
                  pl.BlockSpec((Q_BLK, pw), lambda b, p, n: (b * nq + n, p)),
                  pl.BlockSpec((Q_BLK, LANES), lambda b, p, n: (b * nq + n, p)),
                  pl.BlockSpec((None, None, ncp, PAIR_W), lambda b, p, n: (b, p, 0, 0)),
                  pl.BlockSpec((None, None, ncp, PAIR_W), lambda b, p, n: (b, p, 0, 0)),
                  pl.BlockSpec((t, PAIR_W), kvcol(2)), pl.BlockSpec((t, PAIR_W), kvcol(3)),
                  pl.BlockSpec((t, PAIR_W), kvcol(4)), pl.BlockSpec((t, PAIR_W), kvcol(5))],
        out_specs=pl.BlockSpec((Q_BLK, pw), lambda b, p, n: (b * nq + n, p)),
        out_shape=jax.ShapeDtypeStruct((nb * t, NSA_Q_W), BF16),
        scratch_shapes=[pltpu.VMEM((2, krows, AUG_W), BF16), pltpu.VMEM((2, krows, LANES), BF16),
                        pltpu.VMEM((2, ncp, AUG_W), BF16),
                        pltpu.VMEM((2, rows_all, AUG_W), BF16), pltpu.VMEM((2, rows_all, AUG_W), BF16),
                        pltpu.VMEM((2, 2, rows_all, KEY_CHUNK), F32), pltpu.VMEM((2, 2, rows_all, LANES), F32),
                        pltpu.VMEM((2, 2, rows_all, LANES), F32), pltpu.VMEM((2, 2, rows_all, LANES), F32)],
        compiler_params=_cparams("parallel", "parallel", "arbitrary"),
        name="nsa_attn",
    )(slope_feats, q, gates, kc, vc, kvb, kvb, kvb, kvb)


PAGES_PER_STEP = 32


def _page_compress_kernel(pt_ref, cache_ref, wbd_ref, w2_ref, pos_ref, kvc_ref,
                          buf0, buf1, buf2, buf3, parts_scr, sem, *, n_pages, page):
    b = pl.program_id(0)
    hf = pl.program_id(1)
    bufs = (buf0, buf1, buf2, buf3)
    copies = []
    for i in range(PAGES_PER_STEP):
        pg = pt_ref[b * n_pages + hf * PAGES_PER_STEP + i]
        for cb in range(4):
            cp = pltpu.make_async_copy(cache_ref.at[pg, :, cb, :], bufs[cb].at[pl.ds(i * page, page), :], sem)
            cp.start()
            copies.append(cp)
    for cp in copies:
        cp.wait()

    nch = PAGES_PER_STEP * page // CMP_STRIDE

    def load_rows(l, c, p):
        return bufs[c * 2 + p][pl.ds(l, nch, stride=CMP_STRIDE), :]

    row0 = pl.multiple_of(hf * nch, nch)
    for c in range(2):
        for p in range(2):
            parts_scr[c * 2 + p, pl.ds(row0, nch), :] = _cmp_parts(load_rows, wbd_ref, c, p)

    @pl.when(hf == pl.num_programs(1) - 1)
    def _():
        for c in range(2):
            posb = _cmp_pos_bias(pos_ref, wbd_ref, c)
            for p in range(2):
                kvc_ref[c, p] = _cmp_finish(parts_scr[c * 2 + p], posb, w2_ref, c).astype(BF16)


def _page_compress_t_kernel(pt_ref, cache_ref, wbd_ref, w2_ref, pos_ref, kvc_ref,
                            buf, x0, x1, x2, x3, parts_scr, sem, *, page):
    b = pl.program_id(0)
    hf = pl.program_id(1)
    nhf = pl.num_programs(1)
    step = b * nhf + hf
    nstep = pl.num_programs(0) * nhf
    xs = (x0, x1, x2, x3)

    def copies(s, slot):
        out = []
        for i in range(PAGES_PER_STEP):
            pg = pt_ref[s * PAGES_PER_STEP + i]
            for c in range(2):
                out.append(pltpu.make_async_copy(cache_ref.at[0, pg, c], buf.at[slot, c, i], sem.at[slot]))
        return out

    @pl.when(step == 0)
    def _():
        for cp in copies(0, 0):
            cp.start()

    slot = step % 2

    @pl.when(step + 1 < nstep)
    def _():
        for cp in copies(step + 1, 1 - slot):
            cp.start()

    for cp in copies(step, slot):
        cp.wait()

    def to_rows(i, carry):
        r0 = pl.multiple_of(i * page, page)
        for c in range(2):
            tile = buf[slot, c, i].reshape(NSA_KV_HEADS * HEAD_DIM, page)
            for p in range(2):
                xs[c * 2 + p][pl.ds(r0, page), :] = tile[p * PAIR_W:(p + 1) * PAIR_W].T
        return carry

    lax.fori_loop(0, PAGES_PER_STEP, to_rows, 0)

    nch = PAGES_PER_STEP * page // CMP_STRIDE

    def load_rows(l, c, p):
        return xs[c * 2 + p][pl.ds(l, nch, stride=CMP_STRIDE), :]

    row0 = pl.multiple_of(hf * nch, nch)
    for c in range(2):
        for p in range(2):
            parts_scr[c * 2 + p, pl.ds(row0, nch), :] = _cmp_parts(load_rows, wbd_ref, c, p)

    @pl.when(hf == nhf - 1)
    def _():
        for c in range(2):
            posb = _cmp_pos_bias(pos_ref, wbd_ref, c)
            for p in range(2):
                kvc_ref[c, p] = _cmp_finish(parts_scr[c * 2 + p], posb, w2_ref, c).astype(BF16)


def _page_compress_t(page_table, cache_t, wbd, w2bd, pos):
    sb, n_pages = page_table.shape
    page = cache_t.shape[-1]
    ncp = n_pages * page // CMP_STRIDE
    rows = PAGES_PER_STEP * page
    kern = functools.partial(_page_compress_t_kernel, page=page)
    grid_spec = pltpu.PrefetchScalarGridSpec(
        num_scalar_prefetch=1,
        grid=(sb, n_pages // PAGES_PER_STEP),
        in_specs=[pl.BlockSpec(memory_space=pl.ANY),
                  pl.BlockSpec(wbd.shape, lambda b, h, pt: (0,) * wbd.ndim),
                  pl.BlockSpec(w2bd.shape, lambda b, h, pt: (0,) * w2bd.ndim),
                  pl.BlockSpec(pos.shape, lambda b, h, pt: (0,) * pos.ndim)],
        out_specs=pl.BlockSpec((None, 2, 2, ncp, PAIR_W), lambda b, h, pt: (b, 0, 0, 0, 0)),
        scratch_shapes=[pltpu.VMEM((2, 2, PAGES_PER_STEP, NSA_KV_HEADS, HEAD_DIM, page), F32)]
        + [pltpu.VMEM((rows, PAIR_W), F32)] * 4
        + [pltpu.VMEM((4, ncp, 2 * PAIR_W), F32), pltpu.SemaphoreType.DMA((2,))],
    )
    return pl.pallas_call(
        kern,
        grid_spec=grid_spec,
        out_shape=jax.ShapeDtypeStruct((sb, 2, 2, ncp, PAIR_W), BF16),
        compiler_params=_cparams("arbitrary", "arbitrary"),
        name="nsa_page_compress",
    )(page_table.reshape(-1), cache_t, wbd, w2bd, pos)


L_GROUP = 4


def _page_compress_native_kernel(pt_ref, cache_ref, wq_ref, w2_ref, xpos_ref, kvc_ref,
                                 buf, parts_scr, tmp_scr, sem, *, n_pages, page):
    b = pl.program_id(0)
    hf = pl.program_id(1)
    nhf = pl.num_programs(1)
    step = b * nhf + hf
    nstep = pl.num_programs(0) * nhf
    ng = NSA_KV_HEADS
    hd = HEAD_DIM

    def copies(s, slot):
        out = []
        for i in range(PAGES_PER_STEP):
            pg = pt_ref[s * PAGES_PER_STEP + i]
            for c in range(2):
                out.append(pltpu.make_async_copy(cache_ref.at[0, pg, :, c, :, :],
                                                 buf.at[slot, c, pl.ds(i * page, page)], sem.at[slot]))
        return out

    @pl.when(step == 0)
    def _():
        for cp in copies(0, 0):
            cp.start()

    slot = step % 2

    @pl.when(step + 1 < nstep)
    def _():
        for cp in copies(step + 1, 1 - slot):
            cp.start()

    for cp in copies(step, slot):
        cp.wait()

    nch = PAGES_PER_STEP * page // CMP_STRIDE
    row0 = pl.multiple_of(hf * nch * ng, nch * ng)
    for c in range(2):
        acc = None
        for lq in range(CMP_STRIDE // L_GROUP):
            xs = [buf[slot, c, pl.ds(lq * L_GROUP + i, nch, stride=CMP_STRIDE), :, :].reshape(nch * ng, hd)
                  for i in range(L_GROUP)]
            part = _dot(jnp.concatenate(xs, axis=1).astype(BF16), wq_ref[c, lq])
            acc = part if acc is None else acc + part
        parts_scr[c, pl.ds(row0, nch * ng), :] = acc

    @pl.when(hf == nhf - 1)
    def _():
        n = parts_scr.shape[1]
        lane = lax.broadcasted_iota(jnp.int32, (1, PAIR_W), 1)
        for c in range(2):
            pacc = None
            for lq in range(CMP_STRIDE // L_GROUP):
                part = _dot(xpos_ref[c, lq].astype(BF16), wq_ref[c, lq])
                pacc = part if pacc is None else pacc + part
            posb = jnp.where(lane < hd, pacc[0:1, :], 0.0) + pltpu.roll(jnp.where(lane >= hd, pacc[1:2, :], 0.0), hd, 1)
            p = parts_scr[c]
            pre = p + pltpu.roll(pltpu.roll(p, n - ng, 0), hd, 1) + posb
            tmp_scr[...] = _dot(_gelu_tanh(pre).astype(BF16), w2_ref[c])
            for pr in range(2):
                even = tmp_scr[pl.ds(2 * pr, n // ng, stride=ng), :]
                odd = tmp_scr[pl.ds(2 * pr + 1, n // ng, stride=ng), :]
                kvc_ref[c, pr] = (even + pltpu.roll(odd, hd, 1)).astype(BF16)


def _page_compress_native(page_table, cache, wq, w2pad, xpos):
    sb, n_pages = page_table.shape
    page = cache.shape[2]
    ncp = n_pages * page // CMP_STRIDE
    rows = PAGES_PER_STEP * page
    kern = functools.partial(_page_compress_native_kernel, n_pages=n_pages, page=page)
    grid_spec = pltpu.PrefetchScalarGridSpec(
        num_scalar_prefetch=1,
        grid=(sb, n_pages // PAGES_PER_STEP),
        in_specs=[pl.BlockSpec(memory_space=pl.ANY),
                  pl.BlockSpec(wq.shape, lambda b, h, pt: (0,) * wq.ndim),
                  pl.BlockSpec(w2pad.shape, lambda b, h, pt: (0,) * w2pad.ndim),
                  pl.BlockSpec(xpos.shape, lambda b, h, pt: (0,) * xpos.ndim)],
        out_specs=pl.BlockSpec((None, 2, 2, ncp, PAIR_W), lambda b, h, pt: (b, 0, 0, 0, 0)),
        scratch_shapes=[pltpu.VMEM((2, 2, rows, NSA_KV_HEADS, HEAD_DIM), F32),
                        pltpu.VMEM((2, ncp * NSA_KV_HEADS, PAIR_W), F32),
                        pltpu.VMEM((ncp * NSA_KV_HEADS, PAIR_W), F32),
                        pltpu.SemaphoreType.DMA((2,))],
    )
    return pl.pallas_call(
        kern,
        grid_spec=grid_spec,
        out_shape=jax.ShapeDtypeStruct((sb, 2, 2, ncp, PAIR_W), BF16),
        compiler_params=_cparams("arbitrary", "arbitrary"),
        name="nsa_page_compress",
    )(page_table.reshape(-1), cache, wq, w2pad, xpos)


def _prep_cmp_native(w_cmp1, w_cmp2, cmp_pos):
    hd = HEAD_DIM
    nq = CMP_STRIDE // L_GROUP
    both = jnp.concatenate([w_cmp1[:, :CMP_STRIDE], w_cmp1[:, CMP_STRIDE:]], axis=-1)
    wq = both.reshape(2, nq, L_GROUP * hd, 2 * hd).astype(BF16)
    w2pad = jnp.pad(w_cmp2, ((0, 0), (0, PAIR_W - hd), (0, PAIR_W - hd))).astype(BF16)
    pos = cmp_pos.transpose(1, 0, 2)
    rows = jnp.stack([pos[:, :CMP_STRIDE].reshape(2, nq, L_GROUP * hd),
                      pos[:, CMP_STRIDE:].reshape(2, nq, L_GROUP * hd)], axis=2)
    xpos = jnp.pad(rows, ((0, 0), (0, 0), (0, SUBLANES - 2), (0, 0)))
    return wq, w2pad, xpos


def _page_compress(page_table, cache4, wbd, w2bd, pos):
    sb, n_pages = page_table.shape
    page = cache4.shape[1]
    ncp = n_pages * page // CMP_STRIDE
    rows = PAGES_PER_STEP * page
    kern = functools.partial(_page_compress_kernel, n_pages=n_pages, page=page)
    grid_spec = pltpu.PrefetchScalarGridSpec(
        num_scalar_prefetch=1,
        grid=(sb, n_pages // PAGES_PER_STEP),
        in_specs=[pl.BlockSpec(memory_space=pl.ANY),
                  pl.BlockSpec(wbd.shape, lambda b, h, pt: (0,) * wbd.ndim),
                  pl.BlockSpec(w2bd.shape, lambda b, h, pt: (0,) * w2bd.ndim),
                  pl.BlockSpec(pos.shape, lambda b, h, pt: (0,) * pos.ndim)],
        out_specs=pl.BlockSpec((None, 2, 2, ncp, PAIR_W), lambda b, h, pt: (b, 0, 0, 0, 0)),
        scratch_shapes=[pltpu.VMEM((rows, PAIR_W), F32)] * 4
        + [pltpu.VMEM((4, ncp, 2 * PAIR_W), F32), pltpu.SemaphoreType.DMA(())],
    )
    return pl.pallas_call(
        kern,
        grid_spec=grid_spec,
        out_shape=jax.ShapeDtypeStruct((sb, 2, 2, ncp, PAIR_W), BF16),
        compiler_params=_cparams("arbitrary", "arbitrary"),
        name="nsa_page_compress",
    )(page_table.reshape(-1), cache4, wbd, w2bd, pos)


def _row_slopes(slope_ref, g, rows):
    hh = lax.broadcasted_iota(jnp.int32, (rows, 1), 0) % NSA_HPG
    col = jnp.zeros((rows, 1), F32)
    for h in range(NSA_HPG):
        col = jnp.where(hh == h, slope_ref[g * NSA_HPG + h], col)
    return col


def _sample_cmp_kernel(slope_ref, q_ref, kvc_ref, ocmp_ref, imp_ref, *, past, nsp):
    rows = q_ref.shape[1]
    ncp = kvc_ref.shape[2]
    t_col = lax.broadcasted_iota(jnp.int32, (rows, 1), 0) // NSA_HPG
    jrow = lax.broadcasted_iota(jnp.int32, (1, ncp), 1)
    dist = (past + t_col) - (CMP_STRIDE * jrow + (CMP_LEN - 1))
    mask = (dist >= 0) & (jrow < ncp - 1)
    distf = dist.astype(F32)
    ri = lax.broadcasted_iota(jnp.int32, (rows, rows), 0) // NSA_HPG
    ci = lax.broadcasted_iota(jnp.int32, (rows, rows), 1) // NSA_HPG
    same_t = (ri == ci).astype(BF16)
    jj = lax.broadcasted_iota(jnp.int32, (ncp, nsp), 0) * CMP_STRIDE
    nn = lax.broadcasted_iota(jnp.int32, (ncp, nsp), 1) * SEL_BLK
    ov = ((jj < nn + SEL_BLK) & (jj + CMP_LEN > nn)).astype(BF16)
    for g in range(NSA_KV_HEADS):
        pr, half = divmod(g, 2)
        lanes = slice(half * HEAD_DIM, (half + 1) * HEAD_DIM)
        s_c = _dot_nt(q_ref[g], kvc_ref[0, pr, :, lanes])
        s = jnp.where(mask, s_c - _row_slopes(slope_ref, g, rows) * distf, MASK_NEG)
        e = jnp.where(mask, jnp.exp(s - jnp.max(s, axis=-1, keepdims=True)), 0.0)
        pc = e / jnp.maximum(jnp.sum(e, axis=-1, keepdims=True), 1e-30)
        ocmp_ref[g] = _dot(pc.astype(BF16), kvc_ref[1, pr])[:, lanes]
        hi = pc.astype(BF16)
        lo = (pc - hi.astype(F32)).astype(BF16)
        pcs = _dot(same_t, hi) + _dot(same_t, lo)
        hi = pcs.astype(BF16)
        lo = (pcs - hi.astype(F32)).astype(BF16)
        imp_ref[g] = _dot(hi, ov) + _dot(lo, ov)


def _sample_cmp(slopes, q_rows, kvc, *, past, nsp):
    sb, ng, rows, hd = q_rows.shape
    ncp = kvc.shape[3]
    kern = functools.partial(_sample_cmp_kernel, past=past, nsp=nsp)
    return pl.pallas_call(
        kern,
        grid=(sb,),
        in_specs=[pl.BlockSpec(memory_space=pltpu.SMEM),
                  pl.BlockSpec((None, ng, rows, hd), lambda b: (b, 0, 0, 0)),
                  pl.BlockSpec((None, 2, 2, ncp, PAIR_W), lambda b: (b, 0, 0, 0, 0))],
        out_specs=[pl.BlockSpec((None, ng, rows, hd), lambda b: (b, 0, 0, 0)),
                   pl.BlockSpec((None, ng, rows, nsp), lambda b: (b, 0, 0, 0))],
        out_shape=[jax.ShapeDtypeStruct((sb, ng, rows, hd), F32),
                   jax.ShapeDtypeStruct((sb, ng, rows, nsp), F32)],
        compiler_params=_cparams("parallel"),
        name="nsa_sample_cmp",
    )(slopes, q_rows, kvc)


def _sample_topk_kernel(imp_ref, idx_ref, score_scr, *, past, n_sel, st):
    nsr, ncol = imp_ref.shape
    bid = lax.broadcasted_iota(jnp.int32, (nsr, ncol), 0)
    tq = past + lax.broadcasted_iota(jnp.int32, (nsr, ncol), 1) % st
    cur = tq // SEL_BLK
    valid = (bid * SEL_BLK <= tq) & (bid < n_sel)
    forced = (bid == 0) | (bid == cur) | (bid == cur - 1)
    score = jnp.where(valid, imp_ref[...] + jnp.where(forced, FORCE_BONUS, 0.0), MASK_NEG)
    score_scr[...] = score

    def body(m, cnt):
        sm = score_scr[pl.ds(m, 1), :]
        ahead = (sm > score) | ((sm == score) & (m < bid))
        return cnt + ahead.astype(jnp.int32)

    cnt = lax.fori_loop(0, n_sel, body, jnp.zeros((nsr, ncol), jnp.int32))
    sel = (cnt < SEL_TOPK) & (bid < n_sel)
    tril = (lax.broadcasted_iota(jnp.int32, (nsr, nsr), 1) <= lax.broadcasted_iota(jnp.int32, (nsr, nsr), 0))
    prefix = _dot(tril.astype(BF16), sel.astype(BF16))
    for k in range(SEL_TOPK):
        hit = sel & (prefix == float(k + 1))
        idx_ref[k:k + 1, :] = jnp.sum(jnp.where(hit, bid, 0), axis=0, keepdims=True)


def _sample_topk(imp_t, *, past, n_sel, st):
    nsr, ncol = imp_t.shape
    kern = functools.partial(_sample_topk_kernel, past=past, n_sel=n_sel, st=st)
    return pl.pallas_call(
        kern,
        grid=(1,),
        in_specs=[_full(imp_t.shape)],
        out_specs=_full((SEL_TOPK, ncol)),
        out_shape=jax.ShapeDtypeStruct((SEL_TOPK, ncol), jnp.int32),
        scratch_shapes=[pltpu.VMEM((nsr, ncol), F32)],
        compiler_params=_cparams("arbitrary"),
        name="nsa_sample_topk",
    )(imp_t)


def _joint_softmax_pv(s_a, v_a_fn, s_b, v_b):
    m = jnp.maximum(jnp.max(s_a, axis=-1, keepdims=True), jnp.max(s_b, axis=-1, keepdims=True))
    p_a = jnp.where(s_a > 0.5 * MASK_NEG, jnp.exp(s_a - m), 0.0)
    p_b = jnp.where(s_b > 0.5 * MASK_NEG, jnp.exp(s_b - m), 0.0)
    l = jnp.sum(p_a, axis=-1, keepdims=True) + jnp.sum(p_b, axis=-1, keepdims=True)
    acc = v_a_fn(p_a.astype(BF16)) + _dot(p_b.astype(BF16), v_b)
    return acc / jnp.maximum(l, 1e-30)


def _sample_attn_kernel(idx_ref, pt_ref, slope_ref, cache_ref, q_ref, idxv_ref, e16_ref, gate_ref, ocmp_ref,
                        ksn_ref, vsn_ref, kwn_ref, vwn_ref, kwin_ref, vwin_ref, o_ref, kbuf, vbuf, sem,
                        *, past, n_pages, st):
    b = pl.program_id(0)
    pr = pl.program_id(1)
    rows = q_ref.shape[1]
    page = kbuf.shape[-1] // SEL_TOPK
    nkey = SEL_TOPK * page
    last_blk = past // SEL_BLK - 1
    npr = pl.num_programs(1)
    step = b * npr + pr
    nstep = pl.num_programs(0) * npr

    def copies(s, slot):
        sb_, spr = s // npr, s % npr
        out = []
        for half in range(2):
            g = spr * 2 + half
            for t in range(st):
                for k in range(SEL_TOPK):
                    blk = jnp.minimum(idx_ref[((sb_ * NSA_KV_HEADS + g) * st + t) * SEL_TOPK + k], last_blk)
                    pg = pt_ref[sb_ * n_pages + lax.shift_right_logical(blk, 1)]
                    dst = pl.ds(k * page, page)
                    for c, buf in ((2, kbuf), (3, vbuf)):
                        out.append(pltpu.make_async_copy(cache_ref.at[0, pg, c, g],
                                                         buf.at[slot, half, t, :, dst], sem.at[slot]))
        return out

    @pl.when(step == 0)
    def _():
        for cp in copies(0, 0):
            cp.start()

    slot = step % 2

    @pl.when(step + 1 < nstep)
    def _():
        for cp in copies(step + 1, 1 - slot):
            cp.start()

    for cp in copies(step, slot):
        cp.wait()

    t_col = lax.broadcasted_iota(jnp.int32, (rows, 1), 0) // NSA_HPG
    tqf = (past + t_col).astype(F32)
    col = lax.broadcasted_iota(jnp.int32, (1, nkey), 1)
    off = (col % SEL_BLK).astype(F32)
    col_half = ((col % page) // SEL_BLK).astype(F32)
    tp = lax.broadcasted_iota(jnp.int32, (1, ksn_ref.shape[0]), 1)
    dn = t_col - tp
    valid_n = (dn >= 0) & (tp < st)
    dnf = dn.astype(F32)
    wb = kwin_ref.shape[0]
    dist_w = (wb + t_col) - lax.broadcasted_iota(jnp.int32, (1, wb), 1)
    valid_w = (dist_w >= 0) & (dist_w < WINDOW)

    for half in range(2):
        g = pr * 2 + half
        lanes = slice(half * HEAD_DIM, (half + 1) * HEAD_DIM)
        q = q_ref[half]
        slope = _row_slopes(slope_ref, g, rows)

        blk = _dot(idxv_ref[half], e16_ref[...])
        is_new = blk > (last_blk + 0.5)
        dist = tqf - (blk * SEL_BLK + off)
        in_half = (blk - 2.0 * jnp.floor(0.5 * blk)) == col_half
        valid = jnp.logical_not(is_new) & in_half & (dist >= 0)
        s_sel = jnp.full((rows, nkey), MASK_NEG, F32)
        for t in range(st):
            s_t = _dot(q, kbuf[slot, half, t].astype(BF16))
            s_sel = jnp.where(t_col == t, s_t, s_sel)
        s_sel = jnp.where(valid, s_sel - slope * dist, MASK_NEG)
        has_new = jnp.max(is_new.astype(F32), axis=-1, keepdims=True) > 0.5
        s_new = jnp.where(valid_n & has_new, _dot_nt(q, ksn_ref[:, lanes]) - slope * dnf, MASK_NEG)

        def pv_sel(p):
            acc = jnp.zeros((rows, HEAD_DIM), F32)
            for t in range(st):
                acc = jnp.where(t_col == t, _dot_nt(p, vbuf[slot, half, t].astype(BF16)), acc)
            return acc

        o_sel = _joint_softmax_pv(s_sel, pv_sel, s_new, vsn_ref[:, lanes])

        s_w = _dot_nt(q, kwin_ref[:, lanes].astype(BF16))
        s_w = jnp.where(valid_w, s_w - slope * dist_w.astype(F32), MASK_NEG)
        s_wn = jnp.where(valid_n, _dot_nt(q, kwn_ref[:, lanes]) - slope * dnf, MASK_NEG)
        vw = vwin_ref[...].astype(BF16)
        o_w = _joint_softmax_pv(s_w, lambda p: _dot(p, vw), s_wn, vwn_ref[...])[:, lanes]

        sg = jax.nn.sigmoid(gate_ref[half])
        o_ref[half] = sg[:, 0:1] * ocmp_ref[half] + sg[:, 1:2] * o_sel + sg[:, 2:3] * o_w


def _sample_attention(idx_flat, pt_flat, slopes, cache4, q_rows, idx_rows, e16, gate_rows, ocmp, new16, win_buf,
                      *, past, n_pages, st):
    sb, ng, rows, hd = q_rows.shape
    wb = win_buf.shape[1]
    nkey = e16.shape[1]
    pairg = lambda w: pl.BlockSpec((None, 2, rows, w), lambda b, p, *_: (b, p, 0, 0))
    newc = lambda c: pl.BlockSpec((None, new16.shape[1], PAIR_W), lambda b, p, *_: (b, 0, 2 * c + p))
    kern = functools.partial(_sample_attn_kernel, past=past, n_pages=n_pages, st=st)
    grid_spec = pltpu.PrefetchScalarGridSpec(
        num_scalar_prefetch=2,
        grid=(sb, 2),
        in_specs=[pl.BlockSpec(memory_space=pltpu.SMEM), pl.BlockSpec(memory_space=pl.ANY),
                  pairg(hd), pairg(LANES), pl.BlockSpec(e16.shape, lambda b, p, *_: (0, 0)), pairg(LANES), pairg(hd),
                  newc(2), newc(3), newc(4), newc(5),
                  pl.BlockSpec((None, wb, PAIR_W), lambda b, p, *_: (b, 0, p)),
                  pl.BlockSpec((None, wb, PAIR_W), lambda b, p, *_: (b, 0, 2 + p))],
        out_specs=pairg(hd),
        scratch_shapes=[pltpu.VMEM((2, 2, st, HEAD_DIM, nkey), F32), pltpu.VMEM((2, 2, st, HEAD_DIM, nkey), F32),
                        pltpu.SemaphoreType.DMA((2,))],
    )
    return pl.pallas_call(
        kern,
        grid_spec=grid_spec,
        out_shape=jax.ShapeDtypeStruct((sb, ng, rows, hd), F32),
        compiler_params=_cparams("arbitrary", "arbitrary"),
        name="nsa_sample_attn",
    )(idx_flat, pt_flat, slopes, cache4, q_rows, idx_rows, e16, gate_rows, ocmp, new16, new16, new16, new16,
      win_buf, win_buf)


def _pad_cols(w, mult=LANES):
    n = w.shape[1]
    return jnp.pad(w, ((0, 0), (0, -n % mult)))


def _prep_nsa_in(w_in):
    body = w_in[:, :NSA_Q_W + 6 * NSA_KV_W]
    gates = w_in[:, NSA_Q_W + 6 * NSA_KV_W:]
    per_pair = 2 * NSA_HPG * 3
    blocks = [_pad_cols(gates[:, p * per_pair:(p + 1) * per_pair]) for p in range(2)]
    return jnp.concatenate([body] + blocks, axis=1).astype(BF16)


def _prep_cmp(w_cmp1, w_cmp2, cmp_pos):
    hd = HEAD_DIM
    z = jnp.zeros((2, CMP_STRIDE, hd, hd), F32)
    wa, wb = w_cmp1[:, :CMP_STRIDE], w_cmp1[:, CMP_STRIDE:]
    top = jnp.concatenate([wa, z, wb, z], axis=-1)
    bot = jnp.concatenate([z, wa, z, wb], axis=-1)
    wbd = jnp.concatenate([top, bot], axis=-2).astype(BF16)
    z2 = jnp.zeros((2, hd, hd), F32)
    w2bd = jnp.concatenate([jnp.concatenate([w_cmp2, z2], -1), jnp.concatenate([z2, w_cmp2], -1)], -2).astype(BF16)
    pos = cmp_pos.transpose(1, 0, 2)
    pos = jnp.concatenate([pos, pos], axis=-1)
    rows = jnp.stack([pos[:, :CMP_STRIDE], pos[:, CMP_STRIDE:]], axis=2)
    pos_tiles = jnp.pad(rows, ((0, 0), (0, 0), (0, SUBLANES - 2), (0, 0)))
    return wbd, w2bd, pos_tiles


def _alibi_slopes():
    h = jnp.arange(1, NSA_HEADS + 1, dtype=F32)
    return jnp.exp2(-8.0 * h / NSA_HEADS)


def _sel_expand(t):
    key_blk = jnp.arange(t, dtype=jnp.int32).reshape(t // SEL_CHUNK, 1, SEL_CHUNK) // SEL_BLK
    blk = jnp.arange(t // SEL_BLK, dtype=jnp.int32).reshape(1, t // SEL_BLK, 1)
    return (key_blk == blk).astype(BF16)


def _prompt_rows_tile(m):
    for tm in (512, 256, 128, 64, 32, 16, 8):
        if m % tm == 0:
            return tm
    raise ValueError(m)


def kernel(x_prompt, x_sample, cache_nsa_kv, cache_nsa_win, state_mlstm_C, state_mlstm_n, state_mlstm_m,
           state_ffn_conv, page_table, norm_g, w_nsa_in, w_nsa_out, w_cmp1, w_cmp2, cmp_pos, w_ml_in,
           b_ml_gate, ml_head_norm, w_ml_out, w_ffn_up, ffn_conv_w, ffn_conv_b, w_ffn_down):
    nb, t, d = x_prompt.shape
    sb, st, _ = x_sample.shape
    mp = nb * t
    tm = _prompt_rows_tile(mp)
    slopes = _alibi_slopes()

    w_in0 = _prep_nsa_in(w_nsa_in[0])
    w_out0 = w_nsa_out[0].astype(BF16)
    wbd, w2bd, pos_tiles = _prep_cmp(w_cmp1[0], w_cmp2[0], cmp_pos[0])
    w_ml = _pad_cols(w_ml_in[0]).astype(BF16)
    w_mlo = w_ml_out[0].astype(BF16)
    w_up = w_ffn_up.astype(BF16)
    w_dn = w_ffn_down.astype(BF16)
    bg_row = _pad_cols(b_ml_gate[0].reshape(1, 2 * ML_HEADS))
    g = norm_g[:, :, None, :]

    xp = x_prompt.reshape(mp, d)
    _, q, kva, kvb, win, gates = _nsa_project(xp, g[0, 0], w_in0, tm)
    kc, vc = _nsa_compress(kva, wbd, w2bd, pos_tiles, nb=nb, t=t)
    o = _nsa_attention5(_slope_feats(), q, gates, kc, vc, kvb, nb=nb, t=t)
    xp = _out_project(o, xp, w_out0, g[0, 1], tm)
    kv_p = kva.reshape(1, nb, t, 4, NSA_KV_HEADS, HEAD_DIM)
    w_keep = min(WINDOW, t)
    win_p = win.reshape(nb, t, 2, NSA_KV_HEADS, HEAD_DIM)[None, :, t - w_keep:]

    halo = SUBLANES
    zstate = jnp.zeros((nb, halo, 2 * D_FF), F32)
    tmf = min(t, 1024)
    conv_p = []

    def ffn_prompt(xp, i):
        xo, ta, tb = _conv_ffn(xp, zstate, g[i, 2], g[i, 3], w_up[i], ffn_conv_w[i], ffn_conv_b[i][None],
                               w_dn[i], tm=tmf, fc=512, shift=1, tiles_per_seq=t // tmf)
        tail = jnp.concatenate([ta, tb], axis=-1).reshape(nb, t // tmf, halo, 2 * D_FF)
        conv_p.append(tail[:, -1, halo - (CONV_W - 1):])
        return xo

    xp = ffn_prompt(xp, 0)

    lp = math.gcd(t, ML_CHUNK)
    qkv, om, gm = _ml_project(xp, g[1, 0], w_ml, tm)
    zc = jnp.zeros((nb, ML_HEADS, ML_DV, ML_DQK), F32)
    zn = jnp.zeros((nb, ML_HEADS, 1, ML_DQK), F32)
    zm = jnp.zeros((nb, ML_HEADS, 1, 1), F32)
    if lp == ML_DQK:
        ym, c_p, n_p, m_p = _mlstm_stacked(qkv, om, gm, bg_row, ml_head_norm[0][None], zc, zn[:, :, 0], zm[:, :, 0],
                                           nb=nb, l=lp)
        n_p, m_p = n_p[:, :, None, :], m_p[:, :, None, :]
    else:
        ym, c_p, n_p, m_p = _mlstm(qkv, om, gm, bg_row, ml_head_norm[0][None], zc, zn, zm, nb=nb, lp=lp, lv=lp)
    xp = _out_project(ym, xp, w_mlo, g[1, 1], tm)
    xp = ffn_prompt(xp, 1)

    ms = sb * st
    ng, hpg, hd = NSA_KV_HEADS, NSA_HPG, HEAD_DIM
    xs = x_sample.transpose(1, 0, 2).reshape(ms, d)
    qs, _, kva_s, kvb_s, win_s, gates_s = _nsa_project(xs, g[0, 0], w_in0, ms)
    n_phys, page = cache_nsa_kv.shape[1:3]
    n_pages = page_table.shape[1]
    past = n_pages * page
    cache4 = cache_nsa_kv.transpose(0, 1, 3, 4, 5, 2)
    kvc = _page_compress_t(page_table, cache4, wbd, w2bd, pos_tiles)

    rows = st * hpg
    q_rows = qs.reshape(st, sb, ng, hpg, hd).transpose(1, 2, 0, 3, 4).reshape(sb, ng, rows, hd)
    n_sel = -(-(past + st) // SEL_BLK)
    nsp = -(-n_sel // LANES) * LANES
    nsr = -(-n_sel // SUBLANES) * SUBLANES
    ocmp, imp = _sample_cmp(slopes, q_rows, kvc, past=past, nsp=nsp)
    imp_t = imp[:, :, ::hpg, :nsr].reshape(sb * ng * st, nsr).T
    idx = _sample_topk(imp_t, past=past, n_sel=n_sel, st=st)
    idx_bgtk = idx.T.reshape(sb, ng, st, SEL_TOPK)
    idx_rows = _pad_cols(jnp.repeat(idx_bgtk, hpg, axis=2).reshape(sb * ng * rows, SEL_TOPK))
    idx_rows = idx_rows.reshape(sb, ng, rows, LANES).astype(BF16)
    e16 = (jnp.arange(LANES, dtype=jnp.int32)[:, None]
           == jnp.arange(SEL_TOPK * page, dtype=jnp.int32)[None, :] // page).astype(BF16)
    gate_rows = gates_s.reshape(st, sb, 2, LANES)[..., :2 * hpg * 3].reshape(st, sb, ng, hpg, 3)
    gate_rows = _pad_cols(gate_rows.transpose(1, 2, 0, 3, 4).reshape(sb * ng * rows, 3)).reshape(sb, ng, rows, LANES)
    new16 = jnp.pad(kvb_s.reshape(st, sb, -1).transpose(1, 0, 2), ((0, 0), (0, 16 - st), (0, 0)))
    wb = cache_nsa_win.shape[2]
    win_buf = cache_nsa_win[0].reshape(sb, wb, 2 * NSA_KV_W)
    o_s = _sample_attention(idx_bgtk.reshape(-1), page_table.reshape(-1), slopes, cache4, q_rows, idx_rows, e16,
                            gate_rows, ocmp, new16, win_buf, past=past, n_pages=n_pages, st=st)
    o_s = o_s.reshape(sb, ng, st, hpg, hd).transpose(2, 0, 1, 3, 4).reshape(ms, NSA_Q_W).astype(BF16)
    xs = _out_project(o_s, xs, w_out0, g[0, 1], ms)
    kv_s = kva_s.reshape(st, sb, 4, ng, hd).transpose(1, 0, 2, 3, 4)[None]
    win_new = win_s.reshape(st, sb, 2, ng, hd).transpose(1, 0, 2, 3, 4)
    win_s_out = jnp.concatenate([cache_nsa_win[0], win_new], axis=1)[None, :, st:]

    conv_s = []

    def ffn_sample(xs, i):
        state = state_ffn_conv[i].transpose(1, 0, 2).reshape(1, (CONV_W - 1) * sb, 2 * D_FF)
        xo, ta, tb = _conv_ffn(xs, state, g[i, 2], g[i, 3], w_up[i], ffn_conv_w[i], ffn_conv_b[i][None],
                               w_dn[i], tm=ms, fc=512, shift=sb, tiles_per_seq=1)
        tail = jnp.concatenate([ta, tb], axis=-1).reshape(CONV_W - 1, sb, 2 * D_FF)
        conv_s.append(tail.transpose(1, 0, 2))
        return xo

    xs = ffn_sample(xs, 0)

    lps = SUBLANES
    to_seq = lambda a: jnp.pad(a.reshape(st, sb, -1).transpose(1, 0, 2),
                               ((0, 0), (0, lps - st), (0, 0))).reshape(sb * lps, -1)
    qkv_s, om_s, gm_s = _ml_project(xs, g[1, 0], w_ml, ms)
    ym_s, c_s, n_s, m_s = _mlstm(to_seq(qkv_s), to_seq(om_s), to_seq(gm_s), bg_row, ml_head_norm[0][None],
                                 state_mlstm_C[0], state_mlstm_n[0][:, :, None, :],
                                 state_mlstm_m[0][:, :, None, None], nb=sb, lp=lps, lv=math.gcd(st, ML_CHUNK))
    ym_s = ym_s.reshape(sb, lps, ML_V_W)[:, :st].transpose(1, 0, 2).reshape(ms, ML_V_W)
    xs = _out_project(ym_s, xs, w_mlo, g[1, 1], ms)
    xs = ffn_sample(xs, 1)

    return (xp.reshape(nb, t, d), xs.reshape(st, sb, d).transpose(1, 0, 2), kv_p, kv_s, win_p, win_s_out,
            c_p[None], c_s[None], n_p[None, :, :, 0], n_s[None, :, :, 0],
            m_p[None, :, :, 0, 0], m_s[None, :, :, 0, 0], jnp.stack(conv_p), jnp.stack(conv_s))
```

```python
import functools
import math

import jax
import jax.numpy as jnp
from jax import lax
from jax.experimental import pallas as pl
from jax.experimental.pallas import tpu as pltpu

F32 = jnp.float32
BF16 = jnp.bfloat16

LANES = 128
SUBLANES = 8
VMEM_LIMIT_BYTES = 56 * 1024 * 1024

D_MODEL = 1024
NSA_HEADS = 16
NSA_KV_HEADS = 4
NSA_HPG = NSA_HEADS // NSA_KV_HEADS
HEAD_DIM = D_MODEL // NSA_HEADS
CMP_LEN = 32
CMP_STRIDE = 16
SEL_BLK = 64
SEL_TOPK = 16
WINDOW = 512
Q_BLK = 128
FORCE_BONUS = 1e4
NSA_Q_W = NSA_HEADS * HEAD_DIM
NSA_KV_W = NSA_KV_HEADS * HEAD_DIM
ML_HEADS = 8
ML_DQK = D_MODEL // (2 * ML_HEADS)
ML_DV = D_MODEL // ML_HEADS
ML_CHUNK = 64
ML_QK_W = ML_HEADS * ML_DQK
ML_V_W = ML_HEADS * ML_DV
D_FF = 4 * D_MODEL
CONV_W = 3
RMS_EPS = 1e-6
MASK_NEG = -1e30
LOG2E = 1.4426950408889634
ML_SEQ_PER_STEP = 4
FFN_SUB_ROWS = 1024
SEL_CHUNK = 512
PAIR_W = 2 * HEAD_DIM


def _cparams(*sem):
    return pltpu.CompilerParams(dimension_semantics=sem, vmem_limit_bytes=VMEM_LIMIT_BYTES)


def _dot(a, b):
    return jnp.dot(a, b, preferred_element_type=F32)


def _dot_nt(a, b):
    return lax.dot_general(a, b, (((1,), (1,)), ((), ())), preferred_element_type=F32)


def _dot_tn(a, b):
    return lax.dot_general(a, b, (((0,), (0,)), ((), ())), preferred_element_type=F32)


def _split3(x):
    hi = x.astype(BF16)
    r1 = x - hi.astype(F32)
    mid = r1.astype(BF16)
    lo = (r1 - mid.astype(F32)).astype(BF16)
    return hi, mid, lo


def _rms(x, g):
    return x * lax.rsqrt(jnp.mean(x * x, axis=-1, keepdims=True) + RMS_EPS) * g


def _gelu_tanh(x):
    return 0.5 * x * (1.0 + jnp.tanh(0.7978845608028654 * (x + 0.044715 * (x * x * x))))


def _full(shape):
    return pl.BlockSpec(shape, lambda *_: (0,) * len(shape))


def _nsa_proj_kernel(x_ref, g_ref, w_ref, q_ref, q2_ref, kva_ref, kvb_ref, win_ref, gate_ref):
    h = _rms(x_ref[...], g_ref[...]).astype(BF16)
    p = _dot(h, w_ref[...])
    kvw = 4 * NSA_KV_W
    q_ref[...] = (p[:, :NSA_Q_W] * HEAD_DIM ** -0.5).astype(BF16)
    q2_ref[...] = (p[:, :NSA_Q_W] * (HEAD_DIM ** -0.5 * LOG2E)).astype(BF16)
    kva_ref[...] = p[:, NSA_Q_W:NSA_Q_W + kvw]
    kvb_ref[...] = p[:, NSA_Q_W:NSA_Q_W + 6 * NSA_KV_W].astype(BF16)
    win_ref[...] = p[:, NSA_Q_W + kvw:NSA_Q_W + 6 * NSA_KV_W]
    gate_ref[...] = p[:, NSA_Q_W + 6 * NSA_KV_W:]


def _nsa_project(x, g, w, tm):
    m = x.shape[0]
    n = w.shape[1]
    row = lambda i: (i, 0)
    return pl.pallas_call(
        _nsa_proj_kernel,
        grid=(m // tm,),
        in_specs=[pl.BlockSpec((tm, D_MODEL), row), _full((1, D_MODEL)), _full((D_MODEL, n))],
        out_specs=[pl.BlockSpec((tm, NSA_Q_W), row), pl.BlockSpec((tm, NSA_Q_W), row),
                   pl.BlockSpec((tm, 4 * NSA_KV_W), row),
                   pl.BlockSpec((tm, 6 * NSA_KV_W), row), pl.BlockSpec((tm, 2 * NSA_KV_W), row),
                   pl.BlockSpec((tm, 2 * LANES), row)],
        out_shape=[jax.ShapeDtypeStruct((m, NSA_Q_W), BF16), jax.ShapeDtypeStruct((m, NSA_Q_W), BF16),
                   jax.ShapeDtypeStruct((m, 4 * NSA_KV_W), F32),
                   jax.ShapeDtypeStruct((m, 6 * NSA_KV_W), BF16), jax.ShapeDtypeStruct((m, 2 * NSA_KV_W), F32),
                   jax.ShapeDtypeStruct((m, 2 * LANES), F32)],
        compiler_params=_cparams("parallel"),
        name="nsa_proj",
    )(x, g, w)


def _ml_proj_kernel(x_ref, g_ref, w_ref, qkv_ref, o_ref, gate_ref):
    h = _rms(x_ref[...], g_ref[...]).astype(BF16)
    p = _dot(h, w_ref[...])
    a = 2 * ML_QK_W + ML_V_W
    qkv_ref[:, :ML_QK_W] = p[:, :ML_QK_W].astype(BF16)
    qkv_ref[:, ML_QK_W:2 * ML_QK_W] = (p[:, ML_QK_W:2 * ML_QK_W] * ML_DQK ** -0.5).astype(BF16)
    qkv_ref[:, 2 * ML_QK_W:] = p[:, 2 * ML_QK_W:a].astype(BF16)
    o_ref[...] = p[:, a:a + ML_V_W]
    gate_ref[...] = p[:, a + ML_V_W:]


def _ml_project(x, g, w, tm):
    m = x.shape[0]
    n = w.shape[1]
    a = 2 * ML_QK_W + ML_V_W
    row = lambda i: (i, 0)
    return pl.pallas_call(
        _ml_proj_kernel,
        grid=(m // tm,),
        in_specs=[pl.BlockSpec((tm, D_MODEL), row), _full((1, D_MODEL)), _full((D_MODEL, n))],
        out_specs=[pl.BlockSpec((tm, a), row), pl.BlockSpec((tm, ML_V_W), row), pl.BlockSpec((tm, LANES), row)],
        out_shape=[jax.ShapeDtypeStruct((m, a), BF16), jax.ShapeDtypeStruct((m, ML_V_W), F32),
                   jax.ShapeDtypeStruct((m, LANES), F32)],
        compiler_params=_cparams("parallel"),
        name="ml_proj",
    )(x, g, w)


def _out_proj_kernel(o_ref, x_ref, w_ref, g_ref, xo_ref):
    y = _dot(o_ref[...], w_ref[...])
    xo_ref[...] = x_ref[...] + _rms(y, g_ref[...])


def _out_project(o, x, w, g, tm):
    m = x.shape[0]
    row = lambda i: (i, 0)
    return pl.pallas_call(
        _out_proj_kernel,
        grid=(m // tm,),
        in_specs=[pl.BlockSpec((tm, o.shape[1]), row), pl.BlockSpec((tm, D_MODEL), row),
                  _full(w.shape), _full((1, D_MODEL))],
        out_specs=pl.BlockSpec((tm, D_MODEL), row),
        out_shape=jax.ShapeDtypeStruct((m, D_MODEL), F32),
        compiler_params=_cparams("parallel"),
        name="out_proj",
    )(o, x, w, g)


def _ffn_kernel(x_ref, sta_ref, stb_ref, g2_ref, g3_ref, wa_ref, wb_ref, cwa_ref, cwb_ref, cba_ref, cbb_ref,
                wd_ref, xo_ref, taila_ref, tailb_ref, h_scr, ua_scr, ub_scr, ca_scr, cb_scr, acc_scr,
                *, tm, halo, shift, tiles_per_seq):
    i = pl.program_id(0)
    c = pl.program_id(1)

    @pl.when(c == 0)
    def _():
        h_scr[...] = _rms(x_ref[...], g2_ref[...]).astype(BF16)
        acc_scr[...] = jnp.zeros_like(acc_scr)

    first = (i % tiles_per_seq) == 0

    @pl.when(first)
    def _():
        ua_scr[0:halo, :] = sta_ref[...]
        ub_scr[0:halo, :] = stb_ref[...]

    @pl.when(jnp.logical_not(first))
    def _():
        ua_scr[0:halo, :] = ca_scr[c]
        ub_scr[0:halo, :] = cb_scr[c]

    nsub = max(1, tm // FFN_SUB_ROWS)
    rs = tm // nsub
    for r in range(nsub):
        h = h_scr[r * rs:(r + 1) * rs, :]
        ua_scr[halo + r * rs:halo + (r + 1) * rs, :] = _dot(h, wa_ref[...])
        ub_scr[halo + r * rs:halo + (r + 1) * rs, :] = _dot(h, wb_ref[...])
    ta = ua_scr[tm:tm + halo, :]
    tb = ub_scr[tm:tm + halo, :]
    ca_scr[c] = ta
    cb_scr[c] = tb
    taila_ref[...] = ta
    tailb_ref[...] = tb

    def conv(u_scr, cw_ref, cb_ref, lo):
        cw = cw_ref[...]
        return (cb_ref[...] + cw[2:3, :] * u_scr[lo:lo + rs, :]
                + cw[1:2, :] * u_scr[lo - shift:lo - shift + rs, :]
                + cw[0:1, :] * u_scr[lo - 2 * shift:lo - 2 * shift + rs, :])

    for r in range(nsub):
        lo = halo + r * rs
        y = _gelu_tanh(conv(ua_scr, cwa_ref, cba_ref, lo)) * conv(ub_scr, cwb_ref, cbb_ref, lo)
        acc_scr[r * rs:(r + 1) * rs, :] += _dot(y.astype(BF16), wd_ref[...])

    @pl.when(c == pl.num_programs(1) - 1)
    def _():
        xo_ref[...] = x_ref[...] + _rms(acc_scr[...], g3_ref[...])


def _conv_ffn(x, state, g2, g3, w_up, conv_w, conv_b, w_down, *, tm, fc, shift, tiles_per_seq):
    m = x.shape[0]
    halo = state.shape[1]
    nfc = D_FF // fc
    n_tiles = m // tm
    row = lambda i, c: (i, 0)
    const = lambda i, c: (0, 0)
    kern = functools.partial(_ffn_kernel, tm=tm, halo=halo, shift=shift, tiles_per_seq=tiles_per_seq)
    return pl.pallas_call(
        kern,
        grid=(n_tiles, nfc),
        in_specs=[
            pl.BlockSpec((tm, D_MODEL), row),
            pl.BlockSpec((None, halo, fc), lambda i, c: (i // tiles_per_seq, 0, c)),
            pl.BlockSpec((None, halo, fc), lambda i, c: (i // tiles_per_seq, 0, nfc + c)),
            pl.BlockSpec((1, D_MODEL), const), pl.BlockSpec((1, D_MODEL), const),
            pl.BlockSpec((D_MODEL, fc), lambda i, c: (0, c)),
            pl.BlockSpec((D_MODEL, fc), lambda i, c: (0, nfc + c)),
            pl.BlockSpec((CONV_W, fc), lambda i, c: (0, c)),
            pl.BlockSpec((CONV_W, fc), lambda i, c: (0, nfc + c)),
            pl.BlockSpec((1, fc), lambda i, c: (0, c)),
            pl.BlockSpec((1, fc), lambda i, c: (0, nfc + c)),
            pl.BlockSpec((fc, D_MODEL), lambda i, c: (c, 0)),
        ],
        out_specs=[pl.BlockSpec((tm, D_MODEL), row),
                   pl.BlockSpec((None, halo, fc), lambda i, c: (i, 0, c)),
                   pl.BlockSpec((None, halo, fc), lambda i, c: (i, 0, c))],
        out_shape=[jax.ShapeDtypeStruct((m, D_MODEL), F32),
                   jax.ShapeDtypeStruct((n_tiles, halo, D_FF), F32),
                   jax.ShapeDtypeStruct((n_tiles, halo, D_FF), F32)],
        scratch_shapes=[pltpu.VMEM((tm, D_MODEL), BF16),
                        pltpu.VMEM((halo + tm, fc), F32), pltpu.VMEM((halo + tm, fc), F32),
                        pltpu.VMEM((nfc, halo, fc), F32), pltpu.VMEM((nfc, halo, fc), F32),
                        pltpu.VMEM((tm, D_MODEL), F32)],
        compiler_params=_cparams("arbitrary", "arbitrary"),
        name="conv_ffn",
    )(x, state, state, g2, g3, w_up, w_up, conv_w, conv_w, conv_b, conv_b, w_down)


def _log_sigmoid(x):
    return jnp.minimum(x, 0.0) - jnp.log1p(jnp.exp(-jnp.abs(x)))


def _exact_nt(sel_bf16, x):
    hi, mid, lo = _split3(x)
    return _dot_nt(sel_bf16, hi) + _dot_nt(sel_bf16, mid) + _dot_nt(sel_bf16, lo)


def _eye(n, m, dtype):
    return (lax.broadcasted_iota(jnp.int32, (n, m), 0) == lax.broadcasted_iota(jnp.int32, (n, m), 1)).astype(dtype)


def _mlstm_kernel(qkv_ref, o_ref, gate_ref, bg_ref, hn_ref, c0_ref, n0_ref, m0_ref,
                  y_ref, cf_ref, nf_ref, mf_ref, ct_scr, n_scr, m_scr, *, lp, lv, bb):
    for bi in range(bb):
        _mlstm_one(qkv_ref.at[bi], o_ref.at[bi], gate_ref.at[bi], bg_ref, hn_ref, c0_ref.at[bi], n0_ref.at[bi],
                   m0_ref.at[bi], y_ref.at[bi], cf_ref.at[bi], nf_ref.at[bi], mf_ref.at[bi],
                   ct_scr.at[bi], n_scr.at[bi], m_scr.at[bi], lp=lp, lv=lv)


def _mlstm_one(qkv_ref, o_ref, gate_ref, bg_ref, hn_ref, c0_ref, n0_ref, m0_ref,
               y_ref, cf_ref, nf_ref, mf_ref, ct_scr, n_scr, m_scr, *, lp, lv):
    ci = pl.program_id(1)
    eye_qk = _eye(ML_DQK, ML_DQK, BF16)

    @pl.when(ci == 0)
    def _():
        for h in range(ML_HEADS):
            ct_scr[h] = _exact_nt(eye_qk, c0_ref[h])
        n_scr[...] = n0_ref[...]
        m_scr[...] = m0_ref[...]

    gp = gate_ref[...] + bg_ref[...]
    ls = _log_sigmoid(gp)
    r_i = lax.broadcasted_iota(jnp.int32, (lp, lp), 0)
    c_i = lax.broadcasted_iota(jnp.int32, (lp, lp), 1)
    causal = c_i <= r_i
    tril = causal.astype(BF16)
    hi, mid, lo = _split3(ls)
    b_all = _dot(tril, hi) + _dot(tril, mid) + _dot(tril, lo)
    sel16 = _eye(2 * ML_HEADS, LANES, BF16)
    gp_t = _exact_nt(sel16, gp)
    b_t = _exact_nt(sel16, b_all)
    row_valid = lax.broadcasted_iota(jnp.int32, (lp, 1), 0) < lv

    for h in range(ML_HEADS):
        q = qkv_ref[:, h * ML_DQK:(h + 1) * ML_DQK]
        k = qkv_ref[:, ML_QK_W + h * ML_DQK:ML_QK_W + (h + 1) * ML_DQK]
        v = qkv_ref[:, 2 * ML_QK_W + h * ML_DV:2 * ML_QK_W + (h + 1) * ML_DV]
        ct = ct_scr[h]
        n_row = n_scr[h]
        m = m_scr[h]
        b_col = b_all[:, ML_HEADS + h:ML_HEADS + h + 1]
        i_col = gp[:, h:h + 1]
        b_row = b_t[ML_HEADS + h:ML_HEADS + h + 1, :]
        i_row = gp_t[h:h + 1, :]
        dmat = jnp.where(causal, b_col - b_row + i_row, MASK_NEG)
        inter = b_col + m
        mt = jnp.maximum(inter, jnp.max(dmat, axis=-1, keepdims=True))
        s = _dot_nt(q, k) * jnp.exp(dmat - mt)
        wi = jnp.exp(inter - mt)
        qf = q.astype(F32)
        num = wi * _dot(q, ct.astype(BF16)) + _dot(s.astype(BF16), v)
        den = wi * jnp.sum(qf * n_row, axis=-1, keepdims=True) + jnp.sum(s, axis=-1, keepdims=True)
        hc = num / jnp.maximum(jnp.abs(den), jnp.exp(-mt))
        m_last = mt[lv - 1:lv, :]
        b_last = b_col[lv - 1:lv, :]
        decay = jnp.exp(b_last + m - m_last)
        ws = jnp.where(row_valid, jnp.exp(b_last - b_col + i_col - m_last), 0.0)
        ct_scr[h] = decay * ct + _dot_tn(k, (ws * v.astype(F32)).astype(BF16))
        n_scr[h] = decay * n_row + jnp.sum(ws * k.astype(F32), axis=0, keepdims=True)
        m_scr[h] = m_last
        hn = hc * lax.rsqrt(jnp.mean(hc * hc, axis=-1, keepdims=True) + RMS_EPS)
        sl = slice(h * ML_DV, (h + 1) * ML_DV)
        y_ref[:, sl] = (hn * hn_ref[:, sl] * jax.nn.sigmoid(o_ref[:, sl])).astype(BF16)

    @pl.when(ci == pl.num_programs(1) - 1)
    def _():
        eye_v = _eye(ML_DV, ML_DV, BF16)
        for h in range(ML_HEADS):
            cf_ref[h] = _exact_nt(eye_v, ct_scr[h])
        nf_ref[...] = n_scr[...]
        mf_ref[...] = m_scr[...]


def _mlstm(qkv, o, gates, b_gate, head_norm, c0, n0, m0, *, nb, lp, lv):
    m = qkv.shape[0]
    tseq = m // nb
    bb = math.gcd(nb, ML_SEQ_PER_STEP)
    seq3 = lambda a: a.reshape(nb, tseq, a.shape[1])
    row = lambda b, c: (b, c, 0)
    st4 = lambda b, c: (b, 0, 0, 0)
    kern = functools.partial(_mlstm_kernel, lp=lp, lv=lv, bb=bb)
    y, cf, nf, mf = pl.pallas_call(
        kern,
        grid=(nb // bb, tseq // lp),
        in_specs=[pl.BlockSpec((bb, lp, qkv.shape[1]), row), pl.BlockSpec((bb, lp, ML_V_W), row),
                  pl.BlockSpec((bb, lp, LANES), row), pl.BlockSpec((1, LANES), lambda b, c: (0, 0)),
                  pl.BlockSpec((1, ML_V_W), lambda b, c: (0, 0)),
                  pl.BlockSpec((bb, ML_HEADS, ML_DV, ML_DQK), st4),
                  pl.BlockSpec((bb, ML_HEADS, 1, ML_DQK), st4),
                  pl.BlockSpec((bb, ML_HEADS, 1, 1), st4)],
        out_specs=[pl.BlockSpec((bb, lp, ML_V_W), row),
                   pl.BlockSpec((bb, ML_HEADS, ML_DV, ML_DQK), st4),
                   pl.BlockSpec((bb, ML_HEADS, 1, ML_DQK), st4),
                   pl.BlockSpec((bb, ML_HEADS, 1, 1), st4)],
        out_shape=[jax.ShapeDtypeStruct((nb, tseq, ML_V_W), BF16),
                   jax.ShapeDtypeStruct((nb, ML_HEADS, ML_DV, ML_DQK), F32),
                   jax.ShapeDtypeStruct((nb, ML_HEADS, 1, ML_DQK), F32),
                   jax.ShapeDtypeStruct((nb, ML_HEADS, 1, 1), F32)],
        scratch_shapes=[pltpu.VMEM((bb, ML_HEADS, ML_DQK, ML_DV), F32),
                        pltpu.VMEM((bb, ML_HEADS, 1, ML_DQK), F32),
                        pltpu.VMEM((bb, ML_HEADS, 1, 1), F32)],
        compiler_params=_cparams("arbitrary", "arbitrary"),
        name="mlstm",
    )(seq3(qkv), seq3(o), seq3(gates), b_gate, head_norm, c0, n0, m0)
    return y.reshape(m, ML_V_W), cf, nf, mf


def _stack_cols(x, cols):
    return jnp.concatenate([x[:, c:c + 1] for c in cols], axis=0)


def _block_const(col, l, row):
    nh = col.shape[0] // l
    return jnp.concatenate([jnp.broadcast_to(col[h * l + row:h * l + row + 1, :], (l, 1)) for h in range(nh)], axis=0)


def _mlstm_stacked_one(qkv_ref, o_ref, gate_ref, bg_ref, hn_ref, c0_ref, n0_ref, m0_ref,
                       y_ref, cf_ref, nf_ref, mf_ref, ctn_scr, m_scr, blockmask, *, l):
    ci = pl.program_id(1)
    nh = ML_HEADS
    ht = nh * l
    dv = ML_DV
    lane = lax.broadcasted_iota(jnp.int32, (ht, LANES), 1)

    @pl.when(ci == 0)
    def _():
        eye = _eye(ML_DQK, ML_DQK, BF16)
        for h in range(nh):
            rows = slice(h * ML_DQK, (h + 1) * ML_DQK)
            ctn_scr[rows, 0:dv] = _exact_nt(eye, c0_ref[h])
        ctn_scr[:, dv:2 * dv] = jnp.where(lane == 0, n0_ref[...], 0.0)
        m_scr[...] = m0_ref[...]

    gp = gate_ref[...] + bg_ref[...]
    ls = _log_sigmoid(gp)
    tril = (lax.broadcasted_iota(jnp.int32, (l, l), 1) <= lax.broadcasted_iota(jnp.int32, (l, l), 0)).astype(BF16)
    hi, mid, lo = _split3(ls)
    b_all = _dot(tril, hi) + _dot(tril, mid) + _dot(tril, lo)
    sel16 = _eye(2 * nh, LANES, BF16)
    dup = lambda x: jnp.concatenate([x, x], axis=0)
    c_t = _exact_nt(sel16, dup(gp))[0:nh] - _exact_nt(sel16, dup(b_all))[nh:2 * nh]
    c_row = jnp.concatenate([jnp.broadcast_to(c_t[h:h + 1, :], (l, 2 * l)) for h in range(nh)], axis=0)
    r_b = _stack_cols(b_all, range(nh, 2 * nh))
    r_i = _stack_cols(gp, range(nh))
    t_row = lax.broadcasted_iota(jnp.int32, (ht, 1), 0) % l
    causal = (lane % l) <= t_row
    dmat = jnp.where(causal, r_b + c_row, MASK_NEG)
    m_col = m_scr[...]
    inter = r_b + m_col
    mt = jnp.maximum(inter, jnp.max(dmat, axis=-1, keepdims=True))

    q = qkv_ref[:, 0:ML_QK_W]
    k = qkv_ref[:, ML_QK_W:2 * ML_QK_W]
    zero = jnp.zeros((), BF16)
    q_bd = jnp.where(blockmask, _tile_rows(q, nh), zero)
    k_bd = jnp.where(blockmask, _tile_rows(k, nh), zero)
    s = _dot_nt(q_bd, dup(k)) * jnp.exp(dmat - mt)
    wi = jnp.exp(inter - mt)
    ones_lane = jnp.where(lane == 0, 1.0, 0.0).astype(BF16)
    v_st = jnp.concatenate([qkv_ref[:, 2 * ML_QK_W + h * dv:2 * ML_QK_W + (h + 1) * dv] for h in range(nh)], axis=0)
    v_aug = jnp.concatenate([v_st, ones_lane], axis=1)
    a = _dot(q_bd, ctn_scr[...].astype(BF16))
    s_bd = jnp.where(blockmask, jnp.concatenate([s.astype(BF16)] * (ML_QK_W // (2 * l)), axis=1), zero)
    bm = _dot(s_bd, v_aug)
    num = wi * a[:, 0:dv] + bm[:, 0:dv]
    den = wi * a[:, dv:dv + 1] + bm[:, dv:dv + 1]
    hc = num / jnp.maximum(jnp.abs(den), jnp.exp(-mt))

    m_last = _block_const(mt, l, l - 1)
    b_last = _block_const(r_b, l, l - 1)
    decay = jnp.exp(b_last + m_col - m_last)
    ws = jnp.exp(b_last - r_b + r_i - m_last)
    wv = (ws * v_aug.astype(F32)).astype(BF16)
    ctn_scr[...] = decay * ctn_scr[...] + _dot_tn(k_bd, wv)
    m_scr[...] = m_last

    hn = hc * lax.rsqrt(jnp.mean(hc * hc, axis=-1, keepdims=True) + RMS_EPS)
    for h in range(nh):
        sl = slice(h * dv, (h + 1) * dv)
        y_ref[:, sl] = (hn[h * l:(h + 1) * l] * hn_ref[:, sl] * jax.nn.sigmoid(o_ref[:, sl])).astype(BF16)

    @pl.when(ci == pl.num_programs(1) - 1)
    def _():
        eye_v = _eye(dv, dv, BF16)
        for h in range(nh):
            rows = slice(h * ML_DQK, (h + 1) * ML_DQK)
            cf_ref[h] = _exact_nt(eye_v, ctn_scr[rows, 0:dv])
        nf_ref[...] = ctn_scr[:, dv:dv + 1]
        mf_ref[...] = m_scr[...]


def _mlstm_stacked_kernel(qkv_ref, o_ref, gate_ref, bg_ref, hn_ref, c0_ref, n0_ref, m0_ref,
                          y_ref, cf_ref, nf_ref, mf_ref, ctn_scr, m_scr, *, l, bb):
    ht = ML_HEADS * l
    blockmask = (lax.broadcasted_iota(jnp.int32, (ht, ML_QK_W), 0) // l
                 == lax.broadcasted_iota(jnp.int32, (ht, ML_QK_W), 1) // ML_DQK)
    for bi in range(bb):
        _mlstm_stacked_one(qkv_ref.at[bi], o_ref.at[bi], gate_ref.at[bi], bg_ref, hn_ref, c0_ref.at[bi],
                           n0_ref.at[bi], m0_ref.at[bi], y_ref.at[bi], cf_ref.at[bi], nf_ref.at[bi],
                           mf_ref.at[bi], ctn_scr.at[bi], m_scr.at[bi], blockmask, l=l)


def _mlstm_stacked(qkv, o, gates, b_gate, head_norm, c0, n0, m0, *, nb, l):
    assert l == ML_DQK and ML_QK_W % (2 * l) == 0
    m = qkv.shape[0]
    tseq = m // nb
    bb = math.gcd(nb, ML_SEQ_PER_STEP)
    ht = ML_HEADS * l
    seq3 = lambda a: a.reshape(nb, tseq, a.shape[1])
    row = lambda b, c: (b, c, 0)
    st4 = lambda b, c: (b, 0, 0, 0)
    st3 = lambda b, c: (b, 0, 0)
    kern = functools.partial(_mlstm_stacked_kernel, l=l, bb=bb)
    y, cf, nf, mf = pl.pallas_call(
        kern,
        grid=(nb // bb, tseq // l),
        in_specs=[pl.BlockSpec((bb, l, qkv.shape[1]), row), pl.BlockSpec((bb, l, ML_V_W), row),
                  pl.BlockSpec((bb, l, LANES), row), pl.BlockSpec((1, LANES), lambda b, c: (0, 0)),
                  pl.BlockSpec((1, ML_V_W), lambda b, c: (0, 0)),
                  pl.BlockSpec((bb, ML_HEADS, ML_DV, ML_DQK), st4),
                  pl.BlockSpec((bb, ht, 1), st3), pl.BlockSpec((bb, ht, 1), st3)],
        out_specs=[pl.BlockSpec((bb, l, ML_V_W), row),
                   pl.BlockSpec((bb, ML_HEADS, ML_DV, ML_DQK), st4),
                   pl.BlockSpec((bb, ht, 1), st3), pl.BlockSpec((bb, ht, 1), st3)],
        out_shape=[jax.ShapeDtypeStruct((nb, tseq, ML_V_W), BF16),
                   jax.ShapeDtypeStruct((nb, ML_HEADS, ML_DV, ML_DQK), F32),
                   jax.ShapeDtypeStruct((nb, ht, 1), F32), jax.ShapeDtypeStruct((nb, ht, 1), F32)],
        scratch_shapes=[pltpu.VMEM((bb, ht, 2 * ML_DV), F32), pltpu.VMEM((bb, ht, 1), F32)],
        compiler_params=_cparams("arbitrary", "arbitrary"),
        name="mlstm",
    )(seq3(qkv), seq3(o), seq3(gates), b_gate, head_norm, c0,
      n0.reshape(nb, ht, 1), jnp.repeat(m0, l, axis=1))
    return (y.reshape(m, ML_V_W), cf, nf.reshape(nb, ML_HEADS, ML_DQK), mf.reshape(nb, ML_HEADS, l)[:, :, :1])


def _cmp_parts(load_rows, wbd_ref, c, p):
    acc = None
    for l in range(CMP_STRIDE):
        part = _dot(load_rows(l, c, p).astype(BF16), wbd_ref[c, l])
        acc = part if acc is None else acc + part
    return acc


def _cmp_pos_bias(pos_ref, wbd_ref, c):
    acc = None
    for l in range(CMP_STRIDE):
        part = _dot(pos_ref[c, l].astype(BF16), wbd_ref[c, l])
        acc = part if acc is None else acc + part
    return acc[0:1, :PAIR_W] + acc[1:2, PAIR_W:]


def _cmp_finish(parts, posb, w2_ref, c):
    n = parts.shape[0]
    pre = parts[:, :PAIR_W] + pltpu.roll(parts[:, PAIR_W:], n - 1, 0) + posb
    return _dot(_gelu_tanh(pre).astype(BF16), w2_ref[c])


def _nsa_compress_kernel(x00_ref, x01_ref, x10_ref, x11_ref, wbd_ref, w2_ref, pos_ref, kc_ref, vc_ref, *, nchunk):
    x_refs = ((x00_ref, x01_ref), (x10_ref, x11_ref))

    def load_rows(l, c, p):
        return x_refs[c][p][pl.ds(l, nchunk, stride=CMP_STRIDE), :]

    for c, out_ref in ((0, kc_ref), (1, vc_ref)):
        posb = _cmp_pos_bias(pos_ref, wbd_ref, c)
        for p in range(2):
            out_ref[p] = _cmp_finish(_cmp_parts(load_rows, wbd_ref, c, p), posb, w2_ref, c).astype(BF16)


def _nsa_compress(kva, wbd, w2bd, pos, *, nb, t):
    nchunk = t // CMP_STRIDE
    kern = functools.partial(_nsa_compress_kernel, nchunk=nchunk)
    out = jax.ShapeDtypeStruct((nb, 2, nchunk, PAIR_W), BF16)
    ospec = pl.BlockSpec((None, 2, nchunk, PAIR_W), lambda b: (b, 0, 0, 0))
    return pl.pallas_call(
        kern,
        grid=(nb,),
        in_specs=[pl.BlockSpec((t, PAIR_W), functools.partial(lambda j, b: (b, j), j)) for j in range(4)]
        + [_full(wbd.shape), _full(w2bd.shape), _full(pos.shape)],
        out_specs=[ospec, ospec],
        out_shape=[out, out],
        compiler_params=_cparams("parallel"),
        name="nsa_compress",
    )(kva, kva, kva, kva, wbd, w2bd, pos)


def _select_mask_t(imp_t, s0, score_scr, ns):
    qb = imp_t.shape[1]
    bid = lax.broadcasted_iota(jnp.int32, (ns, qb), 0)
    tq = s0 + lax.broadcasted_iota(jnp.int32, (ns, qb), 1)
    cur = tq // SEL_BLK
    valid = bid * SEL_BLK <= tq
    forced = (bid == 0) | (bid == cur) | (bid == cur - 1)
    score = jnp.where(valid, imp_t + jnp.where(forced, FORCE_BONUS, 0.0), MASK_NEG)
    score_scr[0:ns, :] = score

    def body(m, cnt):
        sm = score_scr[pl.ds(m, 1), :]
        ahead = (sm > score) | ((sm == score) & (m < bid))
        return cnt + ahead.astype(jnp.int32)

    cnt = lax.fori_loop(0, ns, body, jnp.zeros((ns, qb), jnp.int32), unroll=8)
    return cnt < min(SEL_TOPK, ns)


def _nsa_attn_kernel(slope_ref, q_ref, gate_ref, kc_ref, vc_ref, ksel_ref, vsel_ref, kwin_ref, vwin_ref,
                     e3_ref, o_ref, score_scr, *, t):
    pr = pl.program_id(1)
    nblk = pl.program_id(2)
    qb = Q_BLK
    s0 = nblk * qb
    ncp = t // CMP_STRIDE
    ns = t // SEL_BLK
    wk = WINDOW + qb
    hd = HEAD_DIM
    sg = jax.nn.sigmoid(gate_ref[...])

    tq_col = s0 + lax.broadcasted_iota(jnp.int32, (qb, 1), 0)
    cend = CMP_STRIDE * lax.broadcasted_iota(jnp.int32, (1, ncp), 1) + (CMP_LEN - 1)
    dist_c = tq_col - cend
    mask_c = dist_c >= 0
    distf_c = dist_c.astype(F32)
    jj = lax.broadcasted_iota(jnp.int32, (ns, ncp), 1) * CMP_STRIDE
    nn = lax.broadcasted_iota(jnp.int32, (ns, ncp), 0) * SEL_BLK
    ov_t = ((jj < nn + SEL_BLK) & (jj + CMP_LEN > nn)).astype(BF16)
    wstart = pl.multiple_of(jnp.maximum(s0 - WINDOW, 0), qb)
    dist_w = tq_col - (wstart + lax.broadcasted_iota(jnp.int32, (1, wk), 1))
    mask_w = (dist_w >= 0) & (dist_w < WINDOW)
    distf_w = dist_w.astype(F32)
    n_sel_chunks = (s0 + qb + SEL_CHUNK - 1) // SEL_CHUNK

    for half in range(2):
        lanes = slice(half * hd, (half + 1) * hd)
        qs = [q_ref[:, (half * NSA_HPG + hh) * hd:(half * NSA_HPG + hh + 1) * hd] for hh in range(NSA_HPG)]
        q_all = jnp.concatenate(qs, axis=0)
        slopes = [slope_ref[(pr * 2 + half) * NSA_HPG + hh] for hh in range(NSA_HPG)]

        s_c = _dot_nt(q_all, kc_ref[:, lanes])
        vc = vc_ref[...]
        pc_sum = jnp.zeros((qb, ncp), F32)
        o_cmp = []
        for hh in range(NSA_HPG):
            s = jnp.where(mask_c, s_c[hh * qb:(hh + 1) * qb] - slopes[hh] * distf_c, MASK_NEG)
            e = jnp.where(mask_c, jnp.exp(s - jnp.max(s, axis=-1, keepdims=True)), 0.0)
            pc = e / jnp.maximum(jnp.sum(e, axis=-1, keepdims=True), 1e-30)
            pc_sum = pc_sum + pc
            o_cmp.append(_dot(pc.astype(BF16), vc)[:, lanes])

        hi = pc_sum.astype(BF16)
        lo = (pc_sum - hi.astype(F32)).astype(BF16)
        imp_t = _dot_nt(ov_t, hi) + _dot_nt(ov_t, lo)
        sel_t = _select_mask_t(imp_t, s0, score_scr, ns)
        mneg = jnp.where(sel_t, 0.0, MASK_NEG).T.astype(BF16)

        def sel_chunk(kci, carry):
            m_run, l_run, acc = carry
            r0 = pl.multiple_of(kci * SEL_CHUNK, SEL_CHUNK)
            s_all = _dot_nt(q_all, ksel_ref[pl.ds(r0, SEL_CHUNK), lanes])
            kpos = r0 + lax.broadcasted_iota(jnp.int32, (1, SEL_CHUNK), 1)
            add = jnp.where(kpos > tq_col, MASK_NEG, _dot(mneg, e3_ref[kci]))
            prel = (kpos - s0).astype(F32)
            s = jnp.concatenate([s_all[hh * qb:(hh + 1) * qb] + (add + slopes[hh] * prel)
                                 for hh in range(NSA_HPG)], axis=0)
            m_new = jnp.maximum(m_run, jnp.max(s, axis=-1, keepdims=True))
            alpha = jnp.exp(m_run - m_new)
            p = jnp.exp(s - m_new)
            l_new = alpha * l_run + jnp.sum(p, axis=-1, keepdims=True)
            acc_new = alpha * acc + _dot(p.astype(BF16), vsel_ref[pl.ds(r0, SEL_CHUNK), :])
            return m_new, l_new, acc_new

        init = (jnp.full((NSA_HPG * qb, 1), MASK_NEG, F32), jnp.zeros((NSA_HPG * qb, 1), F32),
                jnp.zeros((NSA_HPG * qb, PAIR_W), F32))
        _, l_sel, acc_sel = lax.fori_loop(0, n_sel_chunks, sel_chunk, init)
        o_sel_all = acc_sel[:, lanes] / l_sel

        s_w = _dot_nt(q_all, kwin_ref[pl.ds(wstart, wk), lanes])
        vw = vwin_ref[pl.ds(wstart, wk), :]
        for hh in range(NSA_HPG):
            s = jnp.where(mask_w, s_w[hh * qb:(hh + 1) * qb] - slopes[hh] * distf_w, MASK_NEG)
            e = jnp.exp(s - jnp.max(s, axis=-1, keepdims=True))
            o_w = _dot(e.astype(BF16), vw)[:, lanes] / jnp.sum(e, axis=-1, keepdims=True)
            gi = (half * NSA_HPG + hh) * 3
            out = (sg[:, gi:gi + 1] * o_cmp[hh] + sg[:, gi + 1:gi + 2] * o_sel_all[hh * qb:(hh + 1) * qb]
                   + sg[:, gi + 2:gi + 3] * o_w)
            oc = (half * NSA_HPG + hh) * hd
            o_ref[:, oc:oc + hd] = out.astype(BF16)


def _nsa_attention(slopes, q, gates, kc, vc, kvb, e3, *, nb, t):
    nq = t // Q_BLK
    ncp = t // CMP_STRIDE
    pw = 2 * NSA_HPG * HEAD_DIM
    kvcol = lambda c: (lambda b, p, n: (b, 2 * c + p))
    kern = functools.partial(_nsa_attn_kernel, t=t)
    return pl.pallas_call(
        kern,
        grid=(nb, 2, nq),
        in_specs=[pl.BlockSpec(memory_space=pltpu.SMEM),
                  pl.BlockSpec((Q_BLK, pw), lambda b, p, n: (b * nq + n, p)),
                  pl.BlockSpec((Q_BLK, LANES), lambda b, p, n: (b * nq + n, p)),
                  pl.BlockSpec((None, None, ncp, PAIR_W), lambda b, p, n: (b, p, 0, 0)),
                  pl.BlockSpec((None, None, ncp, PAIR_W), lambda b, p, n: (b, p, 0, 0)),
                  pl.BlockSpec((t, PAIR_W), kvcol(2)), pl.BlockSpec((t, PAIR_W), kvcol(3)),
                  pl.BlockSpec((t, PAIR_W), kvcol(4)), pl.BlockSpec((t, PAIR_W), kvcol(5)),
                  _full(e3.shape)],
        out_specs=pl.BlockSpec((Q_BLK, pw), lambda b, p, n: (b * nq + n, p)),
        out_shape=jax.ShapeDtypeStruct((nb * t, NSA_Q_W), BF16),
        scratch_shapes=[pltpu.VMEM((max(t // SEL_BLK, SUBLANES), Q_BLK), F32)],
        compiler_params=_cparams("parallel", "parallel", "arbitrary"),
        name="nsa_attn",
    )(slopes, q, gates, kc, vc, kvb, kvb, kvb, kvb, e3)


KEY_CHUNK = 256


def _rank_topk_t(score, ns, qb):
    blocks = [score[SUBLANES * r:SUBLANES * (r + 1)] for r in range(ns // SUBLANES)]
    cnts = [jnp.zeros((SUBLANES, qb), jnp.int32) for _ in blocks]
    sub = lax.broadcasted_iota(jnp.int32, (SUBLANES, qb), 0)
    for m in range(ns):
        sm = jnp.broadcast_to(score[m:m + 1, :], (SUBLANES, qb))
        for r, blk in enumerate(blocks):
            lo = SUBLANES * r
            if lo > m:
                ahead = (sm >= blk).astype(jnp.int32)
            elif lo + SUBLANES - 1 <= m:
                ahead = (sm > blk).astype(jnp.int32)
            else:
                ahead = jnp.where(sub > m - lo, (sm >= blk).astype(jnp.int32), (sm > blk).astype(jnp.int32))
            cnts[r] = cnts[r] + ahead
    return jnp.concatenate(cnts, axis=0) < min(SEL_TOPK, ns)


def _online_unit(q, k, vaug, bias, u, m_scr, acc_scr):
    s = _dot_nt(q, k) + bias
    m_old = m_scr[u]
    m_new = jnp.maximum(m_old, jnp.max(s, axis=-1, keepdims=True))
    alpha = jnp.exp(m_old - m_new)
    p = jnp.exp(s - m_new)
    acc_scr[u] = alpha * acc_scr[u] + _dot(p.astype(BF16), vaug)
    m_scr[u] = m_new


def _nsa_attn2_kernel(slope_ref, q_ref, gate_ref, kc_ref, vc_ref, ksel_ref, vsel_ref, kwin_ref, vwin_ref,
                      e3_ref, o_ref, vs_scr, vw_scr, m_scr, acc_scr, pcs_scr, *, t):
    pr = pl.program_id(1)
    nblk = pl.program_id(2)
    qb = Q_BLK
    s0 = nblk * qb
    ncp = t // CMP_STRIDE
    ns = t // SEL_BLK
    hd = HEAD_DIM
    w = KEY_CHUNK
    nunit = 2 * NSA_HPG

    @pl.when(nblk == 0)
    def _():
        lane = lax.broadcasted_iota(jnp.int32, (t, PAIR_W), 1)
        for src, dst in ((vsel_ref, vs_scr), (vwin_ref, vw_scr)):
            v = src[...].astype(F32)
            dst[0] = jnp.where(lane < hd, v, 1.0).astype(BF16)
            dst[1] = jnp.where(lane < hd, pltpu.roll(v, hd, 1), 1.0).astype(BF16)

    sg = jax.nn.sigmoid(gate_ref[...])
    tq_col = s0 + lax.broadcasted_iota(jnp.int32, (qb, 1), 0)
    slopes = [[slope_ref[(pr * 2 + half) * NSA_HPG + hh] for hh in range(NSA_HPG)] for half in range(2)]
    q_of = lambda half, hh: q_ref[:, (half * NSA_HPG + hh) * hd:(half * NSA_HPG + hh + 1) * hd]
    lanes_of = lambda half: slice(half * hd, (half + 1) * hd)

    def reset_state():
        m_scr[...] = jnp.full(m_scr.shape, MASK_NEG, F32)
        acc_scr[...] = jnp.zeros(acc_scr.shape, F32)

    def read_out(u):
        a = acc_scr[u]
        return a[:, :hd] * (1.0 / a[:, hd:hd + 1])

    cend = CMP_STRIDE * lax.broadcasted_iota(jnp.int32, (1, ncp), 1) + (CMP_LEN - 1)
    add_c = jnp.where(tq_col >= cend, 0.0, MASK_NEG)
    prel_c = (cend - s0).astype(F32)
    jj = lax.broadcasted_iota(jnp.int32, (ns, ncp), 1) * CMP_STRIDE
    nn = lax.broadcasted_iota(jnp.int32, (ns, ncp), 0) * SEL_BLK
    ov_t = ((jj < nn + SEL_BLK) & (jj + CMP_LEN > nn)).astype(BF16)
    bid = lax.broadcasted_iota(jnp.int32, (ns, qb), 0)
    tq_row = s0 + lax.broadcasted_iota(jnp.int32, (ns, qb), 1)
    cur = tq_row // SEL_BLK
    valid_b = bid * SEL_BLK <= tq_row
    bonus = jnp.where((bid == 0) | (bid == cur) | (bid == cur - 1), FORCE_BONUS, 0.0)
    o_cmp = [[None] * NSA_HPG for _ in range(2)]
    mneg = []
    for half in range(2):
        kc = kc_ref[:, lanes_of(half)]
        vc = vc_ref[...]
        for hh in range(NSA_HPG):
            s = _dot_nt(q_of(half, hh), kc) + (add_c + slopes[half][hh] * prel_c)
            m = jnp.max(s, axis=-1, keepdims=True)
            e = jnp.where(s > 0.5 * MASK_NEG, jnp.exp(s - m), 0.0)
            pc = e * (1.0 / jnp.maximum(jnp.sum(e, axis=-1, keepdims=True), 1e-30))
            if hh == 0:
                pcs_scr[...] = pc
            else:
                pcs_scr[...] += pc
            o_cmp[half][hh] = _dot(pc.astype(BF16), vc)[:, lanes_of(half)]
        pc_sum = pcs_scr[...]
        hi = pc_sum.astype(BF16)
        lo = (pc_sum - hi.astype(F32)).astype(BF16)
        imp_t = _dot_nt(ov_t, hi) + _dot_nt(ov_t, lo)
        score = jnp.where(valid_b, imp_t + bonus, MASK_NEG)
        sel_t = _rank_topk_t(score, ns, qb)
        mneg.append(jnp.where(sel_t, 0.0, MASK_NEG).T.astype(BF16))

    reset_state()

    def sel_body(kci, carry):
        r0 = pl.multiple_of(kci * w, w)
        kpos = r0 + lax.broadcasted_iota(jnp.int32, (1, w), 1)
        prel = (kpos - s0).astype(F32)
        for half in range(2):
            add = jnp.where(kpos > tq_col, MASK_NEG, _dot(mneg[half], e3_ref[kci]))
            k = ksel_ref[pl.ds(r0, w), lanes_of(half)]
            vaug = vs_scr[half, pl.ds(r0, w), :]
            for hh in range(NSA_HPG):
                _online_unit(q_of(half, hh), k, vaug, add + slopes[half][hh] * prel,
                             half * NSA_HPG + hh, m_scr, acc_scr)
        return carry

    lax.fori_loop(0, (s0 + qb + w - 1) // w, sel_body, 0)
    o_sel = [read_out(u) for u in range(nunit)]

    reset_state()
    wspan = -(-(WINDOW + qb) // w) * w
    wstart = pl.multiple_of(jnp.maximum(s0 + qb - wspan, 0), qb)
    for ci in range(wspan // w):
        r0 = pl.multiple_of(wstart + ci * w, qb)
        kpos = r0 + lax.broadcasted_iota(jnp.int32, (1, w), 1)
        dist = tq_col - kpos
        add = jnp.where((dist >= 0) & (dist < WINDOW), 0.0, MASK_NEG)
        prel = (kpos - s0).astype(F32)
        for half in range(2):
            k = kwin_ref[pl.ds(r0, w), lanes_of(half)]
            vaug = vw_scr[half, pl.ds(r0, w), :]
            for hh in range(NSA_HPG):
                _online_unit(q_of(half, hh), k, vaug, add + slopes[half][hh] * prel,
                             half * NSA_HPG + hh, m_scr, acc_scr)

    for half in range(2):
        for hh in range(NSA_HPG):
            u = half * NSA_HPG + hh
            gi = u * 3
            out = (sg[:, gi:gi + 1] * o_cmp[half][hh] + sg[:, gi + 1:gi + 2] * o_sel[u]
                   + sg[:, gi + 2:gi + 3] * read_out(u))
            o_ref[:, u * hd:(u + 1) * hd] = out.astype(BF16)


def _nsa_attention2(slopes, q, gates, kc, vc, kvb, e3, *, nb, t):
    nq = t // Q_BLK
    ncp = t // CMP_STRIDE
    pw = 2 * NSA_HPG * HEAD_DIM
    nunit = 2 * NSA_HPG
    kvcol = lambda c: (lambda b, p, n: (b, 2 * c + p))
    kern = functools.partial(_nsa_attn2_kernel, t=t)
    return pl.pallas_call(
        kern,
        grid=(nb, 2, nq),
        in_specs=[pl.BlockSpec(memory_space=pltpu.SMEM),
                  pl.BlockSpec((Q_BLK, pw), lambda b, p, n: (b * nq + n, p)),
                  pl.BlockSpec((Q_BLK, LANES), lambda b, p, n: (b * nq + n, p)),
                  pl.BlockSpec((None, None, ncp, PAIR_W), lambda b, p, n: (b, p, 0, 0)),
                  pl.BlockSpec((None, None, ncp, PAIR_W), lambda b, p, n: (b, p, 0, 0)),
                  pl.BlockSpec((t, PAIR_W), kvcol(2)), pl.BlockSpec((t, PAIR_W), kvcol(3)),
                  pl.BlockSpec((t, PAIR_W), kvcol(4)), pl.BlockSpec((t, PAIR_W), kvcol(5)),
                  _full(e3.shape)],
        out_specs=pl.BlockSpec((Q_BLK, pw), lambda b, p, n: (b * nq + n, p)),
        out_shape=jax.ShapeDtypeStruct((nb * t, NSA_Q_W), BF16),
        scratch_shapes=[pltpu.VMEM((2, t, PAIR_W), BF16), pltpu.VMEM((2, t, PAIR_W), BF16),
                        pltpu.VMEM((nunit, Q_BLK, 1), F32), pltpu.VMEM((nunit, Q_BLK, PAIR_W), F32),
                        pltpu.VMEM((Q_BLK, ncp), F32)],
        compiler_params=_cparams("parallel", "parallel", "arbitrary"),
        name="nsa_attn",
    )(slopes, q, gates, kc, vc, kvb, kvb, kvb, kvb, e3)


def _sel_expand2(t):
    key_blk = jnp.arange(t, dtype=jnp.int32).reshape(t // KEY_CHUNK, 1, KEY_CHUNK) // SEL_BLK
    blk = jnp.arange(t // SEL_BLK, dtype=jnp.int32).reshape(1, t // SEL_BLK, 1)
    return (key_blk == blk).astype(BF16)


N_FEAT = 6
AUG_W = 2 * LANES


def _key_feats(kpos_col):
    r = kpos_col.shape[0]
    lane = lax.broadcasted_iota(jnp.int32, (r, LANES), 1)
    a = lax.shift_right_logical(kpos_col, 6).astype(F32)
    b = (kpos_col & (SEL_BLK - 1)).astype(F32)
    return jnp.where(lane < 3, a, jnp.where(lane < N_FEAT, b, 0.0))


def _tile_rows(x, n):
    return jnp.concatenate([x] * n, axis=0)


def _nsa_attn3_kernel(sf_ref, q_ref, gate_ref, kc_ref, vc_ref, ksel_ref, vsel_ref, kwin_ref, vwin_ref, o_ref,
                      ks_scr, kw_scr, kc_scr, vs_scr, vw_scr, qs_scr, qw_scr, qc_scr, mrun_scr, m_scr, acc_scr,
                      *, t):
    pr = pl.program_id(1)
    nblk = pl.program_id(2)
    qb = Q_BLK
    s0 = nblk * qb
    ncp = t // CMP_STRIDE
    ns = t // SEL_BLK
    hd = HEAD_DIM
    w = KEY_CHUNK
    hpg = NSA_HPG
    rows_all = hpg * qb

    @pl.when(nblk == 0)
    def _():
        lane = lax.broadcasted_iota(jnp.int32, (t, LANES), 1)
        row = lax.broadcasted_iota(jnp.int32, (t, 1), 0)
        feats = _key_feats(row).astype(BF16)
        onehot = jnp.where(lane - hd == lax.shift_right_logical(row, 6), 1.0, 0.0)
        lane_w = lax.broadcasted_iota(jnp.int32, (WINDOW, LANES), 1)
        lane_c = lax.broadcasted_iota(jnp.int32, (ncp, LANES), 1)
        cend = CMP_STRIDE * lax.broadcasted_iota(jnp.int32, (ncp, 1), 0) + (CMP_LEN - 1)
        ks = ksel_ref[...].astype(F32)
        vs = vsel_ref[...].astype(F32)
        kwn = kwin_ref[...].astype(F32)
        vwn = vwin_ref[...].astype(F32)
        kcv = kc_ref[...].astype(F32)
        for half in range(2):
            low = (lambda x: x) if half == 0 else (lambda x: pltpu.roll(x, hd, 1))
            ks_scr[half, :, 0:LANES] = jnp.where(lane < hd, low(ks), onehot).astype(BF16)
            ks_scr[half, :, LANES:AUG_W] = feats
            vs_scr[half] = jnp.where(lane < hd, low(vs), 1.0).astype(BF16)
            kw_scr[half, 0:WINDOW, 0:LANES] = jnp.where(lane_w == hd, 1.0, 0.0).astype(BF16)
            kw_scr[half, 0:WINDOW, LANES:AUG_W] = jnp.zeros((WINDOW, LANES), BF16)
            kw_scr[half, WINDOW:WINDOW + t, 0:LANES] = jnp.where(lane < hd, low(kwn), 0.0).astype(BF16)
            kw_scr[half, WINDOW:WINDOW + t, LANES:AUG_W] = feats
            vw_scr[half, 0:WINDOW] = jnp.ones((WINDOW, LANES), BF16)
            vw_scr[half, WINDOW:WINDOW + t] = jnp.where(lane < hd, low(vwn), 1.0).astype(BF16)
            kc_scr[half, :, 0:LANES] = jnp.where(lane_c < hd, low(kcv), 0.0).astype(BF16)
            kc_scr[half, :, LANES:AUG_W] = _key_feats(cend).astype(BF16)

    sg = jax.nn.sigmoid(gate_ref[...])
    lane = lax.broadcasted_iota(jnp.int32, (qb, LANES), 1)
    lane1 = lax.broadcasted_iota(jnp.int32, (1, LANES), 1)
    t_loc = lax.broadcasted_iota(jnp.int32, (qb, 1), 0)
    tq_col = s0 + t_loc
    pad_row = jnp.where(lane1 == hd, MASK_NEG, 0.0)

    for half in range(2):
        for hh in range(hpg):
            u = half * hpg + hh
            qcol = q_ref[:, (u // 2) * LANES:(u // 2 + 1) * LANES].astype(F32)
            qlow = jnp.where(lane < hd, qcol if u % 2 == 0 else pltpu.roll(qcol, hd, 1), 0.0)
            feat = jnp.zeros((1, LANES), F32)
            for j in range(N_FEAT):
                feat = jnp.where(lane1 == j, sf_ref[(pr * 2 + half) * hpg + hh, j], feat)
            feat = jnp.broadcast_to(feat, (qb, LANES)).astype(BF16)
            rows = slice(hh * qb, (hh + 1) * qb)
            qc_scr[half, rows, 0:LANES] = qlow.astype(BF16)
            qw_scr[half, rows, 0:LANES] = (qlow + pad_row).astype(BF16)
            for scr in (qc_scr, qw_scr, qs_scr):
                scr[half, rows, LANES:AUG_W] = feat

    def online(half, s, vaug):
        nj = s.shape[1] // LANES
        mx = s[:, 0:LANES]
        for j in range(1, nj):
            mx = jnp.maximum(mx, s[:, j * LANES:(j + 1) * LANES])
        m_old = m_scr[half]
        m_new = jnp.maximum(m_old, jnp.max(mx, axis=-1, keepdims=True))
        alpha = jnp.exp(m_old - m_new)
        p = jnp.exp(s - jnp.concatenate([m_new] * nj, axis=1))
        acc_scr[half] = alpha * acc_scr[half] + _dot(p.astype(BF16), vaug)
        m_scr[half] = m_new

    def read_out(half):
        a = acc_scr[half]
        return a[:, 0:hd] * (1.0 / a[:, hd:hd + 1])

    def reset(scr, val):
        scr[...] = jnp.full(scr.shape, val, F32)

    cend_row = CMP_STRIDE * lax.broadcasted_iota(jnp.int32, (1, ncp), 1) + (CMP_LEN - 1)
    add_c = _tile_rows(jnp.where(tq_col >= cend_row, 0.0, MASK_NEG), hpg)
    jj = lax.broadcasted_iota(jnp.int32, (ns, ncp), 1) * CMP_STRIDE
    nn = lax.broadcasted_iota(jnp.int32, (ns, ncp), 0) * SEL_BLK
    ov_t = ((jj < nn + SEL_BLK) & (jj + CMP_LEN > nn)).astype(BF16)
    bid = lax.broadcasted_iota(jnp.int32, (ns, qb), 0)
    tq_row = s0 + lax.broadcasted_iota(jnp.int32, (ns, qb), 1)
    cur = tq_row // SEL_BLK
    valid_b = bid * SEL_BLK <= tq_row
    bonus = jnp.where((bid == 0) | (bid == cur) | (bid == cur - 1), FORCE_BONUS, 0.0)
    o_cmp = []
    for half in range(2):
        s = _dot_nt(qc_scr[half], kc_scr[half]) + add_c
        m = jnp.max(s, axis=-1, keepdims=True)
        e = jnp.where(s > 0.5 * MASK_NEG, jnp.exp(s - m), 0.0)
        pc = e * (1.0 / jnp.maximum(jnp.sum(e, axis=-1, keepdims=True), 1e-30))
        o_cmp.append(_dot(pc.astype(BF16), vc_ref[...])[:, half * hd:(half + 1) * hd])
        pc_sum = pc[0:qb]
        for hh in range(1, hpg):
            pc_sum = pc_sum + pc[hh * qb:(hh + 1) * qb]
        hi = pc_sum.astype(BF16)
        lo = (pc_sum - hi.astype(F32)).astype(BF16)
        imp_t = _dot_nt(ov_t, hi) + _dot_nt(ov_t, lo)
        sel_t = _rank_topk_t(jnp.where(valid_b, imp_t + bonus, MASK_NEG), ns, qb)
        pieces = [jnp.zeros((hd, qb), F32), jnp.where(sel_t, 0.0, MASK_NEG)]
        if LANES - hd - ns > 0:
            pieces.append(jnp.zeros((LANES - hd - ns, qb), F32))
        mneg = jnp.concatenate(pieces, axis=0).T
        for hh in range(hpg):
            rows = slice(hh * qb, (hh + 1) * qb)
            qs_scr[half, rows, 0:LANES] = (qc_scr[half, rows, 0:LANES].astype(F32) + mneg).astype(BF16)

    nfull = nblk // (w // qb)
    r_tail = pl.multiple_of(nfull * w, w)
    kpos_tail = r_tail + lax.broadcasted_iota(jnp.int32, (1, w), 1)
    causal_add = _tile_rows(jnp.where(kpos_tail > tq_col, MASK_NEG, 0.0), hpg)

    def sel_scores(half, r0, causal):
        s = _dot_nt(qs_scr[half], ks_scr[half, pl.ds(r0, w), :])
        return s + causal_add if causal else s

    def sel_body(kci, carry):
        r0 = pl.multiple_of(kci * w, w)
        for half in range(2):
            online(half, sel_scores(half, r0, False), vs_scr[half, pl.ds(r0, w), :])
        return carry

    reset(m_scr, MASK_NEG)
    reset(acc_scr, 0.0)
    lax.fori_loop(0, nfull, sel_body, 0)
    for half in range(2):
        online(half, sel_scores(half, r_tail, True), vs_scr[half, pl.ds(r_tail, w), :])
    o_sel = [read_out(half) for half in range(2)]

    c_loc = lax.broadcasted_iota(jnp.int32, (1, qb), 1)
    left_add = _tile_rows(jnp.where(c_loc > t_loc, 0.0, MASK_NEG), hpg)
    diag_add = _tile_rows(jnp.concatenate([jnp.zeros((qb, qb), F32),
                                           jnp.where(c_loc <= t_loc, 0.0, MASK_NEG)], axis=1), hpg)
    win_chunks = ((0, qb, left_add), (qb, w, None), (qb + w, w, diag_add))

    reset(m_scr, MASK_NEG)
    reset(acc_scr, 0.0)
    for off, wd, add in win_chunks:
        r0 = pl.multiple_of(s0 + off, qb)
        for half in range(2):
            s = _dot_nt(qw_scr[half], kw_scr[half, pl.ds(r0, wd), :])
            online(half, s if add is None else s + add, vw_scr[half, pl.ds(r0, wd), :])

    for half in range(2):
        o_w = read_out(half)
        for hh in range(hpg):
            u = half * hpg + hh
            gi = u * 3
            rows = slice(hh * qb, (hh + 1) * qb)
            out = (sg[:, gi:gi + 1] * o_cmp[half][rows] + sg[:, gi + 1:gi + 2] * o_sel[half][rows]
                   + sg[:, gi + 2:gi + 3] * o_w[rows])
            o_ref[:, u * hd:(u + 1) * hd] = out.astype(BF16)


def _nsa_attention3(slope_feats, q, gates, kc, vc, kvb, *, nb, t):
    assert WINDOW + Q_BLK == Q_BLK + 2 * KEY_CHUNK and t % KEY_CHUNK == 0 and t // SEL_BLK <= LANES - HEAD_DIM
    nq = t // Q_BLK
    ncp = t // CMP_STRIDE
    pw = 2 * NSA_HPG * HEAD_DIM
    rows_all = NSA_HPG * Q_BLK
    kvcol = lambda c: (lambda b, p, n: (b, 2 * c + p))
    kern = functools.partial(_nsa_attn3_kernel, t=t)
    return pl.pallas_call(
        kern,
        grid=(nb, 2, nq),
        in_specs=[pl.BlockSpec(memory_space=pltpu.SMEM),
                  pl.BlockSpec((Q_BLK, pw), lambda b, p, n: (b * nq + n, p)),
                  pl.BlockSpec((Q_BLK, LANES), lambda b, p, n: (b * nq + n, p)),
                  pl.BlockSpec((None, None, ncp, PAIR_W), lambda b, p, n: (b, p, 0, 0)),
                  pl.BlockSpec((None, None, ncp, PAIR_W), lambda b, p, n: (b, p, 0, 0)),
                  pl.BlockSpec((t, PAIR_W), kvcol(2)), pl.BlockSpec((t, PAIR_W), kvcol(3)),
                  pl.BlockSpec((t, PAIR_W), kvcol(4)), pl.BlockSpec((t, PAIR_W), kvcol(5))],
        out_specs=pl.BlockSpec((Q_BLK, pw), lambda b, p, n: (b * nq + n, p)),
        out_shape=jax.ShapeDtypeStruct((nb * t, NSA_Q_W), BF16),
        scratch_shapes=[pltpu.VMEM((2, t, AUG_W), BF16), pltpu.VMEM((2, WINDOW + t, AUG_W), BF16),
                        pltpu.VMEM((2, ncp, AUG_W), BF16),
                        pltpu.VMEM((2, t, LANES), BF16), pltpu.VMEM((2, WINDOW + t, LANES), BF16),
                        pltpu.VMEM((2, rows_all, AUG_W), BF16), pltpu.VMEM((2, rows_all, AUG_W), BF16),
                        pltpu.VMEM((2, rows_all, AUG_W), BF16),
                        pltpu.VMEM((2, rows_all, LANES), F32), pltpu.VMEM((2, rows_all, LANES), F32),
                        pltpu.VMEM((2, rows_all, LANES), F32)],
        compiler_params=_cparams("parallel", "parallel", "arbitrary"),
        name="nsa_attn",
    )(slope_feats, q, gates, kc, vc, kvb, kvb, kvb, kvb)


def _slope_feats():
    s = _alibi_slopes() * LOG2E
    hi = s.astype(BF16).astype(F32)
    mid = (s - hi).astype(BF16).astype(F32)
    lo = (s - hi - mid).astype(BF16).astype(F32)
    z = jnp.zeros_like(s)
    return jnp.stack([SEL_BLK * hi, SEL_BLK * mid, SEL_BLK * lo, hi, mid, lo, z, z], axis=1)


WIN_JOBS = 3
WIN_PAD = WIN_JOBS * KEY_CHUNK - Q_BLK
PAD_FEAT = N_FEAT
NO_SPAN = 1 << 30


def _nsa_attn5_kernel(sf_ref, q_ref, gate_ref, kc_ref, vc_ref, ksel_ref, vsel_ref, kwin_ref, vwin_ref, o_ref,
                      k_scr, v_scr, kc_scr, qa_scr, qc_scr, s_scr, mx_scr, m_scr, acc_scr, *, t):
    pr = pl.program_id(1)
    nblk = pl.program_id(2)
    qb = Q_BLK
    s0 = nblk * qb
    ncp = t // CMP_STRIDE
    ns = t // SEL_BLK
    hd = HEAD_DIM
    w = KEY_CHUNK
    hpg = NSA_HPG
    rows_all = hpg * qb
    wrow0 = t

    @pl.when(nblk == 0)
    def _():
        lane = lax.broadcasted_iota(jnp.int32, (t, LANES), 1)
        row = lax.broadcasted_iota(jnp.int32, (t, 1), 0)
        feats = _key_feats(row).astype(BF16)
        onehot = jnp.where(lane - hd == lax.shift_right_logical(row, 6), 1.0, 0.0)
        lane_p = lax.broadcasted_iota(jnp.int32, (WIN_PAD, LANES), 1)
        lane_c = lax.broadcasted_iota(jnp.int32, (ncp, LANES), 1)
        cend = CMP_STRIDE * lax.broadcasted_iota(jnp.int32, (ncp, 1), 0) + (CMP_LEN - 1)
        ks = ksel_ref[...].astype(F32)
        vs = vsel_ref[...].astype(F32)
        kwn = kwin_ref[...].astype(F32)
        vwn = vwin_ref[...].astype(F32)
        kcv = kc_ref[...].astype(F32)
        for half in range(2):
            low = (lambda x: x) if half == 0 else (lambda x: pltpu.roll(x, hd, 1))
            k_scr[half, 0:t, 0:LANES] = jnp.where(lane < hd, low(ks), onehot).astype(BF16)
            k_scr[half, 0:t, LANES:AUG_W] = feats
            v_scr[half, 0:t] = jnp.where(lane < hd, low(vs), 1.0).astype(BF16)
            k_scr[half, wrow0:wrow0 + WIN_PAD, 0:LANES] = jnp.zeros((WIN_PAD, LANES), BF16)
            k_scr[half, wrow0:wrow0 + WIN_PAD, LANES:AUG_W] = jnp.where(lane_p == PAD_FEAT, 1.0, 0.0).astype(BF16)
            v_scr[half, wrow0:wrow0 + WIN_PAD] = jnp.ones((WIN_PAD, LANES), BF16)
            wr = wrow0 + WIN_PAD
            k_scr[half, wr:wr + t, 0:LANES] = jnp.where(lane < hd, low(kwn), 0.0).astype(BF16)
            k_scr[half, wr:wr + t, LANES:AUG_W] = feats
            v_scr[half, wr:wr + t] = jnp.where(lane < hd, low(vwn), 1.0).astype(BF16)
            kc_scr[half, :, 0:LANES] = jnp.where(lane_c < hd, low(kcv), 0.0).astype(BF16)
            kc_scr[half, :, LANES:AUG_W] = _key_feats(cend).astype(BF16)

    sg = jax.nn.sigmoid(gate_ref[...])
    lane = lax.broadcasted_iota(jnp.int32, (qb, LANES), 1)
    lane1 = lax.broadcasted_iota(jnp.int32, (1, LANES), 1)
    t_loc = lax.broadcasted_iota(jnp.int32, (qb, 1), 0)
    tq_col = s0 + t_loc

    for half in range(2):
        for hh in range(hpg):
            u = half * hpg + hh
            qcol = q_ref[:, (u // 2) * LANES:(u // 2 + 1) * LANES].astype(F32)
            qlow = jnp.where(lane < hd, qcol if u % 2 == 0 else pltpu.roll(qcol, hd, 1), 0.0)
            feat = jnp.where(lane1 == PAD_FEAT, MASK_NEG, 0.0)
            for j in range(N_FEAT):
                feat = jnp.where(lane1 == j, sf_ref[(pr * 2 + half) * hpg + hh, j], feat)
            feat = jnp.broadcast_to(feat, (qb, LANES)).astype(BF16)
            rows = slice(hh * qb, (hh + 1) * qb)
            qc_scr[half, rows, 0:LANES] = qlow.astype(BF16)
            qc_scr[half, rows, LANES:AUG_W] = feat
            qa_scr[half, rows, LANES:AUG_W] = feat

    cend_row = CMP_STRIDE * lax.broadcasted_iota(jnp.int32, (1, ncp), 1) + (CMP_LEN - 1)
    add_c = _tile_rows(jnp.where(tq_col >= cend_row, 0.0, MASK_NEG), hpg)
    jj = lax.broadcasted_iota(jnp.int32, (ns, ncp), 1) * CMP_STRIDE
    nn = lax.broadcasted_iota(jnp.int32, (ns, ncp), 0) * SEL_BLK
    ov_t = ((jj < nn + SEL_BLK) & (jj + CMP_LEN > nn)).astype(BF16)
    bid = lax.broadcasted_iota(jnp.int32, (ns, qb), 0)
    tq_row = s0 + lax.broadcasted_iota(jnp.int32, (ns, qb), 1)
    cur = tq_row // SEL_BLK
    valid_b = bid * SEL_BLK <= tq_row
    bonus = jnp.where((bid == 0) | (bid == cur) | (bid == cur - 1), FORCE_BONUS, 0.0)
    o_cmp = []
    for half in range(2):
        s = _dot_nt(qc_scr[half], kc_scr[half]) + add_c
        m = jnp.max(s, axis=-1, keepdims=True)
        e = jnp.where(s > 0.5 * MASK_NEG, jnp.exp2(s - m), 0.0)
        pc = e * (1.0 / jnp.maximum(jnp.sum(e, axis=-1, keepdims=True), 1e-30))
        o_cmp.append(_dot(pc.astype(BF16), vc_ref[...])[:, half * hd:(half + 1) * hd])
        pc_sum = pc[0:qb]
        for hh in range(1, hpg):
            pc_sum = pc_sum + pc[hh * qb:(hh + 1) * qb]
        hi = pc_sum.astype(BF16)
        lo = (pc_sum - hi.astype(F32)).astype(BF16)
        imp_t = _dot_nt(ov_t, hi) + _dot_nt(ov_t, lo)
        sel_t = _rank_topk_t(jnp.where(valid_b, imp_t + bonus, MASK_NEG), ns, qb)
        pieces = [jnp.zeros((hd, qb), F32), jnp.where(sel_t, 0.0, MASK_NEG)]
        if LANES - hd - ns > 0:
            pieces.append(jnp.zeros((LANES - hd - ns, qb), F32))
        mneg = jnp.concatenate(pieces, axis=0).T
        for hh in range(hpg):
            rows = slice(hh * qb, (hh + 1) * qb)
            qa_scr[half, rows, 0:LANES] = (qc_scr[half, rows, 0:LANES].astype(F32) + mneg).astype(BF16)

    nfull = nblk // (w // qb)
    npairs = nfull // 2
    c_loc = lax.broadcasted_iota(jnp.int32, (1, qb), 1)
    add_w0 = _tile_rows(jnp.concatenate([jnp.full((qb, qb), MASK_NEG, F32),
                                         jnp.where(c_loc > t_loc, 0.0, MASK_NEG)], axis=1), hpg)
    add_w2 = _tile_rows(jnp.concatenate([jnp.zeros((qb, qb), F32),
                                         jnp.where(c_loc <= t_loc, 0.0, MASK_NEG)], axis=1), hpg)
    r_tail = pl.multiple_of(nfull * w, w)
    kpos_tail = r_tail + lax.broadcasted_iota(jnp.int32, (1, w), 1)
    add_tail = _tile_rows(jnp.where(kpos_tail > tq_col, MASK_NEG, 0.0), hpg)
    r_odd = pl.multiple_of(jnp.maximum(nfull - 1, 0) * w, w)
    add_odd = jnp.where(nfull % 2 == 1, 0.0, MASK_NEG)
    w_row = lambda jw: pl.multiple_of(wrow0 + s0 + jw * w, qb)
    masked_jobs = ((qc_scr, w_row(0), add_w0, 1), (qc_scr, w_row(1), None, 1), (qc_scr, w_row(2), add_w2, 1),
                   (qa_scr, r_tail, add_tail, 0), (qa_scr, r_odd, add_odd, 0))

    def loop_row(j):
        return pl.multiple_of(jnp.where(j < 2 * npairs, j * w, wrow0), w)

    def scores(q_scr, r0, add, slot):
        for half in range(2):
            s = _dot_nt(q_scr[half], k_scr[half, pl.ds(r0, w), :])
            if add is not None:
                s = s + add
            s_scr[slot, half] = s
            mx = jnp.max(jnp.maximum(s[:, 0:LANES], s[:, LANES:w]), axis=-1, keepdims=True)
            mx_scr[slot, half] = jnp.broadcast_to(mx, (rows_all, LANES))

    def consume(r0, st, slot):
        for half in range(2):
            m_old = m_scr[st, half]
            m_new = jnp.maximum(m_old, mx_scr[slot, half])
            alpha = jnp.exp2(m_old - m_new)
            p = jnp.exp2(s_scr[slot, half] - jnp.concatenate([m_new, m_new], axis=1))
            acc_scr[st, half] = alpha * acc_scr[st, half] + _dot(p.astype(BF16), v_scr[half, pl.ds(r0, w), :])
            m_scr[st, half] = m_new

    m_scr[...] = jnp.full(m_scr.shape, MASK_NEG, F32)
    acc_scr[...] = jnp.zeros(acc_scr.shape, F32)
    scores(*masked_jobs[0][:3], 0)
    for i, (_, r0, _, st) in enumerate(masked_jobs):
        if i + 1 < len(masked_jobs):
            scores(*masked_jobs[i + 1][:3], (i + 1) % 2)
        else:
            scores(qa_scr, loop_row(0), None, (i + 1) % 2)
        consume(r0, st, i % 2)
    first = len(masked_jobs) % 2

    def step(i, carry):
        scores(qa_scr, loop_row(2 * i + 1), None, 1 - first)
        consume(loop_row(2 * i), 0, first)
        scores(qa_scr, loop_row(2 * i + 2), None, first)
        consume(loop_row(2 * i + 1), 0, 1 - first)
        return carry

    lax.fori_loop(0, npairs, step, 0)

    for half in range(2):
        outs = []
        for st in range(2):
            a = acc_scr[st, half]
            outs.append(a[:, 0:hd] * (1.0 / a[:, hd:hd + 1]))
        for hh in range(hpg):
            u = half * hpg + hh
            gi = u * 3
            rows = slice(hh * qb, (hh + 1) * qb)
            out = (sg[:, gi:gi + 1] * o_cmp[half][rows] + sg[:, gi + 1:gi + 2] * outs[0][rows]
                   + sg[:, gi + 2:gi + 3] * outs[1][rows])
            o_ref[:, u * hd:(u + 1) * hd] = out.astype(BF16)


def _nsa_attention5(slope_feats, q, gates, kc, vc, kvb, *, nb, t):
    assert WIN_JOBS * KEY_CHUNK >= WINDOW + Q_BLK and t % KEY_CHUNK == 0 and t // SEL_BLK <= LANES - HEAD_DIM
    nq = t // Q_BLK
    ncp = t // CMP_STRIDE
    pw = 2 * NSA_HPG * HEAD_DIM
    rows_all = NSA_HPG * Q_BLK
    krows = 2 * t + WIN_PAD
    kvcol = lambda c: (lambda b, p, n: (b, 2 * c + p))
    kern = functools.partial(_nsa_attn5_kernel, t=t)
    return pl.pallas_call(
        kern,
        grid=(nb, 2, nq),
        in_specs=[pl.BlockSpec(memory_space=pltpu.SMEM),
                  pl.BlockSpec((Q_BLK, pw), lambda b, p, n: (b * nq + n, p)),
                  pl.BlockSpec((Q_BLK, LANES), lambda b, p, n: (b * nq + n, p)),
                  pl.BlockSpec((None, None, ncp, PAIR_W), lambda b, p, n: (b, p, 0, 0)),
                  pl.BlockSpec((None, None, ncp, PAIR_W), lambda b, p, n: (b, p, 0, 0)),
                  pl.BlockSpec((t, PAIR_W), kvcol(2)), pl.BlockSpec((t, PAIR_W), kvcol(3)),
                  pl.BlockSpec((t, PAIR_W), kvcol(4)), pl.BlockSpec((t, PAIR_W), kvcol(5))],
        out_specs=pl.BlockSpec((Q_BLK, pw), lambda b, p, n: (b * nq + n, p)),
        out_shape=jax.ShapeDtypeStruct((nb * t, NSA_Q_W), BF16),
        scratch_shapes=[pltpu.VMEM((2, krows, AUG_W), BF16), pltpu.VMEM((2, krows, LANES), BF16),
                        pltpu.VMEM((2, ncp, AUG_W), BF16),
                        pltpu.VMEM((2, rows_all, AUG_W), BF16), pltpu.VMEM((2, rows_all, AUG_W), BF16),
                        pltpu.VMEM((2, 2, rows_all, KEY_CHUNK), F32), pltpu.VMEM((2, 2, rows_all, LANES), F32),
                        pltpu.VMEM((2, 2, rows_all, LANES), F32), pltpu.VMEM((2, 2, rows_all, LANES), F32)],
        compiler_params=_cparams("parallel", "parallel", "arbitrary"),
        name="nsa_attn",
    )(slope_feats, q, gates, kc, vc, kvb, kvb, kvb, kvb)


PAGES_PER_STEP = 32


def _page_compress_t_kernel(pt_ref, cache_ref, wbd_ref, w2_ref, pos_ref, kvc_ref,
                            buf, xa0, xa1, xa2, xa3, xb0, xb1, xb2, xb3, parts_scr, sem, *, page):
    b = pl.program_id(0)
    hf = pl.program_id(1)
    nhf = pl.num_programs(1)
    step = b * nhf + hf
    nstep = pl.num_programs(0) * nhf
    xsets = ((xa0, xa1, xa2, xa3), (xb0, xb1, xb2, xb3))
    nch = PAGES_PER_STEP * page // CMP_STRIDE
    row0 = pl.multiple_of(hf * nch, nch)

    def copies(s, slot):
        out = []
        for i in range(PAGES_PER_STEP):
            pg = pt_ref[s * PAGES_PER_STEP + i]
            for c in range(2):
                out.append(pltpu.make_async_copy(cache_ref.at[0, pg, c], buf.at[slot, c, i], sem.at[slot]))
        return out

    def transpose_pages(slot):
        for i in range(PAGES_PER_STEP):
            for c in range(2):
                tile = buf[slot, c, i].reshape(NSA_KV_HEADS * HEAD_DIM, page)
                for p in range(2):
                    xsets[slot][c * 2 + p][i * page:(i + 1) * page, :] = tile[p * PAIR_W:(p + 1) * PAIR_W].T

    def mlp(slot):
        def load_rows(l, c, p):
            return xsets[slot][c * 2 + p][pl.ds(l, nch, stride=CMP_STRIDE), :]
        for c in range(2):
            for p in range(2):
                parts_scr[c * 2 + p, pl.ds(row0, nch), :] = _cmp_parts(load_rows, wbd_ref, c, p)

    @pl.when(step == 0)
    def _():
        for cp in copies(0, 0):
            cp.start()
        for cp in copies(0, 0):
            cp.wait()
        transpose_pages(0)

        @pl.when(nstep > 1)
        def _():
            for cp in copies(1, 1):
                cp.start()

    for par in range(2):
        @pl.when(step % 2 == par)
        def _():
            @pl.when(step + 1 < nstep)
            def _():
                for cp in copies(step + 1, 1 - par):
                    cp.wait()

            @pl.when(step + 2 < nstep)
            def _():
                for cp in copies(step + 2, par):
                    cp.start()

            transpose_pages(1 - par)
            mlp(par)

    @pl.when(hf == nhf - 1)
    def _():
        for c in range(2):
            posb = _cmp_pos_bias(pos_ref, wbd_ref, c)
            for p in range(2):
                kvc_ref[c, p] = _cmp_finish(parts_scr[c * 2 + p], posb, w2_ref, c).astype(BF16)


def _page_compress_t(page_table, cache_t, wbd, w2bd, pos):
    sb, n_pages = page_table.shape
    page = cache_t.shape[-1]
    ncp = n_pages * page // CMP_STRIDE
    rows = PAGES_PER_STEP * page
    kern = functools.partial(_page_compress_t_kernel, page=page)
    grid_spec = pltpu.PrefetchScalarGridSpec(
        num_scalar_prefetch=1,
        grid=(sb, n_pages // PAGES_PER_STEP),
        in_specs=[pl.BlockSpec(memory_space=pl.ANY),
                  pl.BlockSpec(wbd.shape, lambda b, h, pt: (0,) * wbd.ndim),
                  pl.BlockSpec(w2bd.shape, lambda b, h, pt: (0,) * w2bd.ndim),
                  pl.BlockSpec(pos.shape, lambda b, h, pt: (0,) * pos.ndim)],
        out_specs=pl.BlockSpec((None, 2, 2, ncp, PAIR_W), lambda b, h, pt: (b, 0, 0, 0, 0)),
        scratch_shapes=[pltpu.VMEM((2, 2, PAGES_PER_STEP, NSA_KV_HEADS, HEAD_DIM, page), F32)]
        + [pltpu.VMEM((rows, PAIR_W), F32)] * 8
        + [pltpu.VMEM((4, ncp, 2 * PAIR_W), F32), pltpu.SemaphoreType.DMA((2,))],
    )
    return pl.pallas_call(
        kern,
        grid_spec=grid_spec,
        out_shape=jax.ShapeDtypeStruct((sb, 2, 2, ncp, PAIR_W), BF16),
        compiler_params=_cparams("arbitrary", "arbitrary"),
        name="nsa_page_compress",
    )(page_table.reshape(-1), cache_t, wbd, w2bd, pos)


def _row_slopes(slope_ref, g, rows):
    hh = lax.broadcasted_iota(jnp.int32, (rows, 1), 0) % NSA_HPG
    col = jnp.zeros((rows, 1), F32)
    for h in range(NSA_HPG):
        col = jnp.where(hh == h, slope_ref[g * NSA_HPG + h], col)
    return col


def _sample_cmp_kernel(slope_ref, q_ref, kvc_ref, ocmp_ref, imp_ref, *, past, nsp):
    rows = q_ref.shape[1]
    ncp = kvc_ref.shape[2]
    t_col = lax.broadcasted_iota(jnp.int32, (rows, 1), 0) // NSA_HPG
    jrow = lax.broadcasted_iota(jnp.int32, (1, ncp), 1)
    dist = (past + t_col) - (CMP_STRIDE * jrow + (CMP_LEN - 1))
    mask = (dist >= 0) & (jrow < ncp - 1)
    distf = dist.astype(F32)
    ri = lax.broadcasted_iota(jnp.int32, (rows, rows), 0) // NSA_HPG
    ci = lax.broadcasted_iota(jnp.int32, (rows, rows), 1) // NSA_HPG
    same_t = (ri == ci).astype(BF16)
    jj = lax.broadcasted_iota(jnp.int32, (ncp, nsp), 0) * CMP_STRIDE
    nn = lax.broadcasted_iota(jnp.int32, (ncp, nsp), 1) * SEL_BLK
    ov = ((jj < nn + SEL_BLK) & (jj + CMP_LEN > nn)).astype(BF16)
    for g in range(NSA_KV_HEADS):
        pr, half = divmod(g, 2)
        lanes = slice(half * HEAD_DIM, (half + 1) * HEAD_DIM)
        s_c = _dot_nt(q_ref[g], kvc_ref[0, pr, :, lanes])
        s = jnp.where(mask, s_c - _row_slopes(slope_ref, g, rows) * distf, MASK_NEG)
        e = jnp.where(mask, jnp.exp(s - jnp.max(s, axis=-1, keepdims=True)), 0.0)
        pc = e / jnp.maximum(jnp.sum(e, axis=-1, keepdims=True), 1e-30)
        ocmp_ref[g] = _dot(pc.astype(BF16), kvc_ref[1, pr])[:, lanes]
        hi = pc.astype(BF16)
        lo = (pc - hi.astype(F32)).astype(BF16)
        pcs = _dot(same_t, hi) + _dot(same_t, lo)
        hi = pcs.astype(BF16)
        lo = (pcs - hi.astype(F32)).astype(BF16)
        imp_ref[g] = _dot(hi, ov) + _dot(lo, ov)


def _sample_cmp(slopes, q_rows, kvc, *, past, nsp):
    sb, ng, rows, hd = q_rows.shape
    ncp = kvc.shape[3]
    kern = functools.partial(_sample_cmp_kernel, past=past, nsp=nsp)
    return pl.pallas_call(
        kern,
        grid=(sb,),
        in_specs=[pl.BlockSpec(memory_space=pltpu.SMEM),
                  pl.BlockSpec((None, ng, rows, hd), lambda b: (b, 0, 0, 0)),
                  pl.BlockSpec((None, 2, 2, ncp, PAIR_W), lambda b: (b, 0, 0, 0, 0))],
        out_specs=[pl.BlockSpec((None, ng, rows, hd), lambda b: (b, 0, 0, 0)),
                   pl.BlockSpec((None, ng, rows, nsp), lambda b: (b, 0, 0, 0))],
        out_shape=[jax.ShapeDtypeStruct((sb, ng, rows, hd), F32),
                   jax.ShapeDtypeStruct((sb, ng, rows, nsp), F32)],
        compiler_params=_cparams("parallel"),
        name="nsa_sample_cmp",
    )(slopes, q_rows, kvc)


def _sample_topk_kernel(imp_ref, idx_ref, score_scr, *, past, n_sel, st):
    nsr, ncol = imp_ref.shape
    bid = lax.broadcasted_iota(jnp.int32, (nsr, ncol), 0)
    tq = past + lax.broadcasted_iota(jnp.int32, (nsr, ncol), 1) % st
    cur = tq // SEL_BLK
    valid = (bid * SEL_BLK <= tq) & (bid < n_sel)
    forced = (bid == 0) | (bid == cur) | (bid == cur - 1)
    score = jnp.where(valid, imp_ref[...] + jnp.where(forced, FORCE_BONUS, 0.0), MASK_NEG)
    score_scr[...] = score

    def body(m, cnt):
        sm = score_scr[pl.ds(m, 1), :]
        ahead = (sm > score) | ((sm == score) & (m < bid))
        return cnt + ahead.astype(jnp.int32)

    cnt = lax.fori_loop(0, n_sel, body, jnp.zeros((nsr, ncol), jnp.int32))
    sel = (cnt < SEL_TOPK) & (bid < n_sel)
    tril = (lax.broadcasted_iota(jnp.int32, (nsr, nsr), 1) <= lax.broadcasted_iota(jnp.int32, (nsr, nsr), 0))
    prefix = _dot(tril.astype(BF16), sel.astype(BF16))
    for k in range(SEL_TOPK):
        hit = sel & (prefix == float(k + 1))
        idx_ref[k:k + 1, :] = jnp.sum(jnp.where(hit, bid, 0), axis=0, keepdims=True)


def _sample_topk(imp_t, *, past, n_sel, st):
    nsr, ncol = imp_t.shape
    kern = functools.partial(_sample_topk_kernel, past=past, n_sel=n_sel, st=st)
    return pl.pallas_call(
        kern,
        grid=(1,),
        in_specs=[_full(imp_t.shape)],
        out_specs=_full((SEL_TOPK, ncol)),
        out_shape=jax.ShapeDtypeStruct((SEL_TOPK, ncol), jnp.int32),
        scratch_shapes=[pltpu.VMEM((nsr, ncol), F32)],
        compiler_params=_cparams("arbitrary"),
        name="nsa_sample_topk",
    )(imp_t)


def _joint_softmax_pv(s_a, v_a_fn, s_b, v_b):
    m = jnp.maximum(jnp.max(s_a, axis=-1, keepdims=True), jnp.max(s_b, axis=-1, keepdims=True))
    p_a = jnp.where(s_a > 0.5 * MASK_NEG, jnp.exp(s_a - m), 0.0)
    p_b = jnp.where(s_b > 0.5 * MASK_NEG, jnp.exp(s_b - m), 0.0)
    l = jnp.sum(p_a, axis=-1, keepdims=True) + jnp.sum(p_b, axis=-1, keepdims=True)
    acc = v_a_fn(p_a.astype(BF16)) + _dot(p_b.astype(BF16), v_b)
    return acc / jnp.maximum(l, 1e-30)


def _sample_attn_kernel(idx_ref, pt_ref, slope_ref, cache_ref, q_ref, idxv_ref, e16_ref, gate_ref, ocmp_ref,
                        ksn_ref, vsn_ref, kwn_ref, vwn_ref, kwin_ref, vwin_ref, o_ref, kbuf, vbuf, sem,
                        *, past, n_pages, st):
    b = pl.program_id(0)
    pr = pl.program_id(1)
    rows = q_ref.shape[1]
    page = kbuf.shape[-1] // SEL_TOPK
    nkey = SEL_TOPK * page
    last_blk = past // SEL_BLK - 1
    npr = pl.num_programs(1)
    step = b * npr + pr
    nstep = pl.num_programs(0) * npr

    def copies(s, slot):
        sb_, spr = s // npr, s % npr
        out = []
        for half in range(2):
            g = spr * 2 + half
            for t in range(st):
                for k in range(SEL_TOPK):
                    blk = jnp.minimum(idx_ref[((sb_ * NSA_KV_HEADS + g) * st + t) * SEL_TOPK + k], last_blk)
                    pg = pt_ref[sb_ * n_pages + lax.shift_right_logical(blk, 1)]
                    dst = pl.ds(k * page, page)
                    for c, buf in ((2, kbuf), (3, vbuf)):
                        out.append(pltpu.make_async_copy(cache_ref.at[0, pg, c, g],
                                                         buf.at[slot, half, t, :, dst], sem.at[slot]))
        return out

    @pl.when(step == 0)
    def _():
        for cp in copies(0, 0):
            cp.start()

    slot = step % 2

    @pl.when(step + 1 < nstep)
    def _():
        for cp in copies(step + 1, 1 - slot):
            cp.start()

    for cp in copies(step, slot):
        cp.wait()

    t_col = lax.broadcasted_iota(jnp.int32, (rows, 1), 0) // NSA_HPG
    tqf = (past + t_col).astype(F32)
    col = lax.broadcasted_iota(jnp.int32, (1, nkey), 1)
    off = (col % SEL_BLK).astype(F32)
    col_half = ((col % page) // SEL_BLK).astype(F32)
    tp = lax.broadcasted_iota(jnp.int32, (1, ksn_ref.shape[0]), 1)
    dn = t_col - tp
    valid_n = (dn >= 0) & (tp < st)
    dnf = dn.astype(F32)
    wb = kwin_ref.shape[0]
    dist_w = (wb + t_col) - lax.broadcasted_iota(jnp.int32, (1, wb), 1)
    valid_w = (dist_w >= 0) & (dist_w < WINDOW)

    for half in range(2):
        g = pr * 2 + half
        lanes = slice(half * HEAD_DIM, (half + 1) * HEAD_DIM)
        q = q_ref[half]
        slope = _row_slopes(slope_ref, g, rows)

        blk = _dot(idxv_ref[half], e16_ref[...])
        is_new = blk > (last_blk + 0.5)
        dist = tqf - (blk * SEL_BLK + off)
        in_half = (blk - 2.0 * jnp.floor(0.5 * blk)) == col_half
        valid = jnp.logical_not(is_new) & in_half & (dist >= 0)
        s_sel = jnp.full((rows, nkey), MASK_NEG, F32)
        for t in range(st):
            s_t = _dot(q, kbuf[slot, half, t].astype(BF16))
            s_sel = jnp.where(t_col == t, s_t, s_sel)
        s_sel = jnp.where(valid, s_sel - slope * dist, MASK_NEG)
        has_new = jnp.max(is_new.astype(F32), axis=-1, keepdims=True) > 0.5
        s_new = jnp.where(valid_n & has_new, _dot_nt(q, ksn_ref[:, lanes]) - slope * dnf, MASK_NEG)

        def pv_sel(p):
            acc = jnp.zeros((rows, HEAD_DIM), F32)
            for t in range(st):
                acc = jnp.where(t_col == t, _dot_nt(p, vbuf[slot, half, t].astype(BF16)), acc)
            return acc

        o_sel = _joint_softmax_pv(s_sel, pv_sel, s_new, vsn_ref[:, lanes])

        s_w = _dot_nt(q, kwin_ref[:, lanes].astype(BF16))
        s_w = jnp.where(valid_w, s_w - slope * dist_w.astype(F32), MASK_NEG)
        s_wn = jnp.where(valid_n, _dot_nt(q, kwn_ref[:, lanes]) - slope * dnf, MASK_NEG)
        vw = vwin_ref[...].astype(BF16)
        o_w = _joint_softmax_pv(s_w, lambda p: _dot(p, vw), s_wn, vwn_ref[...])[:, lanes]

        sg = jax.nn.sigmoid(gate_ref[half])
        o_ref[half] = sg[:, 0:1] * ocmp_ref[half] + sg[:, 1:2] * o_sel + sg[:, 2:3] * o_w


def _sample_attention(idx_flat, pt_flat, slopes, cache4, q_rows, idx_rows, e16, gate_rows, ocmp, new16, win_buf,
                      *, past, n_pages, st):
    sb, ng, rows, hd = q_rows.shape
    wb = win_buf.shape[1]
    nkey = e16.shape[1]
    pairg = lambda w: pl.BlockSpec((None, 2, rows, w), lambda b, p, *_: (b, p, 0, 0))
    newc = lambda c: pl.BlockSpec((None, new16.shape[1], PAIR_W), lambda b, p, *_: (b, 0, 2 * c + p))
    kern = functools.partial(_sample_attn_kernel, past=past, n_pages=n_pages, st=st)
    grid_spec = pltpu.PrefetchScalarGridSpec(
        num_scalar_prefetch=2,
        grid=(sb, 2),
        in_specs=[pl.BlockSpec(memory_space=pltpu.SMEM), pl.BlockSpec(memory_space=pl.ANY),
                  pairg(hd), pairg(LANES), pl.BlockSpec(e16.shape, lambda b, p, *_: (0, 0)), pairg(LANES), pairg(hd),
                  newc(2), newc(3), newc(4), newc(5),
                  pl.BlockSpec((None, wb, PAIR_W), lambda b, p, *_: (b, 0, p)),
                  pl.BlockSpec((None, wb, PAIR_W), lambda b, p, *_: (b, 0, 2 + p))],
        out_specs=pairg(hd),
        scratch_shapes=[pltpu.VMEM((2, 2, st, HEAD_DIM, nkey), F32), pltpu.VMEM((2, 2, st, HEAD_DIM, nkey), F32),
                        pltpu.SemaphoreType.DMA((2,))],
    )
    return pl.pallas_call(
        kern,
        grid_spec=grid_spec,
        out_shape=jax.ShapeDtypeStruct((sb, ng, rows, hd), F32),
        compiler_params=_cparams("arbitrary", "arbitrary"),
        name="nsa_sample_attn",
    )(idx_flat, pt_flat, slopes, cache4, q_rows, idx_rows, e16, gate_rows, ocmp, new16, new16, new16, new16,
      win_buf, win_buf)


def _pad_cols(w, mult=LANES):
    n = w.shape[1]
    return jnp.pad(w, ((0, 0), (0, -n % mult)))


def _prep_nsa_in(w_in):
    body = w_in[:, :NSA_Q_W + 6 * NSA_KV_W]
    gates = w_in[:, NSA_Q_W + 6 * NSA_KV_W:]
    per_pair = 2 * NSA_HPG * 3
    blocks = [_pad_cols(gates[:, p * per_pair:(p + 1) * per_pair]) for p in range(2)]
    return jnp.concatenate([body] + blocks, axis=1).astype(BF16)


def _prep_cmp(w_cmp1, w_cmp2, cmp_pos):
    hd = HEAD_DIM
    z = jnp.zeros((2, CMP_STRIDE, hd, hd), F32)
    wa, wb = w_cmp1[:, :CMP_STRIDE], w_cmp1[:, CMP_STRIDE:]
    top = jnp.concatenate([wa, z, wb, z], axis=-1)
    bot = jnp.concatenate([z, wa, z, wb], axis=-1)
    wbd = jnp.concatenate([top, bot], axis=-2).astype(BF16)
    z2 = jnp.zeros((2, hd, hd), F32)
    w2bd = jnp.concatenate([jnp.concatenate([w_cmp2, z2], -1), jnp.concatenate([z2, w_cmp2], -1)], -2).astype(BF16)
    pos = cmp_pos.transpose(1, 0, 2)
    pos = jnp.concatenate([pos, pos], axis=-1)
    rows = jnp.stack([pos[:, :CMP_STRIDE], pos[:, CMP_STRIDE:]], axis=2)
    pos_tiles = jnp.pad(rows, ((0, 0), (0, 0), (0, SUBLANES - 2), (0, 0)))
    return wbd, w2bd, pos_tiles


def _alibi_slopes():
    h = jnp.arange(1, NSA_HEADS + 1, dtype=F32)
    return jnp.exp2(-8.0 * h / NSA_HEADS)


def _prompt_rows_tile(m):
    for tm in (512, 256, 128, 64, 32, 16, 8):
        if m % tm == 0:
            return tm
    raise ValueError(m)


def kernel(x_prompt, x_sample, cache_nsa_kv, cache_nsa_win, state_mlstm_C, state_mlstm_n, state_mlstm_m,
           state_ffn_conv, page_table, norm_g, w_nsa_in, w_nsa_out, w_cmp1, w_cmp2, cmp_pos, w_ml_in,
           b_ml_gate, ml_head_norm, w_ml_out, w_ffn_up, ffn_conv_w, ffn_conv_b, w_ffn_down):
    nb, t, d = x_prompt.shape
    sb, st, _ = x_sample.shape
    mp = nb * t
    tm = _prompt_rows_tile(mp)
    slopes = _alibi_slopes()

    w_in0 = _prep_nsa_in(w_nsa_in[0])
    w_out0 = w_nsa_out[0].astype(BF16)
    wbd, w2bd, pos_tiles = _prep_cmp(w_cmp1[0], w_cmp2[0], cmp_pos[0])
    w_ml = _pad_cols(w_ml_in[0]).astype(BF16)
    w_mlo = w_ml_out[0].astype(BF16)
    w_up = w_ffn_up.astype(BF16)
    w_dn = w_ffn_down.astype(BF16)
    bg_row = _pad_cols(b_ml_gate[0].reshape(1, 2 * ML_HEADS))
    g = norm_g[:, :, None, :]

    xp = x_prompt.reshape(mp, d)
    _, q, kva, kvb, win, gates = _nsa_project(xp, g[0, 0], w_in0, tm)
    kc, vc = _nsa_compress(kva, wbd, w2bd, pos_tiles, nb=nb, t=t)
    o = _nsa_attention5(_slope_feats(), q, gates, kc, vc, kvb, nb=nb, t=t)
    xp = _out_project(o, xp, w_out0, g[0, 1], tm)
    kv_p = kva.reshape(1, nb, t, 4, NSA_KV_HEADS, HEAD_DIM)
    w_keep = min(WINDOW, t)
    win_p = win.reshape(nb, t, 2, NSA_KV_HEADS, HEAD_DIM)[None, :, t - w_keep:]

    halo = SUBLANES
    zstate = jnp.zeros((nb, halo, 2 * D_FF), F32)
    tmf = min(t, 1024)
    conv_p = []

    def ffn_prompt(xp, i):
        xo, ta, tb = _conv_ffn(xp, zstate, g[i, 2], g[i, 3], w_up[i], ffn_conv_w[i], ffn_conv_b[i][None],
                               w_dn[i], tm=tmf, fc=1024, shift=1, tiles_per_seq=t // tmf)
        tail = jnp.concatenate([ta, tb], axis=-1).reshape(nb, t // tmf, halo, 2 * D_FF)
        conv_p.append(tail[:, -1, halo - (CONV_W - 1):])
        return xo

    xp = ffn_prompt(xp, 0)

    lp = math.gcd(t, ML_CHUNK)
    qkv, om, gm = _ml_project(xp, g[1, 0], w_ml, tm)
    zc = jnp.zeros((nb, ML_HEADS, ML_DV, ML_DQK), F32)
    zn = jnp.zeros((nb, ML_HEADS, 1, ML_DQK), F32)
    zm = jnp.zeros((nb, ML_HEADS, 1, 1), F32)
    if lp == ML_DQK:
        ym, c_p, n_p, m_p = _mlstm_stacked(qkv, om, gm, bg_row, ml_head_norm[0][None], zc, zn[:, :, 0], zm[:, :, 0],
                                           nb=nb, l=lp)
        n_p, m_p = n_p[:, :, None, :], m_p[:, :, None, :]
    else:
        ym, c_p, n_p, m_p = _mlstm(qkv, om, gm, bg_row, ml_head_norm[0][None], zc, zn, zm, nb=nb, lp=lp, lv=lp)
    xp = _out_project(ym, xp, w_mlo, g[1, 1], tm)
    xp = ffn_prompt(xp, 1)

    ms = sb * st
    ng, hpg, hd = NSA_KV_HEADS, NSA_HPG, HEAD_DIM
    xs = x_sample.transpose(1, 0, 2).reshape(ms, d)
    qs, _, kva_s, kvb_s, win_s, gates_s = _nsa_project(xs, g[0, 0], w_in0, ms)
    n_phys, page = cache_nsa_kv.shape[1:3]
    n_pages = page_table.shape[1]
    past = n_pages * page
    cache4 = cache_nsa_kv.transpose(0, 1, 3, 4, 5, 2)
    kvc = _page_compress_t(page_table, cache4, wbd, w2bd, pos_tiles)

    rows = st * hpg
    q_rows = qs.reshape(st, sb, ng, hpg, hd).transpose(1, 2, 0, 3, 4).reshape(sb, ng, rows, hd)
    n_sel = -(-(past + st) // SEL_BLK)
    nsp = -(-n_sel // LANES) * LANES
    nsr = -(-n_sel // SUBLANES) * SUBLANES
    ocmp, imp = _sample_cmp(slopes, q_rows, kvc, past=past, nsp=nsp)
    imp_t = imp[:, :, ::hpg, :nsr].reshape(sb * ng * st, nsr).T
    idx = _sample_topk(imp_t, past=past, n_sel=n_sel, st=st)
    idx_bgtk = idx.T.reshape(sb, ng, st, SEL_TOPK)
    idx_rows = _pad_cols(jnp.repeat(idx_bgtk, hpg, axis=2).reshape(sb * ng * rows, SEL_TOPK))
    idx_rows = idx_rows.reshape(sb, ng, rows, LANES).astype(BF16)
    e16 = (jnp.arange(LANES, dtype=jnp.int32)[:, None]
           == jnp.arange(SEL_TOPK * page, dtype=jnp.int32)[None, :] // page).astype(BF16)
    gate_rows = gates_s.reshape(st, sb, 2, LANES)[..., :2 * hpg * 3].reshape(st, sb, ng, hpg, 3)
    gate_rows = _pad_cols(gate_rows.transpose(1, 2, 0, 3, 4).reshape(sb * ng * rows, 3)).reshape(sb, ng, rows, LANES)
    new16 = jnp.pad(kvb_s.reshape(st, sb, -1).transpose(1, 0, 2), ((0, 0), (0, 16 - st), (0, 0)))
    wb = cache_nsa_win.shape[2]
    win_buf = cache_nsa_win[0].reshape(sb, wb, 2 * NSA_KV_W)
    o_s = _sample_attention(idx_bgtk.reshape(-1), page_table.reshape(-1), slopes, cache4, q_rows, idx_rows, e16,
                            gate_rows, ocmp, new16, win_buf, past=past, n_pages=n_pages, st=st)
    o_s = o_s.reshape(sb, ng, st, hpg, hd).transpose(2, 0, 1, 3, 4).reshape(ms, NSA_Q_W).astype(BF16)
    xs = _out_project(o_s, xs, w_out0, g[0, 1], ms)
    kv_s = kva_s.reshape(st, sb, 4, ng, hd).transpose(1, 0, 2, 3, 4)[None]
    win_new = win_s.reshape(st, sb, 2, ng, hd).transpose(1, 0, 2, 3, 4)
    win_s_out = jnp.concatenate([cache_nsa_win[0], win_new], axis=1)[None, :, st:]

    conv_s = []

    def ffn_sample(xs, i):
        state = state_ffn_conv[i].transpose(1, 0, 2).reshape(1, (CONV_W - 1) * sb, 2 * D_FF)
        xo, ta, tb = _conv_ffn(xs, state, g[i, 2], g[i, 3], w_up[i], ffn_conv_w[i], ffn_conv_b[i][None],
                               w_dn[i], tm=ms, fc=512, shift=sb, tiles_per_seq=1)
        tail = jnp.concatenate([ta, tb], axis=-1).reshape(CONV_W - 1, sb, 2 * D_FF)
        conv_s.append(tail.transpose(1, 0, 2))
        return xo

    xs = ffn_sample(xs, 0)

    lps = SUBLANES
    to_seq = lambda a: jnp.pad(a.reshape(st, sb, -1).transpose(1, 0, 2),
                               ((0, 0), (0, lps - st), (0, 0))).reshape(sb * lps, -1)
    qkv_s, om_s, gm_s = _ml_project(xs, g[1, 0], w_ml, ms)
    ym_s, c_s, n_s, m_s = _mlstm(to_seq(qkv_s), to_seq(om_s), to_seq(gm_s), bg_row, ml_head_norm[0][None],
                                 state_mlstm_C[0], state_mlstm_n[0][:, :, None, :],
                                 state_mlstm_m[0][:, :, None, None], nb=sb, lp=lps, lv=math.gcd(st, ML_CHUNK))
    ym_s = ym_s.reshape(sb, lps, ML_V_W)[:, :st].transpose(1, 0, 2).reshape(ms, ML_V_W)
    xs = _out_project(ym_s, xs, w_mlo, g[1, 1], ms)
    xs = ffn_sample(xs, 1)

    return (xp.reshape(nb, t, d), xs.reshape(st, sb, d).transpose(1, 0, 2), kv_p, kv_s, win_p, win_s_out,
            c_p[None], c_s[None], n_p[None, :, :, 0], n_s[None, :, :, 0],
            m_p[None, :, :, 0, 0], m_s[None, :, :, 0, 0], jnp.stack(conv_p), jnp.stack(conv_s))
```

```python
import functools
import math

import jax
import jax.numpy as jnp
from jax import lax
from jax.experimental import pallas as pl
from jax.experimental.pallas import tpu as pltpu

F32 = jnp.float32
BF16 = jnp.bfloat16

LANES = 128
SUBLANES = 8
VMEM_LIMIT_BYTES = 56 * 1024 * 1024

D_MODEL = 1024
NSA_HEADS = 16
NSA_KV_HEADS = 4
NSA_HPG = NSA_HEADS // NSA_KV_HEADS
HEAD_DIM = D_MODEL // NSA_HEADS
CMP_LEN = 32
CMP_STRIDE = 16
SEL_BLK = 64
SEL_TOPK = 16
WINDOW = 512
Q_BLK = 128
FORCE_BONUS = 1e4
NSA_Q_W = NSA_HEADS * HEAD_DIM
NSA_KV_W = NSA_KV_HEADS * HEAD_DIM
ML_HEADS = 8
ML_DQK = D_MODEL // (2 * ML_HEADS)
ML_DV = D_MODEL // ML_HEADS
ML_CHUNK = 64
ML_QK_W = ML_HEADS * ML_DQK
ML_V_W = ML_HEADS * ML_DV
D_FF = 4 * D_MODEL
CONV_W = 3
RMS_EPS = 1e-6
MASK_NEG = -1e30
LOG2E = 1.4426950408889634
ML_SEQ_PER_STEP = 4
FFN_SUB_ROWS = 1024
PAIR_W = 2 * HEAD_DIM


def _cparams(*sem):
    return pltpu.CompilerParams(dimension_semantics=sem, vmem_limit_bytes=VMEM_LIMIT_BYTES)


def _dot(a, b):
    return jnp.dot(a, b, preferred_element_type=F32)


def _dot_nt(a, b):
    return lax.dot_general(a, b, (((1,), (1,)), ((), ())), preferred_element_type=F32)


def _dot_tn(a, b):
    return lax.dot_general(a, b, (((0,), (0,)), ((), ())), preferred_element_type=F32)


def _split3(x):
    hi = x.astype(BF16)
    r1 = x - hi.astype(F32)
    mid = r1.astype(BF16)
    lo = (r1 - mid.astype(F32)).astype(BF16)
    return hi, mid, lo


def _rms(x, g):
    return x * lax.rsqrt(jnp.mean(x * x, axis=-1, keepdims=True) + RMS_EPS) * g


def _gelu_tanh(x):
    return 0.5 * x * (1.0 + jnp.tanh(0.7978845608028654 * (x + 0.044715 * (x * x * x))))


def _full(shape):
    return pl.BlockSpec(shape, lambda *_: (0,) * len(shape))


def _nsa_proj_kernel(x_ref, g_ref, w_ref, q_ref, q2_ref, kva_ref, kvb_ref, win_ref, gate_ref):
    h = _rms(x_ref[...], g_ref[...]).astype(BF16)
    p = _dot(h, w_ref[...])
    kvw = 4 * NSA_KV_W
    q_ref[...] = (p[:, :NSA_Q_W] * HEAD_DIM ** -0.5).astype(BF16)
    q2_ref[...] = (p[:, :NSA_Q_W] * (HEAD_DIM ** -0.5 * LOG2E)).astype(BF16)
    kva_ref[...] = p[:, NSA_Q_W:NSA_Q_W + kvw]
    kvb_ref[...] = p[:, NSA_Q_W:NSA_Q_W + 6 * NSA_KV_W].astype(BF16)
    win_ref[...] = p[:, NSA_Q_W + kvw:NSA_Q_W + 6 * NSA_KV_W]
    gate_ref[...] = p[:, NSA_Q_W + 6 * NSA_KV_W:]


def _nsa_project(x, g, w, tm):
    m = x.shape[0]
    n = w.shape[1]
    row = lambda i: (i, 0)
    return pl.pallas_call(
        _nsa_proj_kernel,
        grid=(m // tm,),
        in_specs=[pl.BlockSpec((tm, D_MODEL), row), _full((1, D_MODEL)), _full((D_MODEL, n))],
        out_specs=[pl.BlockSpec((tm, NSA_Q_W), row), pl.BlockSpec((tm, NSA_Q_W), row),
                   pl.BlockSpec((tm, 4 * NSA_KV_W), row),
                   pl.BlockSpec((tm, 6 * NSA_KV_W), row), pl.BlockSpec((tm, 2 * NSA_KV_W), row),
                   pl.BlockSpec((tm, 2 * LANES), row)],
        out_shape=[jax.ShapeDtypeStruct((m, NSA_Q_W), BF16), jax.ShapeDtypeStruct((m, NSA_Q_W), BF16),
                   jax.ShapeDtypeStruct((m, 4 * NSA_KV_W), F32),
                   jax.ShapeDtypeStruct((m, 6 * NSA_KV_W), BF16), jax.ShapeDtypeStruct((m, 2 * NSA_KV_W), F32),
                   jax.ShapeDtypeStruct((m, 2 * LANES), F32)],
        compiler_params=_cparams("parallel"),
        name="nsa_proj",
    )(x, g, w)


def _ml_proj_kernel(x_ref, g_ref, w_ref, qkv_ref, o_ref, gate_ref):
    h = _rms(x_ref[...], g_ref[...]).astype(BF16)
    p = _dot(h, w_ref[...])
    a = 2 * ML_QK_W + ML_V_W
    qkv_ref[:, :ML_QK_W] = p[:, :ML_QK_W].astype(BF16)
    qkv_ref[:, ML_QK_W:2 * ML_QK_W] = (p[:, ML_QK_W:2 * ML_QK_W] * ML_DQK ** -0.5).astype(BF16)
    qkv_ref[:, 2 * ML_QK_W:] = p[:, 2 * ML_QK_W:a].astype(BF16)
    o_ref[...] = p[:, a:a + ML_V_W]
    gate_ref[...] = p[:, a + ML_V_W:]


def _ml_project(x, g, w, tm):
    m = x.shape[0]
    n = w.shape[1]
    a = 2 * ML_QK_W + ML_V_W
    row = lambda i: (i, 0)
    return pl.pallas_call(
        _ml_proj_kernel,
        grid=(m // tm,),
        in_specs=[pl.BlockSpec((tm, D_MODEL), row), _full((1, D_MODEL)), _full((D_MODEL, n))],
        out_specs=[pl.BlockSpec((tm, a), row), pl.BlockSpec((tm, ML_V_W), row), pl.BlockSpec((tm, LANES), row)],
        out_shape=[jax.ShapeDtypeStruct((m, a), BF16), jax.ShapeDtypeStruct((m, ML_V_W), F32),
                   jax.ShapeDtypeStruct((m, LANES), F32)],
        compiler_params=_cparams("parallel"),
        name="ml_proj",
    )(x, g, w)


def _out_proj_kernel(o_ref, x_ref, w_ref, g_ref, xo_ref):
    y = _dot(o_ref[...], w_ref[...])
    xo_ref[...] = x_ref[...] + _rms(y, g_ref[...])


def _out_project(o, x, w, g, tm):
    m = x.shape[0]
    row = lambda i: (i, 0)
    return pl.pallas_call(
        _out_proj_kernel,
        grid=(m // tm,),
        in_specs=[pl.BlockSpec((tm, o.shape[1]), row), pl.BlockSpec((tm, D_MODEL), row),
                  _full(w.shape), _full((1, D_MODEL))],
        out_specs=pl.BlockSpec((tm, D_MODEL), row),
        out_shape=jax.ShapeDtypeStruct((m, D_MODEL), F32),
        compiler_params=_cparams("parallel"),
        name="out_proj",
    )(o, x, w, g)


def _ffn_kernel(x_ref, sta_ref, stb_ref, g2_ref, g3_ref, wa_ref, wb_ref, cwa_ref, cwb_ref, cba_ref, cbb_ref,
                wd_ref, xo_ref, taila_ref, tailb_ref, h_scr, ua_scr, ub_scr, ca_scr, cb_scr, acc_scr,
                *, tm, halo, shift, tiles_per_seq):
    i = pl.program_id(0)
    c = pl.program_id(1)

    @pl.when(c == 0)
    def _():
        h_scr[...] = _rms(x_ref[...], g2_ref[...]).astype(BF16)
        acc_scr[...] = jnp.zeros_like(acc_scr)

    first = (i % tiles_per_seq) == 0

    @pl.when(first)
    def _():
        ua_scr[0:halo, :] = sta_ref[...]
        ub_scr[0:halo, :] = stb_ref[...]

    @pl.when(jnp.logical_not(first))
    def _():
        ua_scr[0:halo, :] = ca_scr[c]
        ub_scr[0:halo, :] = cb_scr[c]

    nsub = max(1, tm // FFN_SUB_ROWS)
    rs = tm // nsub
    for r in range(nsub):
        h = h_scr[r * rs:(r + 1) * rs, :]
        ua_scr[halo + r * rs:halo + (r + 1) * rs, :] = _dot(h, wa_ref[...])
        ub_scr[halo + r * rs:halo + (r + 1) * rs, :] = _dot(h, wb_ref[...])
    ta = ua_scr[tm:tm + halo, :]
    tb = ub_scr[tm:tm + halo, :]
    ca_scr[c] = ta
    cb_scr[c] = tb
    taila_ref[...] = ta
    tailb_ref[...] = tb

    def conv(u_scr, cw_ref, cb_ref, lo):
        cw = cw_ref[...]
        return (cb_ref[...] + cw[2:3, :] * u_scr[lo:lo + rs, :]
                + cw[1:2, :] * u_scr[lo - shift:lo - shift + rs, :]
                + cw[0:1, :] * u_scr[lo - 2 * shift:lo - 2 * shift + rs, :])

    for r in range(nsub):
        lo = halo + r * rs
        y = _gelu_tanh(conv(ua_scr, cwa_ref, cba_ref, lo)) * conv(ub_scr, cwb_ref, cbb_ref, lo)
        acc_scr[r * rs:(r + 1) * rs, :] += _dot(y.astype(BF16), wd_ref[...])

    @pl.when(c == pl.num_programs(1) - 1)
    def _():
        xo_ref[...] = x_ref[...] + _rms(acc_scr[...], g3_ref[...])


def _conv_ffn(x, state, g2, g3, w_up, conv_w, conv_b, w_down, *, tm, fc, shift, tiles_per_seq):
    m = x.shape[0]
    halo = state.shape[1]
    nfc = D_FF // fc
    n_tiles = m // tm
    row = lambda i, c: (i, 0)
    const = lambda i, c: (0, 0)
    kern = functools.partial(_ffn_kernel, tm=tm, halo=halo, shift=shift, tiles_per_seq=tiles_per_seq)
    return pl.pallas_call(
        kern,
        grid=(n_tiles, nfc),
        in_specs=[
            pl.BlockSpec((tm, D_MODEL), row),
            pl.BlockSpec((None, halo, fc), lambda i, c: (i // tiles_per_seq, 0, c)),
            pl.BlockSpec((None, halo, fc), lambda i, c: (i // tiles_per_seq, 0, nfc + c)),
            pl.BlockSpec((1, D_MODEL), const), pl.BlockSpec((1, D_MODEL), const),
            pl.BlockSpec((D_MODEL, fc), lambda i, c: (0, c)),
            pl.BlockSpec((D_MODEL, fc), lambda i, c: (0, nfc + c)),
            pl.BlockSpec((CONV_W, fc), lambda i, c: (0, c)),
            pl.BlockSpec((CONV_W, fc), lambda i, c: (0, nfc + c)),
            pl.BlockSpec((1, fc), lambda i, c: (0, c)),
            pl.BlockSpec((1, fc), lambda i, c: (0, nfc + c)),
            pl.BlockSpec((fc, D_MODEL), lambda i, c: (c, 0)),
        ],
        out_specs=[pl.BlockSpec((tm, D_MODEL), row),
                   pl.BlockSpec((None, halo, fc), lambda i, c: (i, 0, c)),
                   pl.BlockSpec((None, halo, fc), lambda i, c: (i, 0, c))],
        out_shape=[jax.ShapeDtypeStruct((m, D_MODEL), F32),
                   jax.ShapeDtypeStruct((n_tiles, halo, D_FF), F32),
                   jax.ShapeDtypeStruct((n_tiles, halo, D_FF), F32)],
        scratch_shapes=[pltpu.VMEM((tm, D_MODEL), BF16),
                        pltpu.VMEM((halo + tm, fc), F32), pltpu.VMEM((halo + tm, fc), F32),
                        pltpu.VMEM((nfc, halo, fc), F32), pltpu.VMEM((nfc, halo, fc), F32),
                        pltpu.VMEM((tm, D_MODEL), F32)],
        compiler_params=_cparams("arbitrary", "arbitrary"),
        name="conv_ffn",
    )(x, state, state, g2, g3, w_up, w_up, conv_w, conv_w, conv_b, conv_b, w_down)


def _log_sigmoid(x):
    return jnp.minimum(x, 0.0) - jnp.log1p(jnp.exp(-jnp.abs(x)))


def _exact_nt(sel_bf16, x):
    hi, mid, lo = _split3(x)
    return _dot_nt(sel_bf16, hi) + _dot_nt(sel_bf16, mid) + _dot_nt(sel_bf16, lo)


def _eye(n, m, dtype):
    return (lax.broadcasted_iota(jnp.int32, (n, m), 0) == lax.broadcasted_iota(jnp.int32, (n, m), 1)).astype(dtype)


def _mlstm_kernel(qkv_ref, o_ref, gate_ref, bg_ref, hn_ref, c0_ref, n0_ref, m0_ref,
                  y_ref, cf_ref, nf_ref, mf_ref, ct_scr, n_scr, m_scr, *, lp, lv, bb):
    for bi in range(bb):
        _mlstm_one(qkv_ref.at[bi], o_ref.at[bi], gate_ref.at[bi], bg_ref, hn_ref, c0_ref.at[bi], n0_ref.at[bi],
                   m0_ref.at[bi], y_ref.at[bi], cf_ref.at[bi], nf_ref.at[bi], mf_ref.at[bi],
                   ct_scr.at[bi], n_scr.at[bi], m_scr.at[bi], lp=lp, lv=lv)


def _mlstm_one(qkv_ref, o_ref, gate_ref, bg_ref, hn_ref, c0_ref, n0_ref, m0_ref,
               y_ref, cf_ref, nf_ref, mf_ref, ct_scr, n_scr, m_scr, *, lp, lv):
    ci = pl.program_id(1)
    eye_qk = _eye(ML_DQK, ML_DQK, BF16)

    @pl.when(ci == 0)
    def _():
        for h in range(ML_HEADS):
            ct_scr[h] = _exact_nt(eye_qk, c0_ref[h])
        n_scr[...] = n0_ref[...]
        m_scr[...] = m0_ref[...]

    gp = gate_ref[...] + bg_ref[...]
    ls = _log_sigmoid(gp)
    r_i = lax.broadcasted_iota(jnp.int32, (lp, lp), 0)
    c_i = lax.broadcasted_iota(jnp.int32, (lp, lp), 1)
    causal = c_i <= r_i
    tril = causal.astype(BF16)
    hi, mid, lo = _split3(ls)
    b_all = _dot(tril, hi) + _dot(tril, mid) + _dot(tril, lo)
    sel16 = _eye(2 * ML_HEADS, LANES, BF16)
    gp_t = _exact_nt(sel16, gp)
    b_t = _exact_nt(sel16, b_all)
    row_valid = lax.broadcasted_iota(jnp.int32, (lp, 1), 0) < lv

    for h in range(ML_HEADS):
        q = qkv_ref[:, h * ML_DQK:(h + 1) * ML_DQK]
        k = qkv_ref[:, ML_QK_W + h * ML_DQK:ML_QK_W + (h + 1) * ML_DQK]
        v = qkv_ref[:, 2 * ML_QK_W + h * ML_DV:2 * ML_QK_W + (h + 1) * ML_DV]
        ct = ct_scr[h]
        n_row = n_scr[h]
        m = m_scr[h]
        b_col = b_all[:, ML_HEADS + h:ML_HEADS + h + 1]
        i_col = gp[:, h:h + 1]
        b_row = b_t[ML_HEADS + h:ML_HEADS + h + 1, :]
        i_row = gp_t[h:h + 1, :]
        dmat = jnp.where(causal, b_col - b_row + i_row, MASK_NEG)
        inter = b_col + m
        mt = jnp.maximum(inter, jnp.max(dmat, axis=-1, keepdims=True))
        s = _dot_nt(q, k) * jnp.exp(dmat - mt)
        wi = jnp.exp(inter - mt)
        qf = q.astype(F32)
        num = wi * _dot(q, ct.astype(BF16)) + _dot(s.astype(BF16), v)
        den = wi * jnp.sum(qf * n_row, axis=-1, keepdims=True) + jnp.sum(s, axis=-1, keepdims=True)
        hc = num / jnp.maximum(jnp.abs(den), jnp.exp(-mt))
        m_last = mt[lv - 1:lv, :]
        b_last = b_col[lv - 1:lv, :]
        decay = jnp.exp(b_last + m - m_last)
        ws = jnp.where(row_valid, jnp.exp(b_last - b_col + i_col - m_last), 0.0)
        ct_scr[h] = decay * ct + _dot_tn(k, (ws * v.astype(F32)).astype(BF16))
        n_scr[h] = decay * n_row + jnp.sum(ws * k.astype(F32), axis=0, keepdims=True)
        m_scr[h] = m_last
        hn = hc * lax.rsqrt(jnp.mean(hc * hc, axis=-1, keepdims=True) + RMS_EPS)
        sl = slice(h * ML_DV, (h + 1) * ML_DV)
        y_ref[:, sl] = (hn * hn_ref[:, sl] * jax.nn.sigmoid(o_ref[:, sl])).astype(BF16)

    @pl.when(ci == pl.num_programs(1) - 1)
    def _():
        eye_v = _eye(ML_DV, ML_DV, BF16)
        for h in range(ML_HEADS):
            cf_ref[h] = _exact_nt(eye_v, ct_scr[h])
        nf_ref[...] = n_scr[...]
        mf_ref[...] = m_scr[...]


def _mlstm(qkv, o, gates, b_gate, head_norm, c0, n0, m0, *, nb, lp, lv):
    m = qkv.shape[0]
    tseq = m // nb
    bb = math.gcd(nb, ML_SEQ_PER_STEP)
    seq3 = lambda a: a.reshape(nb, tseq, a.shape[1])
    row = lambda b, c: (b, c, 0)
    st4 = lambda b, c: (b, 0, 0, 0)
    kern = functools.partial(_mlstm_kernel, lp=lp, lv=lv, bb=bb)
    y, cf, nf, mf = pl.pallas_call(
        kern,
        grid=(nb // bb, tseq // lp),
        in_specs=[pl.BlockSpec((bb, lp, qkv.shape[1]), row), pl.BlockSpec((bb, lp, ML_V_W), row),
                  pl.BlockSpec((bb, lp, LANES), row), pl.BlockSpec((1, LANES), lambda b, c: (0, 0)),
                  pl.BlockSpec((1, ML_V_W), lambda b, c: (0, 0)),
                  pl.BlockSpec((bb, ML_HEADS, ML_DV, ML_DQK), st4),
                  pl.BlockSpec((bb, ML_HEADS, 1, ML_DQK), st4),
                  pl.BlockSpec((bb, ML_HEADS, 1, 1), st4)],
        out_specs=[pl.BlockSpec((bb, lp, ML_V_W), row),
                   pl.BlockSpec((bb, ML_HEADS, ML_DV, ML_DQK), st4),
                   pl.BlockSpec((bb, ML_HEADS, 1, ML_DQK), st4),
                   pl.BlockSpec((bb, ML_HEADS, 1, 1), st4)],
        out_shape=[jax.ShapeDtypeStruct((nb, tseq, ML_V_W), BF16),
                   jax.ShapeDtypeStruct((nb, ML_HEADS, ML_DV, ML_DQK), F32),
                   jax.ShapeDtypeStruct((nb, ML_HEADS, 1, ML_DQK), F32),
                   jax.ShapeDtypeStruct((nb, ML_HEADS, 1, 1), F32)],
        scratch_shapes=[pltpu.VMEM((bb, ML_HEADS, ML_DQK, ML_DV), F32),
                        pltpu.VMEM((bb, ML_HEADS, 1, ML_DQK), F32),
                        pltpu.VMEM((bb, ML_HEADS, 1, 1), F32)],
        compiler_params=_cparams("arbitrary", "arbitrary"),
        name="mlstm",
    )(seq3(qkv), seq3(o), seq3(gates), b_gate, head_norm, c0, n0, m0)
    return y.reshape(m, ML_V_W), cf, nf, mf


def _stack_cols(x, cols):
    return jnp.concatenate([x[:, c:c + 1] for c in cols], axis=0)


def _block_const(col, l, row):
    nh = col.shape[0] // l
    return jnp.concatenate([jnp.broadcast_to(col[h * l + row:h * l + row + 1, :], (l, 1)) for h in range(nh)], axis=0)


def _mlstm_stacked_one(qkv_ref, o_ref, gate_ref, bg_ref, hn_ref, c0_ref, n0_ref, m0_ref,
                       y_ref, cf_ref, nf_ref, mf_ref, ctn_scr, m_scr, blockmask, *, l):
    ci = pl.program_id(1)
    nh = ML_HEADS
    ht = nh * l
    dv = ML_DV
    lane = lax.broadcasted_iota(jnp.int32, (ht, LANES), 1)

    @pl.when(ci == 0)
    def _():
        eye = _eye(ML_DQK, ML_DQK, BF16)
        for h in range(nh):
            rows = slice(h * ML_DQK, (h + 1) * ML_DQK)
            ctn_scr[rows, 0:dv] = _exact_nt(eye, c0_ref[h])
        ctn_scr[:, dv:2 * dv] = jnp.where(lane == 0, n0_ref[...], 0.0)
        m_scr[...] = m0_ref[...]

    gp = gate_ref[...] + bg_ref[...]
    ls = _log_sigmoid(gp)
    tril = (lax.broadcasted_iota(jnp.int32, (l, l), 1) <= lax.broadcasted_iota(jnp.int32, (l, l), 0)).astype(BF16)
    hi, mid, lo = _split3(ls)
    b_all = _dot(tril, hi) + _dot(tril, mid) + _dot(tril, lo)
    sel16 = _eye(2 * nh, LANES, BF16)
    dup = lambda x: jnp.concatenate([x, x], axis=0)
    c_t = _exact_nt(sel16, dup(gp))[0:nh] - _exact_nt(sel16, dup(b_all))[nh:2 * nh]
    c_row = jnp.concatenate([jnp.broadcast_to(c_t[h:h + 1, :], (l, 2 * l)) for h in range(nh)], axis=0)
    r_b = _stack_cols(b_all, range(nh, 2 * nh))
    r_i = _stack_cols(gp, range(nh))
    t_row = lax.broadcasted_iota(jnp.int32, (ht, 1), 0) % l
    causal = (lane % l) <= t_row
    dmat = jnp.where(causal, r_b + c_row, MASK_NEG)
    m_col = m_scr[...]
    inter = r_b + m_col
    mt = jnp.maximum(inter, jnp.max(dmat, axis=-1, keepdims=True))

    q = qkv_ref[:, 0:ML_QK_W]
    k = qkv_ref[:, ML_QK_W:2 * ML_QK_W]
    zero = jnp.zeros((), BF16)
    q_bd = jnp.where(blockmask, _tile_rows(q, nh), zero)
    k_bd = jnp.where(blockmask, _tile_rows(k, nh), zero)
    s = _dot_nt(q_bd, dup(k)) * jnp.exp(dmat - mt)
    wi = jnp.exp(inter - mt)
    ones_lane = jnp.where(lane == 0, 1.0, 0.0).astype(BF16)
    v_st = jnp.concatenate([qkv_ref[:, 2 * ML_QK_W + h * dv:2 * ML_QK_W + (h + 1) * dv] for h in range(nh)], axis=0)
    v_aug = jnp.concatenate([v_st, ones_lane], axis=1)
    a = _dot(q_bd, ctn_scr[...].astype(BF16))
    s_bd = jnp.where(blockmask, jnp.concatenate([s.astype(BF16)] * (ML_QK_W // (2 * l)), axis=1), zero)
    bm = _dot(s_bd, v_aug)
    num = wi * a[:, 0:dv] + bm[:, 0:dv]
    den = wi * a[:, dv:dv + 1] + bm[:, dv:dv + 1]
    hc = num / jnp.maximum(jnp.abs(den), jnp.exp(-mt))

    m_last = _block_const(mt, l, l - 1)
    b_last = _block_const(r_b, l, l - 1)
    decay = jnp.exp(b_last + m_col - m_last)
    ws = jnp.exp(b_last - r_b + r_i - m_last)
    wv = (ws * v_aug.astype(F32)).astype(BF16)
    ctn_scr[...] = decay * ctn_scr[...] + _dot_tn(k_bd, wv)
    m_scr[...] = m_last

    hn = hc * lax.rsqrt(jnp.mean(hc * hc, axis=-1, keepdims=True) + RMS_EPS)
    for h in range(nh):
        sl = slice(h * dv, (h + 1) * dv)
        y_ref[:, sl] = (hn[h * l:(h + 1) * l] * hn_ref[:, sl] * jax.nn.sigmoid(o_ref[:, sl])).astype(BF16)

    @pl.when(ci == pl.num_programs(1) - 1)
    def _():
        eye_v = _eye(dv, dv, BF16)
        for h in range(nh):
            rows = slice(h * ML_DQK, (h + 1) * ML_DQK)
            cf_ref[h] = _exact_nt(eye_v, ctn_scr[rows, 0:dv])
        nf_ref[...] = ctn_scr[:, dv:dv + 1]
        mf_ref[...] = m_scr[...]


def _mlstm_stacked_kernel(qkv_ref, o_ref, gate_ref, bg_ref, hn_ref, c0_ref, n0_ref, m0_ref,
                          y_ref, cf_ref, nf_ref, mf_ref, ctn_scr, m_scr, *, l, bb):
    ht = ML_HEADS * l
    blockmask = (lax.broadcasted_iota(jnp.int32, (ht, ML_QK_W), 0) // l
                 == lax.broadcasted_iota(jnp.int32, (ht, ML_QK_W), 1) // ML_DQK)
    for bi in range(bb):
        _mlstm_stacked_one(qkv_ref.at[bi], o_ref.at[bi], gate_ref.at[bi], bg_ref, hn_ref, c0_ref.at[bi],
                           n0_ref.at[bi], m0_ref.at[bi], y_ref.at[bi], cf_ref.at[bi], nf_ref.at[bi],
                           mf_ref.at[bi], ctn_scr.at[bi], m_scr.at[bi], blockmask, l=l)


def _mlstm_stacked(qkv, o, gates, b_gate, head_norm, c0, n0, m0, *, nb, l):
    assert l == ML_DQK and ML_QK_W % (2 * l) == 0
    m = qkv.shape[0]
    tseq = m // nb
    bb = math.gcd(nb, ML_SEQ_PER_STEP)
    ht = ML_HEADS * l
    seq3 = lambda a: a.reshape(nb, tseq, a.shape[1])
    row = lambda b, c: (b, c, 0)
    st4 = lambda b, c: (b, 0, 0, 0)
    st3 = lambda b, c: (b, 0, 0)
    kern = functools.partial(_mlstm_stacked_kernel, l=l, bb=bb)
    y, cf, nf, mf = pl.pallas_call(
        kern,
        grid=(nb // bb, tseq // l),
        in_specs=[pl.BlockSpec((bb, l, qkv.shape[1]), row), pl.BlockSpec((bb, l, ML_V_W), row),
                  pl.BlockSpec((bb, l, LANES), row), pl.BlockSpec((1, LANES), lambda b, c: (0, 0)),
                  pl.BlockSpec((1, ML_V_W), lambda b, c: (0, 0)),
                  pl.BlockSpec((bb, ML_HEADS, ML_DV, ML_DQK), st4),
                  pl.BlockSpec((bb, ht, 1), st3), pl.BlockSpec((bb, ht, 1), st3)],
        out_specs=[pl.BlockSpec((bb, l, ML_V_W), row),
                   pl.BlockSpec((bb, ML_HEADS, ML_DV, ML_DQK), st4),
                   pl.BlockSpec((bb, ht, 1), st3), pl.BlockSpec((bb, ht, 1), st3)],
        out_shape=[jax.ShapeDtypeStruct((nb, tseq, ML_V_W), BF16),
                   jax.ShapeDtypeStruct((nb, ML_HEADS, ML_DV, ML_DQK), F32),
                   jax.ShapeDtypeStruct((nb, ht, 1), F32), jax.ShapeDtypeStruct((nb, ht, 1), F32)],
        scratch_shapes=[pltpu.VMEM((bb, ht, 2 * ML_DV), F32), pltpu.VMEM((bb, ht, 1), F32)],
        compiler_params=_cparams("arbitrary", "arbitrary"),
        name="mlstm",
    )(seq3(qkv), seq3(o), seq3(gates), b_gate, head_norm, c0,
      n0.reshape(nb, ht, 1), jnp.repeat(m0, l, axis=1))
    return (y.reshape(m, ML_V_W), cf, nf.reshape(nb, ML_HEADS, ML_DQK), mf.reshape(nb, ML_HEADS, l)[:, :, :1])


def _cmp_parts(load_rows, wbd_ref, c, p):
    acc = None
    for l in range(CMP_STRIDE):
        part = _dot(load_rows(l, c, p).astype(BF16), wbd_ref[c, l])
        acc = part if acc is None else acc + part
    return acc


def _cmp_pos_bias(pos_ref, wbd_ref, c):
    acc = None
    for l in range(CMP_STRIDE):
        part = _dot(pos_ref[c, l].astype(BF16), wbd_ref[c, l])
        acc = part if acc is None else acc + part
    return acc[0:1, :PAIR_W] + acc[1:2, PAIR_W:]


def _cmp_finish(parts, posb, w2_ref, c):
    n = parts.shape[0]
    pre = parts[:, :PAIR_W] + pltpu.roll(parts[:, PAIR_W:], n - 1, 0) + posb
    return _dot(_gelu_tanh(pre).astype(BF16), w2_ref[c])


def _nsa_compress_kernel(x00_ref, x01_ref, x10_ref, x11_ref, wbd_ref, w2_ref, pos_ref, kc_ref, vc_ref, *, nchunk):
    x_refs = ((x00_ref, x01_ref), (x10_ref, x11_ref))

    def load_rows(l, c, p):
        return x_refs[c][p][pl.ds(l, nchunk, stride=CMP_STRIDE), :]

    for c, out_ref in ((0, kc_ref), (1, vc_ref)):
        posb = _cmp_pos_bias(pos_ref, wbd_ref, c)
        for p in range(2):
            out_ref[p] = _cmp_finish(_cmp_parts(load_rows, wbd_ref, c, p), posb, w2_ref, c).astype(BF16)


def _nsa_compress(kva, wbd, w2bd, pos, *, nb, t):
    nchunk = t // CMP_STRIDE
    kern = functools.partial(_nsa_compress_kernel, nchunk=nchunk)
    out = jax.ShapeDtypeStruct((nb, 2, nchunk, PAIR_W), BF16)
    ospec = pl.BlockSpec((None, 2, nchunk, PAIR_W), lambda b: (b, 0, 0, 0))
    return pl.pallas_call(
        kern,
        grid=(nb,),
        in_specs=[pl.BlockSpec((t, PAIR_W), functools.partial(lambda j, b: (b, j), j)) for j in range(4)]
        + [_full(wbd.shape), _full(w2bd.shape), _full(pos.shape)],
        out_specs=[ospec, ospec],
        out_shape=[out, out],
        compiler_params=_cparams("parallel"),
        name="nsa_compress",
    )(kva, kva, kva, kva, wbd, w2bd, pos)


KEY_CHUNK = 256
N_FEAT = 6
AUG_W = 2 * LANES
WIN_JOBS = 3
WIN_PAD = WIN_JOBS * KEY_CHUNK - Q_BLK
PAD_FEAT = N_FEAT


def _rank_topk_t(score, ns, qb):
    blocks = [score[SUBLANES * r:SUBLANES * (r + 1)] for r in range(ns // SUBLANES)]
    cnts = [jnp.zeros((SUBLANES, qb), jnp.int32) for _ in blocks]
    sub = lax.broadcasted_iota(jnp.int32, (SUBLANES, qb), 0)
    for m in range(ns):
        sm = jnp.broadcast_to(score[m:m + 1, :], (SUBLANES, qb))
        for r, blk in enumerate(blocks):
            lo = SUBLANES * r
            if lo > m:
                ahead = (sm >= blk).astype(jnp.int32)
            elif lo + SUBLANES - 1 <= m:
                ahead = (sm > blk).astype(jnp.int32)
            else:
                ahead = jnp.where(sub > m - lo, (sm >= blk).astype(jnp.int32), (sm > blk).astype(jnp.int32))
            cnts[r] = cnts[r] + ahead
    return jnp.concatenate(cnts, axis=0) < min(SEL_TOPK, ns)


def _key_feats(kpos_col):
    r = kpos_col.shape[0]
    lane = lax.broadcasted_iota(jnp.int32, (r, LANES), 1)
    a = lax.shift_right_logical(kpos_col, 6).astype(F32)
    b = (kpos_col & (SEL_BLK - 1)).astype(F32)
    return jnp.where(lane < 3, a, jnp.where(lane < N_FEAT, b, 0.0))


def _tile_rows(x, n):
    return jnp.concatenate([x] * n, axis=0)


def _slope_feats():
    s = _alibi_slopes() * LOG2E
    hi = s.astype(BF16).astype(F32)
    mid = (s - hi).astype(BF16).astype(F32)
    lo = (s - hi - mid).astype(BF16).astype(F32)
    z = jnp.zeros_like(s)
    return jnp.stack([SEL_BLK * hi, SEL_BLK * mid, SEL_BLK * lo, hi, mid, lo, z, z], axis=1)


def _nsa_attn_kernel(sf_ref, q_ref, gate_ref, kc_ref, vc_ref, ksel_ref, vsel_ref, kwin_ref, vwin_ref, o_ref,
                      k_scr, v_scr, kc_scr, qa_scr, qc_scr, s_scr, mx_scr, m_scr, acc_scr, *, t):
    pr = pl.program_id(1)
    nblk = pl.program_id(2)
    qb = Q_BLK
    s0 = nblk * qb
    ncp = t // CMP_STRIDE
    ns = t // SEL_BLK
    hd = HEAD_DIM
    w = KEY_CHUNK
    hpg = NSA_HPG
    rows_all = hpg * qb
    wrow0 = t

    @pl.when(nblk == 0)
    def _():
        lane = lax.broadcasted_iota(jnp.int32, (t, LANES), 1)
        row = lax.broadcasted_iota(jnp.int32, (t, 1), 0)
        feats = _key_feats(row).astype(BF16)
        onehot = jnp.where(lane - hd == lax.shift_right_logical(row, 6), 1.0, 0.0)
        lane_p = lax.broadcasted_iota(jnp.int32, (WIN_PAD, LANES), 1)
        lane_c = lax.broadcasted_iota(jnp.int32, (ncp, LANES), 1)
        cend = CMP_STRIDE * lax.broadcasted_iota(jnp.int32, (ncp, 1), 0) + (CMP_LEN - 1)
        ks = ksel_ref[...].astype(F32)
        vs = vsel_ref[...].astype(F32)
        kwn = kwin_ref[...].astype(F32)
        vwn = vwin_ref[...].astype(F32)
        kcv = kc_ref[...].astype(F32)
        for half in range(2):
            low = (lambda x: x) if half == 0 else (lambda x: pltpu.roll(x, hd, 1))
            k_scr[half, 0:t, 0:LANES] = jnp.where(lane < hd, low(ks), onehot).astype(BF16)
            k_scr[half, 0:t, LANES:AUG_W] = feats
            v_scr[half, 0:t] = jnp.where(lane < hd, low(vs), 1.0).astype(BF16)
            k_scr[half, wrow0:wrow0 + WIN_PAD, 0:LANES] = jnp.zeros((WIN_PAD, LANES), BF16)
            k_scr[half, wrow0:wrow0 + WIN_PAD, LANES:AUG_W] = jnp.where(lane_p == PAD_FEAT, 1.0, 0.0).astype(BF16)
            v_scr[half, wrow0:wrow0 + WIN_PAD] = jnp.ones((WIN_PAD, LANES), BF16)
            wr = wrow0 + WIN_PAD
            k_scr[half, wr:wr + t, 0:LANES] = jnp.where(lane < hd, low(kwn), 0.0).astype(BF16)
            k_scr[half, wr:wr + t, LANES:AUG_W] = feats
            v_scr[half, wr:wr + t] = jnp.where(lane < hd, low(vwn), 1.0).astype(BF16)
            kc_scr[half, :, 0:LANES] = jnp.where(lane_c < hd, low(kcv), 0.0).astype(BF16)
            kc_scr[half, :, LANES:AUG_W] = _key_feats(cend).astype(BF16)

    sg = jax.nn.sigmoid(gate_ref[...])
    lane = lax.broadcasted_iota(jnp.int32, (qb, LANES), 1)
    lane1 = lax.broadcasted_iota(jnp.int32, (1, LANES), 1)
    t_loc = lax.broadcasted_iota(jnp.int32, (qb, 1), 0)
    tq_col = s0 + t_loc

    for half in range(2):
        for hh in range(hpg):
            u = half * hpg + hh
            qcol = q_ref[:, (u // 2) * LANES:(u // 2 + 1) * LANES].astype(F32)
            qlow = jnp.where(lane < hd, qcol if u % 2 == 0 else pltpu.roll(qcol, hd, 1), 0.0)
            feat = jnp.where(lane1 == PAD_FEAT, MASK_NEG, 0.0)
            for j in range(N_FEAT):
                feat = jnp.where(lane1 == j, sf_ref[(pr * 2 + half) * hpg + hh, j], feat)
            feat = jnp.broadcast_to(feat, (qb, LANES)).astype(BF16)
            rows = slice(hh * qb, (hh + 1) * qb)
            qc_scr[half, rows, 0:LANES] = qlow.astype(BF16)
            qc_scr[half, rows, LANES:AUG_W] = feat
            qa_scr[half, rows, LANES:AUG_W] = feat

    cend_row = CMP_STRIDE * lax.broadcasted_iota(jnp.int32, (1, ncp), 1) + (CMP_LEN - 1)
    add_c = _tile_rows(jnp.where(tq_col >= cend_row, 0.0, MASK_NEG), hpg)
    jj = lax.broadcasted_iota(jnp.int32, (ns, ncp), 1) * CMP_STRIDE
    nn = lax.broadcasted_iota(jnp.int32, (ns, ncp), 0) * SEL_BLK
    ov_t = ((jj < nn + SEL_BLK) & (jj + CMP_LEN > nn)).astype(BF16)
    bid = lax.broadcasted_iota(jnp.int32, (ns, qb), 0)
    tq_row = s0 + lax.broadcasted_iota(jnp.int32, (ns, qb), 1)
    cur = tq_row // SEL_BLK
    valid_b = bid * SEL_BLK <= tq_row
    bonus = jnp.where((bid == 0) | (bid == cur) | (bid == cur - 1), FORCE_BONUS, 0.0)
    o_cmp = []
    for half in range(2):
        s = _dot_nt(qc_scr[half], kc_scr[half]) + add_c
        m = jnp.max(s, axis=-1, keepdims=True)
        e = jnp.where(s > 0.5 * MASK_NEG, jnp.exp2(s - m), 0.0)
        pc = e * (1.0 / jnp.maximum(jnp.sum(e, axis=-1, keepdims=True), 1e-30))
        o_cmp.append(_dot(pc.astype(BF16), vc_ref[...])[:, half * hd:(half + 1) * hd])
        pc_sum = pc[0:qb]
        for hh in range(1, hpg):
            pc_sum = pc_sum + pc[hh * qb:(hh + 1) * qb]
        hi = pc_sum.astype(BF16)
        lo = (pc_sum - hi.astype(F32)).astype(BF16)
        imp_t = _dot_nt(ov_t, hi) + _dot_nt(ov_t, lo)
        sel_t = _rank_topk_t(jnp.where(valid_b, imp_t + bonus, MASK_NEG), ns, qb)
        pieces = [jnp.zeros((hd, qb), F32), jnp.where(sel_t, 0.0, MASK_NEG)]
        if LANES - hd - ns > 0:
            pieces.append(jnp.zeros((LANES - hd - ns, qb), F32))
        mneg = jnp.concatenate(pieces, axis=0).T
        for hh in range(hpg):
            rows = slice(hh * qb, (hh + 1) * qb)
            qa_scr[half, rows, 0:LANES] = (qc_scr[half, rows, 0:LANES].astype(F32) + mneg).astype(BF16)

    nfull = nblk // (w // qb)
    npairs = nfull // 2
    c_loc = lax.broadcasted_iota(jnp.int32, (1, qb), 1)
    add_w0 = _tile_rows(jnp.concatenate([jnp.full((qb, qb), MASK_NEG, F32),
                                         jnp.where(c_loc > t_loc, 0.0, MASK_NEG)], axis=1), hpg)
    add_w2 = _tile_rows(jnp.concatenate([jnp.zeros((qb, qb), F32),
                                         jnp.where(c_loc <= t_loc, 0.0, MASK_NEG)], axis=1), hpg)
    r_tail = pl.multiple_of(nfull * w, w)
    kpos_tail = r_tail + lax.broadcasted_iota(jnp.int32, (1, w), 1)
    add_tail = _tile_rows(jnp.where(kpos_tail > tq_col, MASK_NEG, 0.0), hpg)
    r_odd = pl.multiple_of(jnp.maximum(nfull - 1, 0) * w, w)
    add_odd = jnp.where(nfull % 2 == 1, 0.0, MASK_NEG)
    w_row = lambda jw: pl.multiple_of(wrow0 + s0 + jw * w, qb)
    masked_jobs = ((qc_scr, w_row(0), add_w0, 1), (qc_scr, w_row(1), None, 1), (qc_scr, w_row(2), add_w2, 1),
                   (qa_scr, r_tail, add_tail, 0), (qa_scr, r_odd, add_odd, 0))

    def loop_row(j):
        return pl.multiple_of(jnp.where(j < 2 * npairs, j * w, wrow0), w)

    def scores(q_scr, r0, add, slot):
        for half in range(2):
            s = _dot_nt(q_scr[half], k_scr[half, pl.ds(r0, w), :])
            if add is not None:
                s = s + add
            s_scr[slot, half] = s
            mx = jnp.max(jnp.maximum(s[:, 0:LANES], s[:, LANES:w]), axis=-1, keepdims=True)
            mx_scr[slot, half] = jnp.broadcast_to(mx, (rows_all, LANES))

    def consume(r0, st, slot):
        for half in range(2):
            m_old = m_scr[st, half]
            m_new = jnp.maximum(m_old, mx_scr[slot, half])
            alpha = jnp.exp2(m_old - m_new)
            p = jnp.exp2(s_scr[slot, half] - jnp.concatenate([m_new, m_new], axis=1))
            acc_scr[st, half] = alpha * acc_scr[st, half] + _dot(p.astype(BF16), v_scr[half, pl.ds(r0, w), :])
            m_scr[st, half] = m_new

    m_scr[...] = jnp.full(m_scr.shape, MASK_NEG, F32)
    acc_scr[...] = jnp.zeros(acc_scr.shape, F32)
    scores(*masked_jobs[0][:3], 0)
    for i, (_, r0, _, st) in enumerate(masked_jobs):
        if i + 1 < len(masked_jobs):
            scores(*masked_jobs[i + 1][:3], (i + 1) % 2)
        else:
            scores(qa_scr, loop_row(0), None, (i + 1) % 2)
        consume(r0, st, i % 2)
    first = len(masked_jobs) % 2

    def step(i, carry):
        scores(qa_scr, loop_row(2 * i + 1), None, 1 - first)
        consume(loop_row(2 * i), 0, first)
        scores(qa_scr, loop_row(2 * i + 2), None, first)
        consume(loop_row(2 * i + 1), 0, 1 - first)
        return carry

    lax.fori_loop(0, npairs, step, 0)

    for half in range(2):
        outs = []
        for st in range(2):
            a = acc_scr[st, half]
            outs.append(a[:, 0:hd] * (1.0 / a[:, hd:hd + 1]))
        for hh in range(hpg):
            u = half * hpg + hh
            gi = u * 3
            rows = slice(hh * qb, (hh + 1) * qb)
            out = (sg[:, gi:gi + 1] * o_cmp[half][rows] + sg[:, gi + 1:gi + 2] * outs[0][rows]
                   + sg[:, gi + 2:gi + 3] * outs[1][rows])
            o_ref[:, u * hd:(u + 1) * hd] = out.astype(BF16)


def _nsa_attention(slope_feats, q, gates, kc, vc, kvb, *, nb, t):
    assert WIN_JOBS * KEY_CHUNK >= WINDOW + Q_BLK and t % KEY_CHUNK == 0 and t // SEL_BLK <= LANES - HEAD_DIM
    nq = t // Q_BLK
    ncp = t // CMP_STRIDE
    pw = 2 * NSA_HPG * HEAD_DIM
    rows_all = NSA_HPG * Q_BLK
    krows = 2 * t + WIN_PAD
    kvcol = lambda c: (lambda b, p, n: (b, 2 * c + p))
    kern = functools.partial(_nsa_attn_kernel, t=t)
    return pl.pallas_call(
        kern,
        grid=(nb, 2, nq),
        in_specs=[pl.BlockSpec(memory_space=pltpu.SMEM),
                  pl.BlockSpec((Q_BLK, pw), lambda b, p, n: (b * nq + n, p)),
                  pl.BlockSpec((Q_BLK, LANES), lambda b, p, n: (b * nq + n, p)),
                  pl.BlockSpec((None, None, ncp, PAIR_W), lambda b, p, n: (b, p, 0, 0)),
                  pl.BlockSpec((None, None, ncp, PAIR_W), lambda b, p, n: (b, p, 0, 0)),
                  pl.BlockSpec((t, PAIR_W), kvcol(2)), pl.BlockSpec((t, PAIR_W), kvcol(3)),
                  pl.BlockSpec((t, PAIR_W), kvcol(4)), pl.BlockSpec((t, PAIR_W), kvcol(5))],
        out_specs=pl.BlockSpec((Q_BLK, pw), lambda b, p, n: (b * nq + n, p)),
        out_shape=jax.ShapeDtypeStruct((nb * t, NSA_Q_W), BF16),
        scratch_shapes=[pltpu.VMEM((2, krows, AUG_W), BF16), pltpu.VMEM((2, krows, LANES), BF16),
                        pltpu.VMEM((2, ncp, AUG_W), BF16),
                        pltpu.VMEM((2, rows_all, AUG_W), BF16), pltpu.VMEM((2, rows_all, AUG_W), BF16),
                        pltpu.VMEM((2, 2, rows_all, KEY_CHUNK), F32), pltpu.VMEM((2, 2, rows_all, LANES), F32),
                        pltpu.VMEM((2, 2, rows_all, LANES), F32), pltpu.VMEM((2, 2, rows_all, LANES), F32)],
        compiler_params=_cparams("parallel", "parallel", "arbitrary"),
        name="nsa_attn",
    )(slope_feats, q, gates, kc, vc, kvb, kvb, kvb, kvb)


PAGES_PER_STEP = 32


def _page_compress_t_kernel(pt_ref, cache_ref, wbd_ref, w2_ref, pos_ref, kvc_ref,
                            buf, xa0, xa1, xa2, xa3, xb0, xb1, xb2, xb3, parts_scr, sem, *, page):
    b = pl.program_id(0)
    hf = pl.program_id(1)
    nhf = pl.num_programs(1)
    step = b * nhf + hf
    nstep = pl.num_programs(0) * nhf
    xsets = ((xa0, xa1, xa2, xa3), (xb0, xb1, xb2, xb3))
    nch = PAGES_PER_STEP * page // CMP_STRIDE
    row0 = pl.multiple_of(hf * nch, nch)

    def copies(s, slot):
        out = []
        for i in range(PAGES_PER_STEP):
            pg = pt_ref[s * PAGES_PER_STEP + i]
            for c in range(2):
                out.append(pltpu.make_async_copy(cache_ref.at[0, pg, c], buf.at[slot, c, i], sem.at[slot]))
        return out

    def transpose_pages(slot):
        for i in range(PAGES_PER_STEP):
            for c in range(2):
                tile = buf[slot, c, i].reshape(NSA_KV_HEADS * HEAD_DIM, page)
                for p in range(2):
                    xsets[slot][c * 2 + p][i * page:(i + 1) * page, :] = tile[p * PAIR_W:(p + 1) * PAIR_W].T

    def mlp(slot):
        def load_rows(l, c, p):
            return xsets[slot][c * 2 + p][pl.ds(l, nch, stride=CMP_STRIDE), :]
        for c in range(2):
            for p in range(2):
                parts_scr[c * 2 + p, pl.ds(row0, nch), :] = _cmp_parts(load_rows, wbd_ref, c, p)

    @pl.when(step == 0)
    def _():
        for cp in copies(0, 0):
            cp.start()
        for cp in copies(0, 0):
            cp.wait()
        transpose_pages(0)

        @pl.when(nstep > 1)
        def _():
            for cp in copies(1, 1):
                cp.start()

    for par in range(2):
        @pl.when(step % 2 == par)
        def _():
            @pl.when(step + 1 < nstep)
            def _():
                for cp in copies(step + 1, 1 - par):
                    cp.wait()

            @pl.when(step + 2 < nstep)
            def _():
                for cp in copies(step + 2, par):
                    cp.start()

            transpose_pages(1 - par)
            mlp(par)

    @pl.when(hf == nhf - 1)
    def _():
        for c in range(2):
            posb = _cmp_pos_bias(pos_ref, wbd_ref, c)
            for p in range(2):
                kvc_ref[c, p] = _cmp_finish(parts_scr[c * 2 + p], posb, w2_ref, c).astype(BF16)


def _page_compress_t(page_table, cache_t, wbd, w2bd, pos):
    sb, n_pages = page_table.shape
    page = cache_t.shape[-1]
    ncp = n_pages * page // CMP_STRIDE
    rows = PAGES_PER_STEP * page
    kern = functools.partial(_page_compress_t_kernel, page=page)
    grid_spec = pltpu.PrefetchScalarGridSpec(
        num_scalar_prefetch=1,
        grid=(sb, n_pages // PAGES_PER_STEP),
        in_specs=[pl.BlockSpec(memory_space=pl.ANY),
                  pl.BlockSpec(wbd.shape, lambda b, h, pt: (0,) * wbd.ndim),
                  pl.BlockSpec(w2bd.shape, lambda b, h, pt: (0,) * w2bd.ndim),
                  pl.BlockSpec(pos.shape, lambda b, h, pt: (0,) * pos.ndim)],
        out_specs=pl.BlockSpec((None, 2, 2, ncp, PAIR_W), lambda b, h, pt: (b, 0, 0, 0, 0)),
        scratch_shapes=[pltpu.VMEM((2, 2, PAGES_PER_STEP, NSA_KV_HEADS, HEAD_DIM, page), F32)]
        + [pltpu.VMEM((rows, PAIR_W), F32)] * 8
        + [pltpu.VMEM((4, ncp, 2 * PAIR_W), F32), pltpu.SemaphoreType.DMA((2,))],
    )
    return pl.pallas_call(
        kern,
        grid_spec=grid_spec,
        out_shape=jax.ShapeDtypeStruct((sb, 2, 2, ncp, PAIR_W), BF16),
        compiler_params=_cparams("arbitrary", "arbitrary"),
        name="nsa_page_compress",
    )(page_table.reshape(-1), cache_t, wbd, w2bd, pos)


def _row_slopes(slope_ref, g, rows):
    hh = lax.broadcasted_iota(jnp.int32, (rows, 1), 0) % NSA_HPG
    col = jnp.zeros((rows, 1), F32)
    for h in range(NSA_HPG):
        col = jnp.where(hh == h, slope_ref[g * NSA_HPG + h], col)
    return col


def _sample_cmp_kernel(slope_ref, q_ref, kvc_ref, ocmp_ref, imp_ref, *, past, nsp):
    rows = q_ref.shape[1]
    ncp = kvc_ref.shape[2]
    t_col = lax.broadcasted_iota(jnp.int32, (rows, 1), 0) // NSA_HPG
    jrow = lax.broadcasted_iota(jnp.int32, (1, ncp), 1)
    dist = (past + t_col) - (CMP_STRIDE * jrow + (CMP_LEN - 1))
    mask = (dist >= 0) & (jrow < ncp - 1)
    distf = dist.astype(F32)
    ri = lax.broadcasted_iota(jnp.int32, (rows, rows), 0) // NSA_HPG
    ci = lax.broadcasted_iota(jnp.int32, (rows, rows), 1) // NSA_HPG
    same_t = (ri == ci).astype(BF16)
    jj = lax.broadcasted_iota(jnp.int32, (ncp, nsp), 0) * CMP_STRIDE
    nn = lax.broadcasted_iota(jnp.int32, (ncp, nsp), 1) * SEL_BLK
    ov = ((jj < nn + SEL_BLK) & (jj + CMP_LEN > nn)).astype(BF16)
    for g in range(NSA_KV_HEADS):
        pr, half = divmod(g, 2)
        lanes = slice(half * HEAD_DIM, (half + 1) * HEAD_DIM)
        s_c = _dot_nt(q_ref[g], kvc_ref[0, pr, :, lanes])
        s = jnp.where(mask, s_c - _row_slopes(slope_ref, g, rows) * distf, MASK_NEG)
        e = jnp.where(mask, jnp.exp(s - jnp.max(s, axis=-1, keepdims=True)), 0.0)
        pc = e / jnp.maximum(jnp.sum(e, axis=-1, keepdims=True), 1e-30)
        ocmp_ref[g] = _dot(pc.astype(BF16), kvc_ref[1, pr])[:, lanes]
        hi = pc.astype(BF16)
        lo = (pc - hi.astype(F32)).astype(BF16)
        pcs = _dot(same_t, hi) + _dot(same_t, lo)
        hi = pcs.astype(BF16)
        lo = (pcs - hi.astype(F32)).astype(BF16)
        imp_ref[g] = _dot(hi, ov) + _dot(lo, ov)


def _sample_cmp(slopes, q_rows, kvc, *, past, nsp):
    sb, ng, rows, hd = q_rows.shape
    ncp = kvc.shape[3]
    kern = functools.partial(_sample_cmp_kernel, past=past, nsp=nsp)
    return pl.pallas_call(
        kern,
        grid=(sb,),
        in_specs=[pl.BlockSpec(memory_space=pltpu.SMEM),
                  pl.BlockSpec((None, ng, rows, hd), lambda b: (b, 0, 0, 0)),
                  pl.BlockSpec((None, 2, 2, ncp, PAIR_W), lambda b: (b, 0, 0, 0, 0))],
        out_specs=[pl.BlockSpec((None, ng, rows, hd), lambda b: (b, 0, 0, 0)),
                   pl.BlockSpec((None, ng, rows, nsp), lambda b: (b, 0, 0, 0))],
        out_shape=[jax.ShapeDtypeStruct((sb, ng, rows, hd), F32),
                   jax.ShapeDtypeStruct((sb, ng, rows, nsp), F32)],
        compiler_params=_cparams("parallel"),
        name="nsa_sample_cmp",
    )(slopes, q_rows, kvc)


def _sample_topk_kernel(imp_ref, idx_ref, score_scr, *, past, n_sel, st):
    nsr, ncol = imp_ref.shape
    bid = lax.broadcasted_iota(jnp.int32, (nsr, ncol), 0)
    tq = past + lax.broadcasted_iota(jnp.int32, (nsr, ncol), 1) % st
    cur = tq // SEL_BLK
    valid = (bid * SEL_BLK <= tq) & (bid < n_sel)
    forced = (bid == 0) | (bid == cur) | (bid == cur - 1)
    score = jnp.where(valid, imp_ref[...] + jnp.where(forced, FORCE_BONUS, 0.0), MASK_NEG)
    score_scr[...] = score

    def body(m, cnt):
        sm = score_scr[pl.ds(m, 1), :]
        ahead = (sm > score) | ((sm == score) & (m < bid))
        return cnt + ahead.astype(jnp.int32)

    cnt = lax.fori_loop(0, n_sel, body, jnp.zeros((nsr, ncol), jnp.int32))
    sel = (cnt < SEL_TOPK) & (bid < n_sel)
    tril = (lax.broadcasted_iota(jnp.int32, (nsr, nsr), 1) <= lax.broadcasted_iota(jnp.int32, (nsr, nsr), 0))
    prefix = _dot(tril.astype(BF16), sel.astype(BF16))
    for k in range(SEL_TOPK):
        hit = sel & (prefix == float(k + 1))
        idx_ref[k:k + 1, :] = jnp.sum(jnp.where(hit, bid, 0), axis=0, keepdims=True)


def _sample_topk(imp_t, *, past, n_sel, st):
    nsr, ncol = imp_t.shape
    kern = functools.partial(_sample_topk_kernel, past=past, n_sel=n_sel, st=st)
    return pl.pallas_call(
        kern,
        grid=(1,),
        in_specs=[_full(imp_t.shape)],
        out_specs=_full((SEL_TOPK, ncol)),
        out_shape=jax.ShapeDtypeStruct((SEL_TOPK, ncol), jnp.int32),
        scratch_shapes=[pltpu.VMEM((nsr, ncol), F32)],
        compiler_params=_cparams("arbitrary"),
        name="nsa_sample_topk",
    )(imp_t)


def _joint_softmax_pv(s_a, v_a_fn, s_b, v_b):
    m = jnp.maximum(jnp.max(s_a, axis=-1, keepdims=True), jnp.max(s_b, axis=-1, keepdims=True))
    p_a = jnp.where(s_a > 0.5 * MASK_NEG, jnp.exp(s_a - m), 0.0)
    p_b = jnp.where(s_b > 0.5 * MASK_NEG, jnp.exp(s_b - m), 0.0)
    l = jnp.sum(p_a, axis=-1, keepdims=True) + jnp.sum(p_b, axis=-1, keepdims=True)
    acc = v_a_fn(p_a.astype(BF16)) + _dot(p_b.astype(BF16), v_b)
    return acc / jnp.maximum(l, 1e-30)


def _sample_attn_kernel(idx_ref, pt_ref, slope_ref, cache_ref, q_ref, idxv_ref, e16_ref, gate_ref, ocmp_ref,
                        ksn_ref, vsn_ref, kwn_ref, vwn_ref, kwin_ref, vwin_ref, o_ref, kvbuf, sem,
                        *, past, n_pages, st):
    b = pl.program_id(0)
    pr = pl.program_id(1)
    rows = q_ref.shape[1]
    page = kvbuf.shape[-1] // SEL_TOPK
    nkey = SEL_TOPK * page
    last_blk = past // SEL_BLK - 1
    npr = pl.num_programs(1)
    step = b * npr + pr
    nstep = pl.num_programs(0) * npr

    def copies(s, slot):
        sb_, spr = s // npr, s % npr
        out = []
        for half in range(2):
            g = spr * 2 + half
            for t in range(st):
                for k in range(SEL_TOPK):
                    blk = jnp.minimum(idx_ref[((sb_ * NSA_KV_HEADS + g) * st + t) * SEL_TOPK + k], last_blk)
                    pg = pt_ref[sb_ * n_pages + lax.shift_right_logical(blk, 1)]
                    dst = pl.ds(k * page, page)
                    out.append(pltpu.make_async_copy(cache_ref.at[0, pg, pl.ds(2, 2), g],
                                                     kvbuf.at[slot, half, t, :, :, dst], sem.at[slot]))
        return out

    @pl.when(step == 0)
    def _():
        for cp in copies(0, 0):
            cp.start()

    slot = step % 2

    @pl.when(step + 1 < nstep)
    def _():
        for cp in copies(step + 1, 1 - slot):
            cp.start()

    for cp in copies(step, slot):
        cp.wait()

    t_col = lax.broadcasted_iota(jnp.int32, (rows, 1), 0) // NSA_HPG
    tqf = (past + t_col).astype(F32)
    col = lax.broadcasted_iota(jnp.int32, (1, nkey), 1)
    off = (col % SEL_BLK).astype(F32)
    col_half = ((col % page) // SEL_BLK).astype(F32)
    tp = lax.broadcasted_iota(jnp.int32, (1, ksn_ref.shape[0]), 1)
    dn = t_col - tp
    valid_n = (dn >= 0) & (tp < st)
    dnf = dn.astype(F32)
    wb = kwin_ref.shape[0]
    dist_w = (wb + t_col) - lax.broadcasted_iota(jnp.int32, (1, wb), 1)
    valid_w = (dist_w >= 0) & (dist_w < WINDOW)

    for half in range(2):
        g = pr * 2 + half
        lanes = slice(half * HEAD_DIM, (half + 1) * HEAD_DIM)
        q = q_ref[half]
        slope = _row_slopes(slope_ref, g, rows)

        blk = _dot(idxv_ref[half], e16_ref[...])
        is_new = blk > (last_blk + 0.5)
        dist = tqf - (blk * SEL_BLK + off)
        in_half = (blk - 2.0 * jnp.floor(0.5 * blk)) == col_half
        valid = jnp.logical_not(is_new) & in_half & (dist >= 0)
        s_sel = jnp.full((rows, nkey), MASK_NEG, F32)
        for t in range(st):
            s_t = _dot(q, kvbuf[slot, half, t, 0].astype(BF16))
            s_sel = jnp.where(t_col == t, s_t, s_sel)
        s_sel = jnp.where(valid, s_sel - slope * dist, MASK_NEG)
        has_new = jnp.max(is_new.astype(F32), axis=-1, keepdims=True) > 0.5
        s_new = jnp.where(valid_n & has_new, _dot_nt(q, ksn_ref[:, lanes]) - slope * dnf, MASK_NEG)

        def pv_sel(p):
            acc = jnp.zeros((rows, HEAD_DIM), F32)
            for t in range(st):
                acc = jnp.where(t_col == t, _dot_nt(p, kvbuf[slot, half, t, 1].astype(BF16)), acc)
            return acc

        o_sel = _joint_softmax_pv(s_sel, pv_sel, s_new, vsn_ref[:, lanes])

        s_w = _dot_nt(q, kwin_ref[:, lanes].astype(BF16))
        s_w = jnp.where(valid_w, s_w - slope * dist_w.astype(F32), MASK_NEG)
        s_wn = jnp.where(valid_n, _dot_nt(q, kwn_ref[:, lanes]) - slope * dnf, MASK_NEG)
        vw = vwin_ref[...].astype(BF16)
        o_w = _joint_softmax_pv(s_w, lambda p: _dot(p, vw), s_wn, vwn_ref[...])[:, lanes]

        sg = jax.nn.sigmoid(gate_ref[half])
        o_ref[half] = sg[:, 0:1] * ocmp_ref[half] + sg[:, 1:2] * o_sel + sg[:, 2:3] * o_w


def _sample_attention(idx_flat, pt_flat, slopes, cache_t, q_rows, idx_rows, e16, gate_rows, ocmp, new16, win_buf,
                      *, past, n_pages, st):
    sb, ng, rows, hd = q_rows.shape
    wb = win_buf.shape[1]
    nkey = e16.shape[1]
    pairg = lambda w: pl.BlockSpec((None, 2, rows, w), lambda b, p, *_: (b, p, 0, 0))
    newc = lambda c: pl.BlockSpec((None, new16.shape[1], PAIR_W), lambda b, p, *_: (b, 0, 2 * c + p))
    kern = functools.partial(_sample_attn_kernel, past=past, n_pages=n_pages, st=st)
    grid_spec = pltpu.PrefetchScalarGridSpec(
        num_scalar_prefetch=2,
        grid=(sb, 2),
        in_specs=[pl.BlockSpec(memory_space=pltpu.SMEM), pl.BlockSpec(memory_space=pl.ANY),
                  pairg(hd), pairg(LANES), pl.BlockSpec(e16.shape, lambda b, p, *_: (0, 0)), pairg(LANES), pairg(hd),
                  newc(2), newc(3), newc(4), newc(5),
                  pl.BlockSpec((None, wb, PAIR_W), lambda b, p, *_: (b, 0, p)),
                  pl.BlockSpec((None, wb, PAIR_W), lambda b, p, *_: (b, 0, 2 + p))],
        out_specs=pairg(hd),
        scratch_shapes=[pltpu.VMEM((2, 2, st, 2, HEAD_DIM, nkey), F32), pltpu.SemaphoreType.DMA((2,))],
    )
    return pl.pallas_call(
        kern,
        grid_spec=grid_spec,
        out_shape=jax.ShapeDtypeStruct((sb, ng, rows, hd), F32),
        compiler_params=_cparams("arbitrary", "arbitrary"),
        name="nsa_sample_attn",
    )(idx_flat, pt_flat, slopes, cache_t, q_rows, idx_rows, e16, gate_rows, ocmp, new16, new16, new16, new16,
      win_buf, win_buf)


def _pad_cols(w, mult=LANES):
    n = w.shape[1]
    return jnp.pad(w, ((0, 0), (0, -n % mult)))


def _prep_nsa_in(w_in):
    body = w_in[:, :NSA_Q_W + 6 * NSA_KV_W]
    gates = w_in[:, NSA_Q_W + 6 * NSA_KV_W:]
    per_pair = 2 * NSA_HPG * 3
    blocks = [_pad_cols(gates[:, p * per_pair:(p + 1) * per_pair]) for p in range(2)]
    return jnp.concatenate([body] + blocks, axis=1).astype(BF16)


def _prep_cmp(w_cmp1, w_cmp2, cmp_pos):
    hd = HEAD_DIM
    z = jnp.zeros((2, CMP_STRIDE, hd, hd), F32)
    wa, wb = w_cmp1[:, :CMP_STRIDE], w_cmp1[:, CMP_STRIDE:]
    top = jnp.concatenate([wa, z, wb, z], axis=-1)
    bot = jnp.concatenate([z, wa, z, wb], axis=-1)
    wbd = jnp.concatenate([top, bot], axis=-2).astype(BF16)
    z2 = jnp.zeros((2, hd, hd), F32)
    w2bd = jnp.concatenate([jnp.concatenate([w_cmp2, z2], -1), jnp.concatenate([z2, w_cmp2], -1)], -2).astype(BF16)
    pos = cmp_pos.transpose(1, 0, 2)
    pos = jnp.concatenate([pos, pos], axis=-1)
    rows = jnp.stack([pos[:, :CMP_STRIDE], pos[:, CMP_STRIDE:]], axis=2)
    pos_tiles = jnp.pad(rows, ((0, 0), (0, 0), (0, SUBLANES - 2), (0, 0)))
    return wbd, w2bd, pos_tiles


def _alibi_slopes():
    h = jnp.arange(1, NSA_HEADS + 1, dtype=F32)
    return jnp.exp2(-8.0 * h / NSA_HEADS)


def _prompt_rows_tile(m):
    for tm in (512, 256, 128, 64, 32, 16, 8):
        if m % tm == 0:
            return tm
    raise ValueError(m)


def kernel(x_prompt, x_sample, cache_nsa_kv, cache_nsa_win, state_mlstm_C, state_mlstm_n, state_mlstm_m,
           state_ffn_conv, page_table, norm_g, w_nsa_in, w_nsa_out, w_cmp1, w_cmp2, cmp_pos, w_ml_in,
           b_ml_gate, ml_head_norm, w_ml_out, w_ffn_up, ffn_conv_w, ffn_conv_b, w_ffn_down):
    nb, t, d = x_prompt.shape
    sb, st, _ = x_sample.shape
    mp = nb * t
    tm = _prompt_rows_tile(mp)
    slopes = _alibi_slopes()

    w_in0 = _prep_nsa_in(w_nsa_in[0])
    w_out0 = w_nsa_out[0].astype(BF16)
    wbd, w2bd, pos_tiles = _prep_cmp(w_cmp1[0], w_cmp2[0], cmp_pos[0])
    w_ml = _pad_cols(w_ml_in[0]).astype(BF16)
    w_mlo = w_ml_out[0].astype(BF16)
    w_up = w_ffn_up.astype(BF16)
    w_dn = w_ffn_down.astype(BF16)
    bg_row = _pad_cols(b_ml_gate[0].reshape(1, 2 * ML_HEADS))
    g = norm_g[:, :, None, :]

    xp = x_prompt.reshape(mp, d)
    _, q, kva, kvb, win, gates = _nsa_project(xp, g[0, 0], w_in0, tm)
    kc, vc = _nsa_compress(kva, wbd, w2bd, pos_tiles, nb=nb, t=t)
    o = _nsa_attention(_slope_feats(), q, gates, kc, vc, kvb, nb=nb, t=t)
    xp = _out_project(o, xp, w_out0, g[0, 1], tm)
    kv_p = kva.reshape(1, nb, t, 4, NSA_KV_HEADS, HEAD_DIM)
    w_keep = min(WINDOW, t)
    win_p = win.reshape(nb, t, 2, NSA_KV_HEADS, HEAD_DIM)[None, :, t - w_keep:]

    halo = SUBLANES
    zstate = jnp.zeros((nb, halo, 2 * D_FF), F32)
    tmf = min(t, 1024)
    conv_p = []

    def ffn_prompt(xp, i):
        xo, ta, tb = _conv_ffn(xp, zstate, g[i, 2], g[i, 3], w_up[i], ffn_conv_w[i], ffn_conv_b[i][None],
                               w_dn[i], tm=tmf, fc=1024, shift=1, tiles_per_seq=t // tmf)
        tail = jnp.concatenate([ta, tb], axis=-1).reshape(nb, t // tmf, halo, 2 * D_FF)
        conv_p.append(tail[:, -1, halo - (CONV_W - 1):])
        return xo

    xp = ffn_prompt(xp, 0)

    lp = math.gcd(t, ML_CHUNK)
    qkv, om, gm = _ml_project(xp, g[1, 0], w_ml, tm)
    zc = jnp.zeros((nb, ML_HEADS, ML_DV, ML_DQK), F32)
    zn = jnp.zeros((nb, ML_HEADS, 1, ML_DQK), F32)
    zm = jnp.zeros((nb, ML_HEADS, 1, 1), F32)
    if lp == ML_DQK:
        ym, c_p, n_p, m_p = _mlstm_stacked(qkv, om, gm, bg_row, ml_head_norm[0][None], zc, zn[:, :, 0], zm[:, :, 0],
                                           nb=nb, l=lp)
        n_p, m_p = n_p[:, :, None, :], m_p[:, :, None, :]
    else:
        ym, c_p, n_p, m_p = _mlstm(qkv, om, gm, bg_row, ml_head_norm[0][None], zc, zn, zm, nb=nb, lp=lp, lv=lp)
    xp = _out_project(ym, xp, w_mlo, g[1, 1], tm)
    xp = ffn_prompt(xp, 1)

    ms = sb * st
    ng, hpg, hd = NSA_KV_HEADS, NSA_HPG, HEAD_DIM
    xs = x_sample.transpose(1, 0, 2).reshape(ms, d)
    qs, _, kva_s, kvb_s, win_s, gates_s = _nsa_project(xs, g[0, 0], w_in0, ms)
    n_phys, page = cache_nsa_kv.shape[1:3]
    n_pages = page_table.shape[1]
    past = n_pages * page
    cache_t = cache_nsa_kv.transpose(0, 1, 3, 4, 5, 2)
    kvc = _page_compress_t(page_table, cache_t, wbd, w2bd, pos_tiles)

    rows = st * hpg
    q_rows = qs.reshape(st, sb, ng, hpg, hd).transpose(1, 2, 0, 3, 4).reshape(sb, ng, rows, hd)
    n_sel = -(-(past + st) // SEL_BLK)
    nsp = -(-n_sel // LANES) * LANES
    nsr = -(-n_sel // SUBLANES) * SUBLANES
    ocmp, imp = _sample_cmp(slopes, q_rows, kvc, past=past, nsp=nsp)
    imp_t = imp[:, :, ::hpg, :nsr].reshape(sb * ng * st, nsr).T
    idx = _sample_topk(imp_t, past=past, n_sel=n_sel, st=st)
    idx_bgtk = idx.T.reshape(sb, ng, st, SEL_TOPK)
    idx_rows = _pad_cols(jnp.repeat(idx_bgtk, hpg, axis=2).reshape(sb * ng * rows, SEL_TOPK))
    idx_rows = idx_rows.reshape(sb, ng, rows, LANES).astype(BF16)
    e16 = (jnp.arange(LANES, dtype=jnp.int32)[:, None]
           == jnp.arange(SEL_TOPK * page, dtype=jnp.int32)[None, :] // page).astype(BF16)
    gate_rows = gates_s.reshape(st, sb, 2, LANES)[..., :2 * hpg * 3].reshape(st, sb, ng, hpg, 3)
    gate_rows = _pad_cols(gate_rows.transpose(1, 2, 0, 3, 4).reshape(sb * ng * rows, 3)).reshape(sb, ng, rows, LANES)
    new16 = jnp.pad(kvb_s.reshape(st, sb, -1).transpose(1, 0, 2), ((0, 0), (0, 16 - st), (0, 0)))
    wb = cache_nsa_win.shape[2]
    win_buf = cache_nsa_win[0].reshape(sb, wb, 2 * NSA_KV_W)
    o_s = _sample_attention(idx_bgtk.reshape(-1), page_table.reshape(-1), slopes, cache_t, q_rows, idx_rows, e16,
                            gate_rows, ocmp, new16, win_buf, past=past, n_pages=n_pages, st=st)
    o_s = o_s.reshape(sb, ng, st, hpg, hd).transpose(2, 0, 1, 3, 4).reshape(ms, NSA_Q_W).astype(BF16)
    xs = _out_project(o_s, xs, w_out0, g[0, 1], ms)
    kv_s = kva_s.reshape(st, sb, 4, ng, hd).transpose(1, 0, 2, 3, 4)[None]
    win_new = win_s.reshape(st, sb, 2, ng, hd).transpose(1, 0, 2, 3, 4)
    win_s_out = jnp.concatenate([cache_nsa_win[0], win_new], axis=1)[None, :, st:]

    conv_s = []

    def ffn_sample(xs, i):
        state = state_ffn_conv[i].transpose(1, 0, 2).reshape(1, (CONV_W - 1) * sb, 2 * D_FF)
        xo, ta, tb = _conv_ffn(xs, state, g[i, 2], g[i, 3], w_up[i], ffn_conv_w[i], ffn_conv_b[i][None],
                               w_dn[i], tm=ms, fc=512, shift=sb, tiles_per_seq=1)
        tail = jnp.concatenate([ta, tb], axis=-1).reshape(CONV_W - 1, sb, 2 * D_FF)
        conv_s.append(tail.transpose(1, 0, 2))
        return xo

    xs = ffn_sample(xs, 0)

    lps = SUBLANES
    to_seq = lambda a: jnp.pad(a.reshape(st, sb, -1).transpose(1, 0, 2),
                               ((0, 0), (0, lps - st), (0, 0))).reshape(sb * lps, -1)
    qkv_s, om_s, gm_s = _ml_project(xs, g[1, 0], w_ml, ms)
    ym_s, c_s, n_s, m_s = _mlstm(to_seq(qkv_s), to_seq(om_s), to_seq(gm_s), bg_row, ml_head_norm[0][None],
                                 state_mlstm_C[0], state_mlstm_n[0][:, :, None, :],
                                 state_mlstm_m[0][:, :, None, None], nb=sb, lp=lps, lv=math.gcd(st, ML_CHUNK))
    ym_s = ym_s.reshape(sb, lps, ML_V_W)[:, :st].transpose(1, 0, 2).reshape(ms, ML_V_W)
    xs = _out_project(ym_s, xs, w_mlo, g[1, 1], ms)
    xs = ffn_sample(xs, 1)

    return (xp.reshape(nb, t, d), xs.reshape(st, sb, d).transpose(1, 0, 2), kv_p, kv_s, win_p, win_s_out,
            c_p[None], c_s[None], n_p[None, :, :, 0], n_s[None, :, :, 0],
            m_p[None, :, :, 0, 0], m_s[None, :, :, 0, 0], jnp.stack(conv_p), jnp.stack(conv_s))
```

```python
import functools
import math

import jax
import jax.numpy as jnp
from jax import lax
from jax.experimental import pallas as pl
from jax.experimental.pallas import tpu as pltpu

F32 = jnp.float32
BF16 = jnp.bfloat16

LANES = 128
SUBLANES = 8
VMEM_LIMIT_BYTES = 56 * 1024 * 1024

D_MODEL = 1024
NSA_HEADS = 16
NSA_KV_HEADS = 4
NSA_HPG = NSA_HEADS // NSA_KV_HEADS
HEAD_DIM = D_MODEL // NSA_HEADS
CMP_LEN = 32
CMP_STRIDE = 16
SEL_BLK = 64
SEL_TOPK = 16
WINDOW = 512
Q_BLK = 128
FORCE_BONUS = 1e4
NSA_Q_W = NSA_HEADS * HEAD_DIM
NSA_KV_W = NSA_KV_HEADS * HEAD_DIM
ML_HEADS = 8
ML_DQK = D_MODEL // (2 * ML_HEADS)
ML_DV = D_MODEL // ML_HEADS
ML_CHUNK = 64
ML_QK_W = ML_HEADS * ML_DQK
ML_V_W = ML_HEADS * ML_DV
D_FF = 4 * D_MODEL
CONV_W = 3
RMS_EPS = 1e-6
MASK_NEG = -1e30
LOG2E = 1.4426950408889634
ML_SEQ_PER_STEP = 4
FFN_SUB_ROWS = 1024
PAIR_W = 2 * HEAD_DIM


def _cparams(*sem):
    return pltpu.CompilerParams(dimension_semantics=sem, vmem_limit_bytes=VMEM_LIMIT_BYTES)


def _dot(a, b):
    return jnp.dot(a, b, preferred_element_type=F32)


def _dot_nt(a, b):
    return lax.dot_general(a, b, (((1,), (1,)), ((), ())), preferred_element_type=F32)


def _dot_tn(a, b):
    return lax.dot_general(a, b, (((0,), (0,)), ((), ())), preferred_element_type=F32)


def _split3(x):
    hi = x.astype(BF16)
    r1 = x - hi.astype(F32)
    mid = r1.astype(BF16)
    lo = (r1 - mid.astype(F32)).astype(BF16)
    return hi, mid, lo


def _rms(x, g):
    return x * lax.rsqrt(jnp.mean(x * x, axis=-1, keepdims=True) + RMS_EPS) * g


def _gelu_tanh(x):
    return 0.5 * x * (1.0 + jnp.tanh(0.7978845608028654 * (x + 0.044715 * (x * x * x))))


def _full(shape):
    return pl.BlockSpec(shape, lambda *_: (0,) * len(shape))


def _nsa_proj_kernel(x_ref, g_ref, w_ref, q_ref, q2_ref, kva_ref, kvb_ref, win_ref, gate_ref):
    h = _rms(x_ref[...], g_ref[...]).astype(BF16)
    p = _dot(h, w_ref[...])
    kvw = 4 * NSA_KV_W
    q_ref[...] = (p[:, :NSA_Q_W] * HEAD_DIM ** -0.5).astype(BF16)
    q2_ref[...] = (p[:, :NSA_Q_W] * (HEAD_DIM ** -0.5 * LOG2E)).astype(BF16)
    kva_ref[...] = p[:, NSA_Q_W:NSA_Q_W + kvw]
    kvb_ref[...] = p[:, NSA_Q_W:NSA_Q_W + 6 * NSA_KV_W].astype(BF16)
    win_ref[...] = p[:, NSA_Q_W + kvw:NSA_Q_W + 6 * NSA_KV_W]
    gate_ref[...] = p[:, NSA_Q_W + 6 * NSA_KV_W:]


def _nsa_project(x, g, w, tm):
    m = x.shape[0]
    n = w.shape[1]
    row = lambda i: (i, 0)
    return pl.pallas_call(
        _nsa_proj_kernel,
        grid=(m // tm,),
        in_specs=[pl.BlockSpec((tm, D_MODEL), row), _full((1, D_MODEL)), _full((D_MODEL, n))],
        out_specs=[pl.BlockSpec((tm, NSA_Q_W), row), pl.BlockSpec((tm, NSA_Q_W), row),
                   pl.BlockSpec((tm, 4 * NSA_KV_W), row),
                   pl.BlockSpec((tm, 6 * NSA_KV_W), row), pl.BlockSpec((tm, 2 * NSA_KV_W), row),
                   pl.BlockSpec((tm, 2 * LANES), row)],
        out_shape=[jax.ShapeDtypeStruct((m, NSA_Q_W), BF16), jax.ShapeDtypeStruct((m, NSA_Q_W), BF16),
                   jax.ShapeDtypeStruct((m, 4 * NSA_KV_W), F32),
                   jax.ShapeDtypeStruct((m, 6 * NSA_KV_W), BF16), jax.ShapeDtypeStruct((m, 2 * NSA_KV_W), F32),
                   jax.ShapeDtypeStruct((m, 2 * LANES), F32)],
        compiler_params=_cparams("parallel"),
        name="nsa_proj",
    )(x, g, w)


def _ml_proj_kernel(x_ref, g_ref, w_ref, qkv_ref, o_ref, gate_ref):
    h = _rms(x_ref[...], g_ref[...]).astype(BF16)
    p = _dot(h, w_ref[...])
    a = 2 * ML_QK_W + ML_V_W
    qkv_ref[:, :ML_QK_W] = p[:, :ML_QK_W].astype(BF16)
    qkv_ref[:, ML_QK_W:2 * ML_QK_W] = (p[:, ML_QK_W:2 * ML_QK_W] * ML_DQK ** -0.5).astype(BF16)
    qkv_ref[:, 2 * ML_QK_W:] = p[:, 2 * ML_QK_W:a].astype(BF16)
    o_ref[...] = p[:, a:a + ML_V_W]
    gate_ref[...] = p[:, a + ML_V_W:]


def _ml_project(x, g, w, tm):
    m = x.shape[0]
    n = w.shape[1]
    a = 2 * ML_QK_W + ML_V_W
    row = lambda i: (i, 0)
    return pl.pallas_call(
        _ml_proj_kernel,
        grid=(m // tm,),
        in_specs=[pl.BlockSpec((tm, D_MODEL), row), _full((1, D_MODEL)), _full((D_MODEL, n))],
        out_specs=[pl.BlockSpec((tm, a), row), pl.BlockSpec((tm, ML_V_W), row), pl.BlockSpec((tm, LANES), row)],
        out_shape=[jax.ShapeDtypeStruct((m, a), BF16), jax.ShapeDtypeStruct((m, ML_V_W), F32),
                   jax.ShapeDtypeStruct((m, LANES), F32)],
        compiler_params=_cparams("parallel"),
        name="ml_proj",
    )(x, g, w)


def _ffn_kernel(o_ref, xp_ref, wo_ref, g1_ref, sta_ref, stb_ref, g2_ref, g3_ref, wa_ref, wb_ref, cwa_ref, cwb_ref,
                cba_ref, cbb_ref, wd_ref, xo_ref, taila_ref, tailb_ref, h_scr, ua_scr, ub_scr, ca_scr, cb_scr,
                acc_scr, *, tm, halo, shift, tiles_per_seq):
    i = pl.program_id(0)
    c = pl.program_id(1)

    @pl.when(c == 0)
    def _():
        x = xp_ref[...] + _rms(_dot(o_ref[...], wo_ref[...]), g1_ref[...])
        xo_ref[...] = x
        h_scr[...] = _rms(x, g2_ref[...]).astype(BF16)
        acc_scr[...] = jnp.zeros_like(acc_scr)

    first = (i % tiles_per_seq) == 0

    @pl.when(first)
    def _():
        ua_scr[0:halo, :] = sta_ref[...]
        ub_scr[0:halo, :] = stb_ref[...]

    @pl.when(jnp.logical_not(first))
    def _():
        ua_scr[0:halo, :] = ca_scr[c]
        ub_scr[0:halo, :] = cb_scr[c]

    nsub = max(1, tm // FFN_SUB_ROWS)
    rs = tm // nsub
    for r in range(nsub):
        h = h_scr[r * rs:(r + 1) * rs, :]
        ua_scr[halo + r * rs:halo + (r + 1) * rs, :] = _dot(h, wa_ref[...])
        ub_scr[halo + r * rs:halo + (r + 1) * rs, :] = _dot(h, wb_ref[...])
    ta = ua_scr[tm:tm + halo, :]
    tb = ub_scr[tm:tm + halo, :]
    ca_scr[c] = ta
    cb_scr[c] = tb
    taila_ref[...] = ta
    tailb_ref[...] = tb

    def conv(u_scr, cw_ref, cb_ref, lo):
        cw = cw_ref[...]
        return (cb_ref[...] + cw[2:3, :] * u_scr[lo:lo + rs, :]
                + cw[1:2, :] * u_scr[lo - shift:lo - shift + rs, :]
                + cw[0:1, :] * u_scr[lo - 2 * shift:lo - 2 * shift + rs, :])

    for r in range(nsub):
        lo = halo + r * rs
        y = _gelu_tanh(conv(ua_scr, cwa_ref, cba_ref, lo)) * conv(ub_scr, cwb_ref, cbb_ref, lo)
        acc_scr[r * rs:(r + 1) * rs, :] += _dot(y.astype(BF16), wd_ref[...])

    @pl.when(c == pl.num_programs(1) - 1)
    def _():
        xo_ref[...] = xo_ref[...] + _rms(acc_scr[...], g3_ref[...])


def _conv_ffn(o, x, w_out, g1, state, g2, g3, w_up, conv_w, conv_b, w_down, *, tm, fc, shift, tiles_per_seq):
    m = x.shape[0]
    halo = state.shape[1]
    nfc = D_FF // fc
    n_tiles = m // tm
    row = lambda i, c: (i, 0)
    const = lambda i, c: (0, 0)
    kern = functools.partial(_ffn_kernel, tm=tm, halo=halo, shift=shift, tiles_per_seq=tiles_per_seq)
    return pl.pallas_call(
        kern,
        grid=(n_tiles, nfc),
        in_specs=[
            pl.BlockSpec((tm, o.shape[1]), row), pl.BlockSpec((tm, D_MODEL), row),
            pl.BlockSpec(w_out.shape, const), pl.BlockSpec((1, D_MODEL), const),
            pl.BlockSpec((None, halo, fc), lambda i, c: (i // tiles_per_seq, 0, c)),
            pl.BlockSpec((None, halo, fc), lambda i, c: (i // tiles_per_seq, 0, nfc + c)),
            pl.BlockSpec((1, D_MODEL), const), pl.BlockSpec((1, D_MODEL), const),
            pl.BlockSpec((D_MODEL, fc), lambda i, c: (0, c)),
            pl.BlockSpec((D_MODEL, fc), lambda i, c: (0, nfc + c)),
            pl.BlockSpec((CONV_W, fc), lambda i, c: (0, c)),
            pl.BlockSpec((CONV_W, fc), lambda i, c: (0, nfc + c)),
            pl.BlockSpec((1, fc), lambda i, c: (0, c)),
            pl.BlockSpec((1, fc), lambda i, c: (0, nfc + c)),
            pl.BlockSpec((fc, D_MODEL), lambda i, c: (c, 0)),
        ],
        out_specs=[pl.BlockSpec((tm, D_MODEL), row),
                   pl.BlockSpec((None, halo, fc), lambda i, c: (i, 0, c)),
                   pl.BlockSpec((None, halo, fc), lambda i, c: (i, 0, c))],
        out_shape=[jax.ShapeDtypeStruct((m, D_MODEL), F32),
                   jax.ShapeDtypeStruct((n_tiles, halo, D_FF), F32),
                   jax.ShapeDtypeStruct((n_tiles, halo, D_FF), F32)],
        scratch_shapes=[pltpu.VMEM((tm, D_MODEL), BF16),
                        pltpu.VMEM((halo + tm, fc), F32), pltpu.VMEM((halo + tm, fc), F32),
                        pltpu.VMEM((nfc, halo, fc), F32), pltpu.VMEM((nfc, halo, fc), F32),
                        pltpu.VMEM((tm, D_MODEL), F32)],
        compiler_params=_cparams("arbitrary", "arbitrary"),
        name="conv_ffn",
    )(o, x, w_out, g1, state, state, g2, g3, w_up, w_up, conv_w, conv_w, conv_b, conv_b, w_down)


def _log_sigmoid(x):
    return jnp.minimum(x, 0.0) - jnp.log1p(jnp.exp(-jnp.abs(x)))


def _exact_nt(sel_bf16, x):
    hi, mid, lo = _split3(x)
    return _dot_nt(sel_bf16, hi) + _dot_nt(sel_bf16, mid) + _dot_nt(sel_bf16, lo)


def _eye(n, m, dtype):
    return (lax.broadcasted_iota(jnp.int32, (n, m), 0) == lax.broadcasted_iota(jnp.int32, (n, m), 1)).astype(dtype)


def _mlstm_kernel(qkv_ref, o_ref, gate_ref, bg_ref, hn_ref, c0_ref, n0_ref, m0_ref,
                  y_ref, cf_ref, nf_ref, mf_ref, ct_scr, n_scr, m_scr, *, lp, lv, bb):
    for bi in range(bb):
        _mlstm_one(qkv_ref.at[bi], o_ref.at[bi], gate_ref.at[bi], bg_ref, hn_ref, c0_ref.at[bi], n0_ref.at[bi],
                   m0_ref.at[bi], y_ref.at[bi], cf_ref.at[bi], nf_ref.at[bi], mf_ref.at[bi],
                   ct_scr.at[bi], n_scr.at[bi], m_scr.at[bi], lp=lp, lv=lv)


def _mlstm_one(qkv_ref, o_ref, gate_ref, bg_ref, hn_ref, c0_ref, n0_ref, m0_ref,
               y_ref, cf_ref, nf_ref, mf_ref, ct_scr, n_scr, m_scr, *, lp, lv):
    ci = pl.program_id(1)
    eye_qk = _eye(ML_DQK, ML_DQK, BF16)

    @pl.when(ci == 0)
    def _():
        for h in range(ML_HEADS):
            ct_scr[h] = _exact_nt(eye_qk, c0_ref[h])
        n_scr[...] = n0_ref[...]
        m_scr[...] = m0_ref[...]

    gp = gate_ref[...] + bg_ref[...]
    ls = _log_sigmoid(gp)
    r_i = lax.broadcasted_iota(jnp.int32, (lp, lp), 0)
    c_i = lax.broadcasted_iota(jnp.int32, (lp, lp), 1)
    causal = c_i <= r_i
    tril = causal.astype(BF16)
    hi, mid, lo = _split3(ls)
    b_all = _dot(tril, hi) + _dot(tril, mid) + _dot(tril, lo)
    sel16 = _eye(2 * ML_HEADS, LANES, BF16)
    gp_t = _exact_nt(sel16, gp)
    b_t = _exact_nt(sel16, b_all)
    row_valid = lax.broadcasted_iota(jnp.int32, (lp, 1), 0) < lv

    for h in range(ML_HEADS):
        q = qkv_ref[:, h * ML_DQK:(h + 1) * ML_DQK]
        k = qkv_ref[:, ML_QK_W + h * ML_DQK:ML_QK_W + (h + 1) * ML_DQK]
        v = qkv_ref[:, 2 * ML_QK_W + h * ML_DV:2 * ML_QK_W + (h + 1) * ML_DV]
        ct = ct_scr[h]
        n_row = n_scr[h]
        m = m_scr[h]
        b_col = b_all[:, ML_HEADS + h:ML_HEADS + h + 1]
        i_col = gp[:, h:h + 1]
        b_row = b_t[ML_HEADS + h:ML_HEADS + h + 1, :]
        i_row = gp_t[h:h + 1, :]
        dmat = jnp.where(causal, b_col - b_row + i_row, MASK_NEG)
        inter = b_col + m
        mt = jnp.maximum(inter, jnp.max(dmat, axis=-1, keepdims=True))
        s = _dot_nt(q, k) * jnp.exp(dmat - mt)
        wi = jnp.exp(inter - mt)
        qf = q.astype(F32)
        num = wi * _dot(q, ct.astype(BF16)) + _dot(s.astype(BF16), v)
        den = wi * jnp.sum(qf * n_row, axis=-1, keepdims=True) + jnp.sum(s, axis=-1, keepdims=True)
        hc = num / jnp.maximum(jnp.abs(den), jnp.exp(-mt))
        m_last = mt[lv - 1:lv, :]
        b_last = b_col[lv - 1:lv, :]
        decay = jnp.exp(b_last + m - m_last)
        ws = jnp.where(row_valid, jnp.exp(b_last - b_col + i_col - m_last), 0.0)
        ct_scr[h] = decay * ct + _dot_tn(k, (ws * v.astype(F32)).astype(BF16))
        n_scr[h] = decay * n_row + jnp.sum(ws * k.astype(F32), axis=0, keepdims=True)
        m_scr[h] = m_last
        hn = hc * lax.rsqrt(jnp.mean(hc * hc, axis=-1, keepdims=True) + RMS_EPS)
        sl = slice(h * ML_DV, (h + 1) * ML_DV)
        y_ref[:, sl] = (hn * hn_ref[:, sl] * jax.nn.sigmoid(o_ref[:, sl])).astype(BF16)

    @pl.when(ci == pl.num_programs(1) - 1)
    def _():
        eye_v = _eye(ML_DV, ML_DV, BF16)
        for h in range(ML_HEADS):
            cf_ref[h] = _exact_nt(eye_v, ct_scr[h])
        nf_ref[...] = n_scr[...]
        mf_ref[...] = m_scr[...]


def _mlstm(qkv, o, gates, b_gate, head_norm, c0, n0, m0, *, nb, lp, lv):
    m = qkv.shape[0]
    tseq = m // nb
    bb = math.gcd(nb, ML_SEQ_PER_STEP)
    seq3 = lambda a: a.reshape(nb, tseq, a.shape[1])
    row = lambda b, c: (b, c, 0)
    st4 = lambda b, c: (b, 0, 0, 0)
    kern = functools.partial(_mlstm_kernel, lp=lp, lv=lv, bb=bb)
    y, cf, nf, mf = pl.pallas_call(
        kern,
        grid=(nb // bb, tseq // lp),
        in_specs=[pl.BlockSpec((bb, lp, qkv.shape[1]), row), pl.BlockSpec((bb, lp, ML_V_W), row),
                  pl.BlockSpec((bb, lp, LANES), row), pl.BlockSpec((1, LANES), lambda b, c: (0, 0)),
                  pl.BlockSpec((1, ML_V_W), lambda b, c: (0, 0)),
                  pl.BlockSpec((bb, ML_HEADS, ML_DV, ML_DQK), st4),
                  pl.BlockSpec((bb, ML_HEADS, 1, ML_DQK), st4),
                  pl.BlockSpec((bb, ML_HEADS, 1, 1), st4)],
        out_specs=[pl.BlockSpec((bb, lp, ML_V_W), row),
                   pl.BlockSpec((bb, ML_HEADS, ML_DV, ML_DQK), st4),
                   pl.BlockSpec((bb, ML_HEADS, 1, ML_DQK), st4),
                   pl.BlockSpec((bb, ML_HEADS, 1, 1), st4)],
        out_shape=[jax.ShapeDtypeStruct((nb, tseq, ML_V_W), BF16),
                   jax.ShapeDtypeStruct((nb, ML_HEADS, ML_DV, ML_DQK), F32),
                   jax.ShapeDtypeStruct((nb, ML_HEADS, 1, ML_DQK), F32),
                   jax.ShapeDtypeStruct((nb, ML_HEADS, 1, 1), F32)],
        scratch_shapes=[pltpu.VMEM((bb, ML_HEADS, ML_DQK, ML_DV), F32),
                        pltpu.VMEM((bb, ML_HEADS, 1, ML_DQK), F32),
                        pltpu.VMEM((bb, ML_HEADS, 1, 1), F32)],
        compiler_params=_cparams("arbitrary", "arbitrary"),
        name="mlstm",
    )(seq3(qkv), seq3(o), seq3(gates), b_gate, head_norm, c0, n0, m0)
    return y.reshape(m, ML_V_W), cf, nf, mf


def _stack_cols(x, cols):
    return jnp.concatenate([x[:, c:c + 1] for c in cols], axis=0)


def _block_const(col, l, row):
    nh = col.shape[0] // l
    return jnp.concatenate([jnp.broadcast_to(col[h * l + row:h * l + row + 1, :], (l, 1)) for h in range(nh)], axis=0)


def _mlstm_stacked_one(qkv_ref, o_ref, gate_ref, bg_ref, hn_ref, c0_ref, n0_ref, m0_ref,
                       y_ref, cf_ref, nf_ref, mf_ref, ctn_scr, m_scr, blockmask, *, l):
    ci = pl.program_id(1)
    nh = ML_HEADS
    ht = nh * l
    dv = ML_DV
    lane = lax.broadcasted_iota(jnp.int32, (ht, LANES), 1)

    @pl.when(ci == 0)
    def _():
        eye = _eye(ML_DQK, ML_DQK, BF16)
        for h in range(nh):
            rows = slice(h * ML_DQK, (h + 1) * ML_DQK)
            ctn_scr[rows, 0:dv] = _exact_nt(eye, c0_ref[h])
        ctn_scr[:, dv:2 * dv] = jnp.where(lane == 0, n0_ref[...], 0.0)
        m_scr[...] = m0_ref[...]

    gp = gate_ref[...] + bg_ref[...]
    ls = _log_sigmoid(gp)
    tril = (lax.broadcasted_iota(jnp.int32, (l, l), 1) <= lax.broadcasted_iota(jnp.int32, (l, l), 0)).astype(BF16)
    hi, mid, lo = _split3(ls)
    b_all = _dot(tril, hi) + _dot(tril, mid) + _dot(tril, lo)
    sel16 = _eye(2 * nh, LANES, BF16)
    dup = lambda x: jnp.concatenate([x, x], axis=0)
    c_t = _exact_nt(sel16, dup(gp))[0:nh] - _exact_nt(sel16, dup(b_all))[nh:2 * nh]
    c_row = jnp.concatenate([jnp.broadcast_to(c_t[h:h + 1, :], (l, 2 * l)) for h in range(nh)], axis=0)
    r_b = _stack_cols(b_all, range(nh, 2 * nh))
    r_i = _stack_cols(gp, range(nh))
    t_row = lax.broadcasted_iota(jnp.int32, (ht, 1), 0) % l
    causal = (lane % l) <= t_row
    dmat = jnp.where(causal, r_b + c_row, MASK_NEG)
    m_col = m_scr[...]
    inter = r_b + m_col
    mt = jnp.maximum(inter, jnp.max(dmat, axis=-1, keepdims=True))

    q = qkv_ref[:, 0:ML_QK_W]
    k = qkv_ref[:, ML_QK_W:2 * ML_QK_W]
    zero = jnp.zeros((), BF16)
    q_bd = jnp.where(blockmask, _tile_rows(q, nh), zero)
    k_bd = jnp.where(blockmask, _tile_rows(k, nh), zero)
    s = _dot_nt(q_bd, dup(k)) * jnp.exp(dmat - mt)
    wi = jnp.exp(inter - mt)
    ones_lane = jnp.where(lane == 0, 1.0, 0.0).astype(BF16)
    v_st = jnp.concatenate([qkv_ref[:, 2 * ML_QK_W + h * dv:2 * ML_QK_W + (h + 1) * dv] for h in range(nh)], axis=0)
    v_aug = jnp.concatenate([v_st, ones_lane], axis=1)
    a = _dot(q_bd, ctn_scr[...].astype(BF16))
    s_bd = jnp.where(blockmask, jnp.concatenate([s.astype(BF16)] * (ML_QK_W // (2 * l)), axis=1), zero)
    bm = _dot(s_bd, v_aug)
    num = wi * a[:, 0:dv] + bm[:, 0:dv]
    den = wi * a[:, dv:dv + 1] + bm[:, dv:dv + 1]
    hc = num / jnp.maximum(jnp.abs(den), jnp.exp(-mt))

    m_last = _block_const(mt, l, l - 1)
    b_last = _block_const(r_b, l, l - 1)
    decay = jnp.exp(b_last + m_col - m_last)
    ws = jnp.exp(b_last - r_b + r_i - m_last)
    wv = (ws * v_aug.astype(F32)).astype(BF16)
    ctn_scr[...] = decay * ctn_scr[...] + _dot_tn(k_bd, wv)
    m_scr[...] = m_last

    hn = hc * lax.rsqrt(jnp.mean(hc * hc, axis=-1, keepdims=True) + RMS_EPS)
    for h in range(nh):
        sl = slice(h * dv, (h + 1) * dv)
        y_ref[:, sl] = (hn[h * l:(h + 1) * l] * hn_ref[:, sl] * jax.nn.sigmoid(o_ref[:, sl])).astype(BF16)

    @pl.when(ci == pl.num_programs(1) - 1)
    def _():
        eye_v = _eye(dv, dv, BF16)
        for h in range(nh):
            rows = slice(h * ML_DQK, (h + 1) * ML_DQK)
            cf_ref[h] = _exact_nt(eye_v, ctn_scr[rows, 0:dv])
        nf_ref[...] = ctn_scr[:, dv:dv + 1]
        mf_ref[...] = m_scr[...]


def _mlstm_stacked_kernel(qkv_ref, o_ref, gate_ref, bg_ref, hn_ref, c0_ref, n0_ref, m0_ref,
                          y_ref, cf_ref, nf_ref, mf_ref, ctn_scr, m_scr, *, l, bb):
    ht = ML_HEADS * l
    blockmask = (lax.broadcasted_iota(jnp.int32, (ht, ML_QK_W), 0) // l
                 == lax.broadcasted_iota(jnp.int32, (ht, ML_QK_W), 1) // ML_DQK)
    for bi in range(bb):
        _mlstm_stacked_one(qkv_ref.at[bi], o_ref.at[bi], gate_ref.at[bi], bg_ref, hn_ref, c0_ref.at[bi],
                           n0_ref.at[bi], m0_ref.at[bi], y_ref.at[bi], cf_ref.at[bi], nf_ref.at[bi],
                           mf_ref.at[bi], ctn_scr.at[bi], m_scr.at[bi], blockmask, l=l)


def _mlstm_stacked(qkv, o, gates, b_gate, head_norm, c0, n0, m0, *, nb, l):
    assert l == ML_DQK and ML_QK_W % (2 * l) == 0
    m = qkv.shape[0]
    tseq = m // nb
    bb = math.gcd(nb, ML_SEQ_PER_STEP)
    ht = ML_HEADS * l
    seq3 = lambda a: a.reshape(nb, tseq, a.shape[1])
    row = lambda b, c: (b, c, 0)
    st4 = lambda b, c: (b, 0, 0, 0)
    st3 = lambda b, c: (b, 0, 0)
    kern = functools.partial(_mlstm_stacked_kernel, l=l, bb=bb)
    y, cf, nf, mf = pl.pallas_call(
        kern,
        grid=(nb // bb, tseq // l),
        in_specs=[pl.BlockSpec((bb, l, qkv.shape[1]), row), pl.BlockSpec((bb, l, ML_V_W), row),
                  pl.BlockSpec((bb, l, LANES), row), pl.BlockSpec((1, LANES), lambda b, c: (0, 0)),
                  pl.BlockSpec((1, ML_V_W), lambda b, c: (0, 0)),
                  pl.BlockSpec((bb, ML_HEADS, ML_DV, ML_DQK), st4),
                  pl.BlockSpec((bb, ht, 1), st3), pl.BlockSpec((bb, ht, 1), st3)],
        out_specs=[pl.BlockSpec((bb, l, ML_V_W), row),
                   pl.BlockSpec((bb, ML_HEADS, ML_DV, ML_DQK), st4),
                   pl.BlockSpec((bb, ht, 1), st3), pl.BlockSpec((bb, ht, 1), st3)],
        out_shape=[jax.ShapeDtypeStruct((nb, tseq, ML_V_W), BF16),
                   jax.ShapeDtypeStruct((nb, ML_HEADS, ML_DV, ML_DQK), F32),
                   jax.ShapeDtypeStruct((nb, ht, 1), F32), jax.ShapeDtypeStruct((nb, ht, 1), F32)],
        scratch_shapes=[pltpu.VMEM((bb, ht, 2 * ML_DV), F32), pltpu.VMEM((bb, ht, 1), F32)],
        compiler_params=_cparams("arbitrary", "arbitrary"),
        name="mlstm",
    )(seq3(qkv), seq3(o), seq3(gates), b_gate, head_norm, c0,
      n0.reshape(nb, ht, 1), jnp.repeat(m0, l, axis=1))
    return (y.reshape(m, ML_V_W), cf, nf.reshape(nb, ML_HEADS, ML_DQK), mf.reshape(nb, ML_HEADS, l)[:, :, :1])


def _cmp_parts(load_rows, wbd_ref, c, p):
    acc = None
    for l in range(CMP_STRIDE):
        part = _dot(load_rows(l, c, p).astype(BF16), wbd_ref[c, l])
        acc = part if acc is None else acc + part
    return acc


def _cmp_pos_bias(pos_ref, wbd_ref, c):
    acc = None
    for l in range(CMP_STRIDE):
        part = _dot(pos_ref[c, l].astype(BF16), wbd_ref[c, l])
        acc = part if acc is None else acc + part
    return acc[0:1, :PAIR_W] + acc[1:2, PAIR_W:]


def _cmp_finish(parts, posb, w2_ref, c):
    n = parts.shape[0]
    pre = parts[:, :PAIR_W] + pltpu.roll(parts[:, PAIR_W:], n - 1, 0) + posb
    return _dot(_gelu_tanh(pre).astype(BF16), w2_ref[c])


def _nsa_compress_kernel(x00_ref, x01_ref, x10_ref, x11_ref, wbd_ref, w2_ref, pos_ref, kc_ref, vc_ref, *, nchunk):
    x_refs = ((x00_ref, x01_ref), (x10_ref, x11_ref))

    def load_rows(l, c, p):
        return x_refs[c][p][pl.ds(l, nchunk, stride=CMP_STRIDE), :]

    for c, out_ref in ((0, kc_ref), (1, vc_ref)):
        posb = _cmp_pos_bias(pos_ref, wbd_ref, c)
        for p in range(2):
            out_ref[p] = _cmp_finish(_cmp_parts(load_rows, wbd_ref, c, p), posb, w2_ref, c).astype(BF16)


def _nsa_compress(kva, wbd, w2bd, pos, *, nb, t):
    nchunk = t // CMP_STRIDE
    kern = functools.partial(_nsa_compress_kernel, nchunk=nchunk)
    out = jax.ShapeDtypeStruct((nb, 2, nchunk, PAIR_W), BF16)
    ospec = pl.BlockSpec((None, 2, nchunk, PAIR_W), lambda b: (b, 0, 0, 0))
    return pl.pallas_call(
        kern,
        grid=(nb,),
        in_specs=[pl.BlockSpec((t, PAIR_W), functools.partial(lambda j, b: (b, j), j)) for j in range(4)]
        + [_full(wbd.shape), _full(w2bd.shape), _full(pos.shape)],
        out_specs=[ospec, ospec],
        out_shape=[out, out],
        compiler_params=_cparams("parallel"),
        name="nsa_compress",
    )(kva, kva, kva, kva, wbd, w2bd, pos)


KEY_CHUNK = 256
N_FEAT = 6
AUG_W = 2 * LANES
WIN_JOBS = 3
WIN_PAD = WIN_JOBS * KEY_CHUNK - Q_BLK
PAD_FEAT = N_FEAT


def _rank_topk_t(score, ns, qb):
    blocks = [score[SUBLANES * r:SUBLANES * (r + 1)] for r in range(ns // SUBLANES)]
    cnts = [jnp.zeros((SUBLANES, qb), jnp.int32) for _ in blocks]
    sub = lax.broadcasted_iota(jnp.int32, (SUBLANES, qb), 0)
    for m in range(ns):
        sm = jnp.broadcast_to(score[m:m + 1, :], (SUBLANES, qb))
        for r, blk in enumerate(blocks):
            lo = SUBLANES * r
            if lo > m:
                ahead = (sm >= blk).astype(jnp.int32)
            elif lo + SUBLANES - 1 <= m:
                ahead = (sm > blk).astype(jnp.int32)
            else:
                ahead = jnp.where(sub > m - lo, (sm >= blk).astype(jnp.int32), (sm > blk).astype(jnp.int32))
            cnts[r] = cnts[r] + ahead
    return jnp.concatenate(cnts, axis=0) < min(SEL_TOPK, ns)


def _key_feats(kpos_col):
    r = kpos_col.shape[0]
    lane = lax.broadcasted_iota(jnp.int32, (r, LANES), 1)
    a = lax.shift_right_logical(kpos_col, 6).astype(F32)
    b = (kpos_col & (SEL_BLK - 1)).astype(F32)
    return jnp.where(lane < 3, a, jnp.where(lane < N_FEAT, b, 0.0))


def _tile_rows(x, n):
    return jnp.concatenate([x] * n, axis=0)


def _slope_feats():
    s = _alibi_slopes() * LOG2E
    hi = s.astype(BF16).astype(F32)
    mid = (s - hi).astype(BF16).astype(F32)
    lo = (s - hi - mid).astype(BF16).astype(F32)
    z = jnp.zeros_like(s)
    return jnp.stack([SEL_BLK * hi, SEL_BLK * mid, SEL_BLK * lo, hi, mid, lo, z, z], axis=1)


def _nsa_attn_kernel(sf_ref, q_ref, gate_ref, kc_ref, vc_ref, ksel_ref, vsel_ref, kwin_ref, vwin_ref, o_ref,
                      k_scr, v_scr, kc_scr, qa_scr, qc_scr, s_scr, mx_scr, m_scr, acc_scr, *, t):
    pr = pl.program_id(1)
    nblk = pl.program_id(2)
    qb = Q_BLK
    s0 = nblk * qb
    ncp = t // CMP_STRIDE
    ns = t // SEL_BLK
    hd = HEAD_DIM
    w = KEY_CHUNK
    hpg = NSA_HPG
    rows_all = hpg * qb
    wrow0 = t

    @pl.when(nblk == 0)
    def _():
        lane = lax.broadcasted_iota(jnp.int32, (t, LANES), 1)
        row = lax.broadcasted_iota(jnp.int32, (t, 1), 0)
        feats = _key_feats(row).astype(BF16)
        onehot = jnp.where(lane - hd == lax.shift_right_logical(row, 6), 1.0, 0.0)
        lane_p = lax.broadcasted_iota(jnp.int32, (WIN_PAD, LANES), 1)
        lane_c = lax.broadcasted_iota(jnp.int32, (ncp, LANES), 1)
        cend = CMP_STRIDE * lax.broadcasted_iota(jnp.int32, (ncp, 1), 0) + (CMP_LEN - 1)
        ks = ksel_ref[...].astype(F32)
        vs = vsel_ref[...].astype(F32)
        kwn = kwin_ref[...].astype(F32)
        vwn = vwin_ref[...].astype(F32)
        kcv = kc_ref[...].astype(F32)
        for half in range(2):
            low = (lambda x: x) if half == 0 else (lambda x: pltpu.roll(x, hd, 1))
            k_scr[half, 0:t, 0:LANES] = jnp.where(lane < hd, low(ks), onehot).astype(BF16)
            k_scr[half, 0:t, LANES:AUG_W] = feats
            v_scr[half, 0:t] = jnp.where(lane < hd, low(vs), 1.0).astype(BF16)
            k_scr[half, wrow0:wrow0 + WIN_PAD, 0:LANES] = jnp.zeros((WIN_PAD, LANES), BF16)
            k_scr[half, wrow0:wrow0 + WIN_PAD, LANES:AUG_W] = jnp.where(lane_p == PAD_FEAT, 1.0, 0.0).astype(BF16)
            v_scr[half, wrow0:wrow0 + WIN_PAD] = jnp.ones((WIN_PAD, LANES), BF16)
            wr = wrow0 + WIN_PAD
            k_scr[half, wr:wr + t, 0:LANES] = jnp.where(lane < hd, low(kwn), 0.0).astype(BF16)
            k_scr[half, wr:wr + t, LANES:AUG_W] = feats
            v_scr[half, wr:wr + t] = jnp.where(lane < hd, low(vwn), 1.0).astype(BF16)
            kc_scr[half, :, 0:LANES] = jnp.where(lane_c < hd, low(kcv), 0.0).astype(BF16)
            kc_scr[half, :, LANES:AUG_W] = _key_feats(cend).astype(BF16)

    sg = jax.nn.sigmoid(gate_ref[...])
    lane = lax.broadcasted_iota(jnp.int32, (qb, LANES), 1)
    lane1 = lax.broadcasted_iota(jnp.int32, (1, LANES), 1)
    t_loc = lax.broadcasted_iota(jnp.int32, (qb, 1), 0)
    tq_col = s0 + t_loc

    for half in range(2):
        for hh in range(hpg):
            u = half * hpg + hh
            qcol = q_ref[:, (u // 2) * LANES:(u // 2 + 1) * LANES].astype(F32)
            qlow = jnp.where(lane < hd, qcol if u % 2 == 0 else pltpu.roll(qcol, hd, 1), 0.0)
            feat = jnp.where(lane1 == PAD_FEAT, MASK_NEG, 0.0)
            for j in range(N_FEAT):
                feat = jnp.where(lane1 == j, sf_ref[(pr * 2 + half) * hpg + hh, j], feat)
            feat = jnp.broadcast_to(feat, (qb, LANES)).astype(BF16)
            rows = slice(hh * qb, (hh + 1) * qb)
            qc_scr[half, rows, 0:LANES] = qlow.astype(BF16)
            qc_scr[half, rows, LANES:AUG_W] = feat
            qa_scr[half, rows, LANES:AUG_W] = feat

    cend_row = CMP_STRIDE * lax.broadcasted_iota(jnp.int32, (1, ncp), 1) + (CMP_LEN - 1)
    add_c = _tile_rows(jnp.where(tq_col >= cend_row, 0.0, MASK_NEG), hpg)
    jj = lax.broadcasted_iota(jnp.int32, (ns, ncp), 1) * CMP_STRIDE
    nn = lax.broadcasted_iota(jnp.int32, (ns, ncp), 0) * SEL_BLK
    ov_t = ((jj < nn + SEL_BLK) & (jj + CMP_LEN > nn)).astype(BF16)
    bid = lax.broadcasted_iota(jnp.int32, (ns, qb), 0)
    tq_row = s0 + lax.broadcasted_iota(jnp.int32, (ns, qb), 1)
    cur = tq_row // SEL_BLK
    valid_b = bid * SEL_BLK <= tq_row
    bonus = jnp.where((bid == 0) | (bid == cur) | (bid == cur - 1), FORCE_BONUS, 0.0)
    o_cmp = []
    for half in range(2):
        s = _dot_nt(qc_scr[half], kc_scr[half]) + add_c
        m = jnp.max(s, axis=-1, keepdims=True)
        e = jnp.where(s > 0.5 * MASK_NEG, jnp.exp2(s - m), 0.0)
        pc = e * (1.0 / jnp.maximum(jnp.sum(e, axis=-1, keepdims=True), 1e-30))
        o_cmp.append(_dot(pc.astype(BF16), vc_ref[...])[:, half * hd:(half + 1) * hd])
        pc_sum = pc[0:qb]
        for hh in range(1, hpg):
            pc_sum = pc_sum + pc[hh * qb:(hh + 1) * qb]
        hi = pc_sum.astype(BF16)
        lo = (pc_sum - hi.astype(F32)).astype(BF16)
        imp_t = _dot_nt(ov_t, hi) + _dot_nt(ov_t, lo)
        sel_t = _rank_topk_t(jnp.where(valid_b, imp_t + bonus, MASK_NEG), ns, qb)
        pieces = [jnp.zeros((hd, qb), F32), jnp.where(sel_t, 0.0, MASK_NEG)]
        if LANES - hd - ns > 0:
            pieces.append(jnp.zeros((LANES - hd - ns, qb), F32))
        mneg = jnp.concatenate(pieces, axis=0).T
        for hh in range(hpg):
            rows = slice(hh * qb, (hh + 1) * qb)
            qa_scr[half, rows, 0:LANES] = (qc_scr[half, rows, 0:LANES].astype(F32) + mneg).astype(BF16)

    nfull = nblk // (w // qb)
    npairs = nfull // 2
    c_loc = lax.broadcasted_iota(jnp.int32, (1, qb), 1)
    add_w0 = _tile_rows(jnp.concatenate([jnp.full((qb, qb), MASK_NEG, F32),
                                         jnp.where(c_loc > t_loc, 0.0, MASK_NEG)], axis=1), hpg)
    add_w2 = _tile_rows(jnp.concatenate([jnp.zeros((qb, qb), F32),
                                         jnp.where(c_loc <= t_loc, 0.0, MASK_NEG)], axis=1), hpg)
    r_tail = pl.multiple_of(nfull * w, w)
    kpos_tail = r_tail + lax.broadcasted_iota(jnp.int32, (1, w), 1)
    add_tail = _tile_rows(jnp.where(kpos_tail > tq_col, MASK_NEG, 0.0), hpg)
    r_odd = pl.multiple_of(jnp.maximum(nfull - 1, 0) * w, w)
    add_odd = jnp.where(nfull % 2 == 1, 0.0, MASK_NEG)
    w_row = lambda jw: pl.multiple_of(wrow0 + s0 + jw * w, qb)
    masked_jobs = ((qc_scr, w_row(0), add_w0, 1), (qc_scr, w_row(1), None, 1), (qc_scr, w_row(2), add_w2, 1),
                   (qa_scr, r_tail, add_tail, 0), (qa_scr, r_odd, add_odd, 0))

    def loop_row(j):
        return pl.multiple_of(jnp.where(j < 2 * npairs, j * w, wrow0), w)

    def scores(q_scr, r0, add, slot):
        for half in range(2):
            s = _dot_nt(q_scr[half], k_scr[half, pl.ds(r0, w), :])
            if add is not None:
                s = s + add
            s_scr[slot, half] = s
            mx = jnp.max(jnp.maximum(s[:, 0:LANES], s[:, LANES:w]), axis=-1, keepdims=True)
            mx_scr[slot, half] = jnp.broadcast_to(mx, (rows_all, LANES))

    def consume(r0, st, slot):
        for half in range(2):
            m_old = m_scr[st, half]
            m_new = jnp.maximum(m_old, mx_scr[slot, half])
            alpha = jnp.exp2(m_old - m_new)
            p = jnp.exp2(s_scr[slot, half] - jnp.concatenate([m_new, m_new], axis=1))
            acc_scr[st, half] = alpha * acc_scr[st, half] + _dot(p.astype(BF16), v_scr[half, pl.ds(r0, w), :])
            m_scr[st, half] = m_new

    m_scr[...] = jnp.full(m_scr.shape, MASK_NEG, F32)
    acc_scr[...] = jnp.zeros(acc_scr.shape, F32)
    scores(*masked_jobs[0][:3], 0)
    for i, (_, r0, _, st) in enumerate(masked_jobs):
        if i + 1 < len(masked_jobs):
            scores(*masked_jobs[i + 1][:3], (i + 1) % 2)
        else:
            scores(qa_scr, loop_row(0), None, (i + 1) % 2)
        consume(r0, st, i % 2)
    first = len(masked_jobs) % 2

    def step(i, carry):
        scores(qa_scr, loop_row(2 * i + 1), None, 1 - first)
        consume(loop_row(2 * i), 0, first)
        scores(qa_scr, loop_row(2 * i + 2), None, first)
        consume(loop_row(2 * i + 1), 0, 1 - first)
        return carry

    lax.fori_loop(0, npairs, step, 0)

    for half in range(2):
        outs = []
        for st in range(2):
            a = acc_scr[st, half]
            outs.append(a[:, 0:hd] * (1.0 / a[:, hd:hd + 1]))
        for hh in range(hpg):
            u = half * hpg + hh
            gi = u * 3
            rows = slice(hh * qb, (hh + 1) * qb)
            out = (sg[:, gi:gi + 1] * o_cmp[half][rows] + sg[:, gi + 1:gi + 2] * outs[0][rows]
                   + sg[:, gi + 2:gi + 3] * outs[1][rows])
            o_ref[:, u * hd:(u + 1) * hd] = out.astype(BF16)


def _nsa_attention(slope_feats, q, gates, kc, vc, kvb, *, nb, t):
    assert WIN_JOBS * KEY_CHUNK >= WINDOW + Q_BLK and t % KEY_CHUNK == 0 and t // SEL_BLK <= LANES - HEAD_DIM
    nq = t // Q_BLK
    ncp = t // CMP_STRIDE
    pw = 2 * NSA_HPG * HEAD_DIM
    rows_all = NSA_HPG * Q_BLK
    krows = 2 * t + WIN_PAD
    kvcol = lambda c: (lambda b, p, n: (b, 2 * c + p))
    kern = functools.partial(_nsa_attn_kernel, t=t)
    return pl.pallas_call(
        kern,
        grid=(nb, 2, nq),
        in_specs=[pl.BlockSpec(memory_space=pltpu.SMEM),
                  pl.BlockSpec((Q_BLK, pw), lambda b, p, n: (b * nq + n, p)),
                  pl.BlockSpec((Q_BLK, LANES), lambda b, p, n: (b * nq + n, p)),
                  pl.BlockSpec((None, None, ncp, PAIR_W), lambda b, p, n: (b, p, 0, 0)),
                  pl.BlockSpec((None, None, ncp, PAIR_W), lambda b, p, n: (b, p, 0, 0)),
                  pl.BlockSpec((t, PAIR_W), kvcol(2)), pl.BlockSpec((t, PAIR_W), kvcol(3)),
                  pl.BlockSpec((t, PAIR_W), kvcol(4)), pl.BlockSpec((t, PAIR_W), kvcol(5))],
        out_specs=pl.BlockSpec((Q_BLK, pw), lambda b, p, n: (b * nq + n, p)),
        out_shape=jax.ShapeDtypeStruct((nb * t, NSA_Q_W), BF16),
        scratch_shapes=[pltpu.VMEM((2, krows, AUG_W), BF16), pltpu.VMEM((2, krows, LANES), BF16),
                        pltpu.VMEM((2, ncp, AUG_W), BF16),
                        pltpu.VMEM((2, rows_all, AUG_W), BF16), pltpu.VMEM((2, rows_all, AUG_W), BF16),
                        pltpu.VMEM((2, 2, rows_all, KEY_CHUNK), F32), pltpu.VMEM((2, 2, rows_all, LANES), F32),
                        pltpu.VMEM((2, 2, rows_all, LANES), F32), pltpu.VMEM((2, 2, rows_all, LANES), F32)],
        compiler_params=_cparams("parallel", "parallel", "arbitrary"),
        name="nsa_attn",
    )(slope_feats, q, gates, kc, vc, kvb, kvb, kvb, kvb)


PAGES_PER_STEP = 32


def _page_compress_t_kernel(pt_ref, cache_ref, wbd_ref, w2_ref, pos_ref, kvc_ref,
                            buf, xa0, xa1, xa2, xa3, xb0, xb1, xb2, xb3, parts_scr, sem, *, page):
    b = pl.program_id(0)
    hf = pl.program_id(1)
    nhf = pl.num_programs(1)
    step = b * nhf + hf
    nstep = pl.num_programs(0) * nhf
    xsets = ((xa0, xa1, xa2, xa3), (xb0, xb1, xb2, xb3))
    nch = PAGES_PER_STEP * page // CMP_STRIDE
    row0 = pl.multiple_of(hf * nch, nch)

    def copies(s, slot):
        out = []
        for i in range(PAGES_PER_STEP):
            pg = pt_ref[s * PAGES_PER_STEP + i]
            for c in range(2):
                out.append(pltpu.make_async_copy(cache_ref.at[0, pg, c], buf.at[slot, c, i], sem.at[slot]))
        return out

    def transpose_pages(slot):
        for i in range(PAGES_PER_STEP):
            for c in range(2):
                tile = buf[slot, c, i].reshape(NSA_KV_HEADS * HEAD_DIM, page)
                for p in range(2):
                    xsets[slot][c * 2 + p][i * page:(i + 1) * page, :] = tile[p * PAIR_W:(p + 1) * PAIR_W].T

    def mlp(slot):
        def load_rows(l, c, p):
            return xsets[slot][c * 2 + p][pl.ds(l, nch, stride=CMP_STRIDE), :]
        for c in range(2):
            for p in range(2):
                parts_scr[c * 2 + p, pl.ds(row0, nch), :] = _cmp_parts(load_rows, wbd_ref, c, p)

    @pl.when(step == 0)
    def _():
        for cp in copies(0, 0):
            cp.start()
        for cp in copies(0, 0):
            cp.wait()
        transpose_pages(0)

        @pl.when(nstep > 1)
        def _():
            for cp in copies(1, 1):
                cp.start()

    for par in range(2):
        @pl.when(step % 2 == par)
        def _():
            @pl.when(step + 1 < nstep)
            def _():
                for cp in copies(step + 1, 1 - par):
                    cp.wait()

            @pl.when(step + 2 < nstep)
            def _():
                for cp in copies(step + 2, par):
                    cp.start()

            transpose_pages(1 - par)
            mlp(par)

    @pl.when(hf == nhf - 1)
    def _():
        for c in range(2):
            posb = _cmp_pos_bias(pos_ref, wbd_ref, c)
            for p in range(2):
                kvc_ref[c, p] = _cmp_finish(parts_scr[c * 2 + p], posb, w2_ref, c).astype(BF16)


def _page_compress_t(page_table, cache_t, wbd, w2bd, pos):
    sb, n_pages = page_table.shape
    page = cache_t.shape[-1]
    ncp = n_pages * page // CMP_STRIDE
    rows = PAGES_PER_STEP * page
    kern = functools.partial(_page_compress_t_kernel, page=page)
    grid_spec = pltpu.PrefetchScalarGridSpec(
        num_scalar_prefetch=1,
        grid=(sb, n_pages // PAGES_PER_STEP),
        in_specs=[pl.BlockSpec(memory_space=pl.ANY),
                  pl.BlockSpec(wbd.shape, lambda b, h, pt: (0,) * wbd.ndim),
                  pl.BlockSpec(w2bd.shape, lambda b, h, pt: (0,) * w2bd.ndim),
                  pl.BlockSpec(pos.shape, lambda b, h, pt: (0,) * pos.ndim)],
        out_specs=pl.BlockSpec((None, 2, 2, ncp, PAIR_W), lambda b, h, pt: (b, 0, 0, 0, 0)),
        scratch_shapes=[pltpu.VMEM((2, 2, PAGES_PER_STEP, NSA_KV_HEADS, HEAD_DIM, page), F32)]
        + [pltpu.VMEM((rows, PAIR_W), F32)] * 8
        + [pltpu.VMEM((4, ncp, 2 * PAIR_W), F32), pltpu.SemaphoreType.DMA((2,))],
    )
    return pl.pallas_call(
        kern,
        grid_spec=grid_spec,
        out_shape=jax.ShapeDtypeStruct((sb, 2, 2, ncp, PAIR_W), BF16),
        compiler_params=_cparams("arbitrary", "arbitrary"),
        name="nsa_page_compress",
    )(page_table.reshape(-1), cache_t, wbd, w2bd, pos)


def _row_slopes(slope_ref, g, rows):
    hh = lax.broadcasted_iota(jnp.int32, (rows, 1), 0) % NSA_HPG
    col = jnp.zeros((rows, 1), F32)
    for h in range(NSA_HPG):
        col = jnp.where(hh == h, slope_ref[g * NSA_HPG + h], col)
    return col


def _sample_cmp_kernel(slope_ref, q_ref, kvc_ref, ocmp_ref, imp_ref, *, past, nsp):
    rows = q_ref.shape[1]
    ncp = kvc_ref.shape[2]
    t_col = lax.broadcasted_iota(jnp.int32, (rows, 1), 0) // NSA_HPG
    jrow = lax.broadcasted_iota(jnp.int32, (1, ncp), 1)
    dist = (past + t_col) - (CMP_STRIDE * jrow + (CMP_LEN - 1))
    mask = (dist >= 0) & (jrow < ncp - 1)
    distf = dist.astype(F32)
    ri = lax.broadcasted_iota(jnp.int32, (rows, rows), 0) // NSA_HPG
    ci = lax.broadcasted_iota(jnp.int32, (rows, rows), 1) // NSA_HPG
    same_t = (ri == ci).astype(BF16)
    jj = lax.broadcasted_iota(jnp.int32, (ncp, nsp), 0) * CMP_STRIDE
    nn = lax.broadcasted_iota(jnp.int32, (ncp, nsp), 1) * SEL_BLK
    ov = ((jj < nn + SEL_BLK) & (jj + CMP_LEN > nn)).astype(BF16)
    for g in range(NSA_KV_HEADS):
        pr, half = divmod(g, 2)
        lanes = slice(half * HEAD_DIM, (half + 1) * HEAD_DIM)
        s_c = _dot_nt(q_ref[g], kvc_ref[0, pr, :, lanes])
        s = jnp.where(mask, s_c - _row_slopes(slope_ref, g, rows) * distf, MASK_NEG)
        e = jnp.where(mask, jnp.exp(s - jnp.max(s, axis=-1, keepdims=True)), 0.0)
        pc = e / jnp.maximum(jnp.sum(e, axis=-1, keepdims=True), 1e-30)
        ocmp_ref[g] = _dot(pc.astype(BF16), kvc_ref[1, pr])[:, lanes]
        hi = pc.astype(BF16)
        lo = (pc - hi.astype(F32)).astype(BF16)
        pcs = _dot(same_t, hi) + _dot(same_t, lo)
        hi = pcs.astype(BF16)
        lo = (pcs - hi.astype(F32)).astype(BF16)
        imp_ref[g] = _dot(hi, ov) + _dot(lo, ov)


def _sample_cmp(slopes, q_rows, kvc, *, past, nsp):
    sb, ng, rows, hd = q_rows.shape
    ncp = kvc.shape[3]
    kern = functools.partial(_sample_cmp_kernel, past=past, nsp=nsp)
    return pl.pallas_call(
        kern,
        grid=(sb,),
        in_specs=[pl.BlockSpec(memory_space=pltpu.SMEM),
                  pl.BlockSpec((None, ng, rows, hd), lambda b: (b, 0, 0, 0)),
                  pl.BlockSpec((None, 2, 2, ncp, PAIR_W), lambda b: (b, 0, 0, 0, 0))],
        out_specs=[pl.BlockSpec((None, ng, rows, hd), lambda b: (b, 0, 0, 0)),
                   pl.BlockSpec((None, ng, rows, nsp), lambda b: (b, 0, 0, 0))],
        out_shape=[jax.ShapeDtypeStruct((sb, ng, rows, hd), F32),
                   jax.ShapeDtypeStruct((sb, ng, rows, nsp), F32)],
        compiler_params=_cparams("parallel"),
        name="nsa_sample_cmp",
    )(slopes, q_rows, kvc)


def _sample_topk_kernel(imp_ref, idx_ref, score_scr, *, past, n_sel, st):
    nsr, ncol = imp_ref.shape
    bid = lax.broadcasted_iota(jnp.int32, (nsr, ncol), 0)
    tq = past + lax.broadcasted_iota(jnp.int32, (nsr, ncol), 1) % st
    cur = tq // SEL_BLK
    valid = (bid * SEL_BLK <= tq) & (bid < n_sel)
    forced = (bid == 0) | (bid == cur) | (bid == cur - 1)
    score = jnp.where(valid, imp_ref[...] + jnp.where(forced, FORCE_BONUS, 0.0), MASK_NEG)
    score_scr[...] = score

    def body(m, cnt):
        sm = score_scr[pl.ds(m, 1), :]
        ahead = (sm > score) | ((sm == score) & (m < bid))
        return cnt + ahead.astype(jnp.int32)

    cnt = lax.fori_loop(0, n_sel, body, jnp.zeros((nsr, ncol), jnp.int32))
    sel = (cnt < SEL_TOPK) & (bid < n_sel)
    tril = (lax.broadcasted_iota(jnp.int32, (nsr, nsr), 1) <= lax.broadcasted_iota(jnp.int32, (nsr, nsr), 0))
    prefix = _dot(tril.astype(BF16), sel.astype(BF16))
    for k in range(SEL_TOPK):
        hit = sel & (prefix == float(k + 1))
        idx_ref[k:k + 1, :] = jnp.sum(jnp.where(hit, bid, 0), axis=0, keepdims=True)


def _sample_topk(imp_t, *, past, n_sel, st):
    nsr, ncol = imp_t.shape
    kern = functools.partial(_sample_topk_kernel, past=past, n_sel=n_sel, st=st)
    return pl.pallas_call(
        kern,
        grid=(1,),
        in_specs=[_full(imp_t.shape)],
        out_specs=_full((SEL_TOPK, ncol)),
        out_shape=jax.ShapeDtypeStruct((SEL_TOPK, ncol), jnp.int32),
        scratch_shapes=[pltpu.VMEM((nsr, ncol), F32)],
        compiler_params=_cparams("arbitrary"),
        name="nsa_sample_topk",
    )(imp_t)


def _joint_softmax_pv(s_a, v_a_fn, s_b, v_b):
    m = jnp.maximum(jnp.max(s_a, axis=-1, keepdims=True), jnp.max(s_b, axis=-1, keepdims=True))
    p_a = jnp.where(s_a > 0.5 * MASK_NEG, jnp.exp(s_a - m), 0.0)
    p_b = jnp.where(s_b > 0.5 * MASK_NEG, jnp.exp(s_b - m), 0.0)
    l = jnp.sum(p_a, axis=-1, keepdims=True) + jnp.sum(p_b, axis=-1, keepdims=True)
    acc = v_a_fn(p_a.astype(BF16)) + _dot(p_b.astype(BF16), v_b)
    return acc / jnp.maximum(l, 1e-30)


def _sample_attn_kernel(idx_ref, pt_ref, slope_ref, cache_ref, q_ref, idxv_ref, e16_ref, gate_ref, ocmp_ref,
                        ksn_ref, vsn_ref, kwn_ref, vwn_ref, kwin_ref, vwin_ref, o_ref, kvbuf, sem,
                        *, past, n_pages, st):
    b = pl.program_id(0)
    pr = pl.program_id(1)
    rows = q_ref.shape[1]
    page = kvbuf.shape[-1] // SEL_TOPK
    nkey = SEL_TOPK * page
    last_blk = past // SEL_BLK - 1
    npr = pl.num_programs(1)
    step = b * npr + pr
    nstep = pl.num_programs(0) * npr

    def copies(s, slot):
        sb_, spr = s // npr, s % npr
        out = []
        for half in range(2):
            g = spr * 2 + half
            for t in range(st):
                for k in range(SEL_TOPK):
                    blk = jnp.minimum(idx_ref[((sb_ * NSA_KV_HEADS + g) * st + t) * SEL_TOPK + k], last_blk)
                    pg = pt_ref[sb_ * n_pages + lax.shift_right_logical(blk, 1)]
                    dst = pl.ds(k * page, page)
                    out.append(pltpu.make_async_copy(cache_ref.at[0, pg, pl.ds(2, 2), g],
                                                     kvbuf.at[slot, half, t, :, :, dst], sem.at[slot]))
        return out

    @pl.when(step == 0)
    def _():
        for cp in copies(0, 0):
            cp.start()

    slot = step % 2

    @pl.when(step + 1 < nstep)
    def _():
        for cp in copies(step + 1, 1 - slot):
            cp.start()

    for cp in copies(step, slot):
        cp.wait()

    t_col = lax.broadcasted_iota(jnp.int32, (rows, 1), 0) // NSA_HPG
    tqf = (past + t_col).astype(F32)
    col = lax.broadcasted_iota(jnp.int32, (1, nkey), 1)
    off = (col % SEL_BLK).astype(F32)
    col_half = ((col % page) // SEL_BLK).astype(F32)
    tp = lax.broadcasted_iota(jnp.int32, (1, ksn_ref.shape[0]), 1)
    dn = t_col - tp
    valid_n = (dn >= 0) & (tp < st)
    dnf = dn.astype(F32)
    wb = kwin_ref.shape[0]
    dist_w = (wb + t_col) - lax.broadcasted_iota(jnp.int32, (1, wb), 1)
    valid_w = (dist_w >= 0) & (dist_w < WINDOW)

    for half in range(2):
        g = pr * 2 + half
        lanes = slice(half * HEAD_DIM, (half + 1) * HEAD_DIM)
        q = q_ref[half]
        slope = _row_slopes(slope_ref, g, rows)

        blk = _dot(idxv_ref[half], e16_ref[...])
        is_new = blk > (last_blk + 0.5)
        dist = tqf - (blk * SEL_BLK + off)
        in_half = (blk - 2.0 * jnp.floor(0.5 * blk)) == col_half
        valid = jnp.logical_not(is_new) & in_half & (dist >= 0)
        s_sel = jnp.full((rows, nkey), MASK_NEG, F32)
        for t in range(st):
            s_t = _dot(q, kvbuf[slot, half, t, 0].astype(BF16))
            s_sel = jnp.where(t_col == t, s_t, s_sel)
        s_sel = jnp.where(valid, s_sel - slope * dist, MASK_NEG)
        has_new = jnp.max(is_new.astype(F32), axis=-1, keepdims=True) > 0.5
        s_new = jnp.where(valid_n & has_new, _dot_nt(q, ksn_ref[:, lanes]) - slope * dnf, MASK_NEG)

        def pv_sel(p):
            acc = jnp.zeros((rows, HEAD_DIM), F32)
            for t in range(st):
                acc = jnp.where(t_col == t, _dot_nt(p, kvbuf[slot, half, t, 1].astype(BF16)), acc)
            return acc

        o_sel = _joint_softmax_pv(s_sel, pv_sel, s_new, vsn_ref[:, lanes])

        s_w = _dot_nt(q, kwin_ref[:, lanes].astype(BF16))
        s_w = jnp.where(valid_w, s_w - slope * dist_w.astype(F32), MASK_NEG)
        s_wn = jnp.where(valid_n, _dot_nt(q, kwn_ref[:, lanes]) - slope * dnf, MASK_NEG)
        vw = vwin_ref[...].astype(BF16)
        o_w = _joint_softmax_pv(s_w, lambda p: _dot(p, vw), s_wn, vwn_ref[...])[:, lanes]

        sg = jax.nn.sigmoid(gate_ref[half])
        o_ref[half] = sg[:, 0:1] * ocmp_ref[half] + sg[:, 1:2] * o_sel + sg[:, 2:3] * o_w


def _sample_attention(idx_flat, pt_flat, slopes, cache_t, q_rows, idx_rows, e16, gate_rows, ocmp, new16, win_buf,
                      *, past, n_pages, st):
    sb, ng, rows, hd = q_rows.shape
    wb = win_buf.shape[1]
    nkey = e16.shape[1]
    pairg = lambda w: pl.BlockSpec((None, 2, rows, w), lambda b, p, *_: (b, p, 0, 0))
    newc = lambda c: pl.BlockSpec((None, new16.shape[1], PAIR_W), lambda b, p, *_: (b, 0, 2 * c + p))
    kern = functools.partial(_sample_attn_kernel, past=past, n_pages=n_pages, st=st)
    grid_spec = pltpu.PrefetchScalarGridSpec(
        num_scalar_prefetch=2,
        grid=(sb, 2),
        in_specs=[pl.BlockSpec(memory_space=pltpu.SMEM), pl.BlockSpec(memory_space=pl.ANY),
                  pairg(hd), pairg(LANES), pl.BlockSpec(e16.shape, lambda b, p, *_: (0, 0)), pairg(LANES), pairg(hd),
                  newc(2), newc(3), newc(4), newc(5),
                  pl.BlockSpec((None, wb, PAIR_W), lambda b, p, *_: (b, 0, p)),
                  pl.BlockSpec((None, wb, PAIR_W), lambda b, p, *_: (b, 0, 2 + p))],
        out_specs=pairg(hd),
        scratch_shapes=[pltpu.VMEM((2, 2, st, 2, HEAD_DIM, nkey), F32), pltpu.SemaphoreType.DMA((2,))],
    )
    return pl.pallas_call(
        kern,
        grid_spec=grid_spec,
        out_shape=jax.ShapeDtypeStruct((sb, ng, rows, hd), F32),
        compiler_params=_cparams("arbitrary", "arbitrary"),
        name="nsa_sample_attn",
    )(idx_flat, pt_flat, slopes, cache_t, q_rows, idx_rows, e16, gate_rows, ocmp, new16, new16, new16, new16,
      win_buf, win_buf)


def _pad_cols(w, mult=LANES):
    n = w.shape[1]
    return jnp.pad(w, ((0, 0), (0, -n % mult)))


def _prep_nsa_in(w_in):
    body = w_in[:, :NSA_Q_W + 6 * NSA_KV_W]
    gates = w_in[:, NSA_Q_W + 6 * NSA_KV_W:]
    per_pair = 2 * NSA_HPG * 3
    blocks = [_pad_cols(gates[:, p * per_pair:(p + 1) * per_pair]) for p in range(2)]
    return jnp.concatenate([body] + blocks, axis=1).astype(BF16)


def _prep_cmp(w_cmp1, w_cmp2, cmp_pos):
    hd = HEAD_DIM
    z = jnp.zeros((2, CMP_STRIDE, hd, hd), F32)
    wa, wb = w_cmp1[:, :CMP_STRIDE], w_cmp1[:, CMP_STRIDE:]
    top = jnp.concatenate([wa, z, wb, z], axis=-1)
    bot = jnp.concatenate([z, wa, z, wb], axis=-1)
    wbd = jnp.concatenate([top, bot], axis=-2).astype(BF16)
    z2 = jnp.zeros((2, hd, hd), F32)
    w2bd = jnp.concatenate([jnp.concatenate([w_cmp2, z2], -1), jnp.concatenate([z2, w_cmp2], -1)], -2).astype(BF16)
    pos = cmp_pos.transpose(1, 0, 2)
    pos = jnp.concatenate([pos, pos], axis=-1)
    rows = jnp.stack([pos[:, :CMP_STRIDE], pos[:, CMP_STRIDE:]], axis=2)
    pos_tiles = jnp.pad(rows, ((0, 0), (0, 0), (0, SUBLANES - 2), (0, 0)))
    return wbd, w2bd, pos_tiles


def _alibi_slopes():
    h = jnp.arange(1, NSA_HEADS + 1, dtype=F32)
    return jnp.exp2(-8.0 * h / NSA_HEADS)


def _prompt_rows_tile(m):
    for tm in (512, 256, 128, 64, 32, 16, 8):
        if m % tm == 0:
            return tm
    raise ValueError(m)


def kernel(x_prompt, x_sample, cache_nsa_kv, cache_nsa_win, state_mlstm_C, state_mlstm_n, state_mlstm_m,
           state_ffn_conv, page_table, norm_g, w_nsa_in, w_nsa_out, w_cmp1, w_cmp2, cmp_pos, w_ml_in,
           b_ml_gate, ml_head_norm, w_ml_out, w_ffn_up, ffn_conv_w, ffn_conv_b, w_ffn_down):
    nb, t, d = x_prompt.shape
    sb, st, _ = x_sample.shape
    mp = nb * t
    tm = _prompt_rows_tile(mp)
    slopes = _alibi_slopes()

    w_in0 = _prep_nsa_in(w_nsa_in[0])
    w_out0 = w_nsa_out[0].astype(BF16)
    wbd, w2bd, pos_tiles = _prep_cmp(w_cmp1[0], w_cmp2[0], cmp_pos[0])
    w_ml = _pad_cols(w_ml_in[0]).astype(BF16)
    w_mlo = w_ml_out[0].astype(BF16)
    w_up = w_ffn_up.astype(BF16)
    w_dn = w_ffn_down.astype(BF16)
    bg_row = _pad_cols(b_ml_gate[0].reshape(1, 2 * ML_HEADS))
    g = norm_g[:, :, None, :]

    xp = x_prompt.reshape(mp, d)
    _, q, kva, kvb, win, gates = _nsa_project(xp, g[0, 0], w_in0, tm)
    kc, vc = _nsa_compress(kva, wbd, w2bd, pos_tiles, nb=nb, t=t)
    o = _nsa_attention(_slope_feats(), q, gates, kc, vc, kvb, nb=nb, t=t)
    kv_p = kva.reshape(1, nb, t, 4, NSA_KV_HEADS, HEAD_DIM)
    w_keep = min(WINDOW, t)
    win_p = win.reshape(nb, t, 2, NSA_KV_HEADS, HEAD_DIM)[None, :, t - w_keep:]

    halo = SUBLANES
    zstate = jnp.zeros((nb, halo, 2 * D_FF), F32)
    tmf = min(t, 1024)
    conv_p = []

    def ffn_prompt(o_mix, xp, w_o, i):
        xo, ta, tb = _conv_ffn(o_mix, xp, w_o, g[i, 1], zstate, g[i, 2], g[i, 3], w_up[i], ffn_conv_w[i],
                               ffn_conv_b[i][None], w_dn[i], tm=tmf, fc=512, shift=1, tiles_per_seq=t // tmf)
        tail = jnp.concatenate([ta, tb], axis=-1).reshape(nb, t // tmf, halo, 2 * D_FF)
        conv_p.append(tail[:, -1, halo - (CONV_W - 1):])
        return xo

    xp = ffn_prompt(o, xp, w_out0, 0)

    lp = math.gcd(t, ML_CHUNK)
    qkv, om, gm = _ml_project(xp, g[1, 0], w_ml, tm)
    zc = jnp.zeros((nb, ML_HEADS, ML_DV, ML_DQK), F32)
    zn = jnp.zeros((nb, ML_HEADS, 1, ML_DQK), F32)
    zm = jnp.zeros((nb, ML_HEADS, 1, 1), F32)
    if lp == ML_DQK:
        ym, c_p, n_p, m_p = _mlstm_stacked(qkv, om, gm, bg_row, ml_head_norm[0][None], zc, zn[:, :, 0], zm[:, :, 0],
                                           nb=nb, l=lp)
        n_p, m_p = n_p[:, :, None, :], m_p[:, :, None, :]
    else:
        ym, c_p, n_p, m_p = _mlstm(qkv, om, gm, bg_row, ml_head_norm[0][None], zc, zn, zm, nb=nb, lp=lp, lv=lp)
    xp = ffn_prompt(ym, xp, w_mlo, 1)

    ms = sb * st
    ng, hpg, hd = NSA_KV_HEADS, NSA_HPG, HEAD_DIM
    xs = x_sample.transpose(1, 0, 2).reshape(ms, d)
    qs, _, kva_s, kvb_s, win_s, gates_s = _nsa_project(xs, g[0, 0], w_in0, ms)
    n_phys, page = cache_nsa_kv.shape[1:3]
    n_pages = page_table.shape[1]
    past = n_pages * page
    cache_t = cache_nsa_kv.transpose(0, 1, 3, 4, 5, 2)
    kvc = _page_compress_t(page_table, cache_t, wbd, w2bd, pos_tiles)

    rows = st * hpg
    q_rows = qs.reshape(st, sb, ng, hpg, hd).transpose(1, 2, 0, 3, 4).reshape(sb, ng, rows, hd)
    n_sel = -(-(past + st) // SEL_BLK)
    nsp = -(-n_sel // LANES) * LANES
    nsr = -(-n_sel // SUBLANES) * SUBLANES
    ocmp, imp = _sample_cmp(slopes, q_rows, kvc, past=past, nsp=nsp)
    imp_t = imp[:, :, ::hpg, :nsr].reshape(sb * ng * st, nsr).T
    idx = _sample_topk(imp_t, past=past, n_sel=n_sel, st=st)
    idx_bgtk = idx.T.reshape(sb, ng, st, SEL_TOPK)
    idx_rows = _pad_cols(jnp.repeat(idx_bgtk, hpg, axis=2).reshape(sb * ng * rows, SEL_TOPK))
    idx_rows = idx_rows.reshape(sb, ng, rows, LANES).astype(BF16)
    e16 = (jnp.arange(LANES, dtype=jnp.int32)[:, None]
           == jnp.arange(SEL_TOPK * page, dtype=jnp.int32)[None, :] // page).astype(BF16)
    gate_rows = gates_s.reshape(st, sb, 2, LANES)[..., :2 * hpg * 3].reshape(st, sb, ng, hpg, 3)
    gate_rows = _pad_cols(gate_rows.transpose(1, 2, 0, 3, 4).reshape(sb * ng * rows, 3)).reshape(sb, ng, rows, LANES)
    new16 = jnp.pad(kvb_s.reshape(st, sb, -1).transpose(1, 0, 2), ((0, 0), (0, 16 - st), (0, 0)))
    wb = cache_nsa_win.shape[2]
    win_buf = cache_nsa_win[0].reshape(sb, wb, 2 * NSA_KV_W)
    o_s = _sample_attention(idx_bgtk.reshape(-1), page_table.reshape(-1), slopes, cache_t, q_rows, idx_rows, e16,
                            gate_rows, ocmp, new16, win_buf, past=past, n_pages=n_pages, st=st)
    o_s = o_s.reshape(sb, ng, st, hpg, hd).transpose(2, 0, 1, 3, 4).reshape(ms, NSA_Q_W).astype(BF16)
    kv_s = kva_s.reshape(st, sb, 4, ng, hd).transpose(1, 0, 2, 3, 4)[None]
    win_new = win_s.reshape(st, sb, 2, ng, hd).transpose(1, 0, 2, 3, 4)
    win_s_out = jnp.concatenate([cache_nsa_win[0], win_new], axis=1)[None, :, st:]

    conv_s = []

    def ffn_sample(o_mix, xs, w_o, i):
        state = state_ffn_conv[i].transpose(1, 0, 2).reshape(1, (CONV_W - 1) * sb, 2 * D_FF)
        xo, ta, tb = _conv_ffn(o_mix, xs, w_o, g[i, 1], state, g[i, 2], g[i, 3], w_up[i], ffn_conv_w[i],
                               ffn_conv_b[i][None], w_dn[i], tm=ms, fc=512, shift=sb, tiles_per_seq=1)
        tail = jnp.concatenate([ta, tb], axis=-1).reshape(CONV_W - 1, sb, 2 * D_FF)
        conv_s.append(tail.transpose(1, 0, 2))
        return xo

    xs = ffn_sample(o_s, xs, w_out0, 0)

    lps = SUBLANES
    to_seq = lambda a: jnp.pad(a.reshape(st, sb, -1).transpose(1, 0, 2),
                               ((0, 0), (0, lps - st), (0, 0))).reshape(sb * lps, -1)
    qkv_s, om_s, gm_s = _ml_project(xs, g[1, 0], w_ml, ms)
    ym_s, c_s, n_s, m_s = _mlstm(to_seq(qkv_s), to_seq(om_s), to_seq(gm_s), bg_row, ml_head_norm[0][None],
                                 state_mlstm_C[0], state_mlstm_n[0][:, :, None, :],
                                 state_mlstm_m[0][:, :, None, None], nb=sb, lp=lps, lv=math.gcd(st, ML_CHUNK))
    ym_s = ym_s.reshape(sb, lps, ML_V_W)[:, :st].transpose(1, 0, 2).reshape(ms, ML_V_W)
    xs = ffn_sample(ym_s, xs, w_mlo, 1)

    return (xp.reshape(nb, t, d), xs.reshape(st, sb, d).transpose(1, 0, 2), kv_p, kv_s, win_p, win_s_out,
            c_p[None], c_s[None], n_p[None, :, :, 0], n_s[None, :, :, 0],
            m_p[None, :, :, 0, 0], m_s[None, :, :, 0, 0], jnp.stack(conv_p), jnp.stack(conv_s))
```
